```python
import math
import jax, jax.numpy as jnp
from jax import lax
import numpy as np

D_MODEL = 1024
BATCH = 4
SEQ = 8192
DEPTH = 1

CHUNK = 64
Q_BLOCK = 128
EPS = 1e-6
ROPE_THETA = 10000.0

DA_HEADS = 4
DA_HEAD_DIM = 64
DA_V_DIM = 2 * DA_HEAD_DIM
DA_QK_WIDTH = DA_HEADS * 2 * DA_HEAD_DIM
DA_WIDTH = DA_HEADS * DA_V_DIM

HG_HEADS = 4
HG_KEY_DIM = 128
HG_V_DIM = 128
HG_WIDTH = HG_HEADS * HG_KEY_DIM

N_BRANCHES = 2
IN_SPLITS = [DA_QK_WIDTH, DA_QK_WIDTH, DA_WIDTH,
             HG_WIDTH, HG_WIDTH, HG_HEADS * HG_V_DIM, HG_HEADS * HG_V_DIM,
             D_MODEL, D_MODEL]
IN_COLS = sum(IN_SPLITS)

N_EXPERTS = 32
TOP_K = 4
D_EXPERT = D_MODEL
SWIGLU_LIMIT = 7.0
SWIGLU_ALPHA = 1.702
MOE_BLOCK = 256

kernel_name = "hybrid_diffattn_hgrn2_moe_block"


def rms_norm(x, w):
    xf = x.astype(jnp.float32)
    y = xf * lax.rsqrt(jnp.mean(xf * xf, axis=-1, keepdims=True) + EPS)
    return (y * w.astype(jnp.float32)).astype(x.dtype)


def rope_tables(seq, dtype):
    pos = jnp.arange(seq, dtype=jnp.float32)
    inv = ROPE_THETA ** (-jnp.arange(0, DA_HEAD_DIM, 2, dtype=jnp.float32) / DA_HEAD_DIM)
    ang = pos[:, None] * inv[None, :]
    cos = jnp.cos(ang)[:, None, None, :].astype(dtype)
    sin = jnp.sin(ang)[:, None, None, :].astype(dtype)
    return cos, sin


def apply_rope(x, cos, sin):
    x1, x2 = jnp.split(x, 2, axis=-1)
    return jnp.concatenate([x1 * cos - x2 * sin, x2 * cos + x1 * sin], axis=-1)


def diff_attention(q, k, v, lam, lambda_init, subln_w, cos, sin):
    B, S = q.shape[0], q.shape[1]
    q = apply_rope(q, cos, sin).transpose(0, 2, 3, 1, 4)
    k = apply_rope(k, cos, sin).transpose(0, 2, 3, 1, 4)
    v = v.transpose(0, 2, 1, 3)
    nb = S // Q_BLOCK
    q_blocks = q.reshape(B, DA_HEADS, 2, nb, Q_BLOCK, DA_HEAD_DIM).transpose(3, 0, 1, 2, 4, 5)
    key_chunk = jnp.arange(S) // CHUNK
    scale = DA_HEAD_DIM ** -0.5
    neg = jnp.finfo(jnp.float32).min

    def one_block(args):
        qb, bi = args
        s = jnp.einsum('bhmqd,bhmkd->bhmqk', qb, k).astype(jnp.float32) * scale
        q_chunk = (bi * Q_BLOCK + jnp.arange(Q_BLOCK)) // CHUNK
        mask = key_chunk[None, :] <= q_chunk[:, None]
        p = jax.nn.softmax(jnp.where(mask, s, neg), axis=-1)
        a = p[:, :, 0] - lam * p[:, :, 1]
        return jnp.einsum('bhqk,bhkv->bhqv', a.astype(v.dtype), v)

    o = lax.map(one_block, (q_blocks, jnp.arange(nb)))
    o = o.transpose(1, 0, 3, 2, 4).reshape(B, S, DA_HEADS, DA_V_DIM)
    o = rms_norm(o, subln_w) * (1.0 - lambda_init)
    return o.reshape(B, S, DA_WIDTH)


def hgrn2(q_raw, f_raw, i_in, g_raw, lb, norm_w):
    B, S = q_raw.shape[0], q_raw.shape[1]
    n = S // CHUNK
    lb = lb.reshape(HG_HEADS, HG_KEY_DIM)
    q = jax.nn.silu(q_raw.astype(jnp.float32))
    f = lb + (1.0 - lb) * jax.nn.sigmoid(f_raw.astype(jnp.float32))
    k = 1.0 - f
    logf = jnp.log(f)
    v = i_in.astype(jnp.float32)

    def to_chunks(t):
        return t.reshape(B, n, CHUNK, HG_HEADS, t.shape[-1]).transpose(0, 3, 1, 2, 4)

    q, k, v, logf = to_chunks(q), to_chunks(k), to_chunks(v), to_chunks(logf)
    G = jnp.cumsum(logf, axis=3)
    G_last = G[:, :, :, -1:, :]
    q_t = q * jnp.exp(G)
    k_t = k * jnp.exp(-G)
    k_dec = k * jnp.exp(G_last - G)
    causal = jnp.tril(jnp.ones((CHUNK, CHUNK), dtype=bool))
    A = jnp.where(causal, jnp.einsum('bhncd,bhnsd->bhncs', q_t, k_t), 0.0)
    o_intra = jnp.einsum('bhncs,bhnsv->bhncv', A, v)
    dS = jnp.einsum('bhncd,bhncv->bhndv', k_dec, v)
    decay = jnp.exp(G_last[:, :, :, 0, :])

    def step(state, inp):
        dec, ds = inp
        return dec[..., None] * state + ds, state

    init = jnp.zeros((B, HG_HEADS, HG_KEY_DIM, HG_V_DIM), jnp.float32)
    _, S_prev = lax.scan(step, init, (decay.transpose(2, 0, 1, 3), dS.transpose(2, 0, 1, 3, 4)))
    o_inter = jnp.einsum('bhncd,nbhdv->bhncv', q_t, S_prev)
    o = (o_intra + o_inter).transpose(0, 2, 3, 1, 4).reshape(B, S, HG_HEADS, HG_V_DIM)
    o = rms_norm(o, norm_w) * jax.nn.silu(g_raw.astype(jnp.float32))
    return o.reshape(B, S, HG_HEADS * HG_V_DIM).astype(q_raw.dtype)


def hybrid_mixer(h, w_in, lam, lambda_init, da_subln, lb, hg_norm,
                 w_branch_a, w_branch_b, w_out, cos, sin):
    B, S, _ = h.shape
    proj = h @ w_in
    da_q, da_k, da_v, hg_q, hg_f, hg_i, hg_g, gate_a, gate_b = jnp.split(
        proj, np.cumsum(IN_SPLITS)[:-1].tolist(), axis=-1)
    ya = diff_attention(da_q.reshape(B, S, DA_HEADS, 2, DA_HEAD_DIM),
                        da_k.reshape(B, S, DA_HEADS, 2, DA_HEAD_DIM),
                        da_v.reshape(B, S, DA_HEADS, DA_V_DIM),
                        lam, lambda_init, da_subln, cos, sin)
    hs = lambda t: t.reshape(B, S, HG_HEADS, -1)
    yb = hgrn2(hs(hg_q), hs(hg_f), hs(hg_i), hs(hg_g), lb, hg_norm)
    merged = jax.nn.sigmoid(gate_a) * (ya @ w_branch_a) + jax.nn.sigmoid(gate_b) * (yb @ w_branch_b)
    return merged @ w_out


def moe(h, router_w, router_b, w_gate_up, b_gate_up, w_down, b_down):
    B, S, D = h.shape
    T = B * S
    xf = h.reshape(T, D)
    logits = (xf @ router_w + router_b).astype(jnp.float32)
    top_v, top_i = lax.top_k(logits, TOP_K)
    gates = jax.nn.softmax(top_v, axis=-1)
    n_assign = T * TOP_K
    flat_e = top_i.reshape(-1)
    flat_tok = jnp.arange(n_assign) // TOP_K
    flat_w = gates.reshape(-1)
    order = jnp.argsort(flat_e)
    sorted_e = flat_e[order]
    counts = jnp.zeros((N_EXPERTS,), jnp.int32).at[flat_e].add(1)
    starts = jnp.cumsum(counts) - counts
    padded = ((counts + MOE_BLOCK - 1) // MOE_BLOCK) * MOE_BLOCK
    pad_ends = jnp.cumsum(padded)
    pad_starts = pad_ends - padded
    dest = pad_starts[sorted_e] + (jnp.arange(n_assign) - starts[sorted_e])
    P = ((n_assign + MOE_BLOCK - 1) // MOE_BLOCK) * MOE_BLOCK + N_EXPERTS * MOE_BLOCK
    row_tok = jnp.zeros((P,), jnp.int32).at[dest].set(flat_tok[order])
    row_w = jnp.zeros((P,), jnp.float32).at[dest].set(flat_w[order])
    n_blocks = P // MOE_BLOCK
    block_e = jnp.clip(jnp.searchsorted(pad_ends, jnp.arange(n_blocks) * MOE_BLOCK, side='right'),
                       0, N_EXPERTS - 1)

    def expert_block(args):
        toks, e = args
        xb = xf[toks]
        gu = xb @ w_gate_up[e] + b_gate_up[e]
        g, u = jnp.split(gu, 2, axis=-1)
        g = jnp.minimum(g, SWIGLU_LIMIT)
        u = jnp.clip(u, -SWIGLU_LIMIT, SWIGLU_LIMIT)
        glu = g * jax.nn.sigmoid(g * SWIGLU_ALPHA)
        return ((u + 1.0) * glu) @ w_down[e] + b_down[e]

    Y = lax.map(expert_block, (row_tok.reshape(n_blocks, MOE_BLOCK), block_e)).reshape(P, D)
    out = jnp.zeros((T, D), Y.dtype).at[row_tok].add(Y * row_w[:, None].astype(Y.dtype))
    return out.reshape(B, S, D)


def setup_inputs(seed: int = 0) -> dict:
    key = jax.random.key(seed)
    ks = jax.random.split(key, 32)
    nrm = lambda k, shape, s: jax.random.normal(k, shape, jnp.float32) * s
    gain = lambda k, shape: 1.0 + 0.05 * jax.random.normal(k, shape, jnp.float32)
    L, D, E = DEPTH, D_MODEL, N_EXPERTS
    return {
        "x": nrm(ks[0], (BATCH, SEQ, D), 1.0),
        "c": nrm(ks[1], (BATCH, D), 1.0),
        "w_mod": nrm(ks[2], (L, D, 6 * D), 0.5 * D ** -0.5),
        "b_mod": nrm(ks[3], (L, 6 * D), 0.02),
        "norm_pre_mix": gain(ks[4], (L, D)),
        "norm_post_mix": gain(ks[5], (L, D)),
        "w_in": nrm(ks[6], (L, D, IN_COLS), D ** -0.5),
        "da_lambda_q1": nrm(ks[7], (L, DA_HEAD_DIM), 0.1),
        "da_lambda_k1": nrm(ks[8], (L, DA_HEAD_DIM), 0.1),
        "da_lambda_q2": nrm(ks[9], (L, DA_HEAD_DIM), 0.1),
        "da_lambda_k2": nrm(ks[10], (L, DA_HEAD_DIM), 0.1),
        "da_subln": gain(ks[11], (L, DA_V_DIM)),
        "hg_lb_logits": nrm(ks[12], (L + 1, HG_WIDTH), 0.1),
        "hg_norm": gain(ks[13], (L, HG_V_DIM)),
        "w_branch_a": nrm(ks[14], (L, DA_WIDTH, D), DA_WIDTH ** -0.5),
        "w_branch_b": nrm(ks[15], (L, HG_HEADS * HG_V_DIM, D), (HG_HEADS * HG_V_DIM) ** -0.5),
        "w_out": nrm(ks[16], (L, D, D), D ** -0.5),
        "norm_pre_ffn": gain(ks[17], (L, D)),
        "norm_post_ffn": gain(ks[18], (L, D)),
        "router_w": nrm(ks[19], (L, D, E), D ** -0.5),
        "router_b": nrm(ks[20], (L, E), 0.01),
        "w_gate_up": nrm(ks[21], (L, E, D, 2 * D_EXPERT), D ** -0.5),
        "b_gate_up": nrm(ks[22], (L, E, 2 * D_EXPERT), 0.02),
        "w_down": nrm(ks[23], (L, E, D_EXPERT, D), D_EXPERT ** -0.5),
        "b_down": nrm(ks[24], (L, E, D), 0.02),
    }


def reference(x, c, w_mod, b_mod, norm_pre_mix, norm_post_mix, w_in,
              da_lambda_q1, da_lambda_k1, da_lambda_q2, da_lambda_k2, da_subln,
              hg_lb_logits, hg_norm, w_branch_a, w_branch_b, w_out,
              norm_pre_ffn, norm_post_ffn, router_w, router_b,
              w_gate_up, b_gate_up, w_down, b_down):
    S = x.shape[1]
    cos, sin = rope_tables(S, x.dtype)
    lb_all = jnp.cumsum(jax.nn.softmax(hg_lb_logits.astype(jnp.float32), axis=0), axis=0)
    c_act = jax.nn.silu(c)
    for l in range(DEPTH):
        mod = (c_act @ w_mod[l] + b_mod[l])[:, None, :]
        sh1, sc1, g1, sh2, sc2, g2 = jnp.split(mod, 6, axis=-1)
        lambda_init = 0.8 - 0.6 * math.exp(-0.3 * l)
        lam = (jnp.exp(jnp.sum(da_lambda_q1[l] * da_lambda_k1[l]).astype(jnp.float32))
               - jnp.exp(jnp.sum(da_lambda_q2[l] * da_lambda_k2[l]).astype(jnp.float32))
               + lambda_init)
        h = rms_norm(x, norm_pre_mix[l]) * (1.0 + sc1) + sh1
        y = hybrid_mixer(h, w_in[l], lam, lambda_init, da_subln[l], lb_all[l], hg_norm[l],
                         w_branch_a[l], w_branch_b[l], w_out[l], cos, sin)
        x = x + g1 * rms_norm(y, norm_post_mix[l])
        h = rms_norm(x, norm_pre_ffn[l]) * (1.0 + sc2) + sh2
        y = moe(h, router_w[l], router_b[l], w_gate_up[l], b_gate_up[l], w_down[l], b_down[l])
        x = x + g2 * rms_norm(y, norm_post_ffn[l])
    return x
```

```python
import functools
import math

import jax
import jax.numpy as jnp
from jax import lax
from jax.experimental import pallas as pl
from jax.experimental.pallas import tpu as pltpu

F32 = jnp.float32
BF16 = jnp.bfloat16
I32 = jnp.int32

D_MODEL = 1024
CHUNK = 64
EPS = 1e-6
ROPE_THETA = 10000.0
DA_HEADS = 4
DA_HEAD_DIM = 64
HG_HEADS = 4
HEAD_W = 128
BRANCH_W = 512
N_EXPERTS = 32
TOP_K = 4
SWIGLU_LIMIT = 7.0
SWIGLU_ALPHA = 1.702
MOE_BLOCK = 256
LANES = 128
NEG_BIG = -1e30

VMEM_LIMIT = 56 * 1024 * 1024


def _cparams(sem):
    return pltpu.CompilerParams(dimension_semantics=sem, vmem_limit_bytes=VMEM_LIMIT)


def _rms(x, w):
    return x * lax.rsqrt(jnp.mean(x * x, axis=-1, keepdims=True) + EPS) * w


def _sigmoid(x):
    return 1.0 / (1.0 + jnp.exp(-x))


def _split3(x):
    a = x.astype(BF16)
    r = x - a.astype(F32)
    b = r.astype(BF16)
    c = (r - b.astype(F32)).astype(BF16)
    return a, b, c


def _mod_kernel(c_ref, w_ref, b_ref, o_ref):
    c = c_ref[...]
    ca = c * _sigmoid(c)
    o_ref[...] = jnp.dot(ca, w_ref[...], preferred_element_type=F32,
                         precision=lax.Precision.HIGHEST) + b_ref[...]


def _modulation(c, w_mod, b_mod):
    bsz, d = c.shape
    n = w_mod.shape[1]
    tn = 1536
    return pl.pallas_call(
        _mod_kernel,
        grid=(n // tn,),
        in_specs=[pl.BlockSpec((bsz, d), lambda j: (0, 0)),
                  pl.BlockSpec((d, tn), lambda j: (0, j)),
                  pl.BlockSpec((1, tn), lambda j: (0, j))],
        out_specs=pl.BlockSpec((bsz, tn), lambda j: (0, j)),
        out_shape=jax.ShapeDtypeStruct((bsz, n), F32),
        compiler_params=_cparams(("arbitrary",)),
        name="modulation",
    )(c, w_mod, b_mod.reshape(1, n))


def _rope(x, cos, sin_lo, sin_hi):
    return x * cos + pltpu.roll(x, 96, 1) * sin_lo + pltpu.roll(x, 32, 1) * sin_hi


def _inproj_kernel(x_ref, mod_ref, nw_ref, w_ref, cos_ref, slo_ref, shi_ref,
                   q_ref, k_ref, v_ref, hq_ref, hf_ref, hi_ref, hg_ref, ga_ref, gb_ref):
    x = x_ref[...]
    sh1 = mod_ref[0:1, :]
    sc1 = mod_ref[1:2, :]
    h = (_rms(x, nw_ref[...]) * (1.0 + sc1) + sh1).astype(BF16)
    cos, slo, shi = cos_ref[...], slo_ref[...], shi_ref[...]
    scale = DA_HEAD_DIM ** -0.5

    def seg(i, width=BRANCH_W):
        return jnp.dot(h, w_ref[:, i:i + width], preferred_element_type=F32)

    q = seg(0)
    k = seg(BRANCH_W)
    for hd in range(DA_HEADS):
        sl = slice(hd * HEAD_W, (hd + 1) * HEAD_W)
        q_ref[:, sl] = (_rope(q[:, sl], cos, slo, shi) * scale).astype(BF16)
        k_ref[:, sl] = _rope(k[:, sl], cos, slo, shi).astype(BF16)
    v_ref[...] = seg(2 * BRANCH_W).astype(BF16)
    hq = seg(3 * BRANCH_W)
    hq_ref[...] = (hq * _sigmoid(hq)).astype(BF16)
    hf_ref[...] = seg(4 * BRANCH_W)
    hi_ref[...] = seg(5 * BRANCH_W).astype(BF16)
    hg = seg(6 * BRANCH_W)
    hg_ref[...] = (hg * _sigmoid(hg)).astype(BF16)
    for j in range(2):
        ga_ref[:, j * BRANCH_W:(j + 1) * BRANCH_W] = _sigmoid(seg((7 + j) * BRANCH_W)).astype(BF16)
        gb_ref[:, j * BRANCH_W:(j + 1) * BRANCH_W] = _sigmoid(seg((9 + j) * BRANCH_W)).astype(BF16)


def _in_projection(xf, mod, norm_w, w_in_bf, cos, slo, shi, seq, tm):
    t, d = xf.shape
    tiles_per_batch = seq // tm
    bspec = lambda w: pl.BlockSpec((tm, w), lambda i: (i, 0))
    tab = pl.BlockSpec((tm, LANES), lambda i: (i % tiles_per_batch, 0))
    widths = [BRANCH_W] * 7 + [D_MODEL, D_MODEL]
    dtypes = [BF16, BF16, BF16, BF16, F32, BF16, BF16, BF16, BF16]
    return pl.pallas_call(
        _inproj_kernel,
        grid=(t // tm,),
        in_specs=[bspec(d),
                  pl.BlockSpec((None, 6, d), lambda i: (i // tiles_per_batch, 0, 0)),
                  pl.BlockSpec((1, d), lambda i: (0, 0)),
                  pl.BlockSpec(w_in_bf.shape, lambda i: (0, 0), pipeline_mode=pl.Buffered(1)),
                  tab, tab, tab],
        out_specs=[bspec(w) for w in widths],
        out_shape=[jax.ShapeDtypeStruct((t, w), dt) for w, dt in zip(widths, dtypes)],
        compiler_params=_cparams(("parallel",)),
        name="in_projection",
    )(xf, mod, norm_w, w_in_bf, cos, slo, shi)


def _attn_kernel(lq1_ref, lk1_ref, lq2_ref, lk2_ref, subln_ref, q_ref, k_ref, v_ref, o_ref,
                 *, tq, lambda_init):
    qi = pl.program_id(2)
    q = q_ref[...]
    lane = lax.broadcasted_iota(I32, q.shape, 1)
    zero = jnp.zeros_like(q)
    qq = jnp.concatenate([jnp.where(lane < DA_HEAD_DIM, q, zero),
                          jnp.where(lane >= DA_HEAD_DIM, q, zero)], axis=0)

    def step(kb, carry, masked):
        m, l, acc = carry
        start = pl.multiple_of(kb * tq, tq)
        kblk = k_ref[pl.ds(start, tq), :]
        vblk = v_ref[pl.ds(start, tq), :]
        s = lax.dot_general(qq, kblk, (((1,), (1,)), ((), ())), preferred_element_type=F32)
        if masked:
            row = lax.broadcasted_iota(I32, s.shape, 0) % tq
            col = lax.broadcasted_iota(I32, s.shape, 1)
            s = jnp.where(col // CHUNK <= row // CHUNK, s, NEG_BIG)
        m_new = jnp.maximum(m, jnp.max(s, axis=1, keepdims=True))
        alpha = jnp.exp(m - m_new)
        p = jnp.exp(s - m_new)
        l = alpha * l + jnp.sum(p, axis=1, keepdims=True)
        acc = alpha * acc + jnp.dot(p.astype(BF16), vblk, preferred_element_type=F32)
        return m_new, l, acc

    init = (jnp.full((2 * tq, 1), NEG_BIG, F32), jnp.zeros((2 * tq, 1), F32),
            jnp.zeros((2 * tq, HEAD_W), F32))
    carry = lax.fori_loop(0, qi, functools.partial(step, masked=False), init)
    _, l, acc = step(qi, carry, masked=True)
    o = acc / l
    lam = (jnp.exp(jnp.sum(lq1_ref[...] * lk1_ref[...], axis=1, keepdims=True))
           - jnp.exp(jnp.sum(lq2_ref[...] * lk2_ref[...], axis=1, keepdims=True))
           + lambda_init)
    o = o[:tq] - lam * o[tq:]
    o_ref[...] = (_rms(o, subln_ref[...]) * (1.0 - lambda_init)).astype(BF16)


def _diff_attention(q, k, v, lq1, lk1, lq2, lk2, subln, bsz, seq, lambda_init, tq):
    q3, k3, v3 = (a.reshape(bsz, seq, BRANCH_W) for a in (q, k, v))
    small = lambda n: pl.BlockSpec((1, n), lambda b, h, i: (0, 0))
    kv = pl.BlockSpec((None, seq, HEAD_W), lambda b, h, i: (b, 0, h))
    qo = pl.BlockSpec((None, tq, HEAD_W), lambda b, h, i: (b, i, h))
    out = pl.pallas_call(
        functools.partial(_attn_kernel, tq=tq, lambda_init=lambda_init),
        grid=(bsz, DA_HEADS, seq // tq),
        in_specs=[small(DA_HEAD_DIM)] * 4 + [small(HEAD_W), qo, kv, kv],
        out_specs=qo,
        out_shape=jax.ShapeDtypeStruct((bsz, seq, BRANCH_W), BF16),
        compiler_params=_cparams(("parallel", "parallel", "arbitrary")),
        name="diff_attention",
    )(lq1, lk1, lq2, lk2, subln, q3, k3, v3)
    return out.reshape(bsz * seq, BRANCH_W)


def _hgrn_kernel(lbl_ref, nw_ref, q_ref, f_ref, i_ref, g_ref, o_ref, st_ref, *, tc):
    @pl.when(pl.program_id(2) == 0)
    def _():
        st_ref[...] = jnp.zeros_like(st_ref)

    nchunk = tc // CHUNK
    lg = lbl_ref[...]
    mx = jnp.max(lg, axis=0, keepdims=True)
    ex = jnp.exp(lg - mx)
    lb = ex[0:1] / jnp.sum(ex, axis=0, keepdims=True)
    q = q_ref[...].astype(F32)
    f = lb + (1.0 - lb) * _sigmoid(f_ref[...])
    k = 1.0 - f
    logf = jnp.log(f)
    v = i_ref[...]

    row = lax.broadcasted_iota(I32, (tc, tc), 0)
    col = lax.broadcasted_iota(I32, (tc, tc), 1)
    same = (row // CHUNK) == (col // CHUNK)
    causal = same & (col <= row)
    tril = jnp.where(causal, 1.0, 0.0).astype(BF16)
    blk = jnp.where(same, 1.0, 0.0).astype(BF16)
    parts = _split3(logf)
    gcum = sum(jnp.dot(tril, p, preferred_element_type=F32) for p in parts)
    gtot = sum(jnp.dot(blk, p, preferred_element_type=F32) for p in parts)

    q_t = (q * jnp.exp(gcum)).astype(BF16)
    k_t = (k * jnp.exp(-gcum)).astype(BF16)
    k_dec = (k * jnp.exp(gtot - gcum)).astype(BF16)
    a = lax.dot_general(q_t, k_t, (((1,), (1,)), ((), ())), preferred_element_type=F32)
    a = jnp.where(causal, a, 0.0).astype(BF16)
    o_intra = jnp.dot(a, v, preferred_element_type=F32)

    st = st_ref[...]
    inter = []
    for c in range(nchunk):
        sl = slice(c * CHUNK, (c + 1) * CHUNK)
        inter.append(lax.dot_general(q_t[sl], st.astype(BF16), (((1,), (1,)), ((), ())),
                                     preferred_element_type=F32))
        ds = lax.dot_general(v[sl], k_dec[sl], (((0,), (0,)), ((), ())),
                             preferred_element_type=F32)
        decay = jnp.exp(gtot[c * CHUNK:c * CHUNK + 1, :])
        st = decay * st + ds
    st_ref[...] = st
    o = o_intra + jnp.concatenate(inter, axis=0)
    o_ref[...] = (_rms(o, nw_ref[...]) * g_ref[...].astype(F32)).astype(BF16)


def _hgrn2(hq, hf, hi, hg, lb_logits, norm_w, bsz, seq, tc):
    r3 = lambda a: a.reshape(bsz, seq, BRANCH_W)
    blk = pl.BlockSpec((None, tc, HEAD_W), lambda b, h, i: (b, i, h))
    out = pl.pallas_call(
        functools.partial(_hgrn_kernel, tc=tc),
        grid=(bsz, HG_HEADS, seq // tc),
        in_specs=[pl.BlockSpec((2, HEAD_W), lambda b, h, i: (0, h)),
                  pl.BlockSpec((1, HEAD_W), lambda b, h, i: (0, 0)),
                  blk, blk, blk, blk],
        out_specs=blk,
        out_shape=jax.ShapeDtypeStruct((bsz, seq, BRANCH_W), BF16),
        scratch_shapes=[pltpu.VMEM((HEAD_W, HEAD_W), F32)],
        compiler_params=_cparams(("parallel", "parallel", "arbitrary")),
        name="hgrn2",
    )(lb_logits, norm_w, r3(hq), r3(hf), r3(hi), r3(hg))
    return out.reshape(bsz * seq, BRANCH_W)


def _merge_kernel(x_ref, mod_ref, ya_ref, yb_ref, ga_ref, gb_ref, wa_ref, wb_ref, wo_ref,
                  npost_ref, npre_ref, rw_ref, rb_ref,
                  x1_ref, h2_ref, topi_ref, gate_ref, rank_ref, cnt_ref, *, tm):
    i = pl.program_id(0)

    @pl.when(i == 0)
    def _():
        cnt_ref[...] = jnp.zeros_like(cnt_ref)

    g1 = mod_ref[2:3, :]
    sh2 = mod_ref[3:4, :]
    sc2 = mod_ref[4:5, :]
    a = jnp.dot(ya_ref[...], wa_ref[...], preferred_element_type=F32)
    b = jnp.dot(yb_ref[...], wb_ref[...], preferred_element_type=F32)
    merged = ga_ref[...].astype(F32) * a + gb_ref[...].astype(F32) * b
    y = jnp.dot(merged.astype(BF16), wo_ref[...], preferred_element_type=F32)
    x1 = x_ref[...] + g1 * _rms(y, npost_ref[...])
    x1_ref[...] = x1
    h2 = _rms(x1, npre_ref[...]) * (1.0 + sc2) + sh2
    h2_ref[...] = h2

    logits = jnp.dot(h2.astype(BF16), rw_ref[...], preferred_element_type=F32) + rb_ref[...]
    lane = lax.broadcasted_iota(I32, logits.shape, 1)
    work = logits
    vals, idxs = [], []
    for _ in range(TOP_K):
        mx = jnp.max(work, axis=1, keepdims=True)
        idx = jnp.min(jnp.where(work == mx, lane, LANES), axis=1, keepdims=True)
        vals.append(mx)
        idxs.append(idx)
        work = jnp.where(lane == idx, -jnp.inf, work)
    es = [jnp.exp(vk - vals[0]) for vk in vals]
    den = es[0] + es[1] + es[2] + es[3]

    onehot = jnp.zeros(logits.shape, F32)
    for idx in idxs:
        onehot = jnp.where(lane == idx, 1.0, onehot)
    row = lax.broadcasted_iota(I32, (tm, tm), 0)
    col = lax.broadcasted_iota(I32, (tm, tm), 1)
    strict = jnp.where(col < row, 1.0, 0.0).astype(BF16)
    prefix = jnp.dot(strict, onehot.astype(BF16), preferred_element_type=F32) + cnt_ref[...]
    cnt_ref[...] = cnt_ref[...] + jnp.sum(onehot, axis=0, keepdims=True)

    topi = jnp.zeros(logits.shape, I32)
    gate = jnp.zeros(logits.shape, F32)
    rank = jnp.zeros(logits.shape, I32)
    for kk in range(TOP_K):
        rk = jnp.sum(jnp.where(lane == idxs[kk], prefix, 0.0), axis=1, keepdims=True)
        topi = jnp.where(lane == kk, idxs[kk], topi)
        gate = jnp.where(lane == kk, es[kk] / den, gate)
        rank = jnp.where(lane == kk, rk.astype(I32), rank)
    topi_ref[...] = topi
    gate_ref[...] = gate
    rank_ref[...] = rank


def _merge_route(xf, mod, ya, yb, ga, gb, wa, wb, wo, npost, npre, rw, rb, seq, tm):
    t, d = xf.shape
    tiles_per_batch = seq // tm
    row = lambda w: pl.BlockSpec((tm, w), lambda i: (i, 0))
    const = lambda a: pl.BlockSpec(a.shape, lambda i: (0,) * a.ndim)
    lane_out = jax.ShapeDtypeStruct((t, LANES), I32)
    return pl.pallas_call(
        functools.partial(_merge_kernel, tm=tm),
        grid=(t // tm,),
        in_specs=[row(d),
                  pl.BlockSpec((None, 6, d), lambda i: (i // tiles_per_batch, 0, 0)),
                  row(BRANCH_W), row(BRANCH_W), row(d), row(d),
                  const(wa), const(wb), const(wo), const(npost), const(npre),
                  const(rw), const(rb)],
        out_specs=[row(d), row(d), row(LANES), row(LANES), row(LANES),
                   pl.BlockSpec((1, LANES), lambda i: (0, 0))],
        out_shape=[jax.ShapeDtypeStruct((t, d), F32), jax.ShapeDtypeStruct((t, d), F32),
                   lane_out, jax.ShapeDtypeStruct((t, LANES), F32), lane_out,
                   jax.ShapeDtypeStruct((1, LANES), F32)],
        compiler_params=_cparams(("arbitrary",)),
        name="merge_route",
    )(xf, mod, ya, yb, ga, gb, wa, wb, wo, npost, npre, rw, rb)


def _dispatch_kernel(dest_hbm, h_ref, xs_in, xs_out, dsm, isem, sem, *, tm):
    del xs_in
    i = pl.program_id(0)
    n = tm * TOP_K
    cp = pltpu.make_async_copy(dest_hbm.at[pl.ds(pl.multiple_of(i * n, n), n)], dsm, isem)
    cp.start()
    cp.wait()

    def row_copy(r, kk):
        return pltpu.make_async_copy(h_ref.at[pl.ds(r, 1)],
                                     xs_out.at[pl.ds(dsm[r * TOP_K + kk], 1)], sem)

    def issue(r, c):
        for kk in range(TOP_K):
            row_copy(r, kk).start()
        return c

    def drain(r, c):
        for kk in range(TOP_K):
            row_copy(r, kk).wait()
        return c

    lax.fori_loop(0, tm, issue, 0)
    lax.fori_loop(0, tm, drain, 0)


def _dispatch(h2, dest, n_rows, tm):
    t, d = h2.shape
    zeros = jnp.zeros((n_rows, d), h2.dtype)
    return pl.pallas_call(
        functools.partial(_dispatch_kernel, tm=tm),
        grid=(t // tm,),
        in_specs=[pl.BlockSpec(memory_space=pl.ANY),
                  pl.BlockSpec((tm, d), lambda i: (i, 0)),
                  pl.BlockSpec(memory_space=pl.ANY)],
        out_specs=pl.BlockSpec(memory_space=pl.ANY),
        out_shape=jax.ShapeDtypeStruct((n_rows, d), h2.dtype),
        scratch_shapes=[pltpu.SMEM((tm * TOP_K,), I32),
                        pltpu.SemaphoreType.DMA, pltpu.SemaphoreType.DMA],
        input_output_aliases={2: 0},
        compiler_params=_cparams(("arbitrary",)),
        name="moe_dispatch",
    )(dest, h2, zeros)


def _expert_kernel(be_ref, nu_ref, x_ref, wgu_ref, bgu_ref, wd_ref, bd_ref, y_ref):
    j = pl.program_id(0)

    @pl.when(j < nu_ref[0])
    def _():
        x = x_ref[...].astype(BF16)
        gu = jnp.dot(x, wgu_ref[...], preferred_element_type=F32) + bgu_ref[...]
        g = jnp.minimum(gu[:, :D_MODEL], SWIGLU_LIMIT)
        u = jnp.clip(gu[:, D_MODEL:], -SWIGLU_LIMIT, SWIGLU_LIMIT)
        act = (u + 1.0) * (g * _sigmoid(g * SWIGLU_ALPHA))
        y_ref[...] = jnp.dot(act.astype(BF16), wd_ref[...], preferred_element_type=F32) + bd_ref[...]

    @pl.when(j >= nu_ref[0])
    def _():
        y_ref[...] = jnp.zeros_like(y_ref)


def _experts(xs, block_e, n_used, wgu, bgu, wd, bd):
    p, d = xs.shape
    n_blocks = p // MOE_BLOCK
    e = wgu.shape[0]
    grid_spec = pltpu.PrefetchScalarGridSpec(
        num_scalar_prefetch=2,
        grid=(n_blocks,),
        in_specs=[pl.BlockSpec((MOE_BLOCK, d), lambda j, be, nu: (j, 0)),
                  pl.BlockSpec((None, d, 2 * d), lambda j, be, nu: (be[j], 0, 0)),
                  pl.BlockSpec((None, 1, 2 * d), lambda j, be, nu: (be[j], 0, 0)),
                  pl.BlockSpec((None, d, d), lambda j, be, nu: (be[j], 0, 0)),
                  pl.BlockSpec((None, 1, d), lambda j, be, nu: (be[j], 0, 0))],
        out_specs=pl.BlockSpec((MOE_BLOCK, d), lambda j, be, nu: (j, 0)),
    )
    return pl.pallas_call(
        _expert_kernel,
        grid_spec=grid_spec,
        out_shape=jax.ShapeDtypeStruct((p, d), F32),
        compiler_params=_cparams(("arbitrary",)),
        name="moe_experts",
    )(block_e, n_used, xs, wgu, bgu.reshape(e, 1, 2 * d), wd, bd.reshape(e, 1, d))


def _combine_kernel(dest_hbm, y_hbm, gate_ref, x1_ref, mod_ref, nw_ref, o_ref,
                    dsm, buf, isem, sem, *, tm):
    i = pl.program_id(0)
    n = tm * TOP_K
    cp = pltpu.make_async_copy(dest_hbm.at[pl.ds(pl.multiple_of(i * n, n), n)], dsm, isem)
    cp.start()
    cp.wait()

    def row_copy(r, kk):
        return pltpu.make_async_copy(y_hbm.at[pl.ds(dsm[r * TOP_K + kk], 1)],
                                     buf.at[kk, pl.ds(r, 1)], sem)

    def issue(r, c):
        for kk in range(TOP_K):
            row_copy(r, kk).start()
        return c

    def drain(r, c):
        for kk in range(TOP_K):
            row_copy(r, kk).wait()
        return c

    lax.fori_loop(0, tm, issue, 0)
    lax.fori_loop(0, tm, drain, 0)

    gate = gate_ref[...]
    moe = gate[:, 0:1] * buf[0]
    for kk in range(1, TOP_K):
        moe = moe + gate[:, kk:kk + 1] * buf[kk]
    g2 = mod_ref[5:6, :]
    o_ref[...] = x1_ref[...] + g2 * _rms(moe, nw_ref[...])


def _combine(dest, y, gate, x1, mod, nw, seq, tm):
    t, d = x1.shape
    tiles_per_batch = seq // tm
    return pl.pallas_call(
        functools.partial(_combine_kernel, tm=tm),
        grid=(t // tm,),
        in_specs=[pl.BlockSpec(memory_space=pl.ANY),
                  pl.BlockSpec(memory_space=pl.ANY),
                  pl.BlockSpec((tm, LANES), lambda i: (i, 0)),
                  pl.BlockSpec((tm, d), lambda i: (i, 0)),
                  pl.BlockSpec((None, 6, d), lambda i: (i // tiles_per_batch, 0, 0)),
                  pl.BlockSpec((1, d), lambda i: (0, 0))],
        out_specs=pl.BlockSpec((tm, d), lambda i: (i, 0)),
        out_shape=jax.ShapeDtypeStruct((t, d), F32),
        scratch_shapes=[pltpu.SMEM((tm * TOP_K,), I32),
                        pltpu.VMEM((TOP_K, tm, d), F32),
                        pltpu.SemaphoreType.DMA, pltpu.SemaphoreType.DMA],
        compiler_params=_cparams(("arbitrary",)),
        name="moe_combine",
    )(dest, y, gate, x1, mod, nw)


def _rope_tables(seq):
    pos = jnp.arange(seq, dtype=F32)
    inv = ROPE_THETA ** (-jnp.arange(0, DA_HEAD_DIM, 2, dtype=F32) / DA_HEAD_DIM)
    ang = pos[:, None] * inv[None, :]
    cos, sin = jnp.cos(ang), jnp.sin(ang)
    zero = jnp.zeros_like(sin)
    cos_t = jnp.tile(cos, (1, LANES // cos.shape[1]))
    sin_lo = jnp.tile(jnp.concatenate([-sin, zero], axis=1), (1, 2))
    sin_hi = jnp.tile(jnp.concatenate([zero, sin], axis=1), (1, 2))
    return cos_t, sin_lo, sin_hi


def kernel(x, c, w_mod, b_mod, norm_pre_mix, norm_post_mix, w_in, da_lambda_q1, da_lambda_k1,
           da_lambda_q2, da_lambda_k2, da_subln, hg_lb_logits, hg_norm, w_branch_a, w_branch_b,
           w_out, norm_pre_ffn, norm_post_ffn, router_w, router_b, w_gate_up, b_gate_up,
           w_down, b_down):
    bsz, seq, d = x.shape
    depth = w_mod.shape[0]
    assert depth == 1 and d == D_MODEL
    t = bsz * seq
    lyr = 0
    lambda_init = 0.8 - 0.6 * math.exp(-0.3 * lyr)
    cos, slo, shi = _rope_tables(seq)

    xf = x.reshape(t, d)
    mod = _modulation(c, w_mod[lyr], b_mod[lyr]).reshape(bsz, 6, d)

    q, k, v, hq, hf, hi, hg, ga, gb = _in_projection(
        xf, mod, norm_pre_mix[lyr][None], w_in[lyr].astype(BF16), cos, slo, shi, seq, tm=512)
    ya = _diff_attention(q, k, v, da_lambda_q1[lyr][None], da_lambda_k1[lyr][None],
                         da_lambda_q2[lyr][None], da_lambda_k2[lyr][None], da_subln[lyr][None],
                         bsz, seq, lambda_init, tq=256)
    yb = _hgrn2(hq, hf, hi, hg, hg_lb_logits[lyr:lyr + 2], hg_norm[lyr][None], bsz, seq, tc=512)

    rw = jnp.zeros((d, LANES), BF16).at[:, :N_EXPERTS].set(router_w[lyr].astype(BF16))
    rb = jnp.full((1, LANES), NEG_BIG, F32).at[0, :N_EXPERTS].set(router_b[lyr])
    x1, h2, topi, gate, rank, counts = _merge_route(
        xf, mod, ya, yb, ga, gb, w_branch_a[lyr].astype(BF16), w_branch_b[lyr].astype(BF16),
        w_out[lyr].astype(BF16), norm_post_mix[lyr][None], norm_pre_ffn[lyr][None], rw, rb,
        seq, tm=512)

    counts = counts[0, :N_EXPERTS].astype(I32)
    padded = ((counts + MOE_BLOCK - 1) // MOE_BLOCK) * MOE_BLOCK
    pad_ends = jnp.cumsum(padded)
    pad_starts = pad_ends - padded
    n_assign = t * TOP_K
    n_rows = ((n_assign + MOE_BLOCK - 1) // MOE_BLOCK) * MOE_BLOCK + N_EXPERTS * MOE_BLOCK
    n_blocks = n_rows // MOE_BLOCK
    dest = (pad_starts[topi[:, :TOP_K]] + rank[:, :TOP_K]).reshape(n_assign)
    block_e = jnp.clip(jnp.searchsorted(pad_ends, jnp.arange(n_blocks, dtype=I32) * MOE_BLOCK,
                                        side='right'), 0, N_EXPERTS - 1).astype(I32)
    n_used = (pad_ends[-1:] // MOE_BLOCK).astype(I32)

    xs = _dispatch(h2, dest, n_rows, tm=512)
    y = _experts(xs, block_e, n_used, w_gate_up[lyr].astype(BF16), b_gate_up[lyr],
                 w_down[lyr].astype(BF16), b_down[lyr])
    out = _combine(dest, y, gate, x1, mod, norm_post_ffn[lyr][None], seq, tm=256)
    return out.reshape(bsz, seq, d)
```

```python
import functools
import math

import jax
import jax.numpy as jnp
from jax import lax
from jax.experimental import pallas as pl
from jax.experimental.pallas import tpu as pltpu

F32 = jnp.float32
BF16 = jnp.bfloat16
I32 = jnp.int32

D_MODEL = 1024
CHUNK = 64
EPS = 1e-6
ROPE_THETA = 10000.0
DA_HEADS = 4
DA_HEAD_DIM = 64
HG_HEADS = 4
HEAD_W = 128
BRANCH_W = 512
V_ROWS = HEAD_W + 16
ATTN_STRIP = 256
N_EXPERTS = 32
TOP_K = 4
SWIGLU_LIMIT = 7.0
SWIGLU_ALPHA = 1.702
MOE_BLOCK = 256
LANES = 128
NEG_BIG = -1e30

VMEM_LIMIT = 56 * 1024 * 1024


def _cparams(sem):
    return pltpu.CompilerParams(dimension_semantics=sem, vmem_limit_bytes=VMEM_LIMIT)


def _rms(x, w):
    return x * lax.rsqrt(jnp.mean(x * x, axis=-1, keepdims=True) + EPS) * w


def _sigmoid(x):
    return 1.0 / (1.0 + jnp.exp(-x))


def _split3(x):
    a = x.astype(BF16)
    r = x - a.astype(F32)
    b = r.astype(BF16)
    c = (r - b.astype(F32)).astype(BF16)
    return a, b, c


def _mod_kernel(c_ref, w_ref, b_ref, o_ref):
    c = c_ref[...]
    ca = c * _sigmoid(c)
    o_ref[...] = jnp.dot(ca, w_ref[...], preferred_element_type=F32,
                         precision=lax.Precision.HIGHEST) + b_ref[...]


def _modulation(c, w_mod, b_mod):
    bsz, d = c.shape
    n = w_mod.shape[1]
    tn = 1536
    return pl.pallas_call(
        _mod_kernel,
        grid=(n // tn,),
        in_specs=[pl.BlockSpec((bsz, d), lambda j: (0, 0)),
                  pl.BlockSpec((d, tn), lambda j: (0, j)),
                  pl.BlockSpec((1, tn), lambda j: (0, j))],
        out_specs=pl.BlockSpec((bsz, tn), lambda j: (0, j)),
        out_shape=jax.ShapeDtypeStruct((bsz, n), F32),
        compiler_params=_cparams(("arbitrary",)),
        name="modulation",
    )(c, w_mod, b_mod.reshape(1, n))


def _rope(x, cos, sin_lo, sin_hi):
    return x * cos + pltpu.roll(x, 96, 1) * sin_lo + pltpu.roll(x, 32, 1) * sin_hi


def _inproj_kernel(x_ref, mod_ref, nw_ref, w_ref, cos_ref, slo_ref, shi_ref,
                   qT_ref, k_ref, vT_ref, hq_ref, hf_ref, hi_ref, hg_ref, ga_ref, gb_ref):
    x = x_ref[...]
    sh1 = mod_ref[0:1, :]
    sc1 = mod_ref[1:2, :]
    h = (_rms(x, nw_ref[...]) * (1.0 + sc1) + sh1).astype(BF16)
    cos, slo, shi = cos_ref[...], slo_ref[...], shi_ref[...]
    scale = DA_HEAD_DIM ** -0.5 * math.log2(math.e)

    def seg(i, width=BRANCH_W):
        return jnp.dot(h, w_ref[:, i:i + width], preferred_element_type=F32)

    q = seg(0)
    k = seg(BRANCH_W)
    for hd in range(DA_HEADS):
        sl = slice(hd * HEAD_W, (hd + 1) * HEAD_W)
        qT_ref[sl, :] = (_rope(q[:, sl], cos, slo, shi) * scale).T.astype(BF16)
        k_ref[:, sl] = _rope(k[:, sl], cos, slo, shi).astype(BF16)
    vT = seg(2 * BRANCH_W).T.astype(BF16)
    ones = jnp.ones((V_ROWS - HEAD_W, vT.shape[1]), BF16)
    for hd in range(DA_HEADS):
        vT_ref[hd * V_ROWS:hd * V_ROWS + HEAD_W, :] = vT[hd * HEAD_W:(hd + 1) * HEAD_W]
        vT_ref[hd * V_ROWS + HEAD_W:(hd + 1) * V_ROWS, :] = ones
    hq = seg(3 * BRANCH_W)
    hq_ref[...] = (hq * _sigmoid(hq)).astype(BF16)
    hf_ref[...] = seg(4 * BRANCH_W)
    hi_ref[...] = seg(5 * BRANCH_W).astype(BF16)
    hg = seg(6 * BRANCH_W)
    hg_ref[...] = (hg * _sigmoid(hg)).astype(BF16)
    for j in range(2):
        ga_ref[:, j * BRANCH_W:(j + 1) * BRANCH_W] = _sigmoid(seg((7 + j) * BRANCH_W)).astype(BF16)
        gb_ref[:, j * BRANCH_W:(j + 1) * BRANCH_W] = _sigmoid(seg((9 + j) * BRANCH_W)).astype(BF16)


def _in_projection(xf, mod, norm_w, w_in_bf, cos, slo, shi, seq, tm):
    t, d = xf.shape
    tiles_per_batch = seq // tm
    bspec = lambda w: pl.BlockSpec((tm, w), lambda i: (i, 0))
    tab = pl.BlockSpec((tm, LANES), lambda i: (i % tiles_per_batch, 0))
    widths = [BRANCH_W] * 7 + [D_MODEL, D_MODEL]
    dtypes = [BF16, BF16, BF16, BF16, F32, BF16, BF16, BF16, BF16]
    out_specs = [bspec(w) for w in widths]
    out_shape = [jax.ShapeDtypeStruct((t, w), dt) for w, dt in zip(widths, dtypes)]
    for idx, rows in ((0, BRANCH_W), (2, DA_HEADS * V_ROWS)):
        out_specs[idx] = pl.BlockSpec((None, rows, tm),
                                      lambda i: (i // tiles_per_batch, 0, i % tiles_per_batch))
        out_shape[idx] = jax.ShapeDtypeStruct((t // seq, rows, seq), BF16)
    return pl.pallas_call(
        _inproj_kernel,
        grid=(t // tm,),
        in_specs=[bspec(d),
                  pl.BlockSpec((None, 6, d), lambda i: (i // tiles_per_batch, 0, 0)),
                  pl.BlockSpec((1, d), lambda i: (0, 0)),
                  pl.BlockSpec(w_in_bf.shape, lambda i: (0, 0), pipeline_mode=pl.Buffered(1)),
                  tab, tab, tab],
        out_specs=out_specs,
        out_shape=out_shape,
        compiler_params=_cparams(("parallel",)),
        name="in_projection",
    )(xf, mod, norm_w, w_in_bf, cos, slo, shi)


def _attn_kernel(lq1_ref, lk1_ref, lq2_ref, lk2_ref, subln_ref, qT_ref, k_ref, vT_ref, o_ref,
                 qq_ref, s0_ref, s1_ref, p0_ref, p1_ref, a0_ref, a1_ref, m_ref, acc_ref,
                 *, tq, lambda_init):
    qi = pl.program_id(2)
    s_refs, p_refs, a_refs = (s0_ref, s1_ref), (p0_ref, p1_ref), (a0_ref, a1_ref)
    qT = qT_ref[...]
    feat = lax.broadcasted_iota(I32, qT.shape, 0)
    zero = jnp.zeros_like(qT)
    qq_ref[:, :tq] = jnp.where(feat < DA_HEAD_DIM, qT, zero)
    qq_ref[:, tq:] = jnp.where(feat >= DA_HEAD_DIM, qT, zero)

    strips = [slice(c * ATTN_STRIP, (c + 1) * ATTN_STRIP) for c in range(2 * tq // ATTN_STRIP)]

    def scores(kb, dst, cs):
        start = pl.multiple_of(kb * tq, tq)
        dst[:, cs] = jnp.dot(k_ref[pl.ds(start, tq), :], qq_ref[:, cs], preferred_element_type=F32)

    def accumulate(kb, slot, cs):
        start = pl.multiple_of(kb * tq, tq)
        pv = jnp.dot(vT_ref[:, pl.ds(start, tq)], p_refs[slot][:, cs], preferred_element_type=F32)
        acc_ref[:, cs] = a_refs[slot][:, cs] * acc_ref[:, cs] + pv

    def softmax(slot, cs, masked):
        s = s_refs[slot][:, cs]
        if masked:
            key = lax.broadcasted_iota(I32, s.shape, 0)
            qry = (lax.broadcasted_iota(I32, s.shape, 1) + cs.start) % tq
            s = jnp.where(key // CHUNK <= qry // CHUNK, s, NEG_BIG)
        m_old = m_ref[:, cs]
        m_new = jnp.maximum(m_old, jnp.max(s, axis=0, keepdims=True))
        m_ref[:, cs] = m_new
        a_refs[slot][:, cs] = jnp.exp2(m_old - m_new)
        p_refs[slot][:, cs] = jnp.exp2((s - m_new).astype(BF16))

    def stage(kb, slot):
        for cs in strips:
            scores(kb + 1, s_refs[1 - slot], cs)
            softmax(slot, cs, masked=False)
            accumulate(jnp.maximum(kb - 1, 0), 1 - slot, cs)

    def last_stage(slot):
        for cs in strips:
            accumulate(jnp.maximum(qi - 1, 0), 1 - slot, cs)
            softmax(slot, cs, masked=True)
            accumulate(qi, slot, cs)

    p1_ref[...] = jnp.zeros_like(p1_ref)
    a1_ref[...] = jnp.ones_like(a1_ref)
    m_ref[...] = jnp.full(m_ref.shape, NEG_BIG, F32)
    acc_ref[...] = jnp.zeros_like(acc_ref)
    for cs in strips:
        scores(0, s0_ref, cs)

    def pair(i, c):
        stage(2 * i, 0)
        stage(2 * i + 1, 1)
        return c

    lax.fori_loop(0, qi // 2, pair, 0)

    @pl.when(qi % 2 == 1)
    def _():
        stage(qi - 1, 0)
        last_stage(1)

    @pl.when(qi % 2 == 0)
    def _():
        last_stage(0)

    acc = acc_ref[...]
    o = acc[:HEAD_W] / acc[HEAD_W:HEAD_W + 1]
    lam = (jnp.exp(jnp.sum(lq1_ref[...] * lk1_ref[...], axis=1, keepdims=True))
           - jnp.exp(jnp.sum(lq2_ref[...] * lk2_ref[...], axis=1, keepdims=True))
           + lambda_init)
    o = o[:, :tq] - lam * o[:, tq:]
    o = o * lax.rsqrt(jnp.mean(o * o, axis=0, keepdims=True) + EPS) * subln_ref[...]
    o_ref[...] = (o * (1.0 - lambda_init)).T.astype(BF16)


def _diff_attention(qT, k, vT, lq1, lk1, lq2, lk2, subln, bsz, seq, lambda_init, tq):
    k3 = k.reshape(bsz, seq, BRANCH_W)
    small = lambda n: pl.BlockSpec((1, n), lambda b, h, i: (0, 0))
    out = pl.pallas_call(
        functools.partial(_attn_kernel, tq=tq, lambda_init=lambda_init),
        grid=(bsz, DA_HEADS, seq // tq),
        in_specs=[small(DA_HEAD_DIM)] * 4 + [
            pl.BlockSpec((HEAD_W, 1), lambda b, h, i: (0, 0)),
            pl.BlockSpec((None, HEAD_W, tq), lambda b, h, i: (b, h, i)),
            pl.BlockSpec((None, seq, HEAD_W), lambda b, h, i: (b, 0, h)),
            pl.BlockSpec((None, V_ROWS, seq), lambda b, h, i: (b, h, 0))],
        out_specs=pl.BlockSpec((None, tq, HEAD_W), lambda b, h, i: (b, i, h)),
        out_shape=jax.ShapeDtypeStruct((bsz, seq, BRANCH_W), BF16),
        scratch_shapes=[pltpu.VMEM((HEAD_W, 2 * tq), BF16),
                        pltpu.VMEM((tq, 2 * tq), F32), pltpu.VMEM((tq, 2 * tq), F32),
                        pltpu.VMEM((tq, 2 * tq), BF16), pltpu.VMEM((tq, 2 * tq), BF16),
                        pltpu.VMEM((1, 2 * tq), F32), pltpu.VMEM((1, 2 * tq), F32),
                        pltpu.VMEM((1, 2 * tq), F32), pltpu.VMEM((V_ROWS, 2 * tq), F32)],
        compiler_params=_cparams(("parallel", "parallel", "arbitrary")),
        name="diff_attention",
    )(lq1, lk1, lq2, lk2, subln.reshape(HEAD_W, 1), qT, k3, vT)
    return out.reshape(bsz * seq, BRANCH_W)


def _hgrn_kernel(lbl_ref, nw_ref, q_ref, f_ref, i_ref, g_ref, o_ref, st_ref, *, tc):
    @pl.when(pl.program_id(2) == 0)
    def _():
        st_ref[...] = jnp.zeros_like(st_ref)

    nchunk = tc // CHUNK
    lg = lbl_ref[...]
    mx = jnp.max(lg, axis=0, keepdims=True)
    ex = jnp.exp(lg - mx)
    lb = ex[0:1] / jnp.sum(ex, axis=0, keepdims=True)
    q = q_ref[...].astype(F32)
    f = lb + (1.0 - lb) * _sigmoid(f_ref[...])
    k = 1.0 - f
    logf = jnp.log(f)
    v = i_ref[...]

    row = lax.broadcasted_iota(I32, (tc, tc), 0)
    col = lax.broadcasted_iota(I32, (tc, tc), 1)
    same = (row // CHUNK) == (col // CHUNK)
    causal = same & (col <= row)
    tril = jnp.where(causal, 1.0, 0.0).astype(BF16)
    blk = jnp.where(same, 1.0, 0.0).astype(BF16)
    parts = _split3(logf)
    gcum = sum(jnp.dot(tril, p, preferred_element_type=F32) for p in parts)
    gtot = sum(jnp.dot(blk, p, preferred_element_type=F32) for p in parts)

    q_t = (q * jnp.exp(gcum)).astype(BF16)
    k_t = (k * jnp.exp(-gcum)).astype(BF16)
    k_dec = (k * jnp.exp(gtot - gcum)).astype(BF16)
    a = lax.dot_general(q_t, k_t, (((1,), (1,)), ((), ())), preferred_element_type=F32)
    a = jnp.where(causal, a, 0.0).astype(BF16)
    o_intra = jnp.dot(a, v, preferred_element_type=F32)

    st = st_ref[...]
    inter = []
    for c in range(nchunk):
        sl = slice(c * CHUNK, (c + 1) * CHUNK)
        inter.append(lax.dot_general(q_t[sl], st.astype(BF16), (((1,), (1,)), ((), ())),
                                     preferred_element_type=F32))
        ds = lax.dot_general(v[sl], k_dec[sl], (((0,), (0,)), ((), ())),
                             preferred_element_type=F32)
        decay = jnp.exp(gtot[c * CHUNK:c * CHUNK + 1, :])
        st = decay * st + ds
    st_ref[...] = st
    o = o_intra + jnp.concatenate(inter, axis=0)
    o_ref[...] = (_rms(o, nw_ref[...]) * g_ref[...].astype(F32)).astype(BF16)


def _hgrn2(hq, hf, hi, hg, lb_logits, norm_w, bsz, seq, tc):
    r3 = lambda a: a.reshape(bsz, seq, BRANCH_W)
    blk = pl.BlockSpec((None, tc, HEAD_W), lambda b, h, i: (b, i, h))
    out = pl.pallas_call(
        functools.partial(_hgrn_kernel, tc=tc),
        grid=(bsz, HG_HEADS, seq // tc),
        in_specs=[pl.BlockSpec((2, HEAD_W), lambda b, h, i: (0, h)),
                  pl.BlockSpec((1, HEAD_W), lambda b, h, i: (0, 0)),
                  blk, blk, blk, blk],
        out_specs=blk,
        out_shape=jax.ShapeDtypeStruct((bsz, seq, BRANCH_W), BF16),
        scratch_shapes=[pltpu.VMEM((HEAD_W, HEAD_W), F32)],
        compiler_params=_cparams(("parallel", "parallel", "arbitrary")),
        name="hgrn2",
    )(lb_logits, norm_w, r3(hq), r3(hf), r3(hi), r3(hg))
    return out.reshape(bsz * seq, BRANCH_W)


def _merge_kernel(x_ref, mod_ref, ya_ref, yb_ref, ga_ref, gb_ref, wa_ref, wb_ref, wo_ref,
                  npost_ref, npre_ref, rw_ref, rb_ref,
                  x1_ref, h2_ref, topi_ref, gate_ref, rank_ref, cnt_ref, *, tm):
    i = pl.program_id(0)

    @pl.when(i == 0)
    def _():
        cnt_ref[...] = jnp.zeros_like(cnt_ref)

    g1 = mod_ref[2:3, :]
    sh2 = mod_ref[3:4, :]
    sc2 = mod_ref[4:5, :]
    a = jnp.dot(ya_ref[...], wa_ref[...], preferred_element_type=F32)
    b = jnp.dot(yb_ref[...], wb_ref[...], preferred_element_type=F32)
    merged = ga_ref[...].astype(F32) * a + gb_ref[...].astype(F32) * b
    y = jnp.dot(merged.astype(BF16), wo_ref[...], preferred_element_type=F32)
    x1 = x_ref[...] + g1 * _rms(y, npost_ref[...])
    x1_ref[...] = x1
    h2 = _rms(x1, npre_ref[...]) * (1.0 + sc2) + sh2
    h2_ref[...] = h2

    logits = jnp.dot(h2.astype(BF16), rw_ref[...], preferred_element_type=F32) + rb_ref[...]
    lane = lax.broadcasted_iota(I32, logits.shape, 1)
    work = logits
    vals, idxs = [], []
    for _ in range(TOP_K):
        mx = jnp.max(work, axis=1, keepdims=True)
        idx = jnp.min(jnp.where(work == mx, lane, LANES), axis=1, keepdims=True)
        vals.append(mx)
        idxs.append(idx)
        work = jnp.where(lane == idx, -jnp.inf, work)
    es = [jnp.exp(vk - vals[0]) for vk in vals]
    den = es[0] + es[1] + es[2] + es[3]

    onehot = jnp.zeros(logits.shape, F32)
    for idx in idxs:
        onehot = jnp.where(lane == idx, 1.0, onehot)
    row = lax.broadcasted_iota(I32, (tm, tm), 0)
    col = lax.broadcasted_iota(I32, (tm, tm), 1)
    strict = jnp.where(col < row, 1.0, 0.0).astype(BF16)
    prefix = jnp.dot(strict, onehot.astype(BF16), preferred_element_type=F32) + cnt_ref[...]
    cnt_ref[...] = cnt_ref[...] + jnp.sum(onehot, axis=0, keepdims=True)

    topi = jnp.zeros(logits.shape, I32)
    gate = jnp.zeros(logits.shape, F32)
    rank = jnp.zeros(logits.shape, I32)
    for kk in range(TOP_K):
        rk = jnp.sum(jnp.where(lane == idxs[kk], prefix, 0.0), axis=1, keepdims=True)
        topi = jnp.where(lane == kk, idxs[kk], topi)
        gate = jnp.where(lane == kk, es[kk] / den, gate)
        rank = jnp.where(lane == kk, rk.astype(I32), rank)
    topi_ref[...] = topi
    gate_ref[...] = gate
    rank_ref[...] = rank


def _merge_route(xf, mod, ya, yb, ga, gb, wa, wb, wo, npost, npre, rw, rb, seq, tm):
    t, d = xf.shape
    tiles_per_batch = seq // tm
    row = lambda w: pl.BlockSpec((tm, w), lambda i: (i, 0))
    const = lambda a: pl.BlockSpec(a.shape, lambda i: (0,) * a.ndim)
    lane_out = jax.ShapeDtypeStruct((t, LANES), I32)
    return pl.pallas_call(
        functools.partial(_merge_kernel, tm=tm),
        grid=(t // tm,),
        in_specs=[row(d),
                  pl.BlockSpec((None, 6, d), lambda i: (i // tiles_per_batch, 0, 0)),
                  row(BRANCH_W), row(BRANCH_W), row(d), row(d),
                  const(wa), const(wb), const(wo), const(npost), const(npre),
                  const(rw), const(rb)],
        out_specs=[row(d), row(d), row(LANES), row(LANES), row(LANES),
                   pl.BlockSpec((1, LANES), lambda i: (0, 0))],
        out_shape=[jax.ShapeDtypeStruct((t, d), F32), jax.ShapeDtypeStruct((t, d), F32),
                   lane_out, jax.ShapeDtypeStruct((t, LANES), F32), lane_out,
                   jax.ShapeDtypeStruct((1, LANES), F32)],
        compiler_params=_cparams(("arbitrary",)),
        name="merge_route",
    )(xf, mod, ya, yb, ga, gb, wa, wb, wo, npost, npre, rw, rb)


def _dispatch_kernel(dest_hbm, h_ref, xs_in, xs_out, dsm, isem, sem, *, tm):
    del xs_in
    i = pl.program_id(0)
    n = tm * TOP_K
    cp = pltpu.make_async_copy(dest_hbm.at[pl.ds(pl.multiple_of(i * n, n), n)], dsm, isem)
    cp.start()
    cp.wait()

    def row_copy(r, kk):
        return pltpu.make_async_copy(h_ref.at[pl.ds(r, 1)],
                                     xs_out.at[pl.ds(dsm[r * TOP_K + kk], 1)], sem)

    def issue(r, c):
        for kk in range(TOP_K):
            row_copy(r, kk).start()
        return c

    def drain(r, c):
        for kk in range(TOP_K):
            row_copy(r, kk).wait()
        return c

    lax.fori_loop(0, tm, issue, 0)
    lax.fori_loop(0, tm, drain, 0)


def _dispatch(h2, dest, n_rows, tm):
    t, d = h2.shape
    zeros = jnp.zeros((n_rows, d), h2.dtype)
    return pl.pallas_call(
        functools.partial(_dispatch_kernel, tm=tm),
        grid=(t // tm,),
        in_specs=[pl.BlockSpec(memory_space=pl.ANY),
                  pl.BlockSpec((tm, d), lambda i: (i, 0)),
                  pl.BlockSpec(memory_space=pl.ANY)],
        out_specs=pl.BlockSpec(memory_space=pl.ANY),
        out_shape=jax.ShapeDtypeStruct((n_rows, d), h2.dtype),
        scratch_shapes=[pltpu.SMEM((tm * TOP_K,), I32),
                        pltpu.SemaphoreType.DMA, pltpu.SemaphoreType.DMA],
        input_output_aliases={2: 0},
        compiler_params=_cparams(("arbitrary",)),
        name="moe_dispatch",
    )(dest, h2, zeros)


def _expert_kernel(be_ref, nu_ref, x_ref, wgu_ref, bgu_ref, wd_ref, bd_ref, y_ref):
    j = pl.program_id(0)

    @pl.when(j < nu_ref[0])
    def _():
        x = x_ref[...].astype(BF16)
        gu = jnp.dot(x, wgu_ref[...], preferred_element_type=F32) + bgu_ref[...]
        g = jnp.minimum(gu[:, :D_MODEL], SWIGLU_LIMIT)
        u = jnp.clip(gu[:, D_MODEL:], -SWIGLU_LIMIT, SWIGLU_LIMIT)
        act = (u + 1.0) * (g * _sigmoid(g * SWIGLU_ALPHA))
        y_ref[...] = jnp.dot(act.astype(BF16), wd_ref[...], preferred_element_type=F32) + bd_ref[...]

    @pl.when(j >= nu_ref[0])
    def _():
        y_ref[...] = jnp.zeros_like(y_ref)


def _experts(xs, block_e, n_used, wgu, bgu, wd, bd):
    p, d = xs.shape
    n_blocks = p // MOE_BLOCK
    e = wgu.shape[0]
    grid_spec = pltpu.PrefetchScalarGridSpec(
        num_scalar_prefetch=2,
        grid=(n_blocks,),
        in_specs=[pl.BlockSpec((MOE_BLOCK, d), lambda j, be, nu: (j, 0)),
                  pl.BlockSpec((None, d, 2 * d), lambda j, be, nu: (be[j], 0, 0)),
                  pl.BlockSpec((None, 1, 2 * d), lambda j, be, nu: (be[j], 0, 0)),
                  pl.BlockSpec((None, d, d), lambda j, be, nu: (be[j], 0, 0)),
                  pl.BlockSpec((None, 1, d), lambda j, be, nu: (be[j], 0, 0))],
        out_specs=pl.BlockSpec((MOE_BLOCK, d), lambda j, be, nu: (j, 0)),
    )
    return pl.pallas_call(
        _expert_kernel,
        grid_spec=grid_spec,
        out_shape=jax.ShapeDtypeStruct((p, d), F32),
        compiler_params=_cparams(("arbitrary",)),
        name="moe_experts",
    )(block_e, n_used, xs, wgu, bgu.reshape(e, 1, 2 * d), wd, bd.reshape(e, 1, d))


def _combine_kernel(dest_hbm, y_hbm, gate_ref, x1_ref, mod_ref, nw_ref, o_ref,
                    dsm, buf, isem, sem, *, tm):
    i = pl.program_id(0)
    n = tm * TOP_K
    cp = pltpu.make_async_copy(dest_hbm.at[pl.ds(pl.multiple_of(i * n, n), n)], dsm, isem)
    cp.start()
    cp.wait()

    def row_copy(r, kk):
        return pltpu.make_async_copy(y_hbm.at[pl.ds(dsm[r * TOP_K + kk], 1)],
                                     buf.at[kk, pl.ds(r, 1)], sem)

    def issue(r, c):
        for kk in range(TOP_K):
            row_copy(r, kk).start()
        return c

    def drain(r, c):
        for kk in range(TOP_K):
            row_copy(r, kk).wait()
        return c

    lax.fori_loop(0, tm, issue, 0)
    lax.fori_loop(0, tm, drain, 0)

    gate = gate_ref[...]
    moe = gate[:, 0:1] * buf[0]
    for kk in range(1, TOP_K):
        moe = moe + gate[:, kk:kk + 1] * buf[kk]
    g2 = mod_ref[5:6, :]
    o_ref[...] = x1_ref[...] + g2 * _rms(moe, nw_ref[...])


def _combine(dest, y, gate, x1, mod, nw, seq, tm):
    t, d = x1.shape
    tiles_per_batch = seq // tm
    return pl.pallas_call(
        functools.partial(_combine_kernel, tm=tm),
        grid=(t // tm,),
        in_specs=[pl.BlockSpec(memory_space=pl.ANY),
                  pl.BlockSpec(memory_space=pl.ANY),
                  pl.BlockSpec((tm, LANES), lambda i: (i, 0)),
                  pl.BlockSpec((tm, d), lambda i: (i, 0)),
                  pl.BlockSpec((None, 6, d), lambda i: (i // tiles_per_batch, 0, 0)),
                  pl.BlockSpec((1, d), lambda i: (0, 0))],
        out_specs=pl.BlockSpec((tm, d), lambda i: (i, 0)),
        out_shape=jax.ShapeDtypeStruct((t, d), F32),
        scratch_shapes=[pltpu.SMEM((tm * TOP_K,), I32),
                        pltpu.VMEM((TOP_K, tm, d), F32),
                        pltpu.SemaphoreType.DMA, pltpu.SemaphoreType.DMA],
        compiler_params=_cparams(("arbitrary",)),
        name="moe_combine",
    )(dest, y, gate, x1, mod, nw)


def _rope_tables(seq):
    pos = jnp.arange(seq, dtype=F32)
    inv = ROPE_THETA ** (-jnp.arange(0, DA_HEAD_DIM, 2, dtype=F32) / DA_HEAD_DIM)
    ang = pos[:, None] * inv[None, :]
    cos, sin = jnp.cos(ang), jnp.sin(ang)
    zero = jnp.zeros_like(sin)
    cos_t = jnp.tile(cos, (1, LANES // cos.shape[1]))
    sin_lo = jnp.tile(jnp.concatenate([-sin, zero], axis=1), (1, 2))
    sin_hi = jnp.tile(jnp.concatenate([zero, sin], axis=1), (1, 2))
    return cos_t, sin_lo, sin_hi


def kernel(x, c, w_mod, b_mod, norm_pre_mix, norm_post_mix, w_in, da_lambda_q1, da_lambda_k1,
           da_lambda_q2, da_lambda_k2, da_subln, hg_lb_logits, hg_norm, w_branch_a, w_branch_b,
           w_out, norm_pre_ffn, norm_post_ffn, router_w, router_b, w_gate_up, b_gate_up,
           w_down, b_down):
    bsz, seq, d = x.shape
    depth = w_mod.shape[0]
    assert depth == 1 and d == D_MODEL
    t = bsz * seq
    lyr = 0
    lambda_init = 0.8 - 0.6 * math.exp(-0.3 * lyr)
    cos, slo, shi = _rope_tables(seq)

    xf = x.reshape(t, d)
    mod = _modulation(c, w_mod[lyr], b_mod[lyr]).reshape(bsz, 6, d)

    qT, k, vT, hq, hf, hi, hg, ga, gb = _in_projection(
        xf, mod, norm_pre_mix[lyr][None], w_in[lyr].astype(BF16), cos, slo, shi, seq, tm=512)
    ya = _diff_attention(qT, k, vT, da_lambda_q1[lyr][None], da_lambda_k1[lyr][None],
                         da_lambda_q2[lyr][None], da_lambda_k2[lyr][None], da_subln[lyr],
                         bsz, seq, lambda_init, tq=512)
    yb = _hgrn2(hq, hf, hi, hg, hg_lb_logits[lyr:lyr + 2], hg_norm[lyr][None], bsz, seq, tc=512)

    rw = jnp.zeros((d, LANES), BF16).at[:, :N_EXPERTS].set(router_w[lyr].astype(BF16))
    rb = jnp.full((1, LANES), NEG_BIG, F32).at[0, :N_EXPERTS].set(router_b[lyr])
    x1, h2, topi, gate, rank, counts = _merge_route(
        xf, mod, ya, yb, ga, gb, w_branch_a[lyr].astype(BF16), w_branch_b[lyr].astype(BF16),
        w_out[lyr].astype(BF16), norm_post_mix[lyr][None], norm_pre_ffn[lyr][None], rw, rb,
        seq, tm=512)

    counts = counts[0, :N_EXPERTS].astype(I32)
    padded = ((counts + MOE_BLOCK - 1) // MOE_BLOCK) * MOE_BLOCK
    pad_ends = jnp.cumsum(padded)
    pad_starts = pad_ends - padded
    n_assign = t * TOP_K
    n_rows = ((n_assign + MOE_BLOCK - 1) // MOE_BLOCK) * MOE_BLOCK + N_EXPERTS * MOE_BLOCK
    n_blocks = n_rows // MOE_BLOCK
    dest = (pad_starts[topi[:, :TOP_K]] + rank[:, :TOP_K]).reshape(n_assign)
    block_start = jnp.arange(n_blocks, dtype=I32) * MOE_BLOCK
    block_e = jnp.minimum(jnp.sum((pad_ends[None, :] <= block_start[:, None]).astype(I32), axis=1),
                          N_EXPERTS - 1)
    n_used = (pad_ends[-1:] // MOE_BLOCK).astype(I32)

    xs = _dispatch(h2, dest, n_rows, tm=512)
    y = _experts(xs, block_e, n_used, w_gate_up[lyr].astype(BF16), b_gate_up[lyr],
                 w_down[lyr].astype(BF16), b_down[lyr])
    out = _combine(dest, y, gate, x1, mod, norm_post_ffn[lyr][None], seq, tm=256)
    return out.reshape(bsz, seq, d)
```

```python
import functools
import math

import jax
import jax.numpy as jnp
from jax import lax
from jax.experimental import pallas as pl
from jax.experimental.pallas import tpu as pltpu

F32 = jnp.float32
BF16 = jnp.bfloat16
I32 = jnp.int32

D_MODEL = 1024
CHUNK = 64
EPS = 1e-6
ROPE_THETA = 10000.0
DA_HEADS = 4
DA_HEAD_DIM = 64
HG_HEADS = 4
HEAD_W = 128
BRANCH_W = 512
V_ROWS = HEAD_W + 16
ATTN_STRIP = 256
N_EXPERTS = 32
TOP_K = 4
SWIGLU_LIMIT = 7.0
SWIGLU_ALPHA = 1.702
MOE_BLOCK = 256
LANES = 128
SUBLANES = 8
NEG_BIG = -1e30

VMEM_LIMIT = 56 * 1024 * 1024


def _cparams(sem):
    return pltpu.CompilerParams(dimension_semantics=sem, vmem_limit_bytes=VMEM_LIMIT)


def _rms(x, w):
    return x * lax.rsqrt(jnp.mean(x * x, axis=-1, keepdims=True) + EPS) * w


def _sigmoid(x):
    return 1.0 / (1.0 + jnp.exp(-x))


def _split3(x):
    a = x.astype(BF16)
    r = x - a.astype(F32)
    b = r.astype(BF16)
    c = (r - b.astype(F32)).astype(BF16)
    return a, b, c


def _mod_kernel(c_ref, w_ref, b_ref, o_ref):
    c = c_ref[...]
    ca = c * _sigmoid(c)
    o_ref[...] = jnp.dot(ca, w_ref[...], preferred_element_type=F32,
                         precision=lax.Precision.HIGHEST) + b_ref[...]


def _modulation(c, w_mod, b_mod):
    bsz, d = c.shape
    n = w_mod.shape[1]
    tn = 1536
    return pl.pallas_call(
        _mod_kernel,
        grid=(n // tn,),
        in_specs=[pl.BlockSpec((bsz, d), lambda j: (0, 0)),
                  pl.BlockSpec((d, tn), lambda j: (0, j)),
                  pl.BlockSpec((1, tn), lambda j: (0, j))],
        out_specs=pl.BlockSpec((bsz, tn), lambda j: (0, j)),
        out_shape=jax.ShapeDtypeStruct((bsz, n), F32),
        compiler_params=_cparams(("arbitrary",)),
        name="modulation",
    )(c, w_mod, b_mod.reshape(1, n))


def _rope(x, cos, sin_lo, sin_hi):
    return x * cos + pltpu.roll(x, 96, 1) * sin_lo + pltpu.roll(x, 32, 1) * sin_hi


def _inproj_kernel(x_ref, mod_ref, nw_ref, w_ref, cos_ref, slo_ref, shi_ref,
                   qT_ref, k_ref, vT_ref, hq_ref, hf_ref, hi_ref, hg_ref, ga_ref, gb_ref):
    x = x_ref[...]
    sh1 = mod_ref[0:1, :]
    sc1 = mod_ref[1:2, :]
    h = (_rms(x, nw_ref[...]) * (1.0 + sc1) + sh1).astype(BF16)
    cos, slo, shi = cos_ref[...], slo_ref[...], shi_ref[...]
    scale = DA_HEAD_DIM ** -0.5 * math.log2(math.e)

    def seg(i, width=BRANCH_W):
        return jnp.dot(h, w_ref[:, i:i + width], preferred_element_type=F32)

    q = seg(0)
    k = seg(BRANCH_W)
    for hd in range(DA_HEADS):
        sl = slice(hd * HEAD_W, (hd + 1) * HEAD_W)
        qT_ref[sl, :] = (_rope(q[:, sl], cos, slo, shi) * scale).T.astype(BF16)
        k_ref[:, sl] = _rope(k[:, sl], cos, slo, shi).astype(BF16)
    vT = seg(2 * BRANCH_W).T.astype(BF16)
    ones = jnp.ones((V_ROWS - HEAD_W, vT.shape[1]), BF16)
    for hd in range(DA_HEADS):
        vT_ref[hd * V_ROWS:hd * V_ROWS + HEAD_W, :] = vT[hd * HEAD_W:(hd + 1) * HEAD_W]
        vT_ref[hd * V_ROWS + HEAD_W:(hd + 1) * V_ROWS, :] = ones
    hq = seg(3 * BRANCH_W)
    hq_ref[...] = (hq * _sigmoid(hq)).astype(BF16)
    hf_ref[...] = seg(4 * BRANCH_W)
    hi_ref[...] = seg(5 * BRANCH_W).astype(BF16)
    hg = seg(6 * BRANCH_W)
    hg_ref[...] = (hg * _sigmoid(hg)).astype(BF16)
    for j in range(2):
        ga_ref[:, j * BRANCH_W:(j + 1) * BRANCH_W] = _sigmoid(seg((7 + j) * BRANCH_W)).astype(BF16)
        gb_ref[:, j * BRANCH_W:(j + 1) * BRANCH_W] = _sigmoid(seg((9 + j) * BRANCH_W)).astype(BF16)


def _in_projection(xf, mod, norm_w, w_in_bf, cos, slo, shi, seq, tm):
    t, d = xf.shape
    tiles_per_batch = seq // tm
    bspec = lambda w: pl.BlockSpec((tm, w), lambda i: (i, 0))
    tab = pl.BlockSpec((tm, LANES), lambda i: (i % tiles_per_batch, 0))
    widths = [BRANCH_W] * 7 + [D_MODEL, D_MODEL]
    dtypes = [BF16, BF16, BF16, BF16, F32, BF16, BF16, BF16, BF16]
    out_specs = [bspec(w) for w in widths]
    out_shape = [jax.ShapeDtypeStruct((t, w), dt) for w, dt in zip(widths, dtypes)]
    for idx, rows in ((0, BRANCH_W), (2, DA_HEADS * V_ROWS)):
        out_specs[idx] = pl.BlockSpec((None, rows, tm),
                                      lambda i: (i // tiles_per_batch, 0, i % tiles_per_batch))
        out_shape[idx] = jax.ShapeDtypeStruct((t // seq, rows, seq), BF16)
    return pl.pallas_call(
        _inproj_kernel,
        grid=(t // tm,),
        in_specs=[bspec(d),
                  pl.BlockSpec((None, 6, d), lambda i: (i // tiles_per_batch, 0, 0)),
                  pl.BlockSpec((1, d), lambda i: (0, 0)),
                  pl.BlockSpec(w_in_bf.shape, lambda i: (0, 0), pipeline_mode=pl.Buffered(1)),
                  tab, tab, tab],
        out_specs=out_specs,
        out_shape=out_shape,
        compiler_params=_cparams(("parallel",)),
        name="in_projection",
    )(xf, mod, norm_w, w_in_bf, cos, slo, shi)


def _attn_kernel(lq1_ref, lk1_ref, lq2_ref, lk2_ref, subln_ref, qT_ref, k_ref, vT_ref, o_ref,
                 qq_ref, s0_ref, s1_ref, p0_ref, p1_ref, a0_ref, a1_ref, m_ref, acc_ref,
                 *, tq, lambda_init):
    qi = pl.program_id(2)
    s_refs, p_refs, a_refs = (s0_ref, s1_ref), (p0_ref, p1_ref), (a0_ref, a1_ref)
    qT = qT_ref[...]
    feat = lax.broadcasted_iota(I32, qT.shape, 0)
    zero = jnp.zeros_like(qT)
    qq_ref[:, :tq] = jnp.where(feat < DA_HEAD_DIM, qT, zero)
    qq_ref[:, tq:] = jnp.where(feat >= DA_HEAD_DIM, qT, zero)

    strips = [slice(c * ATTN_STRIP, (c + 1) * ATTN_STRIP) for c in range(2 * tq // ATTN_STRIP)]

    def scores(kb, dst, cs):
        start = pl.multiple_of(kb * tq, tq)
        dst[:, cs] = jnp.dot(k_ref[pl.ds(start, tq), :], qq_ref[:, cs], preferred_element_type=F32)

    def accumulate(kb, slot, cs):
        start = pl.multiple_of(kb * tq, tq)
        pv = jnp.dot(vT_ref[:, pl.ds(start, tq)], p_refs[slot][:, cs], preferred_element_type=F32)
        acc_ref[:, cs] = a_refs[slot][:, cs] * acc_ref[:, cs] + pv

    def softmax(slot, cs, masked):
        s = s_refs[slot][:, cs]
        if masked:
            key = lax.broadcasted_iota(I32, s.shape, 0)
            qry = (lax.broadcasted_iota(I32, s.shape, 1) + cs.start) % tq
            s = jnp.where(key // CHUNK <= qry // CHUNK, s, NEG_BIG)
        m_old = m_ref[:, cs]
        m_new = jnp.maximum(m_old, jnp.max(s, axis=0, keepdims=True))
        m_ref[:, cs] = m_new
        a_refs[slot][:, cs] = jnp.exp2(m_old - m_new)
        p_refs[slot][:, cs] = jnp.exp2((s - m_new).astype(BF16))

    def stage(kb, slot):
        for cs in strips:
            scores(kb + 1, s_refs[1 - slot], cs)
            softmax(slot, cs, masked=False)
            accumulate(jnp.maximum(kb - 1, 0), 1 - slot, cs)

    def last_stage(slot):
        for cs in strips:
            accumulate(jnp.maximum(qi - 1, 0), 1 - slot, cs)
            softmax(slot, cs, masked=True)
            accumulate(qi, slot, cs)

    p1_ref[...] = jnp.zeros_like(p1_ref)
    a1_ref[...] = jnp.ones_like(a1_ref)
    m_ref[...] = jnp.full(m_ref.shape, NEG_BIG, F32)
    acc_ref[...] = jnp.zeros_like(acc_ref)
    for cs in strips:
        scores(0, s0_ref, cs)

    def pair(i, c):
        stage(2 * i, 0)
        stage(2 * i + 1, 1)
        return c

    lax.fori_loop(0, qi // 2, pair, 0)

    @pl.when(qi % 2 == 1)
    def _():
        stage(qi - 1, 0)
        last_stage(1)

    @pl.when(qi % 2 == 0)
    def _():
        last_stage(0)

    acc = acc_ref[...]
    o = acc[:HEAD_W] / acc[HEAD_W:HEAD_W + 1]
    lam = (jnp.exp(jnp.sum(lq1_ref[...] * lk1_ref[...], axis=1, keepdims=True))
           - jnp.exp(jnp.sum(lq2_ref[...] * lk2_ref[...], axis=1, keepdims=True))
           + lambda_init)
    o = o[:, :tq] - lam * o[:, tq:]
    o = o * lax.rsqrt(jnp.mean(o * o, axis=0, keepdims=True) + EPS) * subln_ref[...]
    o_ref[...] = (o * (1.0 - lambda_init)).T.astype(BF16)


def _diff_attention(qT, k, vT, lq1, lk1, lq2, lk2, subln, bsz, seq, lambda_init, tq):
    k3 = k.reshape(bsz, seq, BRANCH_W)
    small = lambda n: pl.BlockSpec((1, n), lambda b, h, i: (0, 0))
    out = pl.pallas_call(
        functools.partial(_attn_kernel, tq=tq, lambda_init=lambda_init),
        grid=(bsz, DA_HEADS, seq // tq),
        in_specs=[small(DA_HEAD_DIM)] * 4 + [
            pl.BlockSpec((HEAD_W, 1), lambda b, h, i: (0, 0)),
            pl.BlockSpec((None, HEAD_W, tq), lambda b, h, i: (b, h, i)),
            pl.BlockSpec((None, seq, HEAD_W), lambda b, h, i: (b, 0, h)),
            pl.BlockSpec((None, V_ROWS, seq), lambda b, h, i: (b, h, 0))],
        out_specs=pl.BlockSpec((None, tq, HEAD_W), lambda b, h, i: (b, i, h)),
        out_shape=jax.ShapeDtypeStruct((bsz, seq, BRANCH_W), BF16),
        scratch_shapes=[pltpu.VMEM((HEAD_W, 2 * tq), BF16),
                        pltpu.VMEM((tq, 2 * tq), F32), pltpu.VMEM((tq, 2 * tq), F32),
                        pltpu.VMEM((tq, 2 * tq), BF16), pltpu.VMEM((tq, 2 * tq), BF16),
                        pltpu.VMEM((1, 2 * tq), F32), pltpu.VMEM((1, 2 * tq), F32),
                        pltpu.VMEM((1, 2 * tq), F32), pltpu.VMEM((V_ROWS, 2 * tq), F32)],
        compiler_params=_cparams(("parallel", "parallel", "arbitrary")),
        name="diff_attention",
    )(lq1, lk1, lq2, lk2, subln.reshape(HEAD_W, 1), qT, k3, vT)
    return out.reshape(bsz * seq, BRANCH_W)


def _hgrn_kernel(lbl_ref, nw_ref, q_ref, f_ref, i_ref, g_ref, o_ref, st_ref, *, tc, nh):
    @pl.when(pl.program_id(2) == 0)
    def _():
        st_ref[...] = jnp.zeros_like(st_ref)

    row = lax.broadcasted_iota(I32, (CHUNK, CHUNK), 0)
    col = lax.broadcasted_iota(I32, (CHUNK, CHUNK), 1)
    causal = col <= row
    tril = jnp.where(causal, 1.0, 0.0).astype(BF16)
    nt = (((1,), (1,)), ((), ()))
    tn = (((0,), (0,)), ((), ()))

    nchunk = tc // CHUNK
    units = [(hh, c) for hh in range(nh) for c in range(nchunk)]
    rows = lambda c: slice(c * CHUNK, (c + 1) * CHUNK)
    q, k, v, parts = [], [], [], []
    for hh in range(nh):
        hs = slice(hh * HEAD_W, (hh + 1) * HEAD_W)
        lg = lbl_ref[:, hs]
        ex = jnp.exp(lg - jnp.max(lg, axis=0, keepdims=True))
        lb = ex[0:1] / jnp.sum(ex, axis=0, keepdims=True)
        f = lb + (1.0 - lb) * _sigmoid(f_ref[:, hs])
        q.append(q_ref[:, hs].astype(F32))
        k.append(1.0 - f)
        v.append(i_ref[:, hs])
        parts.append(_split3(jnp.log(f)))
    gall = sum(jnp.dot(tril, jnp.concatenate([parts[hh][p][rows(c)] for hh, c in units], axis=1),
                       preferred_element_type=F32) for p in range(3))
    gcum = [gall[:, u * HEAD_W:(u + 1) * HEAD_W] for u in range(len(units))]
    gtot = [g[CHUNK - 1:CHUNK, :] for g in gcum]
    q_t = [(q[hh][rows(c)] * jnp.exp(gcum[u])).astype(BF16) for u, (hh, c) in enumerate(units)]
    k_t = [(k[hh][rows(c)] * jnp.exp(-gcum[u])).astype(BF16) for u, (hh, c) in enumerate(units)]
    k_dec = [(k[hh][rows(c)] * jnp.exp(gtot[u] - gcum[u])).astype(BF16)
             for u, (hh, c) in enumerate(units)]
    a = [lax.dot_general(q_t[u], k_t[u], nt, preferred_element_type=F32) for u in range(len(units))]
    ds = [lax.dot_general(v[hh][rows(c)], k_dec[u], tn, preferred_element_type=F32)
          for u, (hh, c) in enumerate(units)]
    a = [jnp.where(causal, x, 0.0).astype(BF16) for x in a]
    st_in = []
    for hh in range(nh):
        st = st_ref[hh]
        for c in range(nchunk):
            u = hh * nchunk + c
            st_in.append(st.astype(BF16))
            st = jnp.exp(gtot[u]) * st + ds[u]
        st_ref[hh] = st
    outs = [jnp.dot(a[u], v[hh][rows(c)], preferred_element_type=F32)
            + lax.dot_general(q_t[u], st_in[u], nt, preferred_element_type=F32)
            for u, (hh, c) in enumerate(units)]
    for hh in range(nh):
        hs = slice(hh * HEAD_W, (hh + 1) * HEAD_W)
        o = jnp.concatenate(outs[hh * nchunk:(hh + 1) * nchunk], axis=0)
        o_ref[:, hs] = (_rms(o, nw_ref[...]) * g_ref[:, hs].astype(F32)).astype(BF16)


def _hgrn2(hq, hf, hi, hg, lb_logits, norm_w, bsz, seq, tc, nh):
    r3 = lambda a: a.reshape(bsz, seq, BRANCH_W)
    blk = pl.BlockSpec((None, tc, nh * HEAD_W), lambda b, h, i: (b, i, h))
    out = pl.pallas_call(
        functools.partial(_hgrn_kernel, tc=tc, nh=nh),
        grid=(bsz, HG_HEADS // nh, seq // tc),
        in_specs=[pl.BlockSpec((2, nh * HEAD_W), lambda b, h, i: (0, h)),
                  pl.BlockSpec((1, HEAD_W), lambda b, h, i: (0, 0)),
                  blk, blk, blk, blk],
        out_specs=blk,
        out_shape=jax.ShapeDtypeStruct((bsz, seq, BRANCH_W), BF16),
        scratch_shapes=[pltpu.VMEM((nh, HEAD_W, HEAD_W), F32)],
        compiler_params=_cparams(("parallel", "parallel", "arbitrary")),
        name="hgrn2",
    )(lb_logits, norm_w, r3(hq), r3(hf), r3(hi), r3(hg))
    return out.reshape(bsz * seq, BRANCH_W)


def _merge_kernel(x_ref, mod_ref, ya_ref, yb_ref, ga_ref, gb_ref, wa_ref, wb_ref, wo_ref,
                  npost_ref, npre_ref, rw_ref, rb_ref,
                  x1_ref, h2_ref, topi_ref, gate_ref, rank_ref, cnt_ref, *, tm):
    i = pl.program_id(0)

    @pl.when(i == 0)
    def _():
        cnt_ref[...] = jnp.zeros_like(cnt_ref)

    g1 = mod_ref[2:3, :]
    sh2 = mod_ref[3:4, :]
    sc2 = mod_ref[4:5, :]
    a = jnp.dot(ya_ref[...], wa_ref[...], preferred_element_type=F32)
    b = jnp.dot(yb_ref[...], wb_ref[...], preferred_element_type=F32)
    merged = ga_ref[...].astype(F32) * a + gb_ref[...].astype(F32) * b
    y = jnp.dot(merged.astype(BF16), wo_ref[...], preferred_element_type=F32)
    x1 = x_ref[...] + g1 * _rms(y, npost_ref[...])
    x1_ref[...] = x1
    h2 = _rms(x1, npre_ref[...]) * (1.0 + sc2) + sh2
    h2_ref[...] = h2

    logits = jnp.dot(h2.astype(BF16), rw_ref[...], preferred_element_type=F32) + rb_ref[...]
    lane = lax.broadcasted_iota(I32, logits.shape, 1)
    work = logits
    vals, idxs = [], []
    for _ in range(TOP_K):
        mx = jnp.max(work, axis=1, keepdims=True)
        idx = jnp.min(jnp.where(work == mx, lane, LANES), axis=1, keepdims=True)
        vals.append(mx)
        idxs.append(idx)
        work = jnp.where(lane == idx, -jnp.inf, work)
    es = [jnp.exp(vk - vals[0]) for vk in vals]
    den = es[0] + es[1] + es[2] + es[3]

    onehot = jnp.zeros(logits.shape, F32)
    for idx in idxs:
        onehot = jnp.where(lane == idx, 1.0, onehot)
    row = lax.broadcasted_iota(I32, (tm, tm), 0)
    col = lax.broadcasted_iota(I32, (tm, tm), 1)
    strict = jnp.where(col < row, 1.0, 0.0).astype(BF16)
    prefix = jnp.dot(strict, onehot.astype(BF16), preferred_element_type=F32) + cnt_ref[...]
    cnt_ref[...] = cnt_ref[...] + jnp.sum(onehot, axis=0, keepdims=True)

    topi = jnp.zeros(logits.shape, I32)
    gate = jnp.zeros(logits.shape, F32)
    rank = jnp.zeros(logits.shape, I32)
    for kk in range(TOP_K):
        rk = jnp.sum(jnp.where(lane == idxs[kk], prefix, 0.0), axis=1, keepdims=True)
        topi = jnp.where(lane == kk, idxs[kk], topi)
        gate = jnp.where(lane == kk, es[kk] / den, gate)
        rank = jnp.where(lane == kk, rk.astype(I32), rank)
    topi_ref[...] = topi
    gate_ref[...] = gate
    rank_ref[...] = rank


def _merge_route(xf, mod, ya, yb, ga, gb, wa, wb, wo, npost, npre, rw, rb, seq, tm):
    t, d = xf.shape
    tiles_per_batch = seq // tm
    row = lambda w: pl.BlockSpec((tm, w), lambda i: (i, 0))
    const = lambda a: pl.BlockSpec(a.shape, lambda i: (0,) * a.ndim)
    lane_out = jax.ShapeDtypeStruct((t, LANES), I32)
    return pl.pallas_call(
        functools.partial(_merge_kernel, tm=tm),
        grid=(t // tm,),
        in_specs=[row(d),
                  pl.BlockSpec((None, 6, d), lambda i: (i // tiles_per_batch, 0, 0)),
                  row(BRANCH_W), row(BRANCH_W), row(d), row(d),
                  const(wa), const(wb), const(wo), const(npost), const(npre),
                  const(rw), const(rb)],
        out_specs=[row(d), row(d), row(LANES), row(LANES), row(LANES),
                   pl.BlockSpec((1, LANES), lambda i: (0, 0))],
        out_shape=[jax.ShapeDtypeStruct((t, d), F32), jax.ShapeDtypeStruct((t, d), F32),
                   lane_out, jax.ShapeDtypeStruct((t, LANES), F32), lane_out,
                   jax.ShapeDtypeStruct((1, LANES), F32)],
        compiler_params=_cparams(("arbitrary",)),
        name="merge_route",
    )(xf, mod, ya, yb, ga, gb, wa, wb, wo, npost, npre, rw, rb)


def _dispatch_kernel(pend_ref, padded_ref, dest_hbm, h_ref, xs_out, dsm, zbuf, isem, zsem, sem,
                     *, tm):
    i = pl.program_id(0)
    n = tm * TOP_K
    cp = pltpu.make_async_copy(dest_hbm.at[pl.ds(pl.multiple_of(i * n, n), n)], dsm, isem)
    cp.start()

    def tail_copy(e):
        start = pl.multiple_of(pend_ref[e] - MOE_BLOCK, MOE_BLOCK)
        return pltpu.make_async_copy(zbuf, xs_out.at[pl.ds(start, MOE_BLOCK)], zsem)

    n_blocks = xs_out.shape[0] // MOE_BLOCK
    n_used = pend_ref[N_EXPERTS - 1] // MOE_BLOCK

    def unused_copy(blk):
        return pltpu.make_async_copy(zbuf, xs_out.at[pl.ds(blk * MOE_BLOCK, MOE_BLOCK)], zsem)

    @pl.when(i == 0)
    def _():
        zbuf[...] = jnp.zeros_like(zbuf)
        for e in range(N_EXPERTS):
            @pl.when(padded_ref[e] > 0)
            def _():
                tail_copy(e).start()
        for blk in range(n_blocks - N_EXPERTS, n_blocks):
            @pl.when(blk >= n_used)
            def _():
                unused_copy(blk).start()
        for e in range(N_EXPERTS):
            @pl.when(padded_ref[e] > 0)
            def _():
                tail_copy(e).wait()
        for blk in range(n_blocks - N_EXPERTS, n_blocks):
            @pl.when(blk >= n_used)
            def _():
                unused_copy(blk).wait()

    cp.wait()

    def issue(r8, c):
        base = pl.multiple_of(r8 * SUBLANES, SUBLANES)
        for ri in range(SUBLANES):
            for kk in range(TOP_K):
                dst = dsm[(base + ri) * TOP_K + kk]
                pltpu.make_async_copy(h_ref.at[pl.ds(base + ri, 1)],
                                      xs_out.at[pl.ds(dst, 1)], sem).start()
        return c

    lax.fori_loop(0, tm // SUBLANES, issue, 0)
    for _ in range(TOP_K):
        pltpu.make_async_copy(h_ref, xs_out.at[pl.ds(0, tm)], sem).wait()


def _dispatch(h2, dest, pad_ends, padded, n_rows, tm):
    t, d = h2.shape
    grid_spec = pltpu.PrefetchScalarGridSpec(
        num_scalar_prefetch=2,
        grid=(t // tm,),
        in_specs=[pl.BlockSpec(memory_space=pl.ANY),
                  pl.BlockSpec((tm, d), lambda i, pe, pd: (i, 0))],
        out_specs=pl.BlockSpec(memory_space=pl.ANY),
        scratch_shapes=[pltpu.SMEM((tm * TOP_K,), I32),
                        pltpu.VMEM((MOE_BLOCK, d), h2.dtype),
                        pltpu.SemaphoreType.DMA, pltpu.SemaphoreType.DMA, pltpu.SemaphoreType.DMA],
    )
    return pl.pallas_call(
        functools.partial(_dispatch_kernel, tm=tm),
        grid_spec=grid_spec,
        out_shape=jax.ShapeDtypeStruct((n_rows, d), h2.dtype),
        compiler_params=_cparams(("arbitrary",)),
        name="moe_dispatch",
    )(pad_ends, padded, dest, h2)


def _expert_kernel(be_ref, nu_ref, x_ref, wgu_ref, bgu_ref, wd_ref, bd_ref, y_ref,
                   wgu_bf, wd_bf):
    j = pl.program_id(0)

    @pl.when((j == 0) | (be_ref[j] != be_ref[jnp.maximum(j - 1, 0)]))
    def _():
        wgu_bf[...] = wgu_ref[...].astype(BF16)
        wd_bf[...] = wd_ref[...].astype(BF16)

    @pl.when(j < nu_ref[0])
    def _():
        x = x_ref[...].astype(BF16)
        gu = jnp.dot(x, wgu_bf[...], preferred_element_type=F32) + bgu_ref[...]
        g = jnp.minimum(gu[:, :D_MODEL], SWIGLU_LIMIT)
        u = jnp.clip(gu[:, D_MODEL:], -SWIGLU_LIMIT, SWIGLU_LIMIT)
        act = (u + 1.0) * (g * _sigmoid(g * SWIGLU_ALPHA))
        y_ref[...] = jnp.dot(act.astype(BF16), wd_bf[...], preferred_element_type=F32) + bd_ref[...]

    @pl.when(j >= nu_ref[0])
    def _():
        y_ref[...] = jnp.zeros_like(y_ref)


def _experts(xs, block_e, n_used, wgu, bgu, wd, bd):
    p, d = xs.shape
    n_blocks = p // MOE_BLOCK
    e = wgu.shape[0]
    grid_spec = pltpu.PrefetchScalarGridSpec(
        num_scalar_prefetch=2,
        grid=(n_blocks,),
        in_specs=[pl.BlockSpec((MOE_BLOCK, d), lambda j, be, nu: (jnp.minimum(j, nu[0] - 1), 0)),
                  pl.BlockSpec((None, d, 2 * d), lambda j, be, nu: (be[j], 0, 0)),
                  pl.BlockSpec((None, 1, 2 * d), lambda j, be, nu: (be[j], 0, 0)),
                  pl.BlockSpec((None, d, d), lambda j, be, nu: (be[j], 0, 0)),
                  pl.BlockSpec((None, 1, d), lambda j, be, nu: (be[j], 0, 0))],
        out_specs=pl.BlockSpec((MOE_BLOCK, d), lambda j, be, nu: (j, 0)),
        scratch_shapes=[pltpu.VMEM((d, 2 * d), BF16), pltpu.VMEM((d, d), BF16)],
    )
    return pl.pallas_call(
        _expert_kernel,
        grid_spec=grid_spec,
        out_shape=jax.ShapeDtypeStruct((p, d), F32),
        compiler_params=_cparams(("arbitrary",)),
        name="moe_experts",
    )(block_e, n_used, xs, wgu, bgu.reshape(e, 1, 2 * d), wd, bd.reshape(e, 1, d))


def _combine_kernel(dest_hbm, y_hbm, gate_ref, x1_ref, mod_ref, nw_ref, o_ref,
                    dsm0, dsm1, buf0, buf1, isem, sem, *, tm):
    i = pl.program_id(0)
    n = tm * TOP_K
    dsms, bufs = (dsm0, dsm1), (buf0, buf1)

    def start_gather(tile, slot):
        dsm, buf = dsms[slot], bufs[slot]
        cp = pltpu.make_async_copy(dest_hbm.at[pl.ds(pl.multiple_of(tile * n, n), n)], dsm, isem)
        cp.start()
        cp.wait()

        def issue(r8, c):
            base = pl.multiple_of(r8 * SUBLANES, SUBLANES)
            for ri in range(SUBLANES):
                for kk in range(TOP_K):
                    src = dsm[(base + ri) * TOP_K + kk]
                    pltpu.make_async_copy(y_hbm.at[pl.ds(src, 1)],
                                          buf.at[kk, pl.ds(base + ri, 1)], sem.at[slot]).start()
            return c

        lax.fori_loop(0, tm // SUBLANES, issue, 0)

    def finish(slot):
        buf = bufs[slot]
        for kk in range(TOP_K):
            pltpu.make_async_copy(y_hbm.at[pl.ds(0, tm)], buf.at[kk], sem.at[slot]).wait()
        gate = gate_ref[...]
        moe = gate[:, 0:1] * buf[0]
        for kk in range(1, TOP_K):
            moe = moe + gate[:, kk:kk + 1] * buf[kk]
        g2 = mod_ref[5:6, :]
        o_ref[...] = x1_ref[...] + g2 * _rms(moe, nw_ref[...])

    @pl.when(i == 0)
    def _():
        start_gather(0, 0)

    for slot in range(2):
        @pl.when((i + 1 < pl.num_programs(0)) & ((i + 1) % 2 == slot))
        def _():
            start_gather(i + 1, slot)

    for slot in range(2):
        @pl.when(i % 2 == slot)
        def _():
            finish(slot)


def _combine(dest, y, gate, x1, mod, nw, seq, tm):
    t, d = x1.shape
    tiles_per_batch = seq // tm
    return pl.pallas_call(
        functools.partial(_combine_kernel, tm=tm),
        grid=(t // tm,),
        in_specs=[pl.BlockSpec(memory_space=pl.ANY),
                  pl.BlockSpec(memory_space=pl.ANY),
                  pl.BlockSpec((tm, LANES), lambda i: (i, 0)),
                  pl.BlockSpec((tm, d), lambda i: (i, 0)),
                  pl.BlockSpec((None, 6, d), lambda i: (i // tiles_per_batch, 0, 0)),
                  pl.BlockSpec((1, d), lambda i: (0, 0))],
        out_specs=pl.BlockSpec((tm, d), lambda i: (i, 0)),
        out_shape=jax.ShapeDtypeStruct((t, d), F32),
        scratch_shapes=[pltpu.SMEM((tm * TOP_K,), I32), pltpu.SMEM((tm * TOP_K,), I32),
                        pltpu.VMEM((TOP_K, tm, d), F32), pltpu.VMEM((TOP_K, tm, d), F32),
                        pltpu.SemaphoreType.DMA, pltpu.SemaphoreType.DMA((2,))],
        compiler_params=_cparams(("arbitrary",)),
        name="moe_combine",
    )(dest, y, gate, x1, mod, nw)


def _rope_tables(seq):
    pos = jnp.arange(seq, dtype=F32)
    inv = ROPE_THETA ** (-jnp.arange(0, DA_HEAD_DIM, 2, dtype=F32) / DA_HEAD_DIM)
    ang = pos[:, None] * inv[None, :]
    cos, sin = jnp.cos(ang), jnp.sin(ang)
    zero = jnp.zeros_like(sin)
    cos_t = jnp.tile(cos, (1, LANES // cos.shape[1]))
    sin_lo = jnp.tile(jnp.concatenate([-sin, zero], axis=1), (1, 2))
    sin_hi = jnp.tile(jnp.concatenate([zero, sin], axis=1), (1, 2))
    return cos_t, sin_lo, sin_hi


def kernel(x, c, w_mod, b_mod, norm_pre_mix, norm_post_mix, w_in, da_lambda_q1, da_lambda_k1,
           da_lambda_q2, da_lambda_k2, da_subln, hg_lb_logits, hg_norm, w_branch_a, w_branch_b,
           w_out, norm_pre_ffn, norm_post_ffn, router_w, router_b, w_gate_up, b_gate_up,
           w_down, b_down):
    bsz, seq, d = x.shape
    depth = w_mod.shape[0]
    assert depth == 1 and d == D_MODEL
    t = bsz * seq
    lyr = 0
    lambda_init = 0.8 - 0.6 * math.exp(-0.3 * lyr)
    cos, slo, shi = _rope_tables(seq)

    xf = x.reshape(t, d)
    mod = _modulation(c, w_mod[lyr], b_mod[lyr]).reshape(bsz, 6, d)

    qT, k, vT, hq, hf, hi, hg, ga, gb = _in_projection(
        xf, mod, norm_pre_mix[lyr][None], w_in[lyr].astype(BF16), cos, slo, shi, seq, tm=512)
    ya = _diff_attention(qT, k, vT, da_lambda_q1[lyr][None], da_lambda_k1[lyr][None],
                         da_lambda_q2[lyr][None], da_lambda_k2[lyr][None], da_subln[lyr],
                         bsz, seq, lambda_init, tq=512)
    yb = _hgrn2(hq, hf, hi, hg, hg_lb_logits[lyr:lyr + 2], hg_norm[lyr][None], bsz, seq, tc=512, nh=2)

    rw = jnp.zeros((d, LANES), BF16).at[:, :N_EXPERTS].set(router_w[lyr].astype(BF16))
    rb = jnp.full((1, LANES), NEG_BIG, F32).at[0, :N_EXPERTS].set(router_b[lyr])
    x1, h2, topi, gate, rank, counts = _merge_route(
        xf, mod, ya, yb, ga, gb, w_branch_a[lyr].astype(BF16), w_branch_b[lyr].astype(BF16),
        w_out[lyr].astype(BF16), norm_post_mix[lyr][None], norm_pre_ffn[lyr][None], rw, rb,
        seq, tm=512)

    counts = counts[0, :N_EXPERTS].astype(I32)
    padded = ((counts + MOE_BLOCK - 1) // MOE_BLOCK) * MOE_BLOCK
    pad_ends = jnp.cumsum(padded)
    pad_starts = pad_ends - padded
    n_assign = t * TOP_K
    n_rows = ((n_assign + MOE_BLOCK - 1) // MOE_BLOCK) * MOE_BLOCK + N_EXPERTS * MOE_BLOCK
    n_blocks = n_rows // MOE_BLOCK
    dest = (pad_starts[topi[:, :TOP_K]] + rank[:, :TOP_K]).reshape(n_assign)
    block_start = jnp.arange(n_blocks, dtype=I32) * MOE_BLOCK
    block_e = jnp.minimum(jnp.sum((pad_ends[None, :] <= block_start[:, None]).astype(I32), axis=1),
                          N_EXPERTS - 1)
    n_used = (pad_ends[-1:] // MOE_BLOCK).astype(I32)

    xs = _dispatch(h2, dest, pad_ends.astype(I32), padded, n_rows, tm=512)
    y = _experts(xs, block_e, n_used, w_gate_up[lyr], b_gate_up[lyr], w_down[lyr], b_down[lyr])
    out = _combine(dest, y, gate, x1, mod, norm_post_ffn[lyr][None], seq, tm=256)
    return out.reshape(bsz, seq, d)
```

```python
import functools
import math

import jax
import jax.numpy as jnp
from jax import lax
from jax.experimental import pallas as pl
from jax.experimental.pallas import tpu as pltpu

F32 = jnp.float32
BF16 = jnp.bfloat16
I32 = jnp.int32

D_MODEL = 1024
CHUNK = 64
EPS = 1e-6
ROPE_THETA = 10000.0
DA_HEADS = 4
DA_HEAD_DIM = 64
HG_HEADS = 4
HEAD_W = 128
BRANCH_W = 512
V_ROWS = HEAD_W + 16
ATTN_STRIP = 256
N_EXPERTS = 32
TOP_K = 4
SWIGLU_LIMIT = 7.0
SWIGLU_ALPHA = 1.702
MOE_BLOCK = 256
IDX_SLOTS = 4
LANES = 128
NEG_BIG = -1e30

VMEM_LIMIT = 56 * 1024 * 1024


def _cparams(sem):
    return pltpu.CompilerParams(dimension_semantics=sem, vmem_limit_bytes=VMEM_LIMIT)


def _rms(x, w):
    return x * lax.rsqrt(jnp.mean(x * x, axis=-1, keepdims=True) + EPS) * w


def _sigmoid(x):
    return 1.0 / (1.0 + jnp.exp(-x))


def _split3(x):
    a = x.astype(BF16)
    r = x - a.astype(F32)
    b = r.astype(BF16)
    c = (r - b.astype(F32)).astype(BF16)
    return a, b, c


def _mod_kernel(c_ref, w_ref, b_ref, o_ref):
    c = c_ref[...]
    ca = c * _sigmoid(c)
    o_ref[...] = jnp.dot(ca, w_ref[...], preferred_element_type=F32,
                         precision=lax.Precision.HIGHEST) + b_ref[...]


def _modulation(c, w_mod, b_mod):
    bsz, d = c.shape
    n = w_mod.shape[1]
    tn = 1536
    return pl.pallas_call(
        _mod_kernel,
        grid=(n // tn,),
        in_specs=[pl.BlockSpec((bsz, d), lambda j: (0, 0)),
                  pl.BlockSpec((d, tn), lambda j: (0, j)),
                  pl.BlockSpec((1, tn), lambda j: (0, j))],
        out_specs=pl.BlockSpec((bsz, tn), lambda j: (0, j)),
        out_shape=jax.ShapeDtypeStruct((bsz, n), F32),
        compiler_params=_cparams(("arbitrary",)),
        name="modulation",
    )(c, w_mod, b_mod.reshape(1, n))


def _rope(x, cos, sin_lo, sin_hi):
    return x * cos + pltpu.roll(x, 96, 1) * sin_lo + pltpu.roll(x, 32, 1) * sin_hi


def _inproj_kernel(x_ref, mod_ref, nw_ref, w_ref, cos_ref, slo_ref, shi_ref,
                   qT_ref, k_ref, vT_ref, hq_ref, hf_ref, hi_ref, hg_ref, ga_ref, gb_ref):
    x = x_ref[...]
    sh1 = mod_ref[0:1, :]
    sc1 = mod_ref[1:2, :]
    h = (_rms(x, nw_ref[...]) * (1.0 + sc1) + sh1).astype(BF16)
    cos, slo, shi = cos_ref[...], slo_ref[...], shi_ref[...]
    scale = DA_HEAD_DIM ** -0.5 * math.log2(math.e)

    def seg(i, width=BRANCH_W):
        return jnp.dot(h, w_ref[:, i:i + width], preferred_element_type=F32)

    q = seg(0)
    k = seg(BRANCH_W)
    for hd in range(DA_HEADS):
        sl = slice(hd * HEAD_W, (hd + 1) * HEAD_W)
        qT_ref[sl, :] = (_rope(q[:, sl], cos, slo, shi) * scale).T.astype(BF16)
        k_ref[:, sl] = _rope(k[:, sl], cos, slo, shi).astype(BF16)
    vT = seg(2 * BRANCH_W).T.astype(BF16)
    ones = jnp.ones((V_ROWS - HEAD_W, vT.shape[1]), BF16)
    for hd in range(DA_HEADS):
        vT_ref[hd * V_ROWS:hd * V_ROWS + HEAD_W, :] = vT[hd * HEAD_W:(hd + 1) * HEAD_W]
        vT_ref[hd * V_ROWS + HEAD_W:(hd + 1) * V_ROWS, :] = ones
    hq = seg(3 * BRANCH_W)
    hq_ref[...] = (hq * _sigmoid(hq)).astype(BF16)
    hf_ref[...] = seg(4 * BRANCH_W)
    hi_ref[...] = seg(5 * BRANCH_W).astype(BF16)
    hg = seg(6 * BRANCH_W)
    hg_ref[...] = (hg * _sigmoid(hg)).astype(BF16)
    for j in range(2):
        ga_ref[:, j * BRANCH_W:(j + 1) * BRANCH_W] = _sigmoid(seg((7 + j) * BRANCH_W)).astype(BF16)
        gb_ref[:, j * BRANCH_W:(j + 1) * BRANCH_W] = _sigmoid(seg((9 + j) * BRANCH_W)).astype(BF16)


def _in_projection(xf, mod, norm_w, w_in_bf, cos, slo, shi, seq, tm):
    t, d = xf.shape
    tiles_per_batch = seq // tm
    bspec = lambda w: pl.BlockSpec((tm, w), lambda i: (i, 0))
    tab = pl.BlockSpec((tm, LANES), lambda i: (i % tiles_per_batch, 0))
    widths = [BRANCH_W] * 7 + [D_MODEL, D_MODEL]
    dtypes = [BF16, BF16, BF16, BF16, F32, BF16, BF16, BF16, BF16]
    out_specs = [bspec(w) for w in widths]
    out_shape = [jax.ShapeDtypeStruct((t, w), dt) for w, dt in zip(widths, dtypes)]
    for idx, rows in ((0, BRANCH_W), (2, DA_HEADS * V_ROWS)):
        out_specs[idx] = pl.BlockSpec((None, rows, tm),
                                      lambda i: (i // tiles_per_batch, 0, i % tiles_per_batch))
        out_shape[idx] = jax.ShapeDtypeStruct((t // seq, rows, seq), BF16)
    return pl.pallas_call(
        _inproj_kernel,
        grid=(t // tm,),
        in_specs=[bspec(d),
                  pl.BlockSpec((None, 6, d), lambda i: (i // tiles_per_batch, 0, 0)),
                  pl.BlockSpec((1, d), lambda i: (0, 0)),
                  pl.BlockSpec(w_in_bf.shape, lambda i: (0, 0), pipeline_mode=pl.Buffered(1)),
                  tab, tab, tab],
        out_specs=out_specs,
        out_shape=out_shape,
        compiler_params=_cparams(("parallel",)),
        name="in_projection",
    )(xf, mod, norm_w, w_in_bf, cos, slo, shi)


def _attn_kernel(lq1_ref, lk1_ref, lq2_ref, lk2_ref, subln_ref, qT_ref, k_ref, vT_ref, o_ref,
                 qq_ref, s0_ref, s1_ref, p0_ref, p1_ref, a0_ref, a1_ref, m_ref, acc_ref,
                 *, tq, lambda_init):
    qi = pl.program_id(2)
    s_refs, p_refs, a_refs = (s0_ref, s1_ref), (p0_ref, p1_ref), (a0_ref, a1_ref)
    qT = qT_ref[...]
    feat = lax.broadcasted_iota(I32, qT.shape, 0)
    zero = jnp.zeros_like(qT)
    qq_ref[:, :tq] = jnp.where(feat < DA_HEAD_DIM, qT, zero)
    qq_ref[:, tq:] = jnp.where(feat >= DA_HEAD_DIM, qT, zero)

    strips = [slice(c * ATTN_STRIP, (c + 1) * ATTN_STRIP) for c in range(2 * tq // ATTN_STRIP)]

    def scores(kb, dst, cs):
        start = pl.multiple_of(kb * tq, tq)
        dst[:, cs] = jnp.dot(k_ref[pl.ds(start, tq), :], qq_ref[:, cs], preferred_element_type=F32)

    def accumulate(kb, slot, cs):
        start = pl.multiple_of(kb * tq, tq)
        pv = jnp.dot(vT_ref[:, pl.ds(start, tq)], p_refs[slot][:, cs], preferred_element_type=F32)
        acc_ref[:, cs] = a_refs[slot][:, cs] * acc_ref[:, cs] + pv

    def softmax(slot, cs, masked):
        s = s_refs[slot][:, cs]
        if masked:
            key = lax.broadcasted_iota(I32, s.shape, 0)
            qry = (lax.broadcasted_iota(I32, s.shape, 1) + cs.start) % tq
            s = jnp.where(key // CHUNK <= qry // CHUNK, s, NEG_BIG)
        m_old = m_ref[:, cs]
        m_new = jnp.maximum(m_old, jnp.max(s, axis=0, keepdims=True))
        m_ref[:, cs] = m_new
        a_refs[slot][:, cs] = jnp.exp2(m_old - m_new)
        p_refs[slot][:, cs] = jnp.exp2((s - m_new).astype(BF16))

    def stage(kb, slot):
        for cs in strips:
            scores(kb + 1, s_refs[1 - slot], cs)
            softmax(slot, cs, masked=False)
            accumulate(jnp.maximum(kb - 1, 0), 1 - slot, cs)

    def last_stage(slot):
        for cs in strips:
            accumulate(jnp.maximum(qi - 1, 0), 1 - slot, cs)
            softmax(slot, cs, masked=True)
            accumulate(qi, slot, cs)

    p1_ref[...] = jnp.zeros_like(p1_ref)
    a1_ref[...] = jnp.ones_like(a1_ref)
    m_ref[...] = jnp.full(m_ref.shape, NEG_BIG, F32)
    acc_ref[...] = jnp.zeros_like(acc_ref)
    for cs in strips:
        scores(0, s0_ref, cs)

    def pair(i, c):
        stage(2 * i, 0)
        stage(2 * i + 1, 1)
        return c

    lax.fori_loop(0, qi // 2, pair, 0)

    @pl.when(qi % 2 == 1)
    def _():
        stage(qi - 1, 0)
        last_stage(1)

    @pl.when(qi % 2 == 0)
    def _():
        last_stage(0)

    acc = acc_ref[...]
    o = acc[:HEAD_W] / acc[HEAD_W:HEAD_W + 1]
    lam = (jnp.exp(jnp.sum(lq1_ref[...] * lk1_ref[...], axis=1, keepdims=True))
           - jnp.exp(jnp.sum(lq2_ref[...] * lk2_ref[...], axis=1, keepdims=True))
           + lambda_init)
    o = o[:, :tq] - lam * o[:, tq:]
    o = o * lax.rsqrt(jnp.mean(o * o, axis=0, keepdims=True) + EPS) * subln_ref[...]
    o_ref[...] = (o * (1.0 - lambda_init)).T.astype(BF16)


def _diff_attention(qT, k, vT, lq1, lk1, lq2, lk2, subln, bsz, seq, lambda_init, tq):
    k3 = k.reshape(bsz, seq, BRANCH_W)
    small = lambda n: pl.BlockSpec((1, n), lambda b, h, i: (0, 0))
    out = pl.pallas_call(
        functools.partial(_attn_kernel, tq=tq, lambda_init=lambda_init),
        grid=(bsz, DA_HEADS, seq // tq),
        in_specs=[small(DA_HEAD_DIM)] * 4 + [
            pl.BlockSpec((HEAD_W, 1), lambda b, h, i: (0, 0)),
            pl.BlockSpec((None, HEAD_W, tq), lambda b, h, i: (b, h, i)),
            pl.BlockSpec((None, seq, HEAD_W), lambda b, h, i: (b, 0, h)),
            pl.BlockSpec((None, V_ROWS, seq), lambda b, h, i: (b, h, 0))],
        out_specs=pl.BlockSpec((None, tq, HEAD_W), lambda b, h, i: (b, i, h)),
        out_shape=jax.ShapeDtypeStruct((bsz, seq, BRANCH_W), BF16),
        scratch_shapes=[pltpu.VMEM((HEAD_W, 2 * tq), BF16),
                        pltpu.VMEM((tq, 2 * tq), F32), pltpu.VMEM((tq, 2 * tq), F32),
                        pltpu.VMEM((tq, 2 * tq), BF16), pltpu.VMEM((tq, 2 * tq), BF16),
                        pltpu.VMEM((1, 2 * tq), F32), pltpu.VMEM((1, 2 * tq), F32),
                        pltpu.VMEM((1, 2 * tq), F32), pltpu.VMEM((V_ROWS, 2 * tq), F32)],
        compiler_params=_cparams(("parallel", "parallel", "arbitrary")),
        name="diff_attention",
    )(lq1, lk1, lq2, lk2, subln.reshape(HEAD_W, 1), qT, k3, vT)
    return out.reshape(bsz * seq, BRANCH_W)


def _hgrn_kernel(lbl_ref, nw_ref, q_ref, f_ref, i_ref, g_ref, o_ref, st_ref, *, tc, nh):
    @pl.when(pl.program_id(2) == 0)
    def _():
        st_ref[...] = jnp.zeros_like(st_ref)

    row = lax.broadcasted_iota(I32, (CHUNK, CHUNK), 0)
    col = lax.broadcasted_iota(I32, (CHUNK, CHUNK), 1)
    causal = col <= row
    tril = jnp.where(causal, 1.0, 0.0).astype(BF16)
    nt = (((1,), (1,)), ((), ()))
    tn = (((0,), (0,)), ((), ()))

    nchunk = tc // CHUNK
    units = [(hh, c) for hh in range(nh) for c in range(nchunk)]
    rows = lambda c: slice(c * CHUNK, (c + 1) * CHUNK)
    q, k, v, parts = [], [], [], []
    for hh in range(nh):
        hs = slice(hh * HEAD_W, (hh + 1) * HEAD_W)
        lg = lbl_ref[:, hs]
        ex = jnp.exp(lg - jnp.max(lg, axis=0, keepdims=True))
        lb = ex[0:1] / jnp.sum(ex, axis=0, keepdims=True)
        f = lb + (1.0 - lb) * _sigmoid(f_ref[:, hs])
        q.append(q_ref[:, hs].astype(F32))
        k.append(1.0 - f)
        v.append(i_ref[:, hs])
        parts.append(_split3(jnp.log(f)))
    gall = sum(jnp.dot(tril, jnp.concatenate([parts[hh][p][rows(c)] for hh, c in units], axis=1),
                       preferred_element_type=F32) for p in range(3))
    gcum = [gall[:, u * HEAD_W:(u + 1) * HEAD_W] for u in range(len(units))]
    gtot = [g[CHUNK - 1:CHUNK, :] for g in gcum]
    q_t = [(q[hh][rows(c)] * jnp.exp(gcum[u])).astype(BF16) for u, (hh, c) in enumerate(units)]
    k_t = [(k[hh][rows(c)] * jnp.exp(-gcum[u])).astype(BF16) for u, (hh, c) in enumerate(units)]
    k_dec = [(k[hh][rows(c)] * jnp.exp(gtot[u] - gcum[u])).astype(BF16)
             for u, (hh, c) in enumerate(units)]
    a = [lax.dot_general(q_t[u], k_t[u], nt, preferred_element_type=F32) for u in range(len(units))]
    ds = [lax.dot_general(v[hh][rows(c)], k_dec[u], tn, preferred_element_type=F32)
          for u, (hh, c) in enumerate(units)]
    a = [jnp.where(causal, x, 0.0).astype(BF16) for x in a]
    st_in = []
    for hh in range(nh):
        st = st_ref[hh]
        for c in range(nchunk):
            u = hh * nchunk + c
            st_in.append(st.astype(BF16))
            st = jnp.exp(gtot[u]) * st + ds[u]
        st_ref[hh] = st
    outs = [jnp.dot(a[u], v[hh][rows(c)], preferred_element_type=F32)
            + lax.dot_general(q_t[u], st_in[u], nt, preferred_element_type=F32)
            for u, (hh, c) in enumerate(units)]
    for hh in range(nh):
        hs = slice(hh * HEAD_W, (hh + 1) * HEAD_W)
        o = jnp.concatenate(outs[hh * nchunk:(hh + 1) * nchunk], axis=0)
        o_ref[:, hs] = (_rms(o, nw_ref[...]) * g_ref[:, hs].astype(F32)).astype(BF16)


def _hgrn2(hq, hf, hi, hg, lb_logits, norm_w, bsz, seq, tc, nh):
    r3 = lambda a: a.reshape(bsz, seq, BRANCH_W)
    blk = pl.BlockSpec((None, tc, nh * HEAD_W), lambda b, h, i: (b, i, h))
    out = pl.pallas_call(
        functools.partial(_hgrn_kernel, tc=tc, nh=nh),
        grid=(bsz, HG_HEADS // nh, seq // tc),
        in_specs=[pl.BlockSpec((2, nh * HEAD_W), lambda b, h, i: (0, h)),
                  pl.BlockSpec((1, HEAD_W), lambda b, h, i: (0, 0)),
                  blk, blk, blk, blk],
        out_specs=blk,
        out_shape=jax.ShapeDtypeStruct((bsz, seq, BRANCH_W), BF16),
        scratch_shapes=[pltpu.VMEM((nh, HEAD_W, HEAD_W), F32)],
        compiler_params=_cparams(("parallel", "parallel", "arbitrary")),
        name="hgrn2",
    )(lb_logits, norm_w, r3(hq), r3(hf), r3(hi), r3(hg))
    return out.reshape(bsz * seq, BRANCH_W)


def _merge_kernel(x_ref, mod_ref, ya_ref, yb_ref, ga_ref, gb_ref, wa_ref, wb_ref, wo_ref,
                  npost_ref, npre_ref, rw_ref, rb_ref,
                  x1_ref, h2_ref, topi_ref, gate_ref, rank_ref, cnt_ref, *, tm):
    i = pl.program_id(0)

    @pl.when(i == 0)
    def _():
        cnt_ref[...] = jnp.zeros_like(cnt_ref)

    g1 = mod_ref[2:3, :]
    sh2 = mod_ref[3:4, :]
    sc2 = mod_ref[4:5, :]
    a = jnp.dot(ya_ref[...], wa_ref[...], preferred_element_type=F32)
    b = jnp.dot(yb_ref[...], wb_ref[...], preferred_element_type=F32)
    merged = ga_ref[...].astype(F32) * a + gb_ref[...].astype(F32) * b
    y = jnp.dot(merged.astype(BF16), wo_ref[...], preferred_element_type=F32)
    x1 = x_ref[...] + g1 * _rms(y, npost_ref[...])
    x1_ref[...] = x1
    h2 = _rms(x1, npre_ref[...]) * (1.0 + sc2) + sh2
    h2_ref[...] = h2

    logits = jnp.dot(h2.astype(BF16), rw_ref[...], preferred_element_type=F32) + rb_ref[...]
    lane = lax.broadcasted_iota(I32, logits.shape, 1)
    work = logits
    vals, idxs = [], []
    for _ in range(TOP_K):
        mx = jnp.max(work, axis=1, keepdims=True)
        idx = jnp.min(jnp.where(work == mx, lane, LANES), axis=1, keepdims=True)
        vals.append(mx)
        idxs.append(idx)
        work = jnp.where(lane == idx, -jnp.inf, work)
    es = [jnp.exp(vk - vals[0]) for vk in vals]
    den = es[0] + es[1] + es[2] + es[3]

    onehot = jnp.zeros(logits.shape, F32)
    for idx in idxs:
        onehot = jnp.where(lane == idx, 1.0, onehot)
    row = lax.broadcasted_iota(I32, (tm, tm), 0)
    col = lax.broadcasted_iota(I32, (tm, tm), 1)
    strict = jnp.where(col < row, 1.0, 0.0).astype(BF16)
    prefix = jnp.dot(strict, onehot.astype(BF16), preferred_element_type=F32) + cnt_ref[...]
    cnt_ref[...] = cnt_ref[...] + jnp.sum(onehot, axis=0, keepdims=True)

    topi = jnp.zeros(logits.shape, I32)
    gate = jnp.zeros(logits.shape, F32)
    rank = jnp.zeros(logits.shape, I32)
    for kk in range(TOP_K):
        rk = jnp.sum(jnp.where(lane == idxs[kk], prefix, 0.0), axis=1, keepdims=True)
        topi = jnp.where(lane == kk, idxs[kk], topi)
        gate = jnp.where(lane == kk, es[kk] / den, gate)
        rank = jnp.where(lane == kk, rk.astype(I32), rank)
    topi_ref[...] = topi
    gate_ref[...] = gate
    rank_ref[...] = rank


def _merge_route(xf, mod, ya, yb, ga, gb, wa, wb, wo, npost, npre, rw, rb, seq, tm):
    t, d = xf.shape
    tiles_per_batch = seq // tm
    row = lambda w: pl.BlockSpec((tm, w), lambda i: (i, 0))
    const = lambda a: pl.BlockSpec(a.shape, lambda i: (0,) * a.ndim)
    lane_out = jax.ShapeDtypeStruct((t, LANES), I32)
    return pl.pallas_call(
        functools.partial(_merge_kernel, tm=tm),
        grid=(t // tm,),
        in_specs=[row(d),
                  pl.BlockSpec((None, 6, d), lambda i: (i // tiles_per_batch, 0, 0)),
                  row(BRANCH_W), row(BRANCH_W), row(d), row(d),
                  const(wa), const(wb), const(wo), const(npost), const(npre),
                  const(rw), const(rb)],
        out_specs=[row(d), row(d), row(LANES), row(LANES), row(LANES),
                   pl.BlockSpec((1, LANES), lambda i: (0, 0))],
        out_shape=[jax.ShapeDtypeStruct((t, d), F32), jax.ShapeDtypeStruct((t, d), F32),
                   lane_out, jax.ShapeDtypeStruct((t, LANES), F32), lane_out,
                   jax.ShapeDtypeStruct((1, LANES), F32)],
        compiler_params=_cparams(("arbitrary",)),
        name="merge_route",
    )(xf, mod, ya, yb, ga, gb, wa, wb, wo, npost, npre, rw, rb)


def _expert_kernel(be_ref, nu_ref, asg_hbm, h_hbm, wgu_ref, bgu_ref, wd_ref, bd_ref, yk_hbm,
                   idx, x0, x1, y0, y1, wgu_bf, wd_bf, isem, gsem, ssem, *, n_assign):
    j = pl.program_id(0)
    n_used = nu_ref[0]
    xb, yb = (x0, x1), (y0, y1)

    def idx_copy(blk):
        slot = blk % IDX_SLOTS
        return pltpu.make_async_copy(asg_hbm.at[pl.ds(blk, 1)], idx.at[pl.ds(slot, 1)],
                                     isem.at[slot])

    def gather_copy(blk, s, r):
        tok = idx[blk % IDX_SLOTS, r]
        return pltpu.make_async_copy(h_hbm.at[pl.ds(tok, 1)], xb[s].at[pl.ds(r, 1)], gsem.at[s])

    def scatter_copy(blk, s, r):
        a = idx[(blk + IDX_SLOTS) % IDX_SLOTS, MOE_BLOCK + r]
        dst = jnp.where(a < n_assign, a, n_assign + s * MOE_BLOCK + r)
        return pltpu.make_async_copy(yb[s].at[pl.ds(r, 1)], yk_hbm.at[pl.ds(dst, 1)], ssem.at[s])

    def rolled(copy, blk, s):
        def body(r, c):
            copy(blk, s, r).start()
            return c
        lax.fori_loop(0, MOE_BLOCK, body, 0)

    def spare_copy(s):
        start = n_assign + s * MOE_BLOCK
        return pltpu.make_async_copy(y1, yk_hbm.at[pl.ds(start, MOE_BLOCK)], ssem.at[s])

    @pl.when(j == 0)
    def _():
        y1[...] = jnp.zeros_like(y1)
        for r in range(MOE_BLOCK):
            idx[IDX_SLOTS - 1, MOE_BLOCK + r] = n_assign
        for s in range(2):
            spare_copy(s).start()
        for s in range(2):
            spare_copy(s).wait()
        first = idx_copy(0)
        first.start()
        first.wait()
        rolled(gather_copy, 0, 0)
        idx_copy(1).start()

    @pl.when(j + 2 <= n_used)
    def _():
        idx_copy(j + 2).start()

    @pl.when(j + 1 <= n_used)
    def _():
        idx_copy(j + 1).wait()

    for s in range(2):
        @pl.when((j >= 1) & (j - 2 < n_used) & (j % 2 == s))
        def _():
            pltpu.make_async_copy(yb[s], yk_hbm.at[pl.ds(0, MOE_BLOCK)], ssem.at[s]).wait()

        @pl.when((j <= n_used) & (j % 2 == s))
        def _():
            pltpu.make_async_copy(h_hbm.at[pl.ds(0, MOE_BLOCK)], xb[s], gsem.at[s]).wait()

    @pl.when((j < n_used) & ((j == 0) | (be_ref[j] != be_ref[jnp.maximum(j - 1, 0)])))
    def _():
        wgu_bf[...] = wgu_ref[...].astype(BF16)
        wd_bf[...] = wd_ref[...].astype(BF16)

    for s in range(2):
        @pl.when((j < n_used) & (j % 2 == s))
        def _():
            for r in range(MOE_BLOCK):
                gather_copy(j + 1, 1 - s, r).start()
                scatter_copy(j - 1, 1 - s, r).start()
            x = xb[s][...].astype(BF16)
            gu = jnp.dot(x, wgu_bf[...], preferred_element_type=F32) + bgu_ref[...]
            g = jnp.minimum(gu[:, :D_MODEL], SWIGLU_LIMIT)
            u = jnp.clip(gu[:, D_MODEL:], -SWIGLU_LIMIT, SWIGLU_LIMIT)
            act = (u + 1.0) * (g * _sigmoid(g * SWIGLU_ALPHA))
            yb[s][...] = jnp.dot(act.astype(BF16), wd_bf[...], preferred_element_type=F32) + bd_ref[...]

        @pl.when((j == n_used) & (j % 2 == s))
        def _():
            rolled(scatter_copy, j - 1, 1 - s)


def _experts(h2, row_asg, block_e, n_used, wgu, bgu, wd, bd, n_assign):
    t, d = h2.shape
    n_blocks = row_asg.shape[0]
    e = wgu.shape[0]
    grid_spec = pltpu.PrefetchScalarGridSpec(
        num_scalar_prefetch=2,
        grid=(n_blocks,),
        in_specs=[pl.BlockSpec(memory_space=pl.ANY),
                  pl.BlockSpec(memory_space=pl.ANY),
                  pl.BlockSpec((None, d, 2 * d), lambda j, be, nu: (be[j], 0, 0)),
                  pl.BlockSpec((None, 1, 2 * d), lambda j, be, nu: (be[j], 0, 0)),
                  pl.BlockSpec((None, d, d), lambda j, be, nu: (be[j], 0, 0)),
                  pl.BlockSpec((None, 1, d), lambda j, be, nu: (be[j], 0, 0))],
        out_specs=pl.BlockSpec(memory_space=pl.ANY),
        scratch_shapes=[pltpu.SMEM((IDX_SLOTS, 2 * MOE_BLOCK), I32),
                        pltpu.VMEM((MOE_BLOCK, d), F32), pltpu.VMEM((MOE_BLOCK, d), F32),
                        pltpu.VMEM((MOE_BLOCK, d), F32), pltpu.VMEM((MOE_BLOCK, d), F32),
                        pltpu.VMEM((d, 2 * d), BF16), pltpu.VMEM((d, d), BF16),
                        pltpu.SemaphoreType.DMA((IDX_SLOTS,)), pltpu.SemaphoreType.DMA((2,)),
                        pltpu.SemaphoreType.DMA((2,))],
    )
    return pl.pallas_call(
        functools.partial(_expert_kernel, n_assign=n_assign),
        grid_spec=grid_spec,
        out_shape=jax.ShapeDtypeStruct((n_assign + 2 * MOE_BLOCK, d), F32),
        compiler_params=_cparams(("arbitrary",)),
        name="moe_experts",
    )(block_e, n_used, row_asg, h2, wgu, bgu.reshape(e, 1, 2 * d), wd, bd.reshape(e, 1, d))


def _finish_kernel(y0_ref, y1_ref, y2_ref, y3_ref, gate_ref, x1_ref, mod_ref, nw_ref, o_ref):
    gate = gate_ref[...]
    moe = gate[:, 0:1] * y0_ref[...]
    for kk, y_ref in enumerate((y1_ref, y2_ref, y3_ref), start=1):
        moe = moe + gate[:, kk:kk + 1] * y_ref[...]
    g2 = mod_ref[5:6, :]
    o_ref[...] = x1_ref[...] + g2 * _rms(moe, nw_ref[...])


def _finish(yk, gate, x1, mod, nw, seq, tm):
    t, d = x1.shape
    tiles_per_batch = seq // tm
    choice = lambda kk: pl.BlockSpec((tm, d), lambda i: (kk * (t // tm) + i, 0))
    return pl.pallas_call(
        _finish_kernel,
        grid=(t // tm,),
        in_specs=[choice(0), choice(1), choice(2), choice(3),
                  pl.BlockSpec((tm, LANES), lambda i: (i, 0)),
                  pl.BlockSpec((tm, d), lambda i: (i, 0)),
                  pl.BlockSpec((None, 6, d), lambda i: (i // tiles_per_batch, 0, 0)),
                  pl.BlockSpec((1, d), lambda i: (0, 0))],
        out_specs=pl.BlockSpec((tm, d), lambda i: (i, 0)),
        out_shape=jax.ShapeDtypeStruct((t, d), F32),
        compiler_params=_cparams(("parallel",)),
        name="moe_finish",
    )(yk, yk, yk, yk, gate, x1, mod, nw)


def _rope_tables(seq):
    pos = jnp.arange(seq, dtype=F32)
    inv = ROPE_THETA ** (-jnp.arange(0, DA_HEAD_DIM, 2, dtype=F32) / DA_HEAD_DIM)
    ang = pos[:, None] * inv[None, :]
    cos, sin = jnp.cos(ang), jnp.sin(ang)
    zero = jnp.zeros_like(sin)
    cos_t = jnp.tile(cos, (1, LANES // cos.shape[1]))
    sin_lo = jnp.tile(jnp.concatenate([-sin, zero], axis=1), (1, 2))
    sin_hi = jnp.tile(jnp.concatenate([zero, sin], axis=1), (1, 2))
    return cos_t, sin_lo, sin_hi


def kernel(x, c, w_mod, b_mod, norm_pre_mix, norm_post_mix, w_in, da_lambda_q1, da_lambda_k1,
           da_lambda_q2, da_lambda_k2, da_subln, hg_lb_logits, hg_norm, w_branch_a, w_branch_b,
           w_out, norm_pre_ffn, norm_post_ffn, router_w, router_b, w_gate_up, b_gate_up,
           w_down, b_down):
    bsz, seq, d = x.shape
    depth = w_mod.shape[0]
    assert depth == 1 and d == D_MODEL
    t = bsz * seq
    lyr = 0
    lambda_init = 0.8 - 0.6 * math.exp(-0.3 * lyr)
    cos, slo, shi = _rope_tables(seq)

    xf = x.reshape(t, d)
    mod = _modulation(c, w_mod[lyr], b_mod[lyr]).reshape(bsz, 6, d)

    qT, k, vT, hq, hf, hi, hg, ga, gb = _in_projection(
        xf, mod, norm_pre_mix[lyr][None], w_in[lyr].astype(BF16), cos, slo, shi, seq, tm=512)
    ya = _diff_attention(qT, k, vT, da_lambda_q1[lyr][None], da_lambda_k1[lyr][None],
                         da_lambda_q2[lyr][None], da_lambda_k2[lyr][None], da_subln[lyr],
                         bsz, seq, lambda_init, tq=512)
    yb = _hgrn2(hq, hf, hi, hg, hg_lb_logits[lyr:lyr + 2], hg_norm[lyr][None], bsz, seq, tc=512, nh=2)

    rw = jnp.zeros((d, LANES), BF16).at[:, :N_EXPERTS].set(router_w[lyr].astype(BF16))
    rb = jnp.full((1, LANES), NEG_BIG, F32).at[0, :N_EXPERTS].set(router_b[lyr])
    x1, h2, topi, gate, rank, counts = _merge_route(
        xf, mod, ya, yb, ga, gb, w_branch_a[lyr].astype(BF16), w_branch_b[lyr].astype(BF16),
        w_out[lyr].astype(BF16), norm_post_mix[lyr][None], norm_pre_ffn[lyr][None], rw, rb,
        seq, tm=512)

    counts = counts[0, :N_EXPERTS].astype(I32)
    padded = ((counts + MOE_BLOCK - 1) // MOE_BLOCK) * MOE_BLOCK
    pad_ends = jnp.cumsum(padded)
    pad_starts = pad_ends - padded
    n_assign = t * TOP_K
    n_rows = n_assign + (N_EXPERTS + 1) * MOE_BLOCK
    n_blocks = n_rows // MOE_BLOCK
    dest = (pad_starts[topi[:, :TOP_K]] + rank[:, :TOP_K]).reshape(n_assign)
    row_asg = jnp.full((n_rows,), n_assign, I32).at[dest].set(
        jnp.arange(n_assign, dtype=I32), unique_indices=True).reshape(n_blocks, MOE_BLOCK)
    row_tok = jnp.minimum(row_asg // TOP_K, t - 1)
    row_dst = jnp.where(row_asg < n_assign, (row_asg % TOP_K) * t + row_asg // TOP_K, n_assign)
    row_idx = jnp.concatenate([row_tok, row_dst], axis=1)
    block_start = jnp.arange(n_blocks, dtype=I32) * MOE_BLOCK
    block_e = jnp.minimum(jnp.sum((pad_ends[None, :] <= block_start[:, None]).astype(I32), axis=1),
                          N_EXPERTS - 1)
    n_used = (pad_ends[-1:] // MOE_BLOCK).astype(I32)

    yk = _experts(h2, row_idx, block_e, n_used, w_gate_up[lyr], b_gate_up[lyr], w_down[lyr],
                  b_down[lyr], n_assign)
    out = _finish(yk, gate, x1, mod, norm_post_ffn[lyr][None], seq, tm=256)
    return out.reshape(bsz, seq, d)
```

```python
import functools
import math

import jax
import jax.numpy as jnp
from jax import lax
from jax.experimental import pallas as pl
from jax.experimental.pallas import tpu as pltpu

F32 = jnp.float32
BF16 = jnp.bfloat16
I32 = jnp.int32

D_MODEL = 1024
CHUNK = 64
EPS = 1e-6
ROPE_THETA = 10000.0
DA_HEADS = 4
DA_HEAD_DIM = 64
HG_HEADS = 4
HEAD_W = 128
BRANCH_W = 512
V_ROWS = HEAD_W + 16
ATTN_STRIP = 256
N_EXPERTS = 32
TOP_K = 4
SWIGLU_LIMIT = 7.0
SWIGLU_ALPHA = 1.702
MOE_BLOCK = 512
LANES = 128
SUBLANES = 8
NEG_BIG = -1e30

VMEM_LIMIT = 56 * 1024 * 1024


def _cparams(sem):
    return pltpu.CompilerParams(dimension_semantics=sem, vmem_limit_bytes=VMEM_LIMIT)


def _rms(x, w):
    return x * lax.rsqrt(jnp.mean(x * x, axis=-1, keepdims=True) + EPS) * w


def _sigmoid(x):
    return 1.0 / (1.0 + jnp.exp(-x))


def _split3(x):
    a = x.astype(BF16)
    r = x - a.astype(F32)
    b = r.astype(BF16)
    c = (r - b.astype(F32)).astype(BF16)
    return a, b, c


def _mod_kernel(c_ref, w_ref, b_ref, o_ref):
    c = c_ref[...]
    ca = c * _sigmoid(c)
    o_ref[...] = jnp.dot(ca, w_ref[...], preferred_element_type=F32,
                         precision=lax.Precision.HIGHEST) + b_ref[...]


def _modulation(c, w_mod, b_mod):
    bsz, d = c.shape
    n = w_mod.shape[1]
    tn = 1536
    return pl.pallas_call(
        _mod_kernel,
        grid=(n // tn,),
        in_specs=[pl.BlockSpec((bsz, d), lambda j: (0, 0)),
                  pl.BlockSpec((d, tn), lambda j: (0, j)),
                  pl.BlockSpec((1, tn), lambda j: (0, j))],
        out_specs=pl.BlockSpec((bsz, tn), lambda j: (0, j)),
        out_shape=jax.ShapeDtypeStruct((bsz, n), F32),
        compiler_params=_cparams(("arbitrary",)),
        name="modulation",
    )(c, w_mod, b_mod.reshape(1, n))


def _rope(x, cos, sin_lo, sin_hi):
    return x * cos + pltpu.roll(x, 96, 1) * sin_lo + pltpu.roll(x, 32, 1) * sin_hi


def _inproj_kernel(x_ref, mod_ref, nw_ref, w_ref, cos_ref, slo_ref, shi_ref,
                   qT_ref, k_ref, vT_ref, hq_ref, hf_ref, hi_ref, hg_ref, ga_ref, gb_ref):
    x = x_ref[...]
    sh1 = mod_ref[0:1, :]
    sc1 = mod_ref[1:2, :]
    h = (_rms(x, nw_ref[...]) * (1.0 + sc1) + sh1).astype(BF16)
    cos, slo, shi = cos_ref[...], slo_ref[...], shi_ref[...]
    scale = DA_HEAD_DIM ** -0.5 * math.log2(math.e)

    def seg(i, width=BRANCH_W):
        return jnp.dot(h, w_ref[:, i:i + width], preferred_element_type=F32)

    q = seg(0)
    k = seg(BRANCH_W)
    for hd in range(DA_HEADS):
        sl = slice(hd * HEAD_W, (hd + 1) * HEAD_W)
        qT_ref[sl, :] = (_rope(q[:, sl], cos, slo, shi) * scale).T.astype(BF16)
        k_ref[:, sl] = _rope(k[:, sl], cos, slo, shi).astype(BF16)
    vT = seg(2 * BRANCH_W).T.astype(BF16)
    ones = jnp.ones((V_ROWS - HEAD_W, vT.shape[1]), BF16)
    for hd in range(DA_HEADS):
        vT_ref[hd * V_ROWS:hd * V_ROWS + HEAD_W, :] = vT[hd * HEAD_W:(hd + 1) * HEAD_W]
        vT_ref[hd * V_ROWS + HEAD_W:(hd + 1) * V_ROWS, :] = ones
    hq = seg(3 * BRANCH_W)
    hq_ref[...] = (hq * _sigmoid(hq)).astype(BF16)
    hf_ref[...] = seg(4 * BRANCH_W)
    hi_ref[...] = seg(5 * BRANCH_W).astype(BF16)
    hg = seg(6 * BRANCH_W)
    hg_ref[...] = (hg * _sigmoid(hg)).astype(BF16)
    for j in range(2):
        ga_ref[:, j * BRANCH_W:(j + 1) * BRANCH_W] = _sigmoid(seg((7 + j) * BRANCH_W)).astype(BF16)
        gb_ref[:, j * BRANCH_W:(j + 1) * BRANCH_W] = _sigmoid(seg((9 + j) * BRANCH_W)).astype(BF16)


def _in_projection(xf, mod, norm_w, w_in_bf, cos, slo, shi, seq, tm):
    t, d = xf.shape
    tiles_per_batch = seq // tm
    bspec = lambda w: pl.BlockSpec((tm, w), lambda i: (i, 0))
    tab = pl.BlockSpec((tm, LANES), lambda i: (i % tiles_per_batch, 0))
    widths = [BRANCH_W] * 7 + [D_MODEL, D_MODEL]
    dtypes = [BF16, BF16, BF16, BF16, F32, BF16, BF16, BF16, BF16]
    out_specs = [bspec(w) for w in widths]
    out_shape = [jax.ShapeDtypeStruct((t, w), dt) for w, dt in zip(widths, dtypes)]
    for idx, rows in ((0, BRANCH_W), (2, DA_HEADS * V_ROWS)):
        out_specs[idx] = pl.BlockSpec((None, rows, tm),
                                      lambda i: (i // tiles_per_batch, 0, i % tiles_per_batch))
        out_shape[idx] = jax.ShapeDtypeStruct((t // seq, rows, seq), BF16)
    return pl.pallas_call(
        _inproj_kernel,
        grid=(t // tm,),
        in_specs=[bspec(d),
                  pl.BlockSpec((None, 6, d), lambda i: (i // tiles_per_batch, 0, 0)),
                  pl.BlockSpec((1, d), lambda i: (0, 0)),
                  pl.BlockSpec(w_in_bf.shape, lambda i: (0, 0), pipeline_mode=pl.Buffered(1)),
                  tab, tab, tab],
        out_specs=out_specs,
        out_shape=out_shape,
        compiler_params=_cparams(("parallel",)),
        name="in_projection",
    )(xf, mod, norm_w, w_in_bf, cos, slo, shi)


def _attn_kernel(lq1_ref, lk1_ref, lq2_ref, lk2_ref, subln_ref, qT_ref, k_ref, vT_ref, o_ref,
                 qq_ref, s0_ref, s1_ref, p0_ref, p1_ref, a0_ref, a1_ref, m_ref, acc_ref,
                 *, tq, tk, lambda_init):
    qi = pl.program_id(2)
    ndiag = tq // tk
    s_refs, p_refs, a_refs = (s0_ref, s1_ref), (p0_ref, p1_ref), (a0_ref, a1_ref)
    qT = qT_ref[...]
    feat = lax.broadcasted_iota(I32, qT.shape, 0)
    zero = jnp.zeros_like(qT)
    qq_ref[:, :tq] = jnp.where(feat < DA_HEAD_DIM, qT, zero)
    qq_ref[:, tq:] = jnp.where(feat >= DA_HEAD_DIM, qT, zero)

    strips = [slice(c * ATTN_STRIP, (c + 1) * ATTN_STRIP) for c in range(2 * tq // ATTN_STRIP)]

    def scores(kb, dst, cs):
        start = pl.multiple_of(kb * tk, tk)
        dst[:, cs] = jnp.dot(k_ref[pl.ds(start, tk), :], qq_ref[:, cs], preferred_element_type=F32)

    def accumulate(kb, slot, cs):
        start = pl.multiple_of(kb * tk, tk)
        pv = jnp.dot(vT_ref[:, pl.ds(start, tk)], p_refs[slot][:, cs], preferred_element_type=F32)
        acc_ref[:, cs] = a_refs[slot][:, cs] * acc_ref[:, cs] + pv

    def softmax(slot, cs, diag):
        s = s_refs[slot][:, cs]
        if diag is not None:
            key = lax.broadcasted_iota(I32, s.shape, 0) + diag * tk
            qry = (lax.broadcasted_iota(I32, s.shape, 1) + cs.start) % tq
            s = jnp.where(key // CHUNK <= qry // CHUNK, s, NEG_BIG)
        m_old = m_ref[:, cs]
        m_new = jnp.maximum(m_old, jnp.max(s, axis=0, keepdims=True))
        m_ref[:, cs] = m_new
        a_refs[slot][:, cs] = jnp.exp2(m_old - m_new)
        p_refs[slot][:, cs] = jnp.exp2((s - m_new).astype(BF16))

    def stage(kb, slot, diag=None, last=False):
        for cs in strips:
            accumulate(jnp.maximum(kb - 1, 0), 1 - slot, cs)
            softmax(slot, cs, diag)
            if not last:
                scores(kb + 1, s_refs[1 - slot], cs)

    p1_ref[...] = jnp.zeros_like(p1_ref)
    a1_ref[...] = jnp.ones_like(a1_ref)
    m_ref[...] = jnp.full(m_ref.shape, NEG_BIG, F32)
    acc_ref[...] = jnp.zeros_like(acc_ref)
    for cs in strips:
        scores(0, s0_ref, cs)

    def pair(i, c):
        stage(2 * i, 0)
        stage(2 * i + 1, 1)
        return c

    first_diag = qi * ndiag
    lax.fori_loop(0, first_diag // 2, pair, 0)
    for dg in range(ndiag):
        stage(first_diag + dg, dg % 2, diag=dg, last=dg == ndiag - 1)
    for cs in strips:
        accumulate(first_diag + ndiag - 1, (ndiag - 1) % 2, cs)

    acc = acc_ref[...]
    o = acc[:HEAD_W] / acc[HEAD_W:HEAD_W + 1]
    lam = (jnp.exp(jnp.sum(lq1_ref[...] * lk1_ref[...], axis=1, keepdims=True))
           - jnp.exp(jnp.sum(lq2_ref[...] * lk2_ref[...], axis=1, keepdims=True))
           + lambda_init)
    o = o[:, :tq] - lam * o[:, tq:]
    o = o * lax.rsqrt(jnp.mean(o * o, axis=0, keepdims=True) + EPS) * subln_ref[...]
    o_ref[...] = (o * (1.0 - lambda_init)).T.astype(BF16)


def _diff_attention(qT, k, vT, lq1, lk1, lq2, lk2, subln, bsz, seq, lambda_init, tq, tk):
    assert tq % (2 * tk) == 0
    k3 = k.reshape(bsz, seq, BRANCH_W)
    small = lambda n: pl.BlockSpec((1, n), lambda b, h, i: (0, 0))
    out = pl.pallas_call(
        functools.partial(_attn_kernel, tq=tq, tk=tk, lambda_init=lambda_init),
        grid=(bsz, DA_HEADS, seq // tq),
        in_specs=[small(DA_HEAD_DIM)] * 4 + [
            pl.BlockSpec((HEAD_W, 1), lambda b, h, i: (0, 0)),
            pl.BlockSpec((None, HEAD_W, tq), lambda b, h, i: (b, h, i)),
            pl.BlockSpec((None, seq, HEAD_W), lambda b, h, i: (b, 0, h)),
            pl.BlockSpec((None, V_ROWS, seq), lambda b, h, i: (b, h, 0))],
        out_specs=pl.BlockSpec((None, tq, HEAD_W), lambda b, h, i: (b, i, h)),
        out_shape=jax.ShapeDtypeStruct((bsz, seq, BRANCH_W), BF16),
        scratch_shapes=[pltpu.VMEM((HEAD_W, 2 * tq), BF16),
                        pltpu.VMEM((tk, 2 * tq), F32), pltpu.VMEM((tk, 2 * tq), F32),
                        pltpu.VMEM((tk, 2 * tq), BF16), pltpu.VMEM((tk, 2 * tq), BF16),
                        pltpu.VMEM((1, 2 * tq), F32), pltpu.VMEM((1, 2 * tq), F32),
                        pltpu.VMEM((1, 2 * tq), F32), pltpu.VMEM((V_ROWS, 2 * tq), F32)],
        compiler_params=_cparams(("parallel", "parallel", "arbitrary")),
        name="diff_attention",
    )(lq1, lk1, lq2, lk2, subln.reshape(HEAD_W, 1), qT, k3, vT)
    return out.reshape(bsz * seq, BRANCH_W)


def _hgrn_kernel(lbl_ref, nw_ref, q_ref, f_ref, i_ref, g_ref, o_ref, st_ref, *, tc, nh):
    @pl.when(pl.program_id(2) == 0)
    def _():
        st_ref[...] = jnp.zeros_like(st_ref)

    row = lax.broadcasted_iota(I32, (CHUNK, CHUNK), 0)
    col = lax.broadcasted_iota(I32, (CHUNK, CHUNK), 1)
    causal = col <= row
    tril = jnp.where(causal, 1.0, 0.0).astype(BF16)
    nt = (((1,), (1,)), ((), ()))
    tn = (((0,), (0,)), ((), ()))

    nchunk = tc // CHUNK
    units = [(hh, c) for hh in range(nh) for c in range(nchunk)]
    rows = lambda c: slice(c * CHUNK, (c + 1) * CHUNK)
    q, k, v, parts = [], [], [], []
    for hh in range(nh):
        hs = slice(hh * HEAD_W, (hh + 1) * HEAD_W)
        lg = lbl_ref[:, hs]
        ex = jnp.exp(lg - jnp.max(lg, axis=0, keepdims=True))
        lb = ex[0:1] / jnp.sum(ex, axis=0, keepdims=True)
        f = lb + (1.0 - lb) * _sigmoid(f_ref[:, hs])
        q.append(q_ref[:, hs].astype(F32))
        k.append(1.0 - f)
        v.append(i_ref[:, hs])
        parts.append(_split3(jnp.log(f)))
    gall = sum(jnp.dot(tril, jnp.concatenate([parts[hh][p][rows(c)] for hh, c in units], axis=1),
                       preferred_element_type=F32) for p in range(3))
    gcum = [gall[:, u * HEAD_W:(u + 1) * HEAD_W] for u in range(len(units))]
    gtot = [g[CHUNK - 1:CHUNK, :] for g in gcum]
    q_t = [(q[hh][rows(c)] * jnp.exp(gcum[u])).astype(BF16) for u, (hh, c) in enumerate(units)]
    k_t = [(k[hh][rows(c)] * jnp.exp(-gcum[u])).astype(BF16) for u, (hh, c) in enumerate(units)]
    k_dec = [(k[hh][rows(c)] * jnp.exp(gtot[u] - gcum[u])).astype(BF16)
             for u, (hh, c) in enumerate(units)]
    a = [lax.dot_general(q_t[u], k_t[u], nt, preferred_element_type=F32) for u in range(len(units))]
    ds = [lax.dot_general(v[hh][rows(c)], k_dec[u], tn, preferred_element_type=F32)
          for u, (hh, c) in enumerate(units)]
    a = [jnp.where(causal, x, 0.0).astype(BF16) for x in a]
    st_in = []
    for hh in range(nh):
        st = st_ref[hh]
        for c in range(nchunk):
            u = hh * nchunk + c
            st_in.append(st.astype(BF16))
            st = jnp.exp(gtot[u]) * st + ds[u]
        st_ref[hh] = st
    outs = [jnp.dot(a[u], v[hh][rows(c)], preferred_element_type=F32)
            + lax.dot_general(q_t[u], st_in[u], nt, preferred_element_type=F32)
            for u, (hh, c) in enumerate(units)]
    for hh in range(nh):
        hs = slice(hh * HEAD_W, (hh + 1) * HEAD_W)
        o = jnp.concatenate(outs[hh * nchunk:(hh + 1) * nchunk], axis=0)
        o_ref[:, hs] = (_rms(o, nw_ref[...]) * g_ref[:, hs].astype(F32)).astype(BF16)


def _hgrn2(hq, hf, hi, hg, lb_logits, norm_w, bsz, seq, tc, nh):
    r3 = lambda a: a.reshape(bsz, seq, BRANCH_W)
    blk = pl.BlockSpec((None, tc, nh * HEAD_W), lambda b, h, i: (b, i, h))
    out = pl.pallas_call(
        functools.partial(_hgrn_kernel, tc=tc, nh=nh),
        grid=(bsz, HG_HEADS // nh, seq // tc),
        in_specs=[pl.BlockSpec((2, nh * HEAD_W), lambda b, h, i: (0, h)),
                  pl.BlockSpec((1, HEAD_W), lambda b, h, i: (0, 0)),
                  blk, blk, blk, blk],
        out_specs=blk,
        out_shape=jax.ShapeDtypeStruct((bsz, seq, BRANCH_W), BF16),
        scratch_shapes=[pltpu.VMEM((nh, HEAD_W, HEAD_W), F32)],
        compiler_params=_cparams(("parallel", "parallel", "arbitrary")),
        name="hgrn2",
    )(lb_logits, norm_w, r3(hq), r3(hf), r3(hi), r3(hg))
    return out.reshape(bsz * seq, BRANCH_W)


def _merge_kernel(x_ref, mod_ref, ya_ref, yb_ref, ga_ref, gb_ref, wa_ref, wb_ref, wo_ref,
                  npost_ref, npre_ref, rw_ref, rb_ref,
                  x1_ref, h2_ref, topi_ref, gate_ref, rank_ref, cnt_ref, *, tm):
    i = pl.program_id(0)

    @pl.when(i == 0)
    def _():
        cnt_ref[...] = jnp.zeros_like(cnt_ref)

    g1 = mod_ref[2:3, :]
    sh2 = mod_ref[3:4, :]
    sc2 = mod_ref[4:5, :]
    a = jnp.dot(ya_ref[...], wa_ref[...], preferred_element_type=F32)
    b = jnp.dot(yb_ref[...], wb_ref[...], preferred_element_type=F32)
    merged = ga_ref[...] * a.astype(BF16) + gb_ref[...] * b.astype(BF16)
    y = jnp.dot(merged, wo_ref[...], preferred_element_type=F32)
    x1 = x_ref[...] + g1 * _rms(y, npost_ref[...])
    x1_ref[...] = x1
    h2 = _rms(x1, npre_ref[...]) * (1.0 + sc2) + sh2
    h2_ref[...] = h2

    logits = jnp.dot(h2.astype(BF16), rw_ref[...], preferred_element_type=F32) + rb_ref[...]
    lane = lax.broadcasted_iota(I32, logits.shape, 1)
    work = logits
    vals, idxs = [], []
    for _ in range(TOP_K):
        mx = jnp.max(work, axis=1, keepdims=True)
        idx = jnp.min(jnp.where(work == mx, lane, LANES), axis=1, keepdims=True)
        vals.append(mx)
        idxs.append(idx)
        work = jnp.where(lane == idx, -jnp.inf, work)
    es = [jnp.exp(vk - vals[0]) for vk in vals]
    den = es[0] + es[1] + es[2] + es[3]

    onehot = jnp.zeros(logits.shape, F32)
    for idx in idxs:
        onehot = jnp.where(lane == idx, 1.0, onehot)
    row = lax.broadcasted_iota(I32, (tm, tm), 0)
    col = lax.broadcasted_iota(I32, (tm, tm), 1)
    strict = jnp.where(col < row, 1.0, 0.0).astype(BF16)
    prefix = jnp.dot(strict, onehot.astype(BF16), preferred_element_type=F32) + cnt_ref[...]
    cnt_ref[...] = cnt_ref[...] + jnp.sum(onehot, axis=0, keepdims=True)

    topi = jnp.zeros(logits.shape, I32)
    gate = jnp.zeros(logits.shape, F32)
    rank = jnp.zeros(logits.shape, I32)
    for kk in range(TOP_K):
        rk = jnp.sum(jnp.where(lane == idxs[kk], prefix, 0.0), axis=1, keepdims=True)
        topi = jnp.where(lane == kk, idxs[kk], topi)
        gate = jnp.where(lane == kk, es[kk] / den, gate)
        rank = jnp.where(lane == kk, rk.astype(I32), rank)
    topi_ref[...] = topi
    gate_ref[...] = gate
    rank_ref[...] = rank


def _merge_route(xf, mod, ya, yb, ga, gb, wa, wb, wo, npost, npre, rw, rb, seq, tm):
    t, d = xf.shape
    tiles_per_batch = seq // tm
    row = lambda w: pl.BlockSpec((tm, w), lambda i: (i, 0))
    const = lambda a: pl.BlockSpec(a.shape, lambda i: (0,) * a.ndim)
    lane_out = jax.ShapeDtypeStruct((t, LANES), I32)
    return pl.pallas_call(
        functools.partial(_merge_kernel, tm=tm),
        grid=(t // tm,),
        in_specs=[row(d),
                  pl.BlockSpec((None, 6, d), lambda i: (i // tiles_per_batch, 0, 0)),
                  row(BRANCH_W), row(BRANCH_W), row(d), row(d),
                  const(wa), const(wb), const(wo), const(npost), const(npre),
                  const(rw), const(rb)],
        out_specs=[row(d), row(d), row(LANES), row(LANES), row(LANES),
                   pl.BlockSpec((1, LANES), lambda i: (0, 0))],
        out_shape=[jax.ShapeDtypeStruct((t, d), F32), jax.ShapeDtypeStruct((t, d), F32),
                   lane_out, jax.ShapeDtypeStruct((t, LANES), F32), lane_out,
                   jax.ShapeDtypeStruct((1, LANES), F32)],
        compiler_params=_cparams(("arbitrary",)),
        name="merge_route",
    )(xf, mod, ya, yb, ga, gb, wa, wb, wo, npost, npre, rw, rb)


def _dispatch_kernel(pend_ref, padded_ref, dest_hbm, h_ref, xs_out, dsm, zbuf, isem, zsem, sem,
                     *, tm):
    i = pl.program_id(0)
    n = tm * TOP_K
    cp = pltpu.make_async_copy(dest_hbm.at[pl.ds(pl.multiple_of(i * n, n), n)], dsm, isem)
    cp.start()

    def tail_copy(e):
        start = pl.multiple_of(pend_ref[e] - MOE_BLOCK, MOE_BLOCK)
        return pltpu.make_async_copy(zbuf, xs_out.at[pl.ds(start, MOE_BLOCK)], zsem)

    n_blocks = xs_out.shape[0] // MOE_BLOCK
    n_used = pend_ref[N_EXPERTS - 1] // MOE_BLOCK

    def unused_copy(blk):
        return pltpu.make_async_copy(zbuf, xs_out.at[pl.ds(blk * MOE_BLOCK, MOE_BLOCK)], zsem)

    @pl.when(i == 0)
    def _():
        zbuf[...] = jnp.zeros_like(zbuf)
        for e in range(N_EXPERTS):
            @pl.when(padded_ref[e] > 0)
            def _():
                tail_copy(e).start()
        for blk in range(n_blocks - N_EXPERTS, n_blocks):
            @pl.when(blk >= n_used)
            def _():
                unused_copy(blk).start()
        for e in range(N_EXPERTS):
            @pl.when(padded_ref[e] > 0)
            def _():
                tail_copy(e).wait()
        for blk in range(n_blocks - N_EXPERTS, n_blocks):
            @pl.when(blk >= n_used)
            def _():
                unused_copy(blk).wait()

    cp.wait()

    def issue(r8, c):
        base = pl.multiple_of(r8 * SUBLANES, SUBLANES)
        for ri in range(SUBLANES):
            for kk in range(TOP_K):
                dst = dsm[(base + ri) * TOP_K + kk]
                pltpu.make_async_copy(h_ref.at[pl.ds(base + ri, 1)],
                                      xs_out.at[pl.ds(dst, 1)], sem).start()
        return c

    lax.fori_loop(0, tm // SUBLANES, issue, 0)
    for _ in range(TOP_K):
        pltpu.make_async_copy(h_ref, xs_out.at[pl.ds(0, tm)], sem).wait()


def _dispatch(h2, dest, pad_ends, padded, n_rows, tm):
    t, d = h2.shape
    grid_spec = pltpu.PrefetchScalarGridSpec(
        num_scalar_prefetch=2,
        grid=(t // tm,),
        in_specs=[pl.BlockSpec(memory_space=pl.ANY),
                  pl.BlockSpec((tm, d), lambda i, pe, pd: (i, 0))],
        out_specs=pl.BlockSpec(memory_space=pl.ANY),
        scratch_shapes=[pltpu.SMEM((tm * TOP_K,), I32),
                        pltpu.VMEM((MOE_BLOCK, d), h2.dtype),
                        pltpu.SemaphoreType.DMA, pltpu.SemaphoreType.DMA, pltpu.SemaphoreType.DMA],
    )
    return pl.pallas_call(
        functools.partial(_dispatch_kernel, tm=tm),
        grid_spec=grid_spec,
        out_shape=jax.ShapeDtypeStruct((n_rows, d), h2.dtype),
        compiler_params=_cparams(("arbitrary",)),
        name="moe_dispatch",
    )(pad_ends, padded, dest, h2)


def _expert_kernel(be_ref, nu_ref, x_ref, wgu_ref, bgu_ref, wd_ref, bd_ref, y_ref,
                   wgu_bf, wd_bf):
    j = pl.program_id(0)

    @pl.when((j == 0) | (be_ref[j] != be_ref[jnp.maximum(j - 1, 0)]))
    def _():
        wgu_bf[...] = wgu_ref[...].astype(BF16)
        wd_bf[...] = wd_ref[...].astype(BF16)

    @pl.when(j < nu_ref[0])
    def _():
        x = x_ref[...].astype(BF16)
        gu = jnp.dot(x, wgu_bf[...], preferred_element_type=F32) + bgu_ref[...]
        g = jnp.minimum(gu[:, :D_MODEL], SWIGLU_LIMIT)
        u = jnp.clip(gu[:, D_MODEL:], -SWIGLU_LIMIT, SWIGLU_LIMIT)
        act = (u + 1.0) * (g * _sigmoid(g * SWIGLU_ALPHA))
        y_ref[...] = jnp.dot(act.astype(BF16), wd_bf[...], preferred_element_type=F32) + bd_ref[...]

    @pl.when(j >= nu_ref[0])
    def _():
        y_ref[...] = jnp.zeros_like(y_ref)


def _experts(xs, block_e, n_used, wgu, bgu, wd, bd):
    p, d = xs.shape
    n_blocks = p // MOE_BLOCK
    e = wgu.shape[0]
    grid_spec = pltpu.PrefetchScalarGridSpec(
        num_scalar_prefetch=2,
        grid=(n_blocks,),
        in_specs=[pl.BlockSpec((MOE_BLOCK, d), lambda j, be, nu: (jnp.minimum(j, nu[0] - 1), 0)),
                  pl.BlockSpec((None, d, 2 * d), lambda j, be, nu: (be[j], 0, 0)),
                  pl.BlockSpec((None, 1, 2 * d), lambda j, be, nu: (be[j], 0, 0)),
                  pl.BlockSpec((None, d, d), lambda j, be, nu: (be[j], 0, 0)),
                  pl.BlockSpec((None, 1, d), lambda j, be, nu: (be[j], 0, 0))],
        out_specs=pl.BlockSpec((MOE_BLOCK, d), lambda j, be, nu: (j, 0)),
        scratch_shapes=[pltpu.VMEM((d, 2 * d), BF16), pltpu.VMEM((d, d), BF16)],
    )
    return pl.pallas_call(
        _expert_kernel,
        grid_spec=grid_spec,
        out_shape=jax.ShapeDtypeStruct((p, d), F32),
        compiler_params=_cparams(("arbitrary",)),
        name="moe_experts",
    )(block_e, n_used, xs, wgu, bgu.reshape(e, 1, 2 * d), wd, bd.reshape(e, 1, d))


def _combine_kernel(dest_hbm, y_hbm, gate_ref, x1_ref, mod_ref, nw_ref, o_ref,
                    dsm0, dsm1, buf0, buf1, isem, sem, *, tm):
    i = pl.program_id(0)
    n = tm * TOP_K
    dsms, bufs = (dsm0, dsm1), (buf0, buf1)

    def start_gather(tile, slot):
        dsm, buf = dsms[slot], bufs[slot]
        cp = pltpu.make_async_copy(dest_hbm.at[pl.ds(pl.multiple_of(tile * n, n), n)], dsm, isem)
        cp.start()
        cp.wait()

        def issue(r8, c):
            base = pl.multiple_of(r8 * SUBLANES, SUBLANES)
            for ri in range(SUBLANES):
                for kk in range(TOP_K):
                    src = dsm[(base + ri) * TOP_K + kk]
                    pltpu.make_async_copy(y_hbm.at[pl.ds(src, 1)],
                                          buf.at[kk, pl.ds(base + ri, 1)], sem.at[slot]).start()
            return c

        lax.fori_loop(0, tm // SUBLANES, issue, 0)

    def finish(slot):
        buf = bufs[slot]
        for kk in range(TOP_K):
            pltpu.make_async_copy(y_hbm.at[pl.ds(0, tm)], buf.at[kk], sem.at[slot]).wait()
        gate = gate_ref[...]
        moe = gate[:, 0:1] * buf[0]
        for kk in range(1, TOP_K):
            moe = moe + gate[:, kk:kk + 1] * buf[kk]
        g2 = mod_ref[5:6, :]
        o_ref[...] = x1_ref[...] + g2 * _rms(moe, nw_ref[...])

    @pl.when(i == 0)
    def _():
        start_gather(0, 0)

    for slot in range(2):
        @pl.when((i + 1 < pl.num_programs(0)) & ((i + 1) % 2 == slot))
        def _():
            start_gather(i + 1, slot)

    for slot in range(2):
        @pl.when(i % 2 == slot)
        def _():
            finish(slot)


def _combine(dest, y, gate, x1, mod, nw, seq, tm):
    t, d = x1.shape
    tiles_per_batch = seq // tm
    return pl.pallas_call(
        functools.partial(_combine_kernel, tm=tm),
        grid=(t // tm,),
        in_specs=[pl.BlockSpec(memory_space=pl.ANY),
                  pl.BlockSpec(memory_space=pl.ANY),
                  pl.BlockSpec((tm, LANES), lambda i: (i, 0)),
                  pl.BlockSpec((tm, d), lambda i: (i, 0)),
                  pl.BlockSpec((None, 6, d), lambda i: (i // tiles_per_batch, 0, 0)),
                  pl.BlockSpec((1, d), lambda i: (0, 0))],
        out_specs=pl.BlockSpec((tm, d), lambda i: (i, 0)),
        out_shape=jax.ShapeDtypeStruct((t, d), F32),
        scratch_shapes=[pltpu.SMEM((tm * TOP_K,), I32), pltpu.SMEM((tm * TOP_K,), I32),
                        pltpu.VMEM((TOP_K, tm, d), F32), pltpu.VMEM((TOP_K, tm, d), F32),
                        pltpu.SemaphoreType.DMA, pltpu.SemaphoreType.DMA((2,))],
        compiler_params=_cparams(("arbitrary",)),
        name="moe_combine",
    )(dest, y, gate, x1, mod, nw)


def _rope_tables(seq):
    pos = jnp.arange(seq, dtype=F32)
    inv = ROPE_THETA ** (-jnp.arange(0, DA_HEAD_DIM, 2, dtype=F32) / DA_HEAD_DIM)
    ang = pos[:, None] * inv[None, :]
    cos, sin = jnp.cos(ang), jnp.sin(ang)
    zero = jnp.zeros_like(sin)
    cos_t = jnp.tile(cos, (1, LANES // cos.shape[1]))
    sin_lo = jnp.tile(jnp.concatenate([-sin, zero], axis=1), (1, 2))
    sin_hi = jnp.tile(jnp.concatenate([zero, sin], axis=1), (1, 2))
    return cos_t, sin_lo, sin_hi


def kernel(x, c, w_mod, b_mod, norm_pre_mix, norm_post_mix, w_in, da_lambda_q1, da_lambda_k1,
           da_lambda_q2, da_lambda_k2, da_subln, hg_lb_logits, hg_norm, w_branch_a, w_branch_b,
           w_out, norm_pre_ffn, norm_post_ffn, router_w, router_b, w_gate_up, b_gate_up,
           w_down, b_down):
    bsz, seq, d = x.shape
    depth = w_mod.shape[0]
    assert depth == 1 and d == D_MODEL
    t = bsz * seq
    lyr = 0
    lambda_init = 0.8 - 0.6 * math.exp(-0.3 * lyr)
    cos, slo, shi = _rope_tables(seq)

    xf = x.reshape(t, d)
    mod = _modulation(c, w_mod[lyr], b_mod[lyr]).reshape(bsz, 6, d)

    qT, k, vT, hq, hf, hi, hg, ga, gb = _in_projection(
        xf, mod, norm_pre_mix[lyr][None], w_in[lyr].astype(BF16), cos, slo, shi, seq, tm=512)
    ya = _diff_attention(qT, k, vT, da_lambda_q1[lyr][None], da_lambda_k1[lyr][None],
                         da_lambda_q2[lyr][None], da_lambda_k2[lyr][None], da_subln[lyr],
                         bsz, seq, lambda_init, tq=1024, tk=512)
    yb = _hgrn2(hq, hf, hi, hg, hg_lb_logits[lyr:lyr + 2], hg_norm[lyr][None], bsz, seq, tc=512, nh=2)

    rw = jnp.zeros((d, LANES), BF16).at[:, :N_EXPERTS].set(router_w[lyr].astype(BF16))
    rb = jnp.full((1, LANES), NEG_BIG, F32).at[0, :N_EXPERTS].set(router_b[lyr])
    x1, h2, topi, gate, rank, counts = _merge_route(
        xf, mod, ya, yb, ga, gb, w_branch_a[lyr].astype(BF16), w_branch_b[lyr].astype(BF16),
        w_out[lyr].astype(BF16), norm_post_mix[lyr][None], norm_pre_ffn[lyr][None], rw, rb,
        seq, tm=512)

    counts = counts[0, :N_EXPERTS].astype(I32)
    padded = ((counts + MOE_BLOCK - 1) // MOE_BLOCK) * MOE_BLOCK
    pad_ends = jnp.cumsum(padded)
    pad_starts = pad_ends - padded
    n_assign = t * TOP_K
    n_rows = ((n_assign + MOE_BLOCK - 1) // MOE_BLOCK) * MOE_BLOCK + N_EXPERTS * MOE_BLOCK
    n_blocks = n_rows // MOE_BLOCK
    dest = (pad_starts[topi[:, :TOP_K]] + rank[:, :TOP_K]).reshape(n_assign)
    block_start = jnp.arange(n_blocks, dtype=I32) * MOE_BLOCK
    block_e = jnp.minimum(jnp.sum((pad_ends[None, :] <= block_start[:, None]).astype(I32), axis=1),
                          N_EXPERTS - 1)
    n_used = (pad_ends[-1:] // MOE_BLOCK).astype(I32)

    xs = _dispatch(h2, dest, pad_ends.astype(I32), padded, n_rows, tm=512)
    y = _experts(xs, block_e, n_used, w_gate_up[lyr], b_gate_up[lyr], w_down[lyr], b_down[lyr])
    out = _combine(dest, y, gate, x1, mod, norm_post_ffn[lyr][None], seq, tm=256)
    return out.reshape(bsz, seq, d)
```

```python
import functools
import math

import jax
import jax.numpy as jnp
from jax import lax
from jax.experimental import pallas as pl
from jax.experimental.pallas import tpu as pltpu

F32 = jnp.float32
BF16 = jnp.bfloat16
I32 = jnp.int32

D_MODEL = 1024
CHUNK = 64
EPS = 1e-6
ROPE_THETA = 10000.0
DA_HEADS = 4
DA_HEAD_DIM = 64
HG_HEADS = 4
HEAD_W = 128
BRANCH_W = 512
V_ROWS = HEAD_W + 16
ATTN_STRIP = 256
N_EXPERTS = 32
TOP_K = 4
SWIGLU_LIMIT = 7.0
SWIGLU_ALPHA = 1.702
MOE_BLOCK = 512
MOE_TILE = 512
LANES = 128
SUBLANES = 8
RUN_ALIGN = SUBLANES
RUN_BITS = (MOE_TILE // RUN_ALIGN).bit_length()
SORT_ROWS = MOE_TILE * TOP_K + N_EXPERTS * RUN_ALIGN
NEG_BIG = -1e30

VMEM_LIMIT = 56 * 1024 * 1024


def _cparams(sem):
    return pltpu.CompilerParams(dimension_semantics=sem, vmem_limit_bytes=VMEM_LIMIT)


def _rms(x, w):
    return x * lax.rsqrt(jnp.mean(x * x, axis=-1, keepdims=True) + EPS) * w


def _sigmoid(x):
    return 1.0 / (1.0 + jnp.exp(-x))


def _split3(x):
    a = x.astype(BF16)
    r = x - a.astype(F32)
    b = r.astype(BF16)
    c = (r - b.astype(F32)).astype(BF16)
    return a, b, c


def _mod_kernel(c_ref, w_ref, b_ref, o_ref):
    c = c_ref[...]
    ca = c * _sigmoid(c)
    o_ref[...] = jnp.dot(ca, w_ref[...], preferred_element_type=F32,
                         precision=lax.Precision.HIGHEST) + b_ref[...]


def _modulation(c, w_mod, b_mod):
    bsz, d = c.shape
    n = w_mod.shape[1]
    tn = 1536
    return pl.pallas_call(
        _mod_kernel,
        grid=(n // tn,),
        in_specs=[pl.BlockSpec((bsz, d), lambda j: (0, 0)),
                  pl.BlockSpec((d, tn), lambda j: (0, j)),
                  pl.BlockSpec((1, tn), lambda j: (0, j))],
        out_specs=pl.BlockSpec((bsz, tn), lambda j: (0, j)),
        out_shape=jax.ShapeDtypeStruct((bsz, n), F32),
        compiler_params=_cparams(("arbitrary",)),
        name="modulation",
    )(c, w_mod, b_mod.reshape(1, n))


def _rope(x, cos, sin_lo, sin_hi):
    return x * cos + pltpu.roll(x, 96, 1) * sin_lo + pltpu.roll(x, 32, 1) * sin_hi


def _inproj_kernel(x_ref, mod_ref, nw_ref, w_ref, cos_ref, slo_ref, shi_ref,
                   qT_ref, k_ref, vT_ref, hq_ref, hf_ref, hi_ref, hg_ref, ga_ref, gb_ref):
    x = x_ref[...]
    sh1 = mod_ref[0:1, :]
    sc1 = mod_ref[1:2, :]
    h = (_rms(x, nw_ref[...]) * (1.0 + sc1) + sh1).astype(BF16)
    cos, slo, shi = cos_ref[...], slo_ref[...], shi_ref[...]
    scale = DA_HEAD_DIM ** -0.5 * math.log2(math.e)

    def seg(i, width=BRANCH_W):
        return jnp.dot(h, w_ref[:, i:i + width], preferred_element_type=F32)

    q = seg(0)
    k = seg(BRANCH_W)
    for hd in range(DA_HEADS):
        sl = slice(hd * HEAD_W, (hd + 1) * HEAD_W)
        qT_ref[sl, :] = (_rope(q[:, sl], cos, slo, shi) * scale).T.astype(BF16)
        k_ref[:, sl] = _rope(k[:, sl], cos, slo, shi).astype(BF16)
    vT = seg(2 * BRANCH_W).T.astype(BF16)
    ones = jnp.ones((V_ROWS - HEAD_W, vT.shape[1]), BF16)
    for hd in range(DA_HEADS):
        vT_ref[hd * V_ROWS:hd * V_ROWS + HEAD_W, :] = vT[hd * HEAD_W:(hd + 1) * HEAD_W]
        vT_ref[hd * V_ROWS + HEAD_W:(hd + 1) * V_ROWS, :] = ones
    hq = seg(3 * BRANCH_W)
    hq_ref[...] = (hq * _sigmoid(hq)).astype(BF16)
    hf_ref[...] = seg(4 * BRANCH_W)
    hi_ref[...] = seg(5 * BRANCH_W).astype(BF16)
    hg = seg(6 * BRANCH_W)
    hg_ref[...] = (hg * _sigmoid(hg)).astype(BF16)
    for j in range(2):
        ga_ref[:, j * BRANCH_W:(j + 1) * BRANCH_W] = _sigmoid(seg((7 + j) * BRANCH_W)).astype(BF16)
        gb_ref[:, j * BRANCH_W:(j + 1) * BRANCH_W] = _sigmoid(seg((9 + j) * BRANCH_W)).astype(BF16)


def _in_projection(xf, mod, norm_w, w_in_bf, cos, slo, shi, seq, tm):
    t, d = xf.shape
    tiles_per_batch = seq // tm
    bspec = lambda w: pl.BlockSpec((tm, w), lambda i: (i, 0))
    tab = pl.BlockSpec((tm, LANES), lambda i: (i % tiles_per_batch, 0))
    widths = [BRANCH_W] * 7 + [D_MODEL, D_MODEL]
    dtypes = [BF16, BF16, BF16, BF16, F32, BF16, BF16, BF16, BF16]
    out_specs = [bspec(w) for w in widths]
    out_shape = [jax.ShapeDtypeStruct((t, w), dt) for w, dt in zip(widths, dtypes)]
    for idx, rows in ((0, BRANCH_W), (2, DA_HEADS * V_ROWS)):
        out_specs[idx] = pl.BlockSpec((None, rows, tm),
                                      lambda i: (i // tiles_per_batch, 0, i % tiles_per_batch))
        out_shape[idx] = jax.ShapeDtypeStruct((t // seq, rows, seq), BF16)
    return pl.pallas_call(
        _inproj_kernel,
        grid=(t // tm,),
        in_specs=[bspec(d),
                  pl.BlockSpec((None, 6, d), lambda i: (i // tiles_per_batch, 0, 0)),
                  pl.BlockSpec((1, d), lambda i: (0, 0)),
                  pl.BlockSpec(w_in_bf.shape, lambda i: (0, 0), pipeline_mode=pl.Buffered(1)),
                  tab, tab, tab],
        out_specs=out_specs,
        out_shape=out_shape,
        compiler_params=_cparams(("parallel",)),
        name="in_projection",
    )(xf, mod, norm_w, w_in_bf, cos, slo, shi)


def _attn_kernel(lq1_ref, lk1_ref, lq2_ref, lk2_ref, subln_ref, qT_ref, k_ref, vT_ref, o_ref,
                 qq_ref, s0_ref, s1_ref, p0_ref, p1_ref, a0_ref, a1_ref, m_ref, acc_ref,
                 *, tq, tk, lambda_init):
    qi = pl.program_id(2)
    ndiag = tq // tk
    s_refs, p_refs, a_refs = (s0_ref, s1_ref), (p0_ref, p1_ref), (a0_ref, a1_ref)
    qT = qT_ref[...]
    feat = lax.broadcasted_iota(I32, qT.shape, 0)
    zero = jnp.zeros_like(qT)
    qq_ref[:, :tq] = jnp.where(feat < DA_HEAD_DIM, qT, zero)
    qq_ref[:, tq:] = jnp.where(feat >= DA_HEAD_DIM, qT, zero)

    strips = [slice(c * ATTN_STRIP, (c + 1) * ATTN_STRIP) for c in range(2 * tq // ATTN_STRIP)]

    def scores(kb, dst, cs):
        start = pl.multiple_of(kb * tk, tk)
        dst[:, cs] = jnp.dot(k_ref[pl.ds(start, tk), :], qq_ref[:, cs], preferred_element_type=F32)

    def accumulate(kb, slot, cs):
        start = pl.multiple_of(kb * tk, tk)
        pv = jnp.dot(vT_ref[:, pl.ds(start, tk)], p_refs[slot][:, cs], preferred_element_type=F32)
        acc_ref[:, cs] = a_refs[slot][:, cs] * acc_ref[:, cs] + pv

    def softmax(slot, cs, diag):
        s = s_refs[slot][:, cs]
        if diag is not None:
            key = lax.broadcasted_iota(I32, s.shape, 0) + diag * tk
            qry = (lax.broadcasted_iota(I32, s.shape, 1) + cs.start) % tq
            s = jnp.where(key // CHUNK <= qry // CHUNK, s, NEG_BIG)
        m_old = m_ref[:, cs]
        m_new = jnp.maximum(m_old, jnp.max(s, axis=0, keepdims=True))
        m_ref[:, cs] = m_new
        a_refs[slot][:, cs] = jnp.exp2(m_old - m_new)
        p_refs[slot][:, cs] = jnp.exp2((s - m_new).astype(BF16))

    def stage(kb, slot, diag=None, last=False):
        for cs in strips:
            accumulate(jnp.maximum(kb - 1, 0), 1 - slot, cs)
            softmax(slot, cs, diag)
            if not last:
                scores(kb + 1, s_refs[1 - slot], cs)

    p1_ref[...] = jnp.zeros_like(p1_ref)
    a1_ref[...] = jnp.ones_like(a1_ref)
    m_ref[...] = jnp.full(m_ref.shape, NEG_BIG, F32)
    acc_ref[...] = jnp.zeros_like(acc_ref)
    for cs in strips:
        scores(0, s0_ref, cs)

    def pair(i, c):
        stage(2 * i, 0)
        stage(2 * i + 1, 1)
        return c

    first_diag = qi * ndiag
    lax.fori_loop(0, first_diag // 2, pair, 0)
    for dg in range(ndiag):
        stage(first_diag + dg, dg % 2, diag=dg, last=dg == ndiag - 1)
    for cs in strips:
        accumulate(first_diag + ndiag - 1, (ndiag - 1) % 2, cs)

    acc = acc_ref[...]
    o = acc[:HEAD_W] / acc[HEAD_W:HEAD_W + 1]
    lam = (jnp.exp(jnp.sum(lq1_ref[...] * lk1_ref[...], axis=1, keepdims=True))
           - jnp.exp(jnp.sum(lq2_ref[...] * lk2_ref[...], axis=1, keepdims=True))
           + lambda_init)
    o = o[:, :tq] - lam * o[:, tq:]
    o = o * lax.rsqrt(jnp.mean(o * o, axis=0, keepdims=True) + EPS) * subln_ref[...]
    o_ref[...] = (o * (1.0 - lambda_init)).T.astype(BF16)


def _diff_attention(qT, k, vT, lq1, lk1, lq2, lk2, subln, bsz, seq, lambda_init, tq, tk):
    assert tq % (2 * tk) == 0
    k3 = k.reshape(bsz, seq, BRANCH_W)
    small = lambda n: pl.BlockSpec((1, n), lambda b, h, i: (0, 0))
    out = pl.pallas_call(
        functools.partial(_attn_kernel, tq=tq, tk=tk, lambda_init=lambda_init),
        grid=(bsz, DA_HEADS, seq // tq),
        in_specs=[small(DA_HEAD_DIM)] * 4 + [
            pl.BlockSpec((HEAD_W, 1), lambda b, h, i: (0, 0)),
            pl.BlockSpec((None, HEAD_W, tq), lambda b, h, i: (b, h, i)),
            pl.BlockSpec((None, seq, HEAD_W), lambda b, h, i: (b, 0, h)),
            pl.BlockSpec((None, V_ROWS, seq), lambda b, h, i: (b, h, 0))],
        out_specs=pl.BlockSpec((None, tq, HEAD_W), lambda b, h, i: (b, i, h)),
        out_shape=jax.ShapeDtypeStruct((bsz, seq, BRANCH_W), BF16),
        scratch_shapes=[pltpu.VMEM((HEAD_W, 2 * tq), BF16),
                        pltpu.VMEM((tk, 2 * tq), F32), pltpu.VMEM((tk, 2 * tq), F32),
                        pltpu.VMEM((tk, 2 * tq), BF16), pltpu.VMEM((tk, 2 * tq), BF16),
                        pltpu.VMEM((1, 2 * tq), F32), pltpu.VMEM((1, 2 * tq), F32),
                        pltpu.VMEM((1, 2 * tq), F32), pltpu.VMEM((V_ROWS, 2 * tq), F32)],
        compiler_params=_cparams(("parallel", "parallel", "arbitrary")),
        name="diff_attention",
    )(lq1, lk1, lq2, lk2, subln.reshape(HEAD_W, 1), qT, k3, vT)
    return out.reshape(bsz * seq, BRANCH_W)


def _hgrn_kernel(lbl_ref, nw_ref, q_ref, f_ref, i_ref, g_ref, o_ref, st_ref, *, tc, nh):
    @pl.when(pl.program_id(2) == 0)
    def _():
        st_ref[...] = jnp.zeros_like(st_ref)

    row = lax.broadcasted_iota(I32, (CHUNK, CHUNK), 0)
    col = lax.broadcasted_iota(I32, (CHUNK, CHUNK), 1)
    causal = col <= row
    tril = jnp.where(causal, 1.0, 0.0).astype(BF16)
    nt = (((1,), (1,)), ((), ()))
    tn = (((0,), (0,)), ((), ()))

    nchunk = tc // CHUNK
    units = [(hh, c) for hh in range(nh) for c in range(nchunk)]
    rows = lambda c: slice(c * CHUNK, (c + 1) * CHUNK)
    q, k, v, parts = [], [], [], []
    for hh in range(nh):
        hs = slice(hh * HEAD_W, (hh + 1) * HEAD_W)
        lg = lbl_ref[:, hs]
        ex = jnp.exp(lg - jnp.max(lg, axis=0, keepdims=True))
        lb = ex[0:1] / jnp.sum(ex, axis=0, keepdims=True)
        f = lb + (1.0 - lb) * _sigmoid(f_ref[:, hs])
        q.append(q_ref[:, hs].astype(F32))
        k.append(1.0 - f)
        v.append(i_ref[:, hs])
        parts.append(_split3(jnp.log(f)))
    gall = sum(jnp.dot(tril, jnp.concatenate([parts[hh][p][rows(c)] for hh, c in units], axis=1),
                       preferred_element_type=F32) for p in range(3))
    gcum = [gall[:, u * HEAD_W:(u + 1) * HEAD_W] for u in range(len(units))]
    gtot = [g[CHUNK - 1:CHUNK, :] for g in gcum]
    q_t = [(q[hh][rows(c)] * jnp.exp(gcum[u])).astype(BF16) for u, (hh, c) in enumerate(units)]
    k_t = [(k[hh][rows(c)] * jnp.exp(-gcum[u])).astype(BF16) for u, (hh, c) in enumerate(units)]
    k_dec = [(k[hh][rows(c)] * jnp.exp(gtot[u] - gcum[u])).astype(BF16)
             for u, (hh, c) in enumerate(units)]
    a = [lax.dot_general(q_t[u], k_t[u], nt, preferred_element_type=F32) for u in range(len(units))]
    ds = [lax.dot_general(v[hh][rows(c)], k_dec[u], tn, preferred_element_type=F32)
          for u, (hh, c) in enumerate(units)]
    a = [jnp.where(causal, x, 0.0).astype(BF16) for x in a]
    st_in = []
    for hh in range(nh):
        st = st_ref[hh]
        for c in range(nchunk):
            u = hh * nchunk + c
            st_in.append(st.astype(BF16))
            st = jnp.exp(gtot[u]) * st + ds[u]
        st_ref[hh] = st
    outs = [jnp.dot(a[u], v[hh][rows(c)], preferred_element_type=F32)
            + lax.dot_general(q_t[u], st_in[u], nt, preferred_element_type=F32)
            for u, (hh, c) in enumerate(units)]
    for hh in range(nh):
        hs = slice(hh * HEAD_W, (hh + 1) * HEAD_W)
        o = jnp.concatenate(outs[hh * nchunk:(hh + 1) * nchunk], axis=0)
        o_ref[:, hs] = (_rms(o, nw_ref[...]) * g_ref[:, hs].astype(F32)).astype(BF16)


def _hgrn2(hq, hf, hi, hg, lb_logits, norm_w, bsz, seq, tc, nh):
    r3 = lambda a: a.reshape(bsz, seq, BRANCH_W)
    blk = pl.BlockSpec((None, tc, nh * HEAD_W), lambda b, h, i: (b, i, h))
    out = pl.pallas_call(
        functools.partial(_hgrn_kernel, tc=tc, nh=nh),
        grid=(bsz, HG_HEADS // nh, seq // tc),
        in_specs=[pl.BlockSpec((2, nh * HEAD_W), lambda b, h, i: (0, h)),
                  pl.BlockSpec((1, HEAD_W), lambda b, h, i: (0, 0)),
                  blk, blk, blk, blk],
        out_specs=blk,
        out_shape=jax.ShapeDtypeStruct((bsz, seq, BRANCH_W), BF16),
        scratch_shapes=[pltpu.VMEM((nh, HEAD_W, HEAD_W), F32)],
        compiler_params=_cparams(("parallel", "parallel", "arbitrary")),
        name="hgrn2",
    )(lb_logits, norm_w, r3(hq), r3(hf), r3(hi), r3(hg))
    return out.reshape(bsz * seq, BRANCH_W)


def _merge_kernel(x_ref, mod_ref, ya_ref, yb_ref, ga_ref, gb_ref, wa_ref, wb_ref, wo_ref,
                  npost_ref, npre_ref, rw_ref, rb_ref,
                  x1_ref, h2_ref, topi_ref, gate_ref, rank_ref, cnt_ref, *, tm):
    g1 = mod_ref[2:3, :]
    sh2 = mod_ref[3:4, :]
    sc2 = mod_ref[4:5, :]
    a = jnp.dot(ya_ref[...], wa_ref[...], preferred_element_type=F32)
    b = jnp.dot(yb_ref[...], wb_ref[...], preferred_element_type=F32)
    merged = ga_ref[...] * a.astype(BF16) + gb_ref[...] * b.astype(BF16)
    y = jnp.dot(merged, wo_ref[...], preferred_element_type=F32)
    x1 = x_ref[...] + g1 * _rms(y, npost_ref[...])
    x1_ref[...] = x1
    h2 = (_rms(x1, npre_ref[...]) * (1.0 + sc2) + sh2).astype(BF16)
    h2_ref[...] = h2

    logits = jnp.dot(h2, rw_ref[...], preferred_element_type=F32) + rb_ref[...]
    lane = lax.broadcasted_iota(I32, logits.shape, 1)
    work = logits
    vals, idxs = [], []
    for _ in range(TOP_K):
        mx = jnp.max(work, axis=1, keepdims=True)
        idx = jnp.min(jnp.where(work == mx, lane, LANES), axis=1, keepdims=True)
        vals.append(mx)
        idxs.append(idx)
        work = jnp.where(lane == idx, -jnp.inf, work)
    es = [jnp.exp(vk - vals[0]) for vk in vals]
    den = es[0] + es[1] + es[2] + es[3]

    onehot = jnp.zeros(logits.shape, F32)
    for idx in idxs:
        onehot = jnp.where(lane == idx, 1.0, onehot)
    row = lax.broadcasted_iota(I32, (tm, tm), 0)
    col = lax.broadcasted_iota(I32, (tm, tm), 1)
    strict = jnp.where(col < row, 1.0, 0.0).astype(BF16)
    prefix = jnp.dot(strict, onehot.astype(BF16), preferred_element_type=F32)
    cnt_ref[...] = jnp.broadcast_to(jnp.sum(onehot, axis=0, keepdims=True), cnt_ref.shape)

    topi = jnp.zeros(logits.shape, I32)
    gate = jnp.zeros(logits.shape, F32)
    rank = jnp.zeros(logits.shape, I32)
    for kk in range(TOP_K):
        rk = jnp.sum(jnp.where(lane == idxs[kk], prefix, 0.0), axis=1, keepdims=True)
        topi = jnp.where(lane == kk, idxs[kk], topi)
        gate = jnp.where(lane == kk, es[kk] / den, gate)
        rank = jnp.where(lane == kk, rk.astype(I32), rank)
    topi_ref[...] = topi
    gate_ref[...] = gate
    rank_ref[...] = rank


def _merge_route(xf, mod, ya, yb, ga, gb, wa, wb, wo, npost, npre, rw, rb, seq, tm):
    t, d = xf.shape
    tiles_per_batch = seq // tm
    row = lambda w: pl.BlockSpec((tm, w), lambda i: (i, 0))
    const = lambda a: pl.BlockSpec(a.shape, lambda i: (0,) * a.ndim)
    lane_out = jax.ShapeDtypeStruct((t, LANES), I32)
    return pl.pallas_call(
        functools.partial(_merge_kernel, tm=tm),
        grid=(t // tm,),
        in_specs=[row(d),
                  pl.BlockSpec((None, 6, d), lambda i: (i // tiles_per_batch, 0, 0)),
                  row(BRANCH_W), row(BRANCH_W), row(d), row(d),
                  const(wa), const(wb), const(wo), const(npost), const(npre),
                  const(rw), const(rb)],
        out_specs=[row(d), row(d), row(LANES), row(LANES), row(LANES),
                   pl.BlockSpec((None, SUBLANES, LANES), lambda i: (i, 0, 0))],
        out_shape=[jax.ShapeDtypeStruct((t, d), F32), jax.ShapeDtypeStruct((t, d), BF16),
                   lane_out, jax.ShapeDtypeStruct((t, LANES), F32), lane_out,
                   jax.ShapeDtypeStruct((t // tm, SUBLANES, LANES), F32)],
        compiler_params=_cparams(("parallel",)),
        name="merge_route",
    )(xf, mod, ya, yb, ga, gb, wa, wb, wo, npost, npre, rw, rb)


def _positions(topi_ref, rank_ref, srow_ref):
    topi = topi_ref[...]
    rank = rank_ref[...]
    lane = lax.broadcasted_iota(I32, topi.shape, 1)
    srow = srow_ref[0:1, :]
    pos = []
    for kk in range(TOP_K):
        base = jnp.sum(jnp.where(lane == topi[:, kk:kk + 1], srow, 0.0), axis=1, keepdims=True)
        pos.append(base.astype(I32) + rank[:, kk:kk + 1])
    return pos


def _run_copies(tile, l8_ref, make_copy, wait):
    for e in range(N_EXPERTS):
        n8 = l8_ref[tile * N_EXPERTS + e]
        off = jnp.int32(0)
        for b in range(RUN_BITS - 1, -1, -1):
            size = RUN_ALIGN << b
            bit = (n8 >> b) & 1

            @pl.when(bit == 1)
            def _(e=e, off=off, size=size):
                cp = make_copy(e, off, size)
                cp.wait() if wait else cp.start()

            off = off + bit * size


def _dispatch_kernel(start_ref, srun_ref, l8_ref, eend_ref, elen_ref,
                     h_ref, topi_ref, rank_ref, srow_ref, xs_out, sorted_ref, zbuf, zsem, sem,
                     *, min_blocks):
    i = pl.program_id(0)
    tm = h_ref.shape[0]
    half = h_ref.shape[1] // 2
    n_blocks = xs_out.shape[0] // MOE_BLOCK
    n_used = eend_ref[N_EXPERTS - 1] // MOE_BLOCK

    def zero_copy(start):
        start = pl.multiple_of(start, MOE_BLOCK)
        return pltpu.make_async_copy(zbuf, xs_out.at[pl.ds(start, MOE_BLOCK)], zsem)

    def zero_fill(wait):
        for e in range(N_EXPERTS):
            @pl.when(elen_ref[e] > 0)
            def _(e=e):
                cp = zero_copy(eend_ref[e] - MOE_BLOCK)
                cp.wait() if wait else cp.start()
        for blk in range(min_blocks, n_blocks):
            @pl.when(blk >= n_used)
            def _(blk=blk):
                cp = zero_copy(blk * MOE_BLOCK)
                cp.wait() if wait else cp.start()

    @pl.when(i == 0)
    def _():
        zbuf[...] = jnp.zeros_like(zbuf)
        zero_fill(wait=False)
        zero_fill(wait=True)

    pos = _positions(topi_ref, rank_ref, srow_ref)
    slot = lax.broadcasted_iota(I32, (tm, SORT_ROWS), 1)
    sel = jnp.zeros((tm, SORT_ROWS), F32)
    for p in pos:
        sel = sel + jnp.where(slot == p, 1.0, 0.0)
    srt = lax.dot_general(sel.astype(BF16), h_ref[...], (((0,), (0,)), ((), ())),
                          preferred_element_type=F32)
    lo = pltpu.bitcast(srt[:, :half], jnp.uint32) >> 16
    hi = pltpu.bitcast(srt[:, half:], jnp.uint32) & jnp.uint32(0xFFFF0000)
    sorted_ref[...] = hi | lo

    def make_copy(e, off, size):
        src = pl.multiple_of(srun_ref[i * N_EXPERTS + e] + off, RUN_ALIGN)
        dst = pl.multiple_of(start_ref[i * N_EXPERTS + e] + off, RUN_ALIGN)
        return pltpu.make_async_copy(sorted_ref.at[pl.ds(src, size)], xs_out.at[pl.ds(dst, size)], sem)

    _run_copies(i, l8_ref, make_copy, wait=False)
    _run_copies(i, l8_ref, make_copy, wait=True)


def _dispatch(h2, topi, rank, srow, tables, n_rows, tm):
    t, d = h2.shape
    tile = lambda w: pl.BlockSpec((tm, w), lambda i, *_: (i, 0))
    grid_spec = pltpu.PrefetchScalarGridSpec(
        num_scalar_prefetch=len(tables),
        grid=(t // tm,),
        in_specs=[tile(d), tile(LANES), tile(LANES),
                  pl.BlockSpec((None, SUBLANES, LANES), lambda i, *_: (i, 0, 0))],
        out_specs=pl.BlockSpec(memory_space=pl.ANY),
        scratch_shapes=[pltpu.VMEM((SORT_ROWS, d // 2), jnp.uint32),
                        pltpu.VMEM((MOE_BLOCK, d // 2), jnp.uint32),
                        pltpu.SemaphoreType.DMA, pltpu.SemaphoreType.DMA],
    )
    return pl.pallas_call(
        functools.partial(_dispatch_kernel, min_blocks=t * TOP_K // MOE_BLOCK),
        grid_spec=grid_spec,
        out_shape=jax.ShapeDtypeStruct((n_rows, d // 2), jnp.uint32),
        compiler_params=_cparams(("arbitrary",)),
        name="moe_dispatch",
    )(*tables, h2, topi, rank, srow)


def _expert_kernel(be_ref, nu_ref, x_ref, wgu_ref, bgu_ref, wd_ref, bd_ref, y_ref,
                   wgu_bf, wd_bf):
    j = pl.program_id(0)
    half = D_MODEL // 2

    @pl.when((j == 0) | (be_ref[j] != be_ref[jnp.maximum(j - 1, 0)]))
    def _():
        wgu_bf[...] = wgu_ref[...].astype(BF16)
        wd_bf[...] = wd_ref[...].astype(BF16)

    @pl.when(j < nu_ref[0])
    def _():
        xu = x_ref[...]
        x_lo = pltpu.bitcast(xu << 16, F32).astype(BF16)
        x_hi = pltpu.bitcast(xu & jnp.uint32(0xFFFF0000), F32).astype(BF16)
        gu = (jnp.dot(x_lo, wgu_bf[:half, :], preferred_element_type=F32)
              + jnp.dot(x_hi, wgu_bf[half:, :], preferred_element_type=F32) + bgu_ref[...])
        g = jnp.minimum(gu[:, :D_MODEL], SWIGLU_LIMIT)
        u = jnp.clip(gu[:, D_MODEL:], -SWIGLU_LIMIT, SWIGLU_LIMIT)
        act = (u + 1.0) * (g * _sigmoid(g * SWIGLU_ALPHA))
        y_ref[...] = jnp.dot(act.astype(BF16), wd_bf[...], preferred_element_type=F32) + bd_ref[...]

    @pl.when(j >= nu_ref[0])
    def _():
        y_ref[...] = jnp.zeros_like(y_ref)


def _experts(xs, block_e, n_used, wgu, bgu, wd, bd):
    p = xs.shape[0]
    e, d, _ = wgu.shape
    n_blocks = p // MOE_BLOCK
    grid_spec = pltpu.PrefetchScalarGridSpec(
        num_scalar_prefetch=2,
        grid=(n_blocks,),
        in_specs=[pl.BlockSpec((MOE_BLOCK, d // 2), lambda j, be, nu: (j, 0)),
                  pl.BlockSpec((None, d, 2 * d), lambda j, be, nu: (be[j], 0, 0)),
                  pl.BlockSpec((None, 1, 2 * d), lambda j, be, nu: (be[j], 0, 0)),
                  pl.BlockSpec((None, d, d), lambda j, be, nu: (be[j], 0, 0)),
                  pl.BlockSpec((None, 1, d), lambda j, be, nu: (be[j], 0, 0))],
        out_specs=pl.BlockSpec((MOE_BLOCK, d), lambda j, be, nu: (j, 0)),
        scratch_shapes=[pltpu.VMEM((d, 2 * d), BF16), pltpu.VMEM((d, d), BF16)],
    )
    return pl.pallas_call(
        _expert_kernel,
        grid_spec=grid_spec,
        out_shape=jax.ShapeDtypeStruct((p, d), F32),
        compiler_params=_cparams(("arbitrary",)),
        name="moe_experts",
    )(block_e, n_used, xs, wgu, bgu.reshape(e, 1, 2 * d), wd, bd.reshape(e, 1, d))


def _combine_kernel(start_ref, srun_ref, l8_ref,
                    y_hbm, topi_ref, rank_ref, srow_ref, gate_ref, x1_ref, mod_ref, nw_ref, o_ref,
                    yb0, yb1, sem):
    i = pl.program_id(0)
    tm = x1_ref.shape[0]
    ybufs = (yb0, yb1)

    def run_copies(tile, slot, wait):
        def make_copy(e, off, size):
            src = pl.multiple_of(start_ref[tile * N_EXPERTS + e] + off, RUN_ALIGN)
            dst = pl.multiple_of(srun_ref[tile * N_EXPERTS + e] + off, RUN_ALIGN)
            return pltpu.make_async_copy(y_hbm.at[pl.ds(src, size)],
                                         ybufs[slot].at[pl.ds(dst, size)], sem.at[slot])
        _run_copies(tile, l8_ref, make_copy, wait)

    def finish(slot):
        run_copies(i, slot, wait=True)
        pos = _positions(topi_ref, rank_ref, srow_ref)
        gate = gate_ref[...]
        row = lax.broadcasted_iota(I32, (tm, SORT_ROWS), 1)
        sel = jnp.zeros((tm, SORT_ROWS), F32)
        for kk, p in enumerate(pos):
            sel = sel + jnp.where(row == p, gate[:, kk:kk + 1], 0.0)
        moe = jnp.dot(sel.astype(BF16), ybufs[slot][...].astype(BF16), preferred_element_type=F32)
        g2 = mod_ref[5:6, :]
        o_ref[...] = x1_ref[...] + g2 * _rms(moe, nw_ref[...])

    @pl.when(i == 0)
    def _():
        yb0[...] = jnp.zeros_like(yb0)
        yb1[...] = jnp.zeros_like(yb1)
        run_copies(0, 0, wait=False)

    for slot in range(2):
        @pl.when((i + 1 < pl.num_programs(0)) & ((i + 1) % 2 == slot))
        def _(slot=slot):
            run_copies(i + 1, slot, wait=False)

    for slot in range(2):
        @pl.when(i % 2 == slot)
        def _(slot=slot):
            finish(slot)


def _combine(y, topi, rank, srow, gate, x1, mod, nw, tables, seq, tm):
    t, d = x1.shape
    tiles_per_batch = seq // tm
    tile = lambda w: pl.BlockSpec((tm, w), lambda i, *_: (i, 0))
    grid_spec = pltpu.PrefetchScalarGridSpec(
        num_scalar_prefetch=len(tables),
        grid=(t // tm,),
        in_specs=[pl.BlockSpec(memory_space=pl.ANY),
                  tile(LANES), tile(LANES),
                  pl.BlockSpec((None, SUBLANES, LANES), lambda i, *_: (i, 0, 0)),
                  tile(LANES), tile(d),
                  pl.BlockSpec((None, 6, d), lambda i, *_: (i // tiles_per_batch, 0, 0)),
                  pl.BlockSpec((1, d), lambda i, *_: (0, 0))],
        out_specs=tile(d),
        scratch_shapes=[pltpu.VMEM((SORT_ROWS, d), F32), pltpu.VMEM((SORT_ROWS, d), F32),
                        pltpu.SemaphoreType.DMA((2,))],
    )
    return pl.pallas_call(
        _combine_kernel,
        grid_spec=grid_spec,
        out_shape=jax.ShapeDtypeStruct((t, d), F32),
        compiler_params=_cparams(("arbitrary",)),
        name="moe_combine",
    )(*tables, y, topi, rank, srow, gate, x1, mod, nw)


def _rope_tables(seq):
    pos = jnp.arange(seq, dtype=F32)
    inv = ROPE_THETA ** (-jnp.arange(0, DA_HEAD_DIM, 2, dtype=F32) / DA_HEAD_DIM)
    ang = pos[:, None] * inv[None, :]
    cos, sin = jnp.cos(ang), jnp.sin(ang)
    zero = jnp.zeros_like(sin)
    cos_t = jnp.tile(cos, (1, LANES // cos.shape[1]))
    sin_lo = jnp.tile(jnp.concatenate([-sin, zero], axis=1), (1, 2))
    sin_hi = jnp.tile(jnp.concatenate([zero, sin], axis=1), (1, 2))
    return cos_t, sin_lo, sin_hi


def kernel(x, c, w_mod, b_mod, norm_pre_mix, norm_post_mix, w_in, da_lambda_q1, da_lambda_k1,
           da_lambda_q2, da_lambda_k2, da_subln, hg_lb_logits, hg_norm, w_branch_a, w_branch_b,
           w_out, norm_pre_ffn, norm_post_ffn, router_w, router_b, w_gate_up, b_gate_up,
           w_down, b_down):
    bsz, seq, d = x.shape
    depth = w_mod.shape[0]
    assert depth == 1 and d == D_MODEL
    t = bsz * seq
    lyr = 0
    lambda_init = 0.8 - 0.6 * math.exp(-0.3 * lyr)
    cos, slo, shi = _rope_tables(seq)

    xf = x.reshape(t, d)
    mod = _modulation(c, w_mod[lyr], b_mod[lyr]).reshape(bsz, 6, d)

    qT, k, vT, hq, hf, hi, hg, ga, gb = _in_projection(
        xf, mod, norm_pre_mix[lyr][None], w_in[lyr].astype(BF16), cos, slo, shi, seq, tm=512)
    ya = _diff_attention(qT, k, vT, da_lambda_q1[lyr][None], da_lambda_k1[lyr][None],
                         da_lambda_q2[lyr][None], da_lambda_k2[lyr][None], da_subln[lyr],
                         bsz, seq, lambda_init, tq=1024, tk=512)
    yb = _hgrn2(hq, hf, hi, hg, hg_lb_logits[lyr:lyr + 2], hg_norm[lyr][None], bsz, seq, tc=512, nh=2)

    rw = jnp.zeros((d, LANES), BF16).at[:, :N_EXPERTS].set(router_w[lyr].astype(BF16))
    rb = jnp.full((1, LANES), NEG_BIG, F32).at[0, :N_EXPERTS].set(router_b[lyr])
    x1, h2, topi, gate, rank, counts = _merge_route(
        xf, mod, ya, yb, ga, gb, w_branch_a[lyr].astype(BF16), w_branch_b[lyr].astype(BF16),
        w_out[lyr].astype(BF16), norm_post_mix[lyr][None], norm_pre_ffn[lyr][None], rw, rb,
        seq, tm=MOE_TILE)

    n_tiles = t // MOE_TILE
    up = lambda a, m: (a + m - 1) // m * m
    cnt = counts[:, 0, :N_EXPERTS].astype(I32)
    run = up(cnt, RUN_ALIGN)
    srun = jnp.cumsum(run, axis=1) - run
    e_len = up(jnp.sum(run, axis=0), MOE_BLOCK)
    e_end = jnp.cumsum(e_len)
    start = (e_end - e_len)[None, :] + jnp.cumsum(run, axis=0) - run
    n_assign = t * TOP_K
    n_rows = (up(n_assign + n_tiles * N_EXPERTS * (RUN_ALIGN - 1), MOE_BLOCK)
              + N_EXPERTS * MOE_BLOCK)
    n_blocks = n_rows // MOE_BLOCK
    block_start = jnp.arange(n_blocks, dtype=I32) * MOE_BLOCK
    block_e = jnp.minimum(jnp.sum((e_end[None, :] <= block_start[:, None]).astype(I32), axis=1),
                          N_EXPERTS - 1)
    n_used = (e_end[-1:] // MOE_BLOCK).astype(I32)
    srow = jnp.broadcast_to(jnp.pad(srun.astype(F32), ((0, 0), (0, LANES - N_EXPERTS)))[:, None, :],
                            (n_tiles, SUBLANES, LANES))
    run_tables = (start.reshape(-1), srun.reshape(-1), (run // RUN_ALIGN).reshape(-1))

    xs = _dispatch(h2, topi, rank, srow, run_tables + (e_end.astype(I32), e_len.astype(I32)),
                   n_rows, tm=MOE_TILE)
    y = _experts(xs, block_e, n_used, w_gate_up[lyr], b_gate_up[lyr], w_down[lyr], b_down[lyr])
    out = _combine(y, topi, rank, srow, gate, x1, mod, norm_post_ffn[lyr][None], run_tables,
                   seq, tm=MOE_TILE)
    return out.reshape(bsz, seq, d)
```

```python
import functools
import math

import jax
import jax.numpy as jnp
from jax import lax
from jax.experimental import pallas as pl
from jax.experimental.pallas import tpu as pltpu

F32 = jnp.float32
BF16 = jnp.bfloat16
I32 = jnp.int32

D_MODEL = 1024
CHUNK = 64
EPS = 1e-6
ROPE_THETA = 10000.0
DA_HEADS = 4
DA_HEAD_DIM = 64
HG_HEADS = 4
HEAD_W = 128
BRANCH_W = 512
V_ROWS = HEAD_W + 16
ATTN_STRIP = 256
N_EXPERTS = 32
TOP_K = 4
SWIGLU_LIMIT = 7.0
SWIGLU_ALPHA = 1.702
MOE_BLOCK = 512
MOE_TILE = 512
LANES = 128
SUBLANES = 8
RUN_ALIGN = SUBLANES
RUN_BITS = (MOE_TILE // RUN_ALIGN).bit_length()
SORT_ROWS = MOE_TILE * TOP_K + N_EXPERTS * RUN_ALIGN
NEG_BIG = -1e30

VMEM_LIMIT = 56 * 1024 * 1024


def _cparams(sem):
    return pltpu.CompilerParams(dimension_semantics=sem, vmem_limit_bytes=VMEM_LIMIT)


def _rms(x, w):
    return x * lax.rsqrt(jnp.mean(x * x, axis=-1, keepdims=True) + EPS) * w


def _sigmoid(x):
    return 1.0 / (1.0 + jnp.exp(-x))


def _split3(x):
    a = x.astype(BF16)
    r = x - a.astype(F32)
    b = r.astype(BF16)
    c = (r - b.astype(F32)).astype(BF16)
    return a, b, c


def _mod_kernel(c_ref, w_ref, b_ref, o_ref):
    c = c_ref[...]
    ca = c * _sigmoid(c)
    o_ref[...] = jnp.dot(ca, w_ref[...], preferred_element_type=F32,
                         precision=lax.Precision.HIGHEST) + b_ref[...]


def _modulation(c, w_mod, b_mod):
    bsz, d = c.shape
    n = w_mod.shape[1]
    tn = 1536
    return pl.pallas_call(
        _mod_kernel,
        grid=(n // tn,),
        in_specs=[pl.BlockSpec((bsz, d), lambda j: (0, 0)),
                  pl.BlockSpec((d, tn), lambda j: (0, j)),
                  pl.BlockSpec((1, tn), lambda j: (0, j))],
        out_specs=pl.BlockSpec((bsz, tn), lambda j: (0, j)),
        out_shape=jax.ShapeDtypeStruct((bsz, n), F32),
        compiler_params=_cparams(("arbitrary",)),
        name="modulation",
    )(c, w_mod, b_mod.reshape(1, n))


def _rope(x, cos, sin_lo, sin_hi):
    return x * cos + pltpu.roll(x, 96, 1) * sin_lo + pltpu.roll(x, 32, 1) * sin_hi


def _inproj_kernel(x_ref, mod_ref, nw_ref, w_ref, cos_ref, slo_ref, shi_ref,
                   qT_ref, k_ref, vT_ref, hq_ref, hf_ref, hi_ref, hg_ref, ga_ref, gb_ref):
    x = x_ref[...]
    sh1 = mod_ref[0:1, :]
    sc1 = mod_ref[1:2, :]
    h = (_rms(x, nw_ref[...]) * (1.0 + sc1) + sh1).astype(BF16)
    cos, slo, shi = cos_ref[...], slo_ref[...], shi_ref[...]
    scale = DA_HEAD_DIM ** -0.5 * math.log2(math.e)

    def seg(i, width=BRANCH_W):
        return jnp.dot(h, w_ref[:, i:i + width], preferred_element_type=F32)

    q = seg(0)
    k = seg(BRANCH_W)
    for hd in range(DA_HEADS):
        sl = slice(hd * HEAD_W, (hd + 1) * HEAD_W)
        qT_ref[sl, :] = (_rope(q[:, sl], cos, slo, shi) * scale).T.astype(BF16)
        k_ref[:, sl] = _rope(k[:, sl], cos, slo, shi).astype(BF16)
    vT = seg(2 * BRANCH_W).T.astype(BF16)
    ones = jnp.ones((V_ROWS - HEAD_W, vT.shape[1]), BF16)
    for hd in range(DA_HEADS):
        vT_ref[hd * V_ROWS:hd * V_ROWS + HEAD_W, :] = vT[hd * HEAD_W:(hd + 1) * HEAD_W]
        vT_ref[hd * V_ROWS + HEAD_W:(hd + 1) * V_ROWS, :] = ones
    hq = seg(3 * BRANCH_W)
    hq_ref[...] = (hq * _sigmoid(hq)).astype(BF16)
    hf_ref[...] = seg(4 * BRANCH_W)
    hi_ref[...] = seg(5 * BRANCH_W).astype(BF16)
    hg = seg(6 * BRANCH_W)
    hg_ref[...] = (hg * _sigmoid(hg)).astype(BF16)
    for j in range(2):
        ga_ref[:, j * BRANCH_W:(j + 1) * BRANCH_W] = _sigmoid(seg((7 + j) * BRANCH_W)).astype(BF16)
        gb_ref[:, j * BRANCH_W:(j + 1) * BRANCH_W] = _sigmoid(seg((9 + j) * BRANCH_W)).astype(BF16)


def _in_projection(xf, mod, norm_w, w_in_bf, cos, slo, shi, seq, tm):
    t, d = xf.shape
    tiles_per_batch = seq // tm
    bspec = lambda w: pl.BlockSpec((tm, w), lambda i: (i, 0))
    tab = pl.BlockSpec((tm, LANES), lambda i: (i % tiles_per_batch, 0))
    widths = [BRANCH_W] * 7 + [D_MODEL, D_MODEL]
    dtypes = [BF16, BF16, BF16, BF16, F32, BF16, BF16, BF16, BF16]
    out_specs = [bspec(w) for w in widths]
    out_shape = [jax.ShapeDtypeStruct((t, w), dt) for w, dt in zip(widths, dtypes)]
    for idx, rows in ((0, BRANCH_W), (2, DA_HEADS * V_ROWS)):
        out_specs[idx] = pl.BlockSpec((None, rows, tm),
                                      lambda i: (i // tiles_per_batch, 0, i % tiles_per_batch))
        out_shape[idx] = jax.ShapeDtypeStruct((t // seq, rows, seq), BF16)
    return pl.pallas_call(
        _inproj_kernel,
        grid=(t // tm,),
        in_specs=[bspec(d),
                  pl.BlockSpec((None, 6, d), lambda i: (i // tiles_per_batch, 0, 0)),
                  pl.BlockSpec((1, d), lambda i: (0, 0)),
                  pl.BlockSpec(w_in_bf.shape, lambda i: (0, 0), pipeline_mode=pl.Buffered(1)),
                  tab, tab, tab],
        out_specs=out_specs,
        out_shape=out_shape,
        compiler_params=_cparams(("parallel",)),
        name="in_projection",
    )(xf, mod, norm_w, w_in_bf, cos, slo, shi)


def _attn_kernel(lq1_ref, lk1_ref, lq2_ref, lk2_ref, subln_ref, qT_ref, k_ref, vT_ref, o_ref,
                 qq_ref, s0_ref, s1_ref, p0_ref, p1_ref, a0_ref, a1_ref, m_ref, acc_ref,
                 *, tq, tk, lambda_init):
    qi = pl.program_id(2)
    ndiag = tq // tk
    s_refs, p_refs, a_refs = (s0_ref, s1_ref), (p0_ref, p1_ref), (a0_ref, a1_ref)
    qT = qT_ref[...]
    feat = lax.broadcasted_iota(I32, qT.shape, 0)
    zero = jnp.zeros_like(qT)
    qq_ref[:, :tq] = jnp.where(feat < DA_HEAD_DIM, qT, zero)
    qq_ref[:, tq:] = jnp.where(feat >= DA_HEAD_DIM, qT, zero)

    strips = [slice(c * ATTN_STRIP, (c + 1) * ATTN_STRIP) for c in range(2 * tq // ATTN_STRIP)]

    def scores(kb, dst, cs):
        start = pl.multiple_of(kb * tk, tk)
        dst[:, cs] = jnp.dot(k_ref[pl.ds(start, tk), :], qq_ref[:, cs], preferred_element_type=F32)

    def accumulate(kb, slot, cs):
        start = pl.multiple_of(kb * tk, tk)
        pv = jnp.dot(vT_ref[:, pl.ds(start, tk)], p_refs[slot][:, cs], preferred_element_type=F32)
        acc_ref[:, cs] = a_refs[slot][:, cs] * acc_ref[:, cs] + pv

    def softmax(slot, cs, diag):
        def piece(rc):
            sc = s_refs[slot][rc * CHUNK:(rc + 1) * CHUNK, cs]
            if diag is not None:
                qchunk = (lax.broadcasted_iota(I32, (1, ATTN_STRIP), 1) + cs.start) % tq // CHUNK
                sc = jnp.where(qchunk >= diag * (tk // CHUNK) + rc, sc, NEG_BIG)
            return sc

        top = piece(0)
        for rc in range(1, tk // CHUNK):
            top = jnp.maximum(top, piece(rc))
        m_old = m_ref[:, cs]
        m_new = jnp.maximum(m_old, jnp.max(top, axis=0, keepdims=True))
        m_ref[:, cs] = m_new
        a_refs[slot][:, cs] = jnp.exp2(m_old - m_new)
        for rc in range(tk // CHUNK):
            p_refs[slot][rc * CHUNK:(rc + 1) * CHUNK, cs] = jnp.exp2((piece(rc) - m_new).astype(BF16))

    def stage(kb, slot, diag=None, last=False):
        for cs in strips:
            accumulate(jnp.maximum(kb - 1, 0), 1 - slot, cs)
            softmax(slot, cs, diag)
            if not last:
                scores(kb + 1, s_refs[1 - slot], cs)

    p1_ref[...] = jnp.zeros_like(p1_ref)
    a1_ref[...] = jnp.ones_like(a1_ref)
    m_ref[...] = jnp.full(m_ref.shape, NEG_BIG, F32)
    acc_ref[...] = jnp.zeros_like(acc_ref)
    for cs in strips:
        scores(0, s0_ref, cs)

    def pair(i, c):
        stage(2 * i, 0)
        stage(2 * i + 1, 1)
        return c

    first_diag = qi * ndiag
    lax.fori_loop(0, first_diag // 2, pair, 0)
    for dg in range(ndiag):
        stage(first_diag + dg, dg % 2, diag=dg, last=dg == ndiag - 1)
    for cs in strips:
        accumulate(first_diag + ndiag - 1, (ndiag - 1) % 2, cs)

    acc = acc_ref[...]
    o = acc[:HEAD_W] / acc[HEAD_W:HEAD_W + 1]
    lam = (jnp.exp(jnp.sum(lq1_ref[...] * lk1_ref[...], axis=1, keepdims=True))
           - jnp.exp(jnp.sum(lq2_ref[...] * lk2_ref[...], axis=1, keepdims=True))
           + lambda_init)
    o = o[:, :tq] - lam * o[:, tq:]
    o = o * lax.rsqrt(jnp.mean(o * o, axis=0, keepdims=True) + EPS) * subln_ref[...]
    o_ref[...] = (o * (1.0 - lambda_init)).T.astype(BF16)


def _diff_attention(qT, k, vT, lq1, lk1, lq2, lk2, subln, bsz, seq, lambda_init, tq, tk):
    assert tq % (2 * tk) == 0
    k3 = k.reshape(bsz, seq, BRANCH_W)
    small = lambda n: pl.BlockSpec((1, n), lambda b, h, i: (0, 0))
    out = pl.pallas_call(
        functools.partial(_attn_kernel, tq=tq, tk=tk, lambda_init=lambda_init),
        grid=(bsz, DA_HEADS, seq // tq),
        in_specs=[small(DA_HEAD_DIM)] * 4 + [
            pl.BlockSpec((HEAD_W, 1), lambda b, h, i: (0, 0)),
            pl.BlockSpec((None, HEAD_W, tq), lambda b, h, i: (b, h, i)),
            pl.BlockSpec((None, seq, HEAD_W), lambda b, h, i: (b, 0, h)),
            pl.BlockSpec((None, V_ROWS, seq), lambda b, h, i: (b, h, 0))],
        out_specs=pl.BlockSpec((None, tq, HEAD_W), lambda b, h, i: (b, i, h)),
        out_shape=jax.ShapeDtypeStruct((bsz, seq, BRANCH_W), BF16),
        scratch_shapes=[pltpu.VMEM((HEAD_W, 2 * tq), BF16),
                        pltpu.VMEM((tk, 2 * tq), F32), pltpu.VMEM((tk, 2 * tq), F32),
                        pltpu.VMEM((tk, 2 * tq), BF16), pltpu.VMEM((tk, 2 * tq), BF16),
                        pltpu.VMEM((1, 2 * tq), F32), pltpu.VMEM((1, 2 * tq), F32),
                        pltpu.VMEM((1, 2 * tq), F32), pltpu.VMEM((V_ROWS, 2 * tq), F32)],
        compiler_params=_cparams(("parallel", "parallel", "arbitrary")),
        name="diff_attention",
    )(lq1, lk1, lq2, lk2, subln.reshape(HEAD_W, 1), qT, k3, vT)
    return out.reshape(bsz * seq, BRANCH_W)


def _hgrn_kernel(lbl_ref, nw_ref, q_ref, f_ref, i_ref, g_ref, o_ref, st_ref, *, tc, nh):
    @pl.when(pl.program_id(2) == 0)
    def _():
        st_ref[...] = jnp.zeros_like(st_ref)

    row = lax.broadcasted_iota(I32, (CHUNK, CHUNK), 0)
    col = lax.broadcasted_iota(I32, (CHUNK, CHUNK), 1)
    causal = col <= row
    tril = jnp.where(causal, 1.0, 0.0).astype(BF16)
    nt = (((1,), (1,)), ((), ()))
    tn = (((0,), (0,)), ((), ()))

    nchunk = tc // CHUNK
    units = [(hh, c) for hh in range(nh) for c in range(nchunk)]
    rows = lambda c: slice(c * CHUNK, (c + 1) * CHUNK)
    q, k, v, parts = [], [], [], []
    for hh in range(nh):
        hs = slice(hh * HEAD_W, (hh + 1) * HEAD_W)
        lg = lbl_ref[:, hs]
        ex = jnp.exp(lg - jnp.max(lg, axis=0, keepdims=True))
        lb = ex[0:1] / jnp.sum(ex, axis=0, keepdims=True)
        f = lb + (1.0 - lb) * _sigmoid(f_ref[:, hs])
        q.append(q_ref[:, hs].astype(F32))
        k.append(1.0 - f)
        v.append(i_ref[:, hs])
        parts.append(_split3(jnp.log(f)))
    gall = sum(jnp.dot(tril, jnp.concatenate([parts[hh][p][rows(c)] for hh, c in units], axis=1),
                       preferred_element_type=F32) for p in range(3))
    gcum = [gall[:, u * HEAD_W:(u + 1) * HEAD_W] for u in range(len(units))]
    gtot = [g[CHUNK - 1:CHUNK, :] for g in gcum]
    q_t = [(q[hh][rows(c)] * jnp.exp(gcum[u])).astype(BF16) for u, (hh, c) in enumerate(units)]
    k_t = [(k[hh][rows(c)] * jnp.exp(-gcum[u])).astype(BF16) for u, (hh, c) in enumerate(units)]
    k_dec = [(k[hh][rows(c)] * jnp.exp(gtot[u] - gcum[u])).astype(BF16)
             for u, (hh, c) in enumerate(units)]
    a = [lax.dot_general(q_t[u], k_t[u], nt, preferred_element_type=F32) for u in range(len(units))]
    ds = [lax.dot_general(v[hh][rows(c)], k_dec[u], tn, preferred_element_type=F32)
          for u, (hh, c) in enumerate(units)]
    a = [jnp.where(causal, x, 0.0).astype(BF16) for x in a]
    st_in = []
    for hh in range(nh):
        st = st_ref[hh]
        for c in range(nchunk):
            u = hh * nchunk + c
            st_in.append(st.astype(BF16))
            st = jnp.exp(gtot[u]) * st + ds[u]
        st_ref[hh] = st
    outs = [jnp.dot(a[u], v[hh][rows(c)], preferred_element_type=F32)
            + lax.dot_general(q_t[u], st_in[u], nt, preferred_element_type=F32)
            for u, (hh, c) in enumerate(units)]
    for hh in range(nh):
        hs = slice(hh * HEAD_W, (hh + 1) * HEAD_W)
        o = jnp.concatenate(outs[hh * nchunk:(hh + 1) * nchunk], axis=0)
        o_ref[:, hs] = (_rms(o, nw_ref[...]) * g_ref[:, hs].astype(F32)).astype(BF16)


def _hgrn2(hq, hf, hi, hg, lb_logits, norm_w, bsz, seq, tc, nh):
    r3 = lambda a: a.reshape(bsz, seq, BRANCH_W)
    blk = pl.BlockSpec((None, tc, nh * HEAD_W), lambda b, h, i: (b, i, h))
    out = pl.pallas_call(
        functools.partial(_hgrn_kernel, tc=tc, nh=nh),
        grid=(bsz, HG_HEADS // nh, seq // tc),
        in_specs=[pl.BlockSpec((2, nh * HEAD_W), lambda b, h, i: (0, h)),
                  pl.BlockSpec((1, HEAD_W), lambda b, h, i: (0, 0)),
                  blk, blk, blk, blk],
        out_specs=blk,
        out_shape=jax.ShapeDtypeStruct((bsz, seq, BRANCH_W), BF16),
        scratch_shapes=[pltpu.VMEM((nh, HEAD_W, HEAD_W), F32)],
        compiler_params=_cparams(("parallel", "parallel", "arbitrary")),
        name="hgrn2",
    )(lb_logits, norm_w, r3(hq), r3(hf), r3(hi), r3(hg))
    return out.reshape(bsz * seq, BRANCH_W)


def _merge_kernel(x_ref, mod_ref, ya_ref, yb_ref, ga_ref, gb_ref, wa_ref, wb_ref, wo_ref,
                  npost_ref, npre_ref, rw_ref, rb_ref,
                  x1_ref, h2_ref, topi_ref, gate_ref, rank_ref, cnt_ref, lg_ref, *, tm):
    @pl.when(pl.program_id(0) == 0)
    def _():
        lg_ref[...] = jnp.zeros_like(lg_ref)

    logits = lg_ref[...]

    g1 = mod_ref[2:3, :]
    sh2 = mod_ref[3:4, :]
    sc2 = mod_ref[4:5, :]
    a = jnp.dot(ya_ref[...], wa_ref[...], preferred_element_type=F32)
    b = jnp.dot(yb_ref[...], wb_ref[...], preferred_element_type=F32)
    merged = ga_ref[...] * a.astype(BF16) + gb_ref[...] * b.astype(BF16)
    y = jnp.dot(merged, wo_ref[...], preferred_element_type=F32)
    x1 = x_ref[...] + g1 * _rms(y, npost_ref[...])
    x1_ref[...] = x1
    h2 = (_rms(x1, npre_ref[...]) * (1.0 + sc2) + sh2).astype(BF16)
    h2_ref[...] = h2
    lg_ref[...] = jnp.dot(h2, rw_ref[...], preferred_element_type=F32) + rb_ref[...]

    lane = lax.broadcasted_iota(I32, logits.shape, 1)
    work = logits
    vals, idxs = [], []
    for _ in range(TOP_K):
        mx = jnp.max(work, axis=1, keepdims=True)
        idx = jnp.min(jnp.where(work == mx, lane, LANES), axis=1, keepdims=True)
        vals.append(mx)
        idxs.append(idx)
        work = jnp.where(lane == idx, -jnp.inf, work)
    es = [jnp.exp(vk - vals[0]) for vk in vals]
    den = es[0] + es[1] + es[2] + es[3]

    onehot = jnp.zeros(logits.shape, F32)
    for idx in idxs:
        onehot = jnp.where(lane == idx, 1.0, onehot)
    row = lax.broadcasted_iota(I32, (tm, tm), 0)
    col = lax.broadcasted_iota(I32, (tm, tm), 1)
    strict = jnp.where(col < row, 1.0, 0.0).astype(BF16)
    prefix = jnp.dot(strict, onehot.astype(BF16), preferred_element_type=F32)
    cnt_ref[...] = jnp.broadcast_to(jnp.sum(onehot, axis=0, keepdims=True), cnt_ref.shape)

    topi = jnp.zeros(logits.shape, I32)
    gate = jnp.zeros(logits.shape, F32)
    rank = jnp.zeros(logits.shape, I32)
    for kk in range(TOP_K):
        rk = jnp.sum(jnp.where(lane == idxs[kk], prefix, 0.0), axis=1, keepdims=True)
        topi = jnp.where(lane == kk, idxs[kk], topi)
        gate = jnp.where(lane == kk, es[kk] / den, gate)
        rank = jnp.where(lane == kk, rk.astype(I32), rank)
    topi_ref[...] = topi
    gate_ref[...] = gate
    rank_ref[...] = rank


def _merge_route(xf, mod, ya, yb, ga, gb, wa, wb, wo, npost, npre, rw, rb, seq, tm):
    t, d = xf.shape
    n = t // tm
    tiles_per_batch = seq // tm
    proj = lambda w: pl.BlockSpec((tm, w), lambda i: (jnp.minimum(i, n - 1), 0))
    route = lambda w: pl.BlockSpec((tm, w), lambda i: (jnp.maximum(i - 1, 0), 0))
    const = lambda a: pl.BlockSpec(a.shape, lambda i: (0,) * a.ndim)
    lane_out = jax.ShapeDtypeStruct((t, LANES), I32)
    return pl.pallas_call(
        functools.partial(_merge_kernel, tm=tm),
        grid=(n + 1,),
        in_specs=[proj(d),
                  pl.BlockSpec((None, 6, d),
                               lambda i: (jnp.minimum(i, n - 1) // tiles_per_batch, 0, 0)),
                  proj(BRANCH_W), proj(BRANCH_W), proj(d), proj(d),
                  const(wa), const(wb), const(wo), const(npost), const(npre),
                  const(rw), const(rb)],
        out_specs=[proj(d), proj(d), route(LANES), route(LANES), route(LANES),
                   pl.BlockSpec((None, SUBLANES, LANES), lambda i: (jnp.maximum(i - 1, 0), 0, 0))],
        out_shape=[jax.ShapeDtypeStruct((t, d), F32), jax.ShapeDtypeStruct((t, d), BF16),
                   lane_out, jax.ShapeDtypeStruct((t, LANES), F32), lane_out,
                   jax.ShapeDtypeStruct((n, SUBLANES, LANES), F32)],
        scratch_shapes=[pltpu.VMEM((tm, LANES), F32)],
        compiler_params=_cparams(("arbitrary",)),
        name="merge_route",
    )(xf, mod, ya, yb, ga, gb, wa, wb, wo, npost, npre, rw, rb)


def _positions(topi_ref, rank_ref, srow_ref):
    topi = topi_ref[...]
    rank = rank_ref[...]
    lane = lax.broadcasted_iota(I32, topi.shape, 1)
    srow = srow_ref[0:1, :]
    pos = []
    for kk in range(TOP_K):
        base = jnp.sum(jnp.where(lane == topi[:, kk:kk + 1], srow, 0.0), axis=1, keepdims=True)
        pos.append(base.astype(I32) + rank[:, kk:kk + 1])
    return pos


def _run_copies(tile, l8_ref, make_copy, wait):
    for e in range(N_EXPERTS):
        n8 = l8_ref[tile * N_EXPERTS + e]
        off = jnp.int32(0)
        for b in range(RUN_BITS - 1, -1, -1):
            size = RUN_ALIGN << b
            bit = (n8 >> b) & 1

            @pl.when(bit == 1)
            def _(e=e, off=off, size=size):
                cp = make_copy(e, off, size)
                cp.wait() if wait else cp.start()

            off = off + bit * size


def _dispatch_kernel(start_ref, srun_ref, l8_ref, eend_ref, elen_ref,
                     h_ref, topi_ref, rank_ref, srow_ref, xs_out, sorted_ref, zbuf, zsem, sem,
                     *, min_blocks):
    i = pl.program_id(0)
    tm = h_ref.shape[0]
    half = h_ref.shape[1] // 2
    n_blocks = xs_out.shape[0] // MOE_BLOCK
    n_used = eend_ref[N_EXPERTS - 1] // MOE_BLOCK

    def zero_copy(start):
        start = pl.multiple_of(start, MOE_BLOCK)
        return pltpu.make_async_copy(zbuf, xs_out.at[pl.ds(start, MOE_BLOCK)], zsem)

    def zero_fill(wait):
        for e in range(N_EXPERTS):
            @pl.when(elen_ref[e] > 0)
            def _(e=e):
                cp = zero_copy(eend_ref[e] - MOE_BLOCK)
                cp.wait() if wait else cp.start()
        for blk in range(min_blocks, n_blocks):
            @pl.when(blk >= n_used)
            def _(blk=blk):
                cp = zero_copy(blk * MOE_BLOCK)
                cp.wait() if wait else cp.start()

    @pl.when(i == 0)
    def _():
        zbuf[...] = jnp.zeros_like(zbuf)
        zero_fill(wait=False)
        zero_fill(wait=True)

    pos = _positions(topi_ref, rank_ref, srow_ref)
    slot = lax.broadcasted_iota(I32, (tm, SORT_ROWS), 1)
    sel = jnp.zeros((tm, SORT_ROWS), F32)
    for p in pos:
        sel = jnp.where(slot == p, 1.0, sel)
    srt = lax.dot_general(sel.astype(BF16), h_ref[...], (((0,), (0,)), ((), ())),
                          preferred_element_type=F32)
    lo = pltpu.bitcast(srt[:, :half], jnp.uint32) >> 16
    hi = pltpu.bitcast(srt[:, half:], jnp.uint32) & jnp.uint32(0xFFFF0000)
    sorted_ref[...] = hi | lo

    def make_copy(e, off, size):
        src = pl.multiple_of(srun_ref[i * N_EXPERTS + e] + off, RUN_ALIGN)
        dst = pl.multiple_of(start_ref[i * N_EXPERTS + e] + off, RUN_ALIGN)
        return pltpu.make_async_copy(sorted_ref.at[pl.ds(src, size)], xs_out.at[pl.ds(dst, size)], sem)

    _run_copies(i, l8_ref, make_copy, wait=False)
    _run_copies(i, l8_ref, make_copy, wait=True)


def _dispatch(h2, topi, rank, srow, tables, n_rows, tm):
    t, d = h2.shape
    tile = lambda w: pl.BlockSpec((tm, w), lambda i, *_: (i, 0))
    grid_spec = pltpu.PrefetchScalarGridSpec(
        num_scalar_prefetch=len(tables),
        grid=(t // tm,),
        in_specs=[tile(d), tile(LANES), tile(LANES),
                  pl.BlockSpec((None, SUBLANES, LANES), lambda i, *_: (i, 0, 0))],
        out_specs=pl.BlockSpec(memory_space=pl.ANY),
        scratch_shapes=[pltpu.VMEM((SORT_ROWS, d // 2), jnp.uint32),
                        pltpu.VMEM((MOE_BLOCK, d // 2), jnp.uint32),
                        pltpu.SemaphoreType.DMA, pltpu.SemaphoreType.DMA],
    )
    return pl.pallas_call(
        functools.partial(_dispatch_kernel, min_blocks=t * TOP_K // MOE_BLOCK),
        grid_spec=grid_spec,
        out_shape=jax.ShapeDtypeStruct((n_rows, d // 2), jnp.uint32),
        compiler_params=_cparams(("arbitrary",)),
        name="moe_dispatch",
    )(*tables, h2, topi, rank, srow)


def _expert_kernel(be_ref, nu_ref, x_ref, wgu_ref, bgu_ref, wd_ref, bd_ref, y_ref,
                   wgu_bf, wd_bf):
    j = pl.program_id(0)
    half = D_MODEL // 2

    @pl.when((j == 0) | (be_ref[j] != be_ref[jnp.maximum(j - 1, 0)]))
    def _():
        wgu_bf[...] = wgu_ref[...].astype(BF16)
        wd_bf[...] = wd_ref[...].astype(BF16)

    @pl.when(j < nu_ref[0])
    def _():
        xu = x_ref[...]
        x_lo = pltpu.bitcast(xu << 16, F32).astype(BF16)
        x_hi = pltpu.bitcast(xu & jnp.uint32(0xFFFF0000), F32).astype(BF16)
        gu = (jnp.dot(x_lo, wgu_bf[:half, :], preferred_element_type=F32)
              + jnp.dot(x_hi, wgu_bf[half:, :], preferred_element_type=F32) + bgu_ref[...])
        g = jnp.minimum(gu[:, :D_MODEL], SWIGLU_LIMIT)
        u = jnp.clip(gu[:, D_MODEL:], -SWIGLU_LIMIT, SWIGLU_LIMIT)
        act = (u + 1.0) * (g * _sigmoid(g * SWIGLU_ALPHA))
        y_ref[...] = jnp.dot(act.astype(BF16), wd_bf[...], preferred_element_type=F32) + bd_ref[...]

    @pl.when(j >= nu_ref[0])
    def _():
        y_ref[...] = jnp.zeros_like(y_ref)


def _experts(xs, block_e, n_used, wgu, bgu, wd, bd):
    p = xs.shape[0]
    e, d, _ = wgu.shape
    n_blocks = p // MOE_BLOCK
    grid_spec = pltpu.PrefetchScalarGridSpec(
        num_scalar_prefetch=2,
        grid=(n_blocks,),
        in_specs=[pl.BlockSpec((MOE_BLOCK, d // 2), lambda j, be, nu: (j, 0)),
                  pl.BlockSpec((None, d, 2 * d), lambda j, be, nu: (be[j], 0, 0)),
                  pl.BlockSpec((None, 1, 2 * d), lambda j, be, nu: (be[j], 0, 0)),
                  pl.BlockSpec((None, d, d), lambda j, be, nu: (be[j], 0, 0)),
                  pl.BlockSpec((None, 1, d), lambda j, be, nu: (be[j], 0, 0))],
        out_specs=pl.BlockSpec((MOE_BLOCK, d), lambda j, be, nu: (j, 0)),
        scratch_shapes=[pltpu.VMEM((d, 2 * d), BF16), pltpu.VMEM((d, d), BF16)],
    )
    return pl.pallas_call(
        _expert_kernel,
        grid_spec=grid_spec,
        out_shape=jax.ShapeDtypeStruct((p, d), F32),
        compiler_params=_cparams(("arbitrary",)),
        name="moe_experts",
    )(block_e, n_used, xs, wgu, bgu.reshape(e, 1, 2 * d), wd, bd.reshape(e, 1, d))


def _combine_kernel(start_ref, srun_ref, l8_ref,
                    y_hbm, topi_ref, rank_ref, srow_ref, gate_ref, x1_ref, mod_ref, nw_ref, o_ref,
                    yb0, yb1, sem):
    i = pl.program_id(0)
    tm = x1_ref.shape[0]
    ybufs = (yb0, yb1)

    def run_copies(tile, slot, wait):
        def make_copy(e, off, size):
            src = pl.multiple_of(start_ref[tile * N_EXPERTS + e] + off, RUN_ALIGN)
            dst = pl.multiple_of(srun_ref[tile * N_EXPERTS + e] + off, RUN_ALIGN)
            return pltpu.make_async_copy(y_hbm.at[pl.ds(src, size)],
                                         ybufs[slot].at[pl.ds(dst, size)], sem.at[slot])
        _run_copies(tile, l8_ref, make_copy, wait)

    def finish(slot):
        run_copies(i, slot, wait=True)
        pos = _positions(topi_ref, rank_ref, srow_ref)
        gate = gate_ref[...]
        row = lax.broadcasted_iota(I32, (tm, SORT_ROWS), 1)
        sel = jnp.zeros((tm, SORT_ROWS), F32)
        for kk, p in enumerate(pos):
            sel = jnp.where(row == p, gate[:, kk:kk + 1], sel)
        moe = jnp.dot(sel.astype(BF16), ybufs[slot][...].astype(BF16), preferred_element_type=F32)
        g2 = mod_ref[5:6, :]
        o_ref[...] = x1_ref[...] + g2 * _rms(moe, nw_ref[...])

    @pl.when(i == 0)
    def _():
        yb0[...] = jnp.zeros_like(yb0)
        yb1[...] = jnp.zeros_like(yb1)
        run_copies(0, 0, wait=False)

    for slot in range(2):
        @pl.when((i + 1 < pl.num_programs(0)) & ((i + 1) % 2 == slot))
        def _(slot=slot):
            run_copies(i + 1, slot, wait=False)

    for slot in range(2):
        @pl.when(i % 2 == slot)
        def _(slot=slot):
            finish(slot)


def _combine(y, topi, rank, srow, gate, x1, mod, nw, tables, seq, tm):
    t, d = x1.shape
    tiles_per_batch = seq // tm
    tile = lambda w: pl.BlockSpec((tm, w), lambda i, *_: (i, 0))
    grid_spec = pltpu.PrefetchScalarGridSpec(
        num_scalar_prefetch=len(tables),
        grid=(t // tm,),
        in_specs=[pl.BlockSpec(memory_space=pl.ANY),
                  tile(LANES), tile(LANES),
                  pl.BlockSpec((None, SUBLANES, LANES), lambda i, *_: (i, 0, 0)),
                  tile(LANES), tile(d),
                  pl.BlockSpec((None, 6, d), lambda i, *_: (i // tiles_per_batch, 0, 0)),
                  pl.BlockSpec((1, d), lambda i, *_: (0, 0))],
        out_specs=tile(d),
        scratch_shapes=[pltpu.VMEM((SORT_ROWS, d), F32), pltpu.VMEM((SORT_ROWS, d), F32),
                        pltpu.SemaphoreType.DMA((2,))],
    )
    return pl.pallas_call(
        _combine_kernel,
        grid_spec=grid_spec,
        out_shape=jax.ShapeDtypeStruct((t, d), F32),
        compiler_params=_cparams(("arbitrary",)),
        name="moe_combine",
    )(*tables, y, topi, rank, srow, gate, x1, mod, nw)


def _rope_tables(seq):
    pos = jnp.arange(seq, dtype=F32)
    inv = ROPE_THETA ** (-jnp.arange(0, DA_HEAD_DIM, 2, dtype=F32) / DA_HEAD_DIM)
    ang = pos[:, None] * inv[None, :]
    cos, sin = jnp.cos(ang), jnp.sin(ang)
    zero = jnp.zeros_like(sin)
    cos_t = jnp.tile(cos, (1, LANES // cos.shape[1]))
    sin_lo = jnp.tile(jnp.concatenate([-sin, zero], axis=1), (1, 2))
    sin_hi = jnp.tile(jnp.concatenate([zero, sin], axis=1), (1, 2))
    return cos_t, sin_lo, sin_hi


def kernel(x, c, w_mod, b_mod, norm_pre_mix, norm_post_mix, w_in, da_lambda_q1, da_lambda_k1,
           da_lambda_q2, da_lambda_k2, da_subln, hg_lb_logits, hg_norm, w_branch_a, w_branch_b,
           w_out, norm_pre_ffn, norm_post_ffn, router_w, router_b, w_gate_up, b_gate_up,
           w_down, b_down):
    bsz, seq, d = x.shape
    depth = w_mod.shape[0]
    assert depth == 1 and d == D_MODEL
    t = bsz * seq
    lyr = 0
    lambda_init = 0.8 - 0.6 * math.exp(-0.3 * lyr)
    cos, slo, shi = _rope_tables(seq)

    xf = x.reshape(t, d)
    mod = _modulation(c, w_mod[lyr], b_mod[lyr]).reshape(bsz, 6, d)

    qT, k, vT, hq, hf, hi, hg, ga, gb = _in_projection(
        xf, mod, norm_pre_mix[lyr][None], w_in[lyr].astype(BF16), cos, slo, shi, seq, tm=512)
    ya = _diff_attention(qT, k, vT, da_lambda_q1[lyr][None], da_lambda_k1[lyr][None],
                         da_lambda_q2[lyr][None], da_lambda_k2[lyr][None], da_subln[lyr],
                         bsz, seq, lambda_init, tq=1024, tk=512)
    yb = _hgrn2(hq, hf, hi, hg, hg_lb_logits[lyr:lyr + 2], hg_norm[lyr][None], bsz, seq, tc=512, nh=4)

    rw = jnp.zeros((d, LANES), BF16).at[:, :N_EXPERTS].set(router_w[lyr].astype(BF16))
    rb = jnp.full((1, LANES), NEG_BIG, F32).at[0, :N_EXPERTS].set(router_b[lyr])
    x1, h2, topi, gate, rank, counts = _merge_route(
        xf, mod, ya, yb, ga, gb, w_branch_a[lyr].astype(BF16), w_branch_b[lyr].astype(BF16),
        w_out[lyr].astype(BF16), norm_post_mix[lyr][None], norm_pre_ffn[lyr][None], rw, rb,
        seq, tm=MOE_TILE)

    n_tiles = t // MOE_TILE
    up = lambda a, m: (a + m - 1) // m * m
    cnt = counts[:, 0, :N_EXPERTS].astype(I32)
    run = up(cnt, RUN_ALIGN)
    srun = jnp.cumsum(run, axis=1) - run
    e_len = up(jnp.sum(run, axis=0), MOE_BLOCK)
    e_end = jnp.cumsum(e_len)
    start = (e_end - e_len)[None, :] + jnp.cumsum(run, axis=0) - run
    n_assign = t * TOP_K
    n_rows = (up(n_assign + n_tiles * N_EXPERTS * (RUN_ALIGN - 1), MOE_BLOCK)
              + N_EXPERTS * MOE_BLOCK)
    n_blocks = n_rows // MOE_BLOCK
    block_start = jnp.arange(n_blocks, dtype=I32) * MOE_BLOCK
    block_e = jnp.minimum(jnp.sum((e_end[None, :] <= block_start[:, None]).astype(I32), axis=1),
                          N_EXPERTS - 1)
    n_used = (e_end[-1:] // MOE_BLOCK).astype(I32)
    srow = jnp.broadcast_to(jnp.pad(srun.astype(F32), ((0, 0), (0, LANES - N_EXPERTS)))[:, None, :],
                            (n_tiles, SUBLANES, LANES))
    run_tables = (start.reshape(-1), srun.reshape(-1), (run // RUN_ALIGN).reshape(-1))

    xs = _dispatch(h2, topi, rank, srow, run_tables + (e_end.astype(I32), e_len.astype(I32)),
                   n_rows, tm=MOE_TILE)
    y = _experts(xs, block_e, n_used, w_gate_up[lyr], b_gate_up[lyr], w_down[lyr], b_down[lyr])
    out = _combine(y, topi, rank, srow, gate, x1, mod, norm_post_ffn[lyr][None], run_tables,
                   seq, tm=MOE_TILE)
    return out.reshape(bsz, seq, d)
```

```python
import functools
import math

import jax
import jax.numpy as jnp
from jax import lax
from jax.experimental import pallas as pl
from jax.experimental.pallas import tpu as pltpu

F32 = jnp.float32
BF16 = jnp.bfloat16
I32 = jnp.int32

D_MODEL = 1024
CHUNK = 64
EPS = 1e-6
ROPE_THETA = 10000.0
DA_HEADS = 4
DA_HEAD_DIM = 64
HG_HEADS = 4
HEAD_W = 128
BRANCH_W = 512
V_ROWS = HEAD_W + 16
ATTN_STRIP = 256
N_EXPERTS = 32
TOP_K = 4
SWIGLU_LIMIT = 7.0
SWIGLU_ALPHA = 1.702
MOE_BLOCK = 512
MOE_TILE = 512
LANES = 128
SUBLANES = 8
RUN_ALIGN = SUBLANES
RUN_BITS = (MOE_TILE // RUN_ALIGN).bit_length()
RUN_SMALL_BITS = 4
SORT_ROWS = MOE_TILE * TOP_K + N_EXPERTS * RUN_ALIGN
NEG_BIG = -1e30

VMEM_LIMIT = 56 * 1024 * 1024


def _cparams(sem):
    return pltpu.CompilerParams(dimension_semantics=sem, vmem_limit_bytes=VMEM_LIMIT)


def _rms(x, w):
    return x * lax.rsqrt(jnp.mean(x * x, axis=-1, keepdims=True) + EPS) * w


def _sigmoid(x):
    return 1.0 / (1.0 + jnp.exp(-x))


def _split3(x):
    a = x.astype(BF16)
    r = x - a.astype(F32)
    b = r.astype(BF16)
    c = (r - b.astype(F32)).astype(BF16)
    return a, b, c


def _mod_kernel(c_ref, w_ref, b_ref, o_ref):
    c = c_ref[...]
    ca = c * _sigmoid(c)
    o_ref[...] = jnp.dot(ca, w_ref[...], preferred_element_type=F32,
                         precision=lax.Precision.HIGHEST) + b_ref[...]


def _modulation(c, w_mod, b_mod):
    bsz, d = c.shape
    n = w_mod.shape[1]
    tn = 1536
    return pl.pallas_call(
        _mod_kernel,
        grid=(n // tn,),
        in_specs=[pl.BlockSpec((bsz, d), lambda j: (0, 0)),
                  pl.BlockSpec((d, tn), lambda j: (0, j)),
                  pl.BlockSpec((1, tn), lambda j: (0, j))],
        out_specs=pl.BlockSpec((bsz, tn), lambda j: (0, j)),
        out_shape=jax.ShapeDtypeStruct((bsz, n), F32),
        compiler_params=_cparams(("arbitrary",)),
        name="modulation",
    )(c, w_mod, b_mod.reshape(1, n))


def _rope(x, cos, sin_lo, sin_hi):
    return x * cos + pltpu.roll(x, 96, 1) * sin_lo + pltpu.roll(x, 32, 1) * sin_hi


def _inproj_kernel(x_ref, mod_ref, nw_ref, w_ref, cos_ref, slo_ref, shi_ref,
                   qT_ref, k_ref, vT_ref, hq_ref, hf_ref, hi_ref, hg_ref, ga_ref, gb_ref):
    x = x_ref[...]
    sh1 = mod_ref[0:1, :]
    sc1 = mod_ref[1:2, :]
    h = (_rms(x, nw_ref[...]) * (1.0 + sc1) + sh1).astype(BF16)
    cos, slo, shi = cos_ref[...], slo_ref[...], shi_ref[...]
    scale = DA_HEAD_DIM ** -0.5 * math.log2(math.e)

    def seg(i, width=BRANCH_W):
        return jnp.dot(h, w_ref[:, i:i + width], preferred_element_type=F32)

    q = seg(0)
    k = seg(BRANCH_W)
    for hd in range(DA_HEADS):
        sl = slice(hd * HEAD_W, (hd + 1) * HEAD_W)
        qT_ref[sl, :] = (_rope(q[:, sl], cos, slo, shi) * scale).T.astype(BF16)
        k_ref[:, sl] = _rope(k[:, sl], cos, slo, shi).astype(BF16)
    vT = seg(2 * BRANCH_W).T.astype(BF16)
    ones = jnp.ones((V_ROWS - HEAD_W, vT.shape[1]), BF16)
    for hd in range(DA_HEADS):
        vT_ref[hd * V_ROWS:hd * V_ROWS + HEAD_W, :] = vT[hd * HEAD_W:(hd + 1) * HEAD_W]
        vT_ref[hd * V_ROWS + HEAD_W:(hd + 1) * V_ROWS, :] = ones
    hq = seg(3 * BRANCH_W)
    hq_ref[...] = (hq * _sigmoid(hq)).astype(BF16)
    hf_ref[...] = seg(4 * BRANCH_W)
    hi_ref[...] = seg(5 * BRANCH_W).astype(BF16)
    hg = seg(6 * BRANCH_W)
    hg_ref[...] = (hg * _sigmoid(hg)).astype(BF16)
    for j in range(2):
        ga_ref[:, j * BRANCH_W:(j + 1) * BRANCH_W] = _sigmoid(seg((7 + j) * BRANCH_W)).astype(BF16)
        gb_ref[:, j * BRANCH_W:(j + 1) * BRANCH_W] = _sigmoid(seg((9 + j) * BRANCH_W)).astype(BF16)


def _in_projection(xf, mod, norm_w, w_in_bf, cos, slo, shi, seq, tm):
    t, d = xf.shape
    tiles_per_batch = seq // tm
    bspec = lambda w: pl.BlockSpec((tm, w), lambda i: (i, 0))
    tab = pl.BlockSpec((tm, LANES), lambda i: (i % tiles_per_batch, 0))
    widths = [BRANCH_W] * 7 + [D_MODEL, D_MODEL]
    dtypes = [BF16, BF16, BF16, BF16, F32, BF16, BF16, BF16, BF16]
    out_specs = [bspec(w) for w in widths]
    out_shape = [jax.ShapeDtypeStruct((t, w), dt) for w, dt in zip(widths, dtypes)]
    for idx, rows in ((0, BRANCH_W), (2, DA_HEADS * V_ROWS)):
        out_specs[idx] = pl.BlockSpec((None, rows, tm),
                                      lambda i: (i // tiles_per_batch, 0, i % tiles_per_batch))
        out_shape[idx] = jax.ShapeDtypeStruct((t // seq, rows, seq), BF16)
    return pl.pallas_call(
        _inproj_kernel,
        grid=(t // tm,),
        in_specs=[bspec(d),
                  pl.BlockSpec((None, 6, d), lambda i: (i // tiles_per_batch, 0, 0)),
                  pl.BlockSpec((1, d), lambda i: (0, 0)),
                  pl.BlockSpec(w_in_bf.shape, lambda i: (0, 0), pipeline_mode=pl.Buffered(1)),
                  tab, tab, tab],
        out_specs=out_specs,
        out_shape=out_shape,
        compiler_params=_cparams(("parallel",)),
        name="in_projection",
    )(xf, mod, norm_w, w_in_bf, cos, slo, shi)


def _attn_kernel(lq1_ref, lk1_ref, lq2_ref, lk2_ref, subln_ref, qT_ref, k_ref, vT_ref, o_ref,
                 qq_ref, s0_ref, s1_ref, p0_ref, p1_ref, a0_ref, a1_ref, m_ref, acc_ref,
                 *, tq, tk, lambda_init):
    qi = pl.program_id(2)
    ndiag = tq // tk
    s_refs, p_refs, a_refs = (s0_ref, s1_ref), (p0_ref, p1_ref), (a0_ref, a1_ref)
    qT = qT_ref[...]
    feat = lax.broadcasted_iota(I32, qT.shape, 0)
    zero = jnp.zeros_like(qT)
    qq_ref[:, :tq] = jnp.where(feat < DA_HEAD_DIM, qT, zero)
    qq_ref[:, tq:] = jnp.where(feat >= DA_HEAD_DIM, qT, zero)

    strips = [slice(c * ATTN_STRIP, (c + 1) * ATTN_STRIP) for c in range(2 * tq // ATTN_STRIP)]

    def scores(kb, dst, cs):
        start = pl.multiple_of(kb * tk, tk)
        dst[:, cs] = jnp.dot(k_ref[pl.ds(start, tk), :], qq_ref[:, cs], preferred_element_type=F32)

    def accumulate(kb, slot, cs):
        start = pl.multiple_of(kb * tk, tk)
        pv = jnp.dot(vT_ref[:, pl.ds(start, tk)], p_refs[slot][:, cs], preferred_element_type=F32)
        acc_ref[:, cs] = a_refs[slot][:, cs] * acc_ref[:, cs] + pv

    def softmax(slot, cs, diag):
        def piece(rc):
            sc = s_refs[slot][rc * CHUNK:(rc + 1) * CHUNK, cs]
            if diag is not None:
                qchunk = (lax.broadcasted_iota(I32, (1, ATTN_STRIP), 1) + cs.start) % tq // CHUNK
                sc = jnp.where(qchunk >= diag * (tk // CHUNK) + rc, sc, NEG_BIG)
            return sc

        top = piece(0)
        for rc in range(1, tk // CHUNK):
            top = jnp.maximum(top, piece(rc))
        m_old = m_ref[:, cs]
        m_new = jnp.maximum(m_old, jnp.max(top, axis=0, keepdims=True))
        m_ref[:, cs] = m_new
        a_refs[slot][:, cs] = jnp.exp2(m_old - m_new)
        for rc in range(tk // CHUNK):
            p_refs[slot][rc * CHUNK:(rc + 1) * CHUNK, cs] = jnp.exp2((piece(rc) - m_new).astype(BF16))

    def stage(kb, slot, diag=None, last=False):
        for cs in strips:
            accumulate(jnp.maximum(kb - 1, 0), 1 - slot, cs)
            softmax(slot, cs, diag)
            if not last:
                scores(kb + 1, s_refs[1 - slot], cs)

    p1_ref[...] = jnp.zeros_like(p1_ref)
    a1_ref[...] = jnp.ones_like(a1_ref)
    m_ref[...] = jnp.full(m_ref.shape, NEG_BIG, F32)
    acc_ref[...] = jnp.zeros_like(acc_ref)
    for cs in strips:
        scores(0, s0_ref, cs)

    def pair(i, c):
        stage(2 * i, 0)
        stage(2 * i + 1, 1)
        return c

    first_diag = qi * ndiag
    lax.fori_loop(0, first_diag // 2, pair, 0)
    for dg in range(ndiag):
        stage(first_diag + dg, dg % 2, diag=dg, last=dg == ndiag - 1)
    for cs in strips:
        accumulate(first_diag + ndiag - 1, (ndiag - 1) % 2, cs)

    acc = acc_ref[...]
    o = acc[:HEAD_W] / acc[HEAD_W:HEAD_W + 1]
    lam = (jnp.exp(jnp.sum(lq1_ref[...] * lk1_ref[...], axis=1, keepdims=True))
           - jnp.exp(jnp.sum(lq2_ref[...] * lk2_ref[...], axis=1, keepdims=True))
           + lambda_init)
    o = o[:, :tq] - lam * o[:, tq:]
    o = o * lax.rsqrt(jnp.mean(o * o, axis=0, keepdims=True) + EPS) * subln_ref[...]
    o_ref[...] = (o * (1.0 - lambda_init)).T.astype(BF16)


def _diff_attention(qT, k, vT, lq1, lk1, lq2, lk2, subln, bsz, seq, lambda_init, tq, tk):
    assert tq % (2 * tk) == 0
    k3 = k.reshape(bsz, seq, BRANCH_W)
    small = lambda n: pl.BlockSpec((1, n), lambda b, h, i: (0, 0))
    out = pl.pallas_call(
        functools.partial(_attn_kernel, tq=tq, tk=tk, lambda_init=lambda_init),
        grid=(bsz, DA_HEADS, seq // tq),
        in_specs=[small(DA_HEAD_DIM)] * 4 + [
            pl.BlockSpec((HEAD_W, 1), lambda b, h, i: (0, 0)),
            pl.BlockSpec((None, HEAD_W, tq), lambda b, h, i: (b, h, i)),
            pl.BlockSpec((None, seq, HEAD_W), lambda b, h, i: (b, 0, h)),
            pl.BlockSpec((None, V_ROWS, seq), lambda b, h, i: (b, h, 0))],
        out_specs=pl.BlockSpec((None, tq, HEAD_W), lambda b, h, i: (b, i, h)),
        out_shape=jax.ShapeDtypeStruct((bsz, seq, BRANCH_W), BF16),
        scratch_shapes=[pltpu.VMEM((HEAD_W, 2 * tq), BF16),
                        pltpu.VMEM((tk, 2 * tq), F32), pltpu.VMEM((tk, 2 * tq), F32),
                        pltpu.VMEM((tk, 2 * tq), BF16), pltpu.VMEM((tk, 2 * tq), BF16),
                        pltpu.VMEM((1, 2 * tq), F32), pltpu.VMEM((1, 2 * tq), F32),
                        pltpu.VMEM((1, 2 * tq), F32), pltpu.VMEM((V_ROWS, 2 * tq), F32)],
        compiler_params=_cparams(("parallel", "parallel", "arbitrary")),
        name="diff_attention",
    )(lq1, lk1, lq2, lk2, subln.reshape(HEAD_W, 1), qT, k3, vT)
    return out.reshape(bsz * seq, BRANCH_W)


def _hgrn_kernel(lbl_ref, nw_ref, q_ref, f_ref, i_ref, g_ref, o_ref, st_ref, *, tc, nh):
    @pl.when(pl.program_id(2) == 0)
    def _():
        st_ref[...] = jnp.zeros_like(st_ref)

    row = lax.broadcasted_iota(I32, (CHUNK, CHUNK), 0)
    col = lax.broadcasted_iota(I32, (CHUNK, CHUNK), 1)
    causal = col <= row
    tril = jnp.where(causal, 1.0, 0.0).astype(BF16)
    nt = (((1,), (1,)), ((), ()))
    tn = (((0,), (0,)), ((), ()))

    nchunk = tc // CHUNK
    units = [(hh, c) for hh in range(nh) for c in range(nchunk)]
    rows = lambda c: slice(c * CHUNK, (c + 1) * CHUNK)
    q, k, v, parts = [], [], [], []
    for hh in range(nh):
        hs = slice(hh * HEAD_W, (hh + 1) * HEAD_W)
        lg = lbl_ref[:, hs]
        ex = jnp.exp(lg - jnp.max(lg, axis=0, keepdims=True))
        lb = ex[0:1] / jnp.sum(ex, axis=0, keepdims=True)
        f = lb + (1.0 - lb) * _sigmoid(f_ref[:, hs])
        q.append(q_ref[:, hs].astype(F32))
        k.append(1.0 - f)
        v.append(i_ref[:, hs])
        parts.append(_split3(jnp.log(f)))
    gall = sum(jnp.dot(tril, jnp.concatenate([parts[hh][p][rows(c)] for hh, c in units], axis=1),
                       preferred_element_type=F32) for p in range(3))
    gcum = [gall[:, u * HEAD_W:(u + 1) * HEAD_W] for u in range(len(units))]
    gtot = [g[CHUNK - 1:CHUNK, :] for g in gcum]
    q_t = [(q[hh][rows(c)] * jnp.exp(gcum[u])).astype(BF16) for u, (hh, c) in enumerate(units)]
    k_t = [(k[hh][rows(c)] * jnp.exp(-gcum[u])).astype(BF16) for u, (hh, c) in enumerate(units)]
    k_dec = [(k[hh][rows(c)] * jnp.exp(gtot[u] - gcum[u])).astype(BF16)
             for u, (hh, c) in enumerate(units)]
    a = [lax.dot_general(q_t[u], k_t[u], nt, preferred_element_type=F32) for u in range(len(units))]
    ds = [lax.dot_general(v[hh][rows(c)], k_dec[u], tn, preferred_element_type=F32)
          for u, (hh, c) in enumerate(units)]
    a = [jnp.where(causal, x, 0.0).astype(BF16) for x in a]
    st_in = []
    for hh in range(nh):
        st = st_ref[hh]
        for c in range(nchunk):
            u = hh * nchunk + c
            st_in.append(st.astype(BF16))
            st = jnp.exp(gtot[u]) * st + ds[u]
        st_ref[hh] = st
    outs = [jnp.dot(a[u], v[hh][rows(c)], preferred_element_type=F32)
            + lax.dot_general(q_t[u], st_in[u], nt, preferred_element_type=F32)
            for u, (hh, c) in enumerate(units)]
    for hh in range(nh):
        hs = slice(hh * HEAD_W, (hh + 1) * HEAD_W)
        o = jnp.concatenate(outs[hh * nchunk:(hh + 1) * nchunk], axis=0)
        o_ref[:, hs] = (_rms(o, nw_ref[...]) * g_ref[:, hs].astype(F32)).astype(BF16)


def _hgrn2(hq, hf, hi, hg, lb_logits, norm_w, bsz, seq, tc, nh):
    r3 = lambda a: a.reshape(bsz, seq, BRANCH_W)
    blk = pl.BlockSpec((None, tc, nh * HEAD_W), lambda b, h, i: (b, i, h))
    out = pl.pallas_call(
        functools.partial(_hgrn_kernel, tc=tc, nh=nh),
        grid=(bsz, HG_HEADS // nh, seq // tc),
        in_specs=[pl.BlockSpec((2, nh * HEAD_W), lambda b, h, i: (0, h)),
                  pl.BlockSpec((1, HEAD_W), lambda b, h, i: (0, 0)),
                  blk, blk, blk, blk],
        out_specs=blk,
        out_shape=jax.ShapeDtypeStruct((bsz, seq, BRANCH_W), BF16),
        scratch_shapes=[pltpu.VMEM((nh, HEAD_W, HEAD_W), F32)],
        compiler_params=_cparams(("parallel", "parallel", "arbitrary")),
        name="hgrn2",
    )(lb_logits, norm_w, r3(hq), r3(hf), r3(hi), r3(hg))
    return out.reshape(bsz * seq, BRANCH_W)


def _merge_kernel(x_ref, mod_ref, ya_ref, yb_ref, ga_ref, gb_ref, wa_ref, wb_ref, wo_ref,
                  npost_ref, npre_ref, rw_ref, rb_ref,
                  x1_ref, h2_ref, topi_ref, gate_ref, rank_ref, cnt_ref, lg_ref, *, tm):
    @pl.when(pl.program_id(0) == 0)
    def _():
        lg_ref[...] = jnp.zeros_like(lg_ref)

    logits = lg_ref[...]

    g1 = mod_ref[2:3, :]
    sh2 = mod_ref[3:4, :]
    sc2 = mod_ref[4:5, :]
    a = jnp.dot(ya_ref[...], wa_ref[...], preferred_element_type=F32)
    b = jnp.dot(yb_ref[...], wb_ref[...], preferred_element_type=F32)
    merged = ga_ref[...] * a.astype(BF16) + gb_ref[...] * b.astype(BF16)
    y = jnp.dot(merged, wo_ref[...], preferred_element_type=F32)
    x1 = x_ref[...] + g1 * _rms(y, npost_ref[...])
    x1_ref[...] = x1
    h2 = (_rms(x1, npre_ref[...]) * (1.0 + sc2) + sh2).astype(BF16)
    h2_ref[...] = h2
    lg_ref[...] = jnp.dot(h2, rw_ref[...], preferred_element_type=F32) + rb_ref[...]

    lane = lax.broadcasted_iota(I32, logits.shape, 1)
    work = logits
    vals, idxs = [], []
    for _ in range(TOP_K):
        mx = jnp.max(work, axis=1, keepdims=True)
        idx = jnp.min(jnp.where(work == mx, lane, LANES), axis=1, keepdims=True)
        vals.append(mx)
        idxs.append(idx)
        work = jnp.where(lane == idx, -jnp.inf, work)
    es = [jnp.exp(vk - vals[0]) for vk in vals]
    den = es[0] + es[1] + es[2] + es[3]

    onehot = jnp.zeros(logits.shape, F32)
    for idx in idxs:
        onehot = jnp.where(lane == idx, 1.0, onehot)
    row = lax.broadcasted_iota(I32, (tm, tm), 0)
    col = lax.broadcasted_iota(I32, (tm, tm), 1)
    strict = jnp.where(col < row, 1.0, 0.0).astype(BF16)
    prefix = jnp.dot(strict, onehot.astype(BF16), preferred_element_type=F32)
    cnt_ref[...] = jnp.broadcast_to(jnp.sum(onehot, axis=0, keepdims=True), cnt_ref.shape)

    topi = jnp.zeros(logits.shape, I32)
    gate = jnp.zeros(logits.shape, F32)
    rank = jnp.zeros(logits.shape, I32)
    for kk in range(TOP_K):
        rk = jnp.sum(jnp.where(lane == idxs[kk], prefix, 0.0), axis=1, keepdims=True)
        topi = jnp.where(lane == kk, idxs[kk], topi)
        gate = jnp.where(lane == kk, es[kk] / den, gate)
        rank = jnp.where(lane == kk, rk.astype(I32), rank)
    topi_ref[...] = topi
    gate_ref[...] = gate
    rank_ref[...] = rank


def _merge_route(xf, mod, ya, yb, ga, gb, wa, wb, wo, npost, npre, rw, rb, seq, tm):
    t, d = xf.shape
    n = t // tm
    tiles_per_batch = seq // tm
    proj = lambda w: pl.BlockSpec((tm, w), lambda i: (jnp.minimum(i, n - 1), 0))
    route = lambda w: pl.BlockSpec((tm, w), lambda i: (jnp.maximum(i - 1, 0), 0))
    const = lambda a: pl.BlockSpec(a.shape, lambda i: (0,) * a.ndim)
    lane_out = jax.ShapeDtypeStruct((t, LANES), I32)
    return pl.pallas_call(
        functools.partial(_merge_kernel, tm=tm),
        grid=(n + 1,),
        in_specs=[proj(d),
                  pl.BlockSpec((None, 6, d),
                               lambda i: (jnp.minimum(i, n - 1) // tiles_per_batch, 0, 0)),
                  proj(BRANCH_W), proj(BRANCH_W), proj(d), proj(d),
                  const(wa), const(wb), const(wo), const(npost), const(npre),
                  const(rw), const(rb)],
        out_specs=[proj(d), proj(d), route(LANES), route(LANES), route(LANES),
                   pl.BlockSpec((None, SUBLANES, LANES), lambda i: (jnp.maximum(i - 1, 0), 0, 0))],
        out_shape=[jax.ShapeDtypeStruct((t, d), F32), jax.ShapeDtypeStruct((t, d), BF16),
                   lane_out, jax.ShapeDtypeStruct((t, LANES), F32), lane_out,
                   jax.ShapeDtypeStruct((n, SUBLANES, LANES), F32)],
        scratch_shapes=[pltpu.VMEM((tm, LANES), F32)],
        compiler_params=_cparams(("arbitrary",)),
        name="merge_route",
    )(xf, mod, ya, yb, ga, gb, wa, wb, wo, npost, npre, rw, rb)


def _positions(topi_ref, rank_ref, srow_ref):
    topi = topi_ref[...]
    rank = rank_ref[...]
    lane = lax.broadcasted_iota(I32, topi.shape, 1)
    srow = srow_ref[0:1, :]
    pos = []
    for kk in range(TOP_K):
        base = jnp.sum(jnp.where(lane == topi[:, kk:kk + 1], srow, 0.0), axis=1, keepdims=True)
        pos.append(base.astype(I32) + rank[:, kk:kk + 1])
    return pos


def _run_copies(tile, l8_ref, make_copy, wait):
    def pieces(e, n8, bits, off):
        for b in bits:
            size = RUN_ALIGN << b
            bit = (n8 >> b) & 1

            @pl.when(bit == 1)
            def _(off=off, size=size):
                cp = make_copy(e, off, size)
                cp.wait() if wait else cp.start()

            off = off + bit * size

    for e in range(N_EXPERTS):
        n8 = l8_ref[tile * N_EXPERTS + e]
        big = (n8 >> RUN_SMALL_BITS) << RUN_SMALL_BITS

        @pl.when(big > 0)
        def _(e=e, n8=n8):
            pieces(e, n8, range(RUN_BITS - 1, RUN_SMALL_BITS - 1, -1), jnp.int32(0))

        pieces(e, n8, range(RUN_SMALL_BITS - 1, -1, -1), big * RUN_ALIGN)


def _dispatch_kernel(start_ref, srun_ref, l8_ref, eend_ref, elen_ref,
                     h_ref, topi_ref, rank_ref, srow_ref, xs_out,
                     sel0, sel1, srt0, srt1, zbuf, zsem, sem, *, min_blocks):
    i = pl.program_id(0)
    n_tiles = pl.num_programs(0) - 1
    sels, srts = (sel0, sel1), (srt0, srt1)
    tm = h_ref.shape[0]
    half = h_ref.shape[1] // 2
    n_blocks = xs_out.shape[0] // MOE_BLOCK
    n_used = eend_ref[N_EXPERTS - 1] // MOE_BLOCK

    def zero_copy(start):
        start = pl.multiple_of(start, MOE_BLOCK)
        return pltpu.make_async_copy(zbuf, xs_out.at[pl.ds(start, MOE_BLOCK)], zsem)

    def zero_fill(wait):
        for e in range(N_EXPERTS):
            @pl.when(elen_ref[e] > 0)
            def _(e=e):
                cp = zero_copy(eend_ref[e] - MOE_BLOCK)
                cp.wait() if wait else cp.start()
        for blk in range(min_blocks, n_blocks):
            @pl.when(blk >= n_used)
            def _(blk=blk):
                cp = zero_copy(blk * MOE_BLOCK)
                cp.wait() if wait else cp.start()

    @pl.when(i == 0)
    def _():
        zbuf[...] = jnp.zeros_like(zbuf)
        sel0[...] = jnp.zeros_like(sel0)
        sel1[...] = jnp.zeros_like(sel1)
        zero_fill(wait=False)
        zero_fill(wait=True)

    def run_copies(tile, slot, wait):
        def make_copy(e, off, size):
            src = pl.multiple_of(srun_ref[tile * N_EXPERTS + e] + off, RUN_ALIGN)
            dst = pl.multiple_of(start_ref[tile * N_EXPERTS + e] + off, RUN_ALIGN)
            return pltpu.make_async_copy(srts[slot].at[pl.ds(src, size)],
                                         xs_out.at[pl.ds(dst, size)], sem.at[slot])
        _run_copies(tile, l8_ref, make_copy, wait)

    for s in range(2):
        @pl.when(i % 2 == s)
        def _(s=s):
            pos = _positions(topi_ref, rank_ref, srow_ref)
            slot = lax.broadcasted_iota(I32, (tm, SORT_ROWS), 1)
            sel = jnp.zeros((tm, SORT_ROWS), F32)
            for p in pos:
                sel = jnp.where(slot == p, 1.0, sel)
            sels[s][...] = sel.astype(BF16)
            srt = lax.dot_general(sels[1 - s][...], h_ref[...], (((0,), (0,)), ((), ())),
                                  preferred_element_type=F32)
            lo = pltpu.bitcast(srt[:, :half], jnp.uint32) >> 16
            hi = pltpu.bitcast(srt[:, half:], jnp.uint32) & jnp.uint32(0xFFFF0000)
            srts[1 - s][...] = hi | lo

        @pl.when((i >= 1) & (i % 2 == s))
        def _(s=s):
            run_copies(i - 1, 1 - s, wait=False)

        @pl.when((i >= 2) & (i % 2 == s))
        def _(s=s):
            run_copies(i - 2, s, wait=True)

        @pl.when((i == n_tiles) & (i % 2 == s))
        def _(s=s):
            run_copies(i - 1, 1 - s, wait=True)


def _dispatch(h2, topi, rank, srow, tables, n_rows, tm):
    t, d = h2.shape
    n = t // tm
    cur = lambda w: pl.BlockSpec((tm, w), lambda i, *_: (jnp.minimum(i, n - 1), 0))
    grid_spec = pltpu.PrefetchScalarGridSpec(
        num_scalar_prefetch=len(tables),
        grid=(n + 1,),
        in_specs=[pl.BlockSpec((tm, d), lambda i, *_: (jnp.maximum(i - 1, 0), 0)),
                  cur(LANES), cur(LANES),
                  pl.BlockSpec((None, SUBLANES, LANES), lambda i, *_: (jnp.minimum(i, n - 1), 0, 0))],
        out_specs=pl.BlockSpec(memory_space=pl.ANY),
        scratch_shapes=[pltpu.VMEM((tm, SORT_ROWS), BF16), pltpu.VMEM((tm, SORT_ROWS), BF16),
                        pltpu.VMEM((SORT_ROWS, d // 2), jnp.uint32),
                        pltpu.VMEM((SORT_ROWS, d // 2), jnp.uint32),
                        pltpu.VMEM((MOE_BLOCK, d // 2), jnp.uint32),
                        pltpu.SemaphoreType.DMA, pltpu.SemaphoreType.DMA((2,))],
    )
    return pl.pallas_call(
        functools.partial(_dispatch_kernel, min_blocks=t * TOP_K // MOE_BLOCK),
        grid_spec=grid_spec,
        out_shape=jax.ShapeDtypeStruct((n_rows, d // 2), jnp.uint32),
        compiler_params=_cparams(("arbitrary",)),
        name="moe_dispatch",
    )(*tables, h2, topi, rank, srow)


def _expert_kernel(be_ref, nu_ref, x_ref, wgu_ref, bgu_ref, wd_ref, bd_ref, y_ref,
                   wgu_bf, wd_bf):
    j = pl.program_id(0)
    half = D_MODEL // 2

    @pl.when((j == 0) | (be_ref[j] != be_ref[jnp.maximum(j - 1, 0)]))
    def _():
        wgu_bf[...] = wgu_ref[...].astype(BF16)
        wd_bf[...] = wd_ref[...].astype(BF16)

    @pl.when(j < nu_ref[0])
    def _():
        xu = x_ref[...]
        x_lo = pltpu.bitcast(xu << 16, F32).astype(BF16)
        x_hi = pltpu.bitcast(xu & jnp.uint32(0xFFFF0000), F32).astype(BF16)
        gu = (jnp.dot(x_lo, wgu_bf[:half, :], preferred_element_type=F32)
              + jnp.dot(x_hi, wgu_bf[half:, :], preferred_element_type=F32) + bgu_ref[...])
        g = jnp.minimum(gu[:, :D_MODEL], SWIGLU_LIMIT)
        u = jnp.clip(gu[:, D_MODEL:], -SWIGLU_LIMIT, SWIGLU_LIMIT)
        act = (u + 1.0) * (g * _sigmoid(g * SWIGLU_ALPHA))
        y_ref[...] = jnp.dot(act.astype(BF16), wd_bf[...], preferred_element_type=F32) + bd_ref[...]

    @pl.when(j >= nu_ref[0])
    def _():
        y_ref[...] = jnp.zeros_like(y_ref)


def _experts(xs, block_e, n_used, wgu, bgu, wd, bd):
    p = xs.shape[0]
    e, d, _ = wgu.shape
    n_blocks = p // MOE_BLOCK
    grid_spec = pltpu.PrefetchScalarGridSpec(
        num_scalar_prefetch=2,
        grid=(n_blocks,),
        in_specs=[pl.BlockSpec((MOE_BLOCK, d // 2), lambda j, be, nu: (j, 0)),
                  pl.BlockSpec((None, d, 2 * d), lambda j, be, nu: (be[j], 0, 0)),
                  pl.BlockSpec((None, 1, 2 * d), lambda j, be, nu: (be[j], 0, 0)),
                  pl.BlockSpec((None, d, d), lambda j, be, nu: (be[j], 0, 0)),
                  pl.BlockSpec((None, 1, d), lambda j, be, nu: (be[j], 0, 0))],
        out_specs=pl.BlockSpec((MOE_BLOCK, d), lambda j, be, nu: (j, 0)),
        scratch_shapes=[pltpu.VMEM((d, 2 * d), BF16), pltpu.VMEM((d, d), BF16)],
    )
    return pl.pallas_call(
        _expert_kernel,
        grid_spec=grid_spec,
        out_shape=jax.ShapeDtypeStruct((p, d), F32),
        compiler_params=_cparams(("arbitrary",)),
        name="moe_experts",
    )(block_e, n_used, xs, wgu, bgu.reshape(e, 1, 2 * d), wd, bd.reshape(e, 1, d))


def _combine_kernel(start_ref, srun_ref, l8_ref,
                    y_hbm, topi_ref, rank_ref, srow_ref, gate_ref, x1_ref, mod_ref, nw_ref, o_ref,
                    sel0, sel1, yb0, yb1, sem):
    i = pl.program_id(0)
    n_tiles = pl.num_programs(0) - 1
    tm = x1_ref.shape[0]
    sels, ybufs = (sel0, sel1), (yb0, yb1)

    def run_copies(tile, slot, wait):
        def make_copy(e, off, size):
            src = pl.multiple_of(start_ref[tile * N_EXPERTS + e] + off, RUN_ALIGN)
            dst = pl.multiple_of(srun_ref[tile * N_EXPERTS + e] + off, RUN_ALIGN)
            return pltpu.make_async_copy(y_hbm.at[pl.ds(src, size)],
                                         ybufs[slot].at[pl.ds(dst, size)], sem.at[slot])
        _run_copies(tile, l8_ref, make_copy, wait)

    @pl.when(i == 0)
    def _():
        yb0[...] = jnp.zeros_like(yb0)
        yb1[...] = jnp.zeros_like(yb1)
        sel0[...] = jnp.zeros_like(sel0)
        sel1[...] = jnp.zeros_like(sel1)

    for s in range(2):
        @pl.when((i < n_tiles) & (i % 2 == s))
        def _(s=s):
            run_copies(i, s, wait=False)

        @pl.when((i >= 1) & (i % 2 == s))
        def _(s=s):
            run_copies(i - 1, 1 - s, wait=True)

        @pl.when(i % 2 == s)
        def _(s=s):
            pos = _positions(topi_ref, rank_ref, srow_ref)
            gate = gate_ref[...]
            row = lax.broadcasted_iota(I32, (tm, SORT_ROWS), 1)
            sel = jnp.zeros((tm, SORT_ROWS), F32)
            for kk, p in enumerate(pos):
                sel = jnp.where(row == p, gate[:, kk:kk + 1], sel)
            sels[s][...] = sel.astype(BF16)
            moe = jnp.dot(sels[1 - s][...], ybufs[1 - s][...].astype(BF16),
                          preferred_element_type=F32)
            g2 = mod_ref[5:6, :]
            o_ref[...] = x1_ref[...] + g2 * _rms(moe, nw_ref[...])


def _combine(y, topi, rank, srow, gate, x1, mod, nw, tables, seq, tm):
    t, d = x1.shape
    n = t // tm
    tiles_per_batch = seq // tm
    cur = lambda w: pl.BlockSpec((tm, w), lambda i, *_: (jnp.minimum(i, n - 1), 0))
    prev = lambda w: pl.BlockSpec((tm, w), lambda i, *_: (jnp.maximum(i - 1, 0), 0))
    grid_spec = pltpu.PrefetchScalarGridSpec(
        num_scalar_prefetch=len(tables),
        grid=(n + 1,),
        in_specs=[pl.BlockSpec(memory_space=pl.ANY),
                  cur(LANES), cur(LANES),
                  pl.BlockSpec((None, SUBLANES, LANES), lambda i, *_: (jnp.minimum(i, n - 1), 0, 0)),
                  cur(LANES), prev(d),
                  pl.BlockSpec((None, 6, d),
                               lambda i, *_: (jnp.maximum(i - 1, 0) // tiles_per_batch, 0, 0)),
                  pl.BlockSpec((1, d), lambda i, *_: (0, 0))],
        out_specs=prev(d),
        scratch_shapes=[pltpu.VMEM((tm, SORT_ROWS), BF16), pltpu.VMEM((tm, SORT_ROWS), BF16),
                        pltpu.VMEM((SORT_ROWS, d), F32), pltpu.VMEM((SORT_ROWS, d), F32),
                        pltpu.SemaphoreType.DMA((2,))],
    )
    return pl.pallas_call(
        _combine_kernel,
        grid_spec=grid_spec,
        out_shape=jax.ShapeDtypeStruct((t, d), F32),
        compiler_params=_cparams(("arbitrary",)),
        name="moe_combine",
    )(*tables, y, topi, rank, srow, gate, x1, mod, nw)


def _rope_tables(seq):
    pos = jnp.arange(seq, dtype=F32)
    inv = ROPE_THETA ** (-jnp.arange(0, DA_HEAD_DIM, 2, dtype=F32) / DA_HEAD_DIM)
    ang = pos[:, None] * inv[None, :]
    cos, sin = jnp.cos(ang), jnp.sin(ang)
    zero = jnp.zeros_like(sin)
    cos_t = jnp.tile(cos, (1, LANES // cos.shape[1]))
    sin_lo = jnp.tile(jnp.concatenate([-sin, zero], axis=1), (1, 2))
    sin_hi = jnp.tile(jnp.concatenate([zero, sin], axis=1), (1, 2))
    return cos_t, sin_lo, sin_hi


def kernel(x, c, w_mod, b_mod, norm_pre_mix, norm_post_mix, w_in, da_lambda_q1, da_lambda_k1,
           da_lambda_q2, da_lambda_k2, da_subln, hg_lb_logits, hg_norm, w_branch_a, w_branch_b,
           w_out, norm_pre_ffn, norm_post_ffn, router_w, router_b, w_gate_up, b_gate_up,
           w_down, b_down):
    bsz, seq, d = x.shape
    depth = w_mod.shape[0]
    assert depth == 1 and d == D_MODEL
    t = bsz * seq
    lyr = 0
    lambda_init = 0.8 - 0.6 * math.exp(-0.3 * lyr)
    cos, slo, shi = _rope_tables(seq)

    xf = x.reshape(t, d)
    mod = _modulation(c, w_mod[lyr], b_mod[lyr]).reshape(bsz, 6, d)

    qT, k, vT, hq, hf, hi, hg, ga, gb = _in_projection(
        xf, mod, norm_pre_mix[lyr][None], w_in[lyr].astype(BF16), cos, slo, shi, seq, tm=512)
    ya = _diff_attention(qT, k, vT, da_lambda_q1[lyr][None], da_lambda_k1[lyr][None],
                         da_lambda_q2[lyr][None], da_lambda_k2[lyr][None], da_subln[lyr],
                         bsz, seq, lambda_init, tq=1024, tk=512)
    yb = _hgrn2(hq, hf, hi, hg, hg_lb_logits[lyr:lyr + 2], hg_norm[lyr][None], bsz, seq, tc=512, nh=4)

    rw = jnp.zeros((d, LANES), BF16).at[:, :N_EXPERTS].set(router_w[lyr].astype(BF16))
    rb = jnp.full((1, LANES), NEG_BIG, F32).at[0, :N_EXPERTS].set(router_b[lyr])
    x1, h2, topi, gate, rank, counts = _merge_route(
        xf, mod, ya, yb, ga, gb, w_branch_a[lyr].astype(BF16), w_branch_b[lyr].astype(BF16),
        w_out[lyr].astype(BF16), norm_post_mix[lyr][None], norm_pre_ffn[lyr][None], rw, rb,
        seq, tm=MOE_TILE)

    n_tiles = t // MOE_TILE
    up = lambda a, m: (a + m - 1) // m * m
    cnt = counts[:, 0, :N_EXPERTS].astype(I32)
    run = up(cnt, RUN_ALIGN)
    srun = jnp.cumsum(run, axis=1) - run
    e_len = up(jnp.sum(run, axis=0), MOE_BLOCK)
    e_end = jnp.cumsum(e_len)
    start = (e_end - e_len)[None, :] + jnp.cumsum(run, axis=0) - run
    n_assign = t * TOP_K
    n_rows = (up(n_assign + n_tiles * N_EXPERTS * (RUN_ALIGN - 1), MOE_BLOCK)
              + N_EXPERTS * MOE_BLOCK)
    n_blocks = n_rows // MOE_BLOCK
    block_start = jnp.arange(n_blocks, dtype=I32) * MOE_BLOCK
    block_e = jnp.minimum(jnp.sum((e_end[None, :] <= block_start[:, None]).astype(I32), axis=1),
                          N_EXPERTS - 1)
    n_used = (e_end[-1:] // MOE_BLOCK).astype(I32)
    srow = jnp.broadcast_to(jnp.pad(srun.astype(F32), ((0, 0), (0, LANES - N_EXPERTS)))[:, None, :],
                            (n_tiles, SUBLANES, LANES))
    run_tables = (start.reshape(-1), srun.reshape(-1), (run // RUN_ALIGN).reshape(-1))

    xs = _dispatch(h2, topi, rank, srow, run_tables + (e_end.astype(I32), e_len.astype(I32)),
                   n_rows, tm=MOE_TILE)
    y = _experts(xs, block_e, n_used, w_gate_up[lyr], b_gate_up[lyr], w_down[lyr], b_down[lyr])
    out = _combine(y, topi, rank, srow, gate, x1, mod, norm_post_ffn[lyr][None], run_tables,
                   seq, tm=MOE_TILE)
    return out.reshape(bsz, seq, d)
```

```python
import functools
import math

import jax
import jax.numpy as jnp
from jax import lax
from jax.experimental import pallas as pl
from jax.experimental.pallas import tpu as pltpu

F32 = jnp.float32
BF16 = jnp.bfloat16
I32 = jnp.int32

D_MODEL = 1024
CHUNK = 64
EPS = 1e-6
ROPE_THETA = 10000.0
DA_HEADS = 4
DA_HEAD_DIM = 64
HG_HEADS = 4
HEAD_W = 128
BRANCH_W = 512
V_ROWS = HEAD_W + 16
ATTN_STRIP = 256
N_EXPERTS = 32
TOP_K = 4
SWIGLU_LIMIT = 7.0
SWIGLU_ALPHA = 1.702
MOE_BLOCK = 512
MOE_TILE = 512
LANES = 128
SUBLANES = 8
RUN_ALIGN = SUBLANES
RUN_BITS = (MOE_TILE // RUN_ALIGN).bit_length()
RUN_SMALL_BITS = 4
SORT_ROWS = MOE_TILE * TOP_K + N_EXPERTS * RUN_ALIGN
NEG_BIG = -1e30

VMEM_LIMIT = 56 * 1024 * 1024


def _cparams(sem):
    return pltpu.CompilerParams(dimension_semantics=sem, vmem_limit_bytes=VMEM_LIMIT)


def _rms(x, w):
    return x * lax.rsqrt(jnp.mean(x * x, axis=-1, keepdims=True) + EPS) * w


def _sigmoid(x):
    return 1.0 / (1.0 + jnp.exp(-x))


def _split3(x):
    a = x.astype(BF16)
    r = x - a.astype(F32)
    b = r.astype(BF16)
    c = (r - b.astype(F32)).astype(BF16)
    return a, b, c


def _mod_kernel(c_ref, w_ref, b_ref, o_ref):
    c = c_ref[...]
    ca = c * _sigmoid(c)
    o_ref[...] = jnp.dot(ca, w_ref[...], preferred_element_type=F32,
                         precision=lax.Precision.HIGHEST) + b_ref[...]


def _modulation(c, w_mod, b_mod):
    bsz, d = c.shape
    n = w_mod.shape[1]
    tn = 1536
    return pl.pallas_call(
        _mod_kernel,
        grid=(n // tn,),
        in_specs=[pl.BlockSpec((bsz, d), lambda j: (0, 0)),
                  pl.BlockSpec((d, tn), lambda j: (0, j)),
                  pl.BlockSpec((1, tn), lambda j: (0, j))],
        out_specs=pl.BlockSpec((bsz, tn), lambda j: (0, j)),
        out_shape=jax.ShapeDtypeStruct((bsz, n), F32),
        compiler_params=_cparams(("arbitrary",)),
        name="modulation",
    )(c, w_mod, b_mod.reshape(1, n))


def _rope(x, cos, sin_lo, sin_hi):
    return x * cos + pltpu.roll(x, 96, 1) * sin_lo + pltpu.roll(x, 32, 1) * sin_hi


def _inproj_kernel(x_ref, mod_ref, nw_ref, w_ref, cos_ref, slo_ref, shi_ref,
                   qT_ref, k_ref, vT_ref, hq_ref, hf_ref, hi_ref, hg_ref, ga_ref, gb_ref):
    x = x_ref[...]
    sh1 = mod_ref[0:1, :]
    sc1 = mod_ref[1:2, :]
    h = (_rms(x, nw_ref[...]) * (1.0 + sc1) + sh1).astype(BF16)
    cos, slo, shi = cos_ref[...], slo_ref[...], shi_ref[...]
    scale = DA_HEAD_DIM ** -0.5 * math.log2(math.e)

    def seg(i, width=BRANCH_W):
        return jnp.dot(h, w_ref[:, i:i + width], preferred_element_type=F32)

    q = seg(0)
    k = seg(BRANCH_W)
    for hd in range(DA_HEADS):
        sl = slice(hd * HEAD_W, (hd + 1) * HEAD_W)
        qT_ref[sl, :] = (_rope(q[:, sl], cos, slo, shi) * scale).T.astype(BF16)
        k_ref[:, sl] = _rope(k[:, sl], cos, slo, shi).astype(BF16)
    vT = seg(2 * BRANCH_W).T.astype(BF16)
    ones = jnp.ones((V_ROWS - HEAD_W, vT.shape[1]), BF16)
    for hd in range(DA_HEADS):
        vT_ref[hd * V_ROWS:hd * V_ROWS + HEAD_W, :] = vT[hd * HEAD_W:(hd + 1) * HEAD_W]
        vT_ref[hd * V_ROWS + HEAD_W:(hd + 1) * V_ROWS, :] = ones
    hq = seg(3 * BRANCH_W)
    hq_ref[...] = (hq * _sigmoid(hq)).astype(BF16)
    hf_ref[...] = seg(4 * BRANCH_W)
    hi_ref[...] = seg(5 * BRANCH_W).astype(BF16)
    hg = seg(6 * BRANCH_W)
    hg_ref[...] = (hg * _sigmoid(hg)).astype(BF16)
    for j in range(2):
        ga_ref[:, j * BRANCH_W:(j + 1) * BRANCH_W] = _sigmoid(seg((7 + j) * BRANCH_W)).astype(BF16)
        gb_ref[:, j * BRANCH_W:(j + 1) * BRANCH_W] = _sigmoid(seg((9 + j) * BRANCH_W)).astype(BF16)


def _in_projection(xf, mod, norm_w, w_in_bf, cos, slo, shi, seq, tm):
    t, d = xf.shape
    tiles_per_batch = seq // tm
    bspec = lambda w: pl.BlockSpec((tm, w), lambda i: (i, 0))
    tab = pl.BlockSpec((tm, LANES), lambda i: (i % tiles_per_batch, 0))
    widths = [BRANCH_W] * 7 + [D_MODEL, D_MODEL]
    dtypes = [BF16, BF16, BF16, BF16, F32, BF16, BF16, BF16, BF16]
    out_specs = [bspec(w) for w in widths]
    out_shape = [jax.ShapeDtypeStruct((t, w), dt) for w, dt in zip(widths, dtypes)]
    for idx, rows in ((0, BRANCH_W), (2, DA_HEADS * V_ROWS)):
        out_specs[idx] = pl.BlockSpec((None, rows, tm),
                                      lambda i: (i // tiles_per_batch, 0, i % tiles_per_batch))
        out_shape[idx] = jax.ShapeDtypeStruct((t // seq, rows, seq), BF16)
    return pl.pallas_call(
        _inproj_kernel,
        grid=(t // tm,),
        in_specs=[bspec(d),
                  pl.BlockSpec((None, 6, d), lambda i: (i // tiles_per_batch, 0, 0)),
                  pl.BlockSpec((1, d), lambda i: (0, 0)),
                  pl.BlockSpec(w_in_bf.shape, lambda i: (0, 0), pipeline_mode=pl.Buffered(1)),
                  tab, tab, tab],
        out_specs=out_specs,
        out_shape=out_shape,
        compiler_params=_cparams(("parallel",)),
        name="in_projection",
    )(xf, mod, norm_w, w_in_bf, cos, slo, shi)


def _attn_kernel(lq1_ref, lk1_ref, lq2_ref, lk2_ref, subln_ref, qT_ref, k_ref, vT_ref, o_ref,
                 qq_ref, s0_ref, s1_ref, p0_ref, p1_ref, a0_ref, a1_ref, m_ref, acc_ref,
                 *, tq, tk, lambda_init):
    qi = pl.program_id(2)
    ndiag = tq // tk
    s_refs, p_refs, a_refs = (s0_ref, s1_ref), (p0_ref, p1_ref), (a0_ref, a1_ref)
    qT = qT_ref[...]
    feat = lax.broadcasted_iota(I32, qT.shape, 0)
    zero = jnp.zeros_like(qT)
    qq_ref[:, :tq] = jnp.where(feat < DA_HEAD_DIM, qT, zero)
    qq_ref[:, tq:] = jnp.where(feat >= DA_HEAD_DIM, qT, zero)

    strips = [slice(c * ATTN_STRIP, (c + 1) * ATTN_STRIP) for c in range(2 * tq // ATTN_STRIP)]

    def scores(kb, dst, cs):
        start = pl.multiple_of(kb * tk, tk)
        dst[:, cs] = jnp.dot(k_ref[pl.ds(start, tk), :], qq_ref[:, cs], preferred_element_type=F32)

    def accumulate(kb, slot, cs):
        start = pl.multiple_of(kb * tk, tk)
        pv = jnp.dot(vT_ref[:, pl.ds(start, tk)], p_refs[slot][:, cs], preferred_element_type=F32)
        acc_ref[:, cs] = a_refs[slot][:, cs] * acc_ref[:, cs] + pv

    def softmax(slot, cs, diag):
        def piece(rc):
            sc = s_refs[slot][rc * CHUNK:(rc + 1) * CHUNK, cs]
            if diag is not None:
                qchunk = (lax.broadcasted_iota(I32, (1, ATTN_STRIP), 1) + cs.start) % tq // CHUNK
                sc = jnp.where(qchunk >= diag * (tk // CHUNK) + rc, sc, NEG_BIG)
            return sc

        top = piece(0)
        for rc in range(1, tk // CHUNK):
            top = jnp.maximum(top, piece(rc))
        m_old = m_ref[:, cs]
        m_new = jnp.maximum(m_old, jnp.max(top, axis=0, keepdims=True))
        m_ref[:, cs] = m_new
        a_refs[slot][:, cs] = jnp.exp2(m_old - m_new)
        for rc in range(tk // CHUNK):
            p_refs[slot][rc * CHUNK:(rc + 1) * CHUNK, cs] = jnp.exp2((piece(rc) - m_new).astype(BF16))

    def stage(kb, slot, diag=None, last=False):
        for cs in strips:
            accumulate(jnp.maximum(kb - 1, 0), 1 - slot, cs)
            softmax(slot, cs, diag)
            if not last:
                scores(kb + 1, s_refs[1 - slot], cs)

    p1_ref[...] = jnp.zeros_like(p1_ref)
    a1_ref[...] = jnp.ones_like(a1_ref)
    m_ref[...] = jnp.full(m_ref.shape, NEG_BIG, F32)
    acc_ref[...] = jnp.zeros_like(acc_ref)
    for cs in strips:
        scores(0, s0_ref, cs)

    def pair(i, c):
        stage(2 * i, 0)
        stage(2 * i + 1, 1)
        return c

    first_diag = qi * ndiag
    lax.fori_loop(0, first_diag // 2, pair, 0)
    for dg in range(ndiag):
        stage(first_diag + dg, dg % 2, diag=dg, last=dg == ndiag - 1)
    for cs in strips:
        accumulate(first_diag + ndiag - 1, (ndiag - 1) % 2, cs)

    acc = acc_ref[...]
    o = acc[:HEAD_W] / acc[HEAD_W:HEAD_W + 1]
    lam = (jnp.exp(jnp.sum(lq1_ref[...] * lk1_ref[...], axis=1, keepdims=True))
           - jnp.exp(jnp.sum(lq2_ref[...] * lk2_ref[...], axis=1, keepdims=True))
           + lambda_init)
    o = o[:, :tq] - lam * o[:, tq:]
    o = o * lax.rsqrt(jnp.mean(o * o, axis=0, keepdims=True) + EPS) * subln_ref[...]
    o_ref[...] = (o * (1.0 - lambda_init)).T.astype(BF16)


def _diff_attention(qT, k, vT, lq1, lk1, lq2, lk2, subln, bsz, seq, lambda_init, tq, tk):
    assert tq % (2 * tk) == 0
    k3 = k.reshape(bsz, seq, BRANCH_W)
    small = lambda n: pl.BlockSpec((1, n), lambda b, h, i: (0, 0))
    out = pl.pallas_call(
        functools.partial(_attn_kernel, tq=tq, tk=tk, lambda_init=lambda_init),
        grid=(bsz, DA_HEADS, seq // tq),
        in_specs=[small(DA_HEAD_DIM)] * 4 + [
            pl.BlockSpec((HEAD_W, 1), lambda b, h, i: (0, 0)),
            pl.BlockSpec((None, HEAD_W, tq), lambda b, h, i: (b, h, i)),
            pl.BlockSpec((None, seq, HEAD_W), lambda b, h, i: (b, 0, h)),
            pl.BlockSpec((None, V_ROWS, seq), lambda b, h, i: (b, h, 0))],
        out_specs=pl.BlockSpec((None, tq, HEAD_W), lambda b, h, i: (b, i, h)),
        out_shape=jax.ShapeDtypeStruct((bsz, seq, BRANCH_W), BF16),
        scratch_shapes=[pltpu.VMEM((HEAD_W, 2 * tq), BF16),
                        pltpu.VMEM((tk, 2 * tq), F32), pltpu.VMEM((tk, 2 * tq), F32),
                        pltpu.VMEM((tk, 2 * tq), BF16), pltpu.VMEM((tk, 2 * tq), BF16),
                        pltpu.VMEM((1, 2 * tq), F32), pltpu.VMEM((1, 2 * tq), F32),
                        pltpu.VMEM((1, 2 * tq), F32), pltpu.VMEM((V_ROWS, 2 * tq), F32)],
        compiler_params=_cparams(("parallel", "parallel", "arbitrary")),
        name="diff_attention",
    )(lq1, lk1, lq2, lk2, subln.reshape(HEAD_W, 1), qT, k3, vT)
    return out.reshape(bsz * seq, BRANCH_W)


def _hgrn_kernel(lbl_ref, nw_ref, q_ref, f_ref, i_ref, g_ref, o_ref, st_ref, *, tc, nh):
    @pl.when(pl.program_id(2) == 0)
    def _():
        st_ref[...] = jnp.zeros_like(st_ref)

    row = lax.broadcasted_iota(I32, (CHUNK, CHUNK), 0)
    col = lax.broadcasted_iota(I32, (CHUNK, CHUNK), 1)
    causal = col <= row
    tril = jnp.where(causal, 1.0, 0.0).astype(BF16)
    nt = (((1,), (1,)), ((), ()))
    tn = (((0,), (0,)), ((), ()))

    nchunk = tc // CHUNK
    units = [(hh, c) for hh in range(nh) for c in range(nchunk)]
    rows = lambda c: slice(c * CHUNK, (c + 1) * CHUNK)
    q, k, v, parts = [], [], [], []
    for hh in range(nh):
        hs = slice(hh * HEAD_W, (hh + 1) * HEAD_W)
        lg = lbl_ref[:, hs]
        ex = jnp.exp(lg - jnp.max(lg, axis=0, keepdims=True))
        lb = ex[0:1] / jnp.sum(ex, axis=0, keepdims=True)
        f = lb + (1.0 - lb) * _sigmoid(f_ref[:, hs])
        q.append(q_ref[:, hs].astype(F32))
        k.append(1.0 - f)
        v.append(i_ref[:, hs])
        parts.append(_split3(jnp.log(f)))
    gall = sum(jnp.dot(tril, jnp.concatenate([parts[hh][p][rows(c)] for hh, c in units], axis=1),
                       preferred_element_type=F32) for p in range(3))
    gcum = [gall[:, u * HEAD_W:(u + 1) * HEAD_W] for u in range(len(units))]
    gtot = [g[CHUNK - 1:CHUNK, :] for g in gcum]
    q_t = [(q[hh][rows(c)] * jnp.exp(gcum[u])).astype(BF16) for u, (hh, c) in enumerate(units)]
    k_t = [(k[hh][rows(c)] * jnp.exp(-gcum[u])).astype(BF16) for u, (hh, c) in enumerate(units)]
    k_dec = [(k[hh][rows(c)] * jnp.exp(gtot[u] - gcum[u])).astype(BF16)
             for u, (hh, c) in enumerate(units)]
    a = [lax.dot_general(q_t[u], k_t[u], nt, preferred_element_type=F32) for u in range(len(units))]
    ds = [lax.dot_general(v[hh][rows(c)], k_dec[u], tn, preferred_element_type=F32)
          for u, (hh, c) in enumerate(units)]
    a = [jnp.where(causal, x, 0.0).astype(BF16) for x in a]
    st_in = []
    for hh in range(nh):
        st = st_ref[hh]
        for c in range(nchunk):
            u = hh * nchunk + c
            st_in.append(st.astype(BF16))
            st = jnp.exp(gtot[u]) * st + ds[u]
        st_ref[hh] = st
    outs = [jnp.dot(a[u], v[hh][rows(c)], preferred_element_type=F32)
            + lax.dot_general(q_t[u], st_in[u], nt, preferred_element_type=F32)
            for u, (hh, c) in enumerate(units)]
    for hh in range(nh):
        hs = slice(hh * HEAD_W, (hh + 1) * HEAD_W)
        o = jnp.concatenate(outs[hh * nchunk:(hh + 1) * nchunk], axis=0)
        o_ref[:, hs] = (_rms(o, nw_ref[...]) * g_ref[:, hs].astype(F32)).astype(BF16)


def _hgrn2(hq, hf, hi, hg, lb_logits, norm_w, bsz, seq, tc, nh):
    r3 = lambda a: a.reshape(bsz, seq, BRANCH_W)
    blk = pl.BlockSpec((None, tc, nh * HEAD_W), lambda b, h, i: (b, i, h))
    out = pl.pallas_call(
        functools.partial(_hgrn_kernel, tc=tc, nh=nh),
        grid=(bsz, HG_HEADS // nh, seq // tc),
        in_specs=[pl.BlockSpec((2, nh * HEAD_W), lambda b, h, i: (0, h)),
                  pl.BlockSpec((1, HEAD_W), lambda b, h, i: (0, 0)),
                  blk, blk, blk, blk],
        out_specs=blk,
        out_shape=jax.ShapeDtypeStruct((bsz, seq, BRANCH_W), BF16),
        scratch_shapes=[pltpu.VMEM((nh, HEAD_W, HEAD_W), F32)],
        compiler_params=_cparams(("parallel", "parallel", "arbitrary")),
        name="hgrn2",
    )(lb_logits, norm_w, r3(hq), r3(hf), r3(hi), r3(hg))
    return out.reshape(bsz * seq, BRANCH_W)


def _merge_kernel(x_ref, mod_ref, ya_ref, yb_ref, ga_ref, gb_ref, wa_ref, wb_ref, wo_ref,
                  npost_ref, npre_ref, rw_ref, rb_ref,
                  x1_ref, h2_ref, topi_ref, gate_ref, rank_ref, cnt_ref, lg_ref, *, tm):
    @pl.when(pl.program_id(0) == 0)
    def _():
        lg_ref[...] = jnp.zeros_like(lg_ref)

    logits = lg_ref[...]

    g1 = mod_ref[2:3, :]
    sh2 = mod_ref[3:4, :]
    sc2 = mod_ref[4:5, :]
    a = jnp.dot(ya_ref[...], wa_ref[...], preferred_element_type=F32)
    b = jnp.dot(yb_ref[...], wb_ref[...], preferred_element_type=F32)
    merged = ga_ref[...] * a.astype(BF16) + gb_ref[...] * b.astype(BF16)
    y = jnp.dot(merged, wo_ref[...], preferred_element_type=F32)
    x1 = x_ref[...] + g1 * _rms(y, npost_ref[...])
    x1_ref[...] = x1
    h2 = (_rms(x1, npre_ref[...]) * (1.0 + sc2) + sh2).astype(BF16)
    h2_ref[...] = h2
    lg_ref[...] = jnp.dot(h2, rw_ref[...], preferred_element_type=F32) + rb_ref[...]

    lane = lax.broadcasted_iota(I32, logits.shape, 1)
    work = logits
    vals, idxs = [], []
    for _ in range(TOP_K):
        mx = jnp.max(work, axis=1, keepdims=True)
        idx = jnp.min(jnp.where(work == mx, lane, LANES), axis=1, keepdims=True)
        vals.append(mx)
        idxs.append(idx)
        work = jnp.where(lane == idx, -jnp.inf, work)
    es = [jnp.exp(vk - vals[0]) for vk in vals]
    den = es[0] + es[1] + es[2] + es[3]

    onehot = jnp.zeros(logits.shape, F32)
    for idx in idxs:
        onehot = jnp.where(lane == idx, 1.0, onehot)
    row = lax.broadcasted_iota(I32, (tm, tm), 0)
    col = lax.broadcasted_iota(I32, (tm, tm), 1)
    strict = jnp.where(col < row, 1.0, 0.0).astype(BF16)
    prefix = jnp.dot(strict, onehot.astype(BF16), preferred_element_type=F32)
    cnt_ref[...] = jnp.broadcast_to(jnp.sum(onehot, axis=0, keepdims=True), cnt_ref.shape)

    topi = jnp.zeros(logits.shape, I32)
    gate = jnp.zeros(logits.shape, F32)
    rank = jnp.zeros(logits.shape, I32)
    for kk in range(TOP_K):
        rk = jnp.sum(jnp.where(lane == idxs[kk], prefix, 0.0), axis=1, keepdims=True)
        topi = jnp.where(lane == kk, idxs[kk], topi)
        gate = jnp.where(lane == kk, es[kk] / den, gate)
        rank = jnp.where(lane == kk, rk.astype(I32), rank)
    topi_ref[...] = topi
    gate_ref[...] = gate
    rank_ref[...] = rank


def _merge_route(xf, mod, ya, yb, ga, gb, wa, wb, wo, npost, npre, rw, rb, seq, tm):
    t, d = xf.shape
    n = t // tm
    tiles_per_batch = seq // tm
    proj = lambda w: pl.BlockSpec((tm, w), lambda i: (jnp.minimum(i, n - 1), 0))
    route = lambda w: pl.BlockSpec((tm, w), lambda i: (jnp.maximum(i - 1, 0), 0))
    const = lambda a: pl.BlockSpec(a.shape, lambda i: (0,) * a.ndim)
    lane_out = jax.ShapeDtypeStruct((t, LANES), I32)
    return pl.pallas_call(
        functools.partial(_merge_kernel, tm=tm),
        grid=(n + 1,),
        in_specs=[proj(d),
                  pl.BlockSpec((None, 6, d),
                               lambda i: (jnp.minimum(i, n - 1) // tiles_per_batch, 0, 0)),
                  proj(BRANCH_W), proj(BRANCH_W), proj(d), proj(d),
                  const(wa), const(wb), const(wo), const(npost), const(npre),
                  const(rw), const(rb)],
        out_specs=[proj(d), proj(d), route(LANES), route(LANES), route(LANES),
                   pl.BlockSpec((None, SUBLANES, LANES), lambda i: (jnp.maximum(i - 1, 0), 0, 0))],
        out_shape=[jax.ShapeDtypeStruct((t, d), F32), jax.ShapeDtypeStruct((t, d), BF16),
                   lane_out, jax.ShapeDtypeStruct((t, LANES), F32), lane_out,
                   jax.ShapeDtypeStruct((n, SUBLANES, LANES), F32)],
        scratch_shapes=[pltpu.VMEM((tm, LANES), F32)],
        compiler_params=_cparams(("arbitrary",)),
        name="merge_route",
    )(xf, mod, ya, yb, ga, gb, wa, wb, wo, npost, npre, rw, rb)


def _positions(topi_ref, rank_ref, srow_ref):
    topi = topi_ref[...]
    rank = rank_ref[...]
    lane = lax.broadcasted_iota(I32, topi.shape, 1)
    srow = srow_ref[0:1, :]
    pos = []
    for kk in range(TOP_K):
        base = jnp.sum(jnp.where(lane == topi[:, kk:kk + 1], srow, 0.0), axis=1, keepdims=True)
        pos.append(base.astype(I32) + rank[:, kk:kk + 1])
    return pos


def _wait_rows(n8, make_copy):
    for b in range((SORT_ROWS // RUN_ALIGN).bit_length()):
        @pl.when((n8 >> b) & 1 == 1)
        def _(b=b):
            make_copy(RUN_ALIGN << b).wait()


def _run_copies(tile, l8_ref, make_copy):
    def pieces(e, n8, bits, off):
        for b in bits:
            size = RUN_ALIGN << b
            bit = (n8 >> b) & 1

            @pl.when(bit == 1)
            def _(off=off, size=size):
                make_copy(e, off, size).start()

            off = off + bit * size

    for e in range(N_EXPERTS):
        n8 = l8_ref[tile * N_EXPERTS + e]
        big = (n8 >> RUN_SMALL_BITS) << RUN_SMALL_BITS

        @pl.when(big > 0)
        def _(e=e, n8=n8):
            pieces(e, n8, range(RUN_BITS - 1, RUN_SMALL_BITS - 1, -1), jnp.int32(0))

        pieces(e, n8, range(RUN_SMALL_BITS - 1, -1, -1), big * RUN_ALIGN)


def _dispatch_kernel(start_ref, srun_ref, l8_ref, tot8_ref, eend_ref, elen_ref,
                     h_ref, topi_ref, rank_ref, srow_ref, xs_out,
                     sel0, sel1, srt0, srt1, zbuf, zsem, sem, *, min_blocks):
    i = pl.program_id(0)
    n_tiles = pl.num_programs(0) - 1
    sels, srts = (sel0, sel1), (srt0, srt1)
    tm = h_ref.shape[0]
    half = h_ref.shape[1] // 2
    n_blocks = xs_out.shape[0] // MOE_BLOCK
    n_used = eend_ref[N_EXPERTS - 1] // MOE_BLOCK

    def zero_copy(start):
        start = pl.multiple_of(start, MOE_BLOCK)
        return pltpu.make_async_copy(zbuf, xs_out.at[pl.ds(start, MOE_BLOCK)], zsem)

    def zero_fill(wait):
        for e in range(N_EXPERTS):
            @pl.when(elen_ref[e] > 0)
            def _(e=e):
                cp = zero_copy(eend_ref[e] - MOE_BLOCK)
                cp.wait() if wait else cp.start()
        for blk in range(min_blocks, n_blocks):
            @pl.when(blk >= n_used)
            def _(blk=blk):
                cp = zero_copy(blk * MOE_BLOCK)
                cp.wait() if wait else cp.start()

    @pl.when(i == 0)
    def _():
        zbuf[...] = jnp.zeros_like(zbuf)
        sel0[...] = jnp.zeros_like(sel0)
        sel1[...] = jnp.zeros_like(sel1)
        zero_fill(wait=False)
        zero_fill(wait=True)

    def start_copies(tile, slot):
        def make_copy(e, off, size):
            src = pl.multiple_of(srun_ref[tile * N_EXPERTS + e] + off, RUN_ALIGN)
            dst = pl.multiple_of(start_ref[tile * N_EXPERTS + e] + off, RUN_ALIGN)
            return pltpu.make_async_copy(srts[slot].at[pl.ds(src, size)],
                                         xs_out.at[pl.ds(dst, size)], sem.at[slot])
        _run_copies(tile, l8_ref, make_copy)

    def wait_copies(tile, slot):
        def make_copy(size):
            return pltpu.make_async_copy(srts[slot].at[pl.ds(0, size)],
                                         xs_out.at[pl.ds(0, size)], sem.at[slot])
        _wait_rows(tot8_ref[tile], make_copy)

    for s in range(2):
        @pl.when(i % 2 == s)
        def _(s=s):
            pos = _positions(topi_ref, rank_ref, srow_ref)
            slot = lax.broadcasted_iota(I32, (tm, SORT_ROWS), 1)
            sel = jnp.zeros((tm, SORT_ROWS), F32)
            for p in pos:
                sel = jnp.where(slot == p, 1.0, sel)
            sels[s][...] = sel.astype(BF16)
            srt = lax.dot_general(sels[1 - s][...], h_ref[...], (((0,), (0,)), ((), ())),
                                  preferred_element_type=F32)
            lo = pltpu.bitcast(srt[:, :half], jnp.uint32) >> 16
            hi = pltpu.bitcast(srt[:, half:], jnp.uint32) & jnp.uint32(0xFFFF0000)
            srts[1 - s][...] = hi | lo

        @pl.when((i >= 1) & (i % 2 == s))
        def _(s=s):
            start_copies(i - 1, 1 - s)

        @pl.when((i >= 2) & (i % 2 == s))
        def _(s=s):
            wait_copies(i - 2, s)

        @pl.when((i == n_tiles) & (i % 2 == s))
        def _(s=s):
            wait_copies(i - 1, 1 - s)


def _dispatch(h2, topi, rank, srow, tables, n_rows, tm):
    t, d = h2.shape
    n = t // tm
    cur = lambda w: pl.BlockSpec((tm, w), lambda i, *_: (jnp.minimum(i, n - 1), 0))
    grid_spec = pltpu.PrefetchScalarGridSpec(
        num_scalar_prefetch=len(tables),
        grid=(n + 1,),
        in_specs=[pl.BlockSpec((tm, d), lambda i, *_: (jnp.maximum(i - 1, 0), 0)),
                  cur(LANES), cur(LANES),
                  pl.BlockSpec((None, SUBLANES, LANES), lambda i, *_: (jnp.minimum(i, n - 1), 0, 0))],
        out_specs=pl.BlockSpec(memory_space=pl.ANY),
        scratch_shapes=[pltpu.VMEM((tm, SORT_ROWS), BF16), pltpu.VMEM((tm, SORT_ROWS), BF16),
                        pltpu.VMEM((SORT_ROWS, d // 2), jnp.uint32),
                        pltpu.VMEM((SORT_ROWS, d // 2), jnp.uint32),
                        pltpu.VMEM((MOE_BLOCK, d // 2), jnp.uint32),
                        pltpu.SemaphoreType.DMA, pltpu.SemaphoreType.DMA((2,))],
    )
    return pl.pallas_call(
        functools.partial(_dispatch_kernel, min_blocks=t * TOP_K // MOE_BLOCK),
        grid_spec=grid_spec,
        out_shape=jax.ShapeDtypeStruct((n_rows, d // 2), jnp.uint32),
        compiler_params=_cparams(("arbitrary",)),
        name="moe_dispatch",
    )(*tables, h2, topi, rank, srow)


def _expert_kernel(be_ref, nu_ref, x_ref, wgu_ref, bgu_ref, wd_ref, bd_ref, y_ref,
                   wgu_bf, wd_bf):
    j = pl.program_id(0)
    half = D_MODEL // 2

    @pl.when((j == 0) | (be_ref[j] != be_ref[jnp.maximum(j - 1, 0)]))
    def _():
        wgu_bf[...] = wgu_ref[...].astype(BF16)
        wd_bf[...] = wd_ref[...].astype(BF16)

    @pl.when(j < nu_ref[0])
    def _():
        xu = x_ref[...]
        x_lo = pltpu.bitcast(xu << 16, F32).astype(BF16)
        x_hi = pltpu.bitcast(xu & jnp.uint32(0xFFFF0000), F32).astype(BF16)
        gu = (jnp.dot(x_lo, wgu_bf[:half, :], preferred_element_type=F32)
              + jnp.dot(x_hi, wgu_bf[half:, :], preferred_element_type=F32) + bgu_ref[...])
        g = jnp.minimum(gu[:, :D_MODEL], SWIGLU_LIMIT)
        u = jnp.clip(gu[:, D_MODEL:], -SWIGLU_LIMIT, SWIGLU_LIMIT)
        act = (u + 1.0) * (g * _sigmoid(g * SWIGLU_ALPHA))
        y_ref[...] = jnp.dot(act.astype(BF16), wd_bf[...], preferred_element_type=F32) + bd_ref[...]

    @pl.when(j >= nu_ref[0])
    def _():
        y_ref[...] = jnp.zeros_like(y_ref)


def _experts(xs, block_e, n_used, wgu, bgu, wd, bd):
    p = xs.shape[0]
    e, d, _ = wgu.shape
    n_blocks = p // MOE_BLOCK
    grid_spec = pltpu.PrefetchScalarGridSpec(
        num_scalar_prefetch=2,
        grid=(n_blocks,),
        in_specs=[pl.BlockSpec((MOE_BLOCK, d // 2), lambda j, be, nu: (j, 0)),
                  pl.BlockSpec((None, d, 2 * d), lambda j, be, nu: (be[j], 0, 0)),
                  pl.BlockSpec((None, 1, 2 * d), lambda j, be, nu: (be[j], 0, 0)),
                  pl.BlockSpec((None, d, d), lambda j, be, nu: (be[j], 0, 0)),
                  pl.BlockSpec((None, 1, d), lambda j, be, nu: (be[j], 0, 0))],
        out_specs=pl.BlockSpec((MOE_BLOCK, d), lambda j, be, nu: (j, 0)),
        scratch_shapes=[pltpu.VMEM((d, 2 * d), BF16), pltpu.VMEM((d, d), BF16)],
    )
    return pl.pallas_call(
        _expert_kernel,
        grid_spec=grid_spec,
        out_shape=jax.ShapeDtypeStruct((p, d), F32),
        compiler_params=_cparams(("arbitrary",)),
        name="moe_experts",
    )(block_e, n_used, xs, wgu, bgu.reshape(e, 1, 2 * d), wd, bd.reshape(e, 1, d))


def _combine_kernel(start_ref, srun_ref, l8_ref, tot8_ref,
                    y_hbm, topi_ref, rank_ref, srow_ref, gate_ref, x1_ref, mod_ref, nw_ref, o_ref,
                    sel0, sel1, yb0, yb1, sem):
    i = pl.program_id(0)
    n_tiles = pl.num_programs(0) - 1
    tm = x1_ref.shape[0]
    sels, ybufs = (sel0, sel1), (yb0, yb1)

    def start_copies(tile, slot):
        def make_copy(e, off, size):
            src = pl.multiple_of(start_ref[tile * N_EXPERTS + e] + off, RUN_ALIGN)
            dst = pl.multiple_of(srun_ref[tile * N_EXPERTS + e] + off, RUN_ALIGN)
            return pltpu.make_async_copy(y_hbm.at[pl.ds(src, size)],
                                         ybufs[slot].at[pl.ds(dst, size)], sem.at[slot])
        _run_copies(tile, l8_ref, make_copy)

    def wait_copies(tile, slot):
        def make_copy(size):
            return pltpu.make_async_copy(y_hbm.at[pl.ds(0, size)],
                                         ybufs[slot].at[pl.ds(0, size)], sem.at[slot])
        _wait_rows(tot8_ref[tile], make_copy)

    @pl.when(i == 0)
    def _():
        yb0[...] = jnp.zeros_like(yb0)
        yb1[...] = jnp.zeros_like(yb1)
        sel0[...] = jnp.zeros_like(sel0)
        sel1[...] = jnp.zeros_like(sel1)

    for s in range(2):
        @pl.when((i < n_tiles) & (i % 2 == s))
        def _(s=s):
            start_copies(i, s)

        @pl.when((i >= 1) & (i % 2 == s))
        def _(s=s):
            wait_copies(i - 1, 1 - s)

        @pl.when(i % 2 == s)
        def _(s=s):
            pos = _positions(topi_ref, rank_ref, srow_ref)
            gate = gate_ref[...]
            row = lax.broadcasted_iota(I32, (tm, SORT_ROWS), 1)
            sel = jnp.zeros((tm, SORT_ROWS), F32)
            for kk, p in enumerate(pos):
                sel = jnp.where(row == p, gate[:, kk:kk + 1], sel)
            sels[s][...] = sel.astype(BF16)
            moe = jnp.dot(sels[1 - s][...], ybufs[1 - s][...].astype(BF16),
                          preferred_element_type=F32)
            g2 = mod_ref[5:6, :]
            o_ref[...] = x1_ref[...] + g2 * _rms(moe, nw_ref[...])


def _combine(y, topi, rank, srow, gate, x1, mod, nw, tables, seq, tm):
    t, d = x1.shape
    n = t // tm
    tiles_per_batch = seq // tm
    cur = lambda w: pl.BlockSpec((tm, w), lambda i, *_: (jnp.minimum(i, n - 1), 0))
    prev = lambda w: pl.BlockSpec((tm, w), lambda i, *_: (jnp.maximum(i - 1, 0), 0))
    grid_spec = pltpu.PrefetchScalarGridSpec(
        num_scalar_prefetch=len(tables),
        grid=(n + 1,),
        in_specs=[pl.BlockSpec(memory_space=pl.ANY),
                  cur(LANES), cur(LANES),
                  pl.BlockSpec((None, SUBLANES, LANES), lambda i, *_: (jnp.minimum(i, n - 1), 0, 0)),
                  cur(LANES), prev(d),
                  pl.BlockSpec((None, 6, d),
                               lambda i, *_: (jnp.maximum(i - 1, 0) // tiles_per_batch, 0, 0)),
                  pl.BlockSpec((1, d), lambda i, *_: (0, 0))],
        out_specs=prev(d),
        scratch_shapes=[pltpu.VMEM((tm, SORT_ROWS), BF16), pltpu.VMEM((tm, SORT_ROWS), BF16),
                        pltpu.VMEM((SORT_ROWS, d), F32), pltpu.VMEM((SORT_ROWS, d), F32),
                        pltpu.SemaphoreType.DMA((2,))],
    )
    return pl.pallas_call(
        _combine_kernel,
        grid_spec=grid_spec,
        out_shape=jax.ShapeDtypeStruct((t, d), F32),
        compiler_params=_cparams(("arbitrary",)),
        name="moe_combine",
    )(*tables, y, topi, rank, srow, gate, x1, mod, nw)


def _rope_tables(seq):
    pos = jnp.arange(seq, dtype=F32)
    inv = ROPE_THETA ** (-jnp.arange(0, DA_HEAD_DIM, 2, dtype=F32) / DA_HEAD_DIM)
    ang = pos[:, None] * inv[None, :]
    cos, sin = jnp.cos(ang), jnp.sin(ang)
    zero = jnp.zeros_like(sin)
    cos_t = jnp.tile(cos, (1, LANES // cos.shape[1]))
    sin_lo = jnp.tile(jnp.concatenate([-sin, zero], axis=1), (1, 2))
    sin_hi = jnp.tile(jnp.concatenate([zero, sin], axis=1), (1, 2))
    return cos_t, sin_lo, sin_hi


def kernel(x, c, w_mod, b_mod, norm_pre_mix, norm_post_mix, w_in, da_lambda_q1, da_lambda_k1,
           da_lambda_q2, da_lambda_k2, da_subln, hg_lb_logits, hg_norm, w_branch_a, w_branch_b,
           w_out, norm_pre_ffn, norm_post_ffn, router_w, router_b, w_gate_up, b_gate_up,
           w_down, b_down):
    bsz, seq, d = x.shape
    depth = w_mod.shape[0]
    assert depth == 1 and d == D_MODEL
    t = bsz * seq
    lyr = 0
    lambda_init = 0.8 - 0.6 * math.exp(-0.3 * lyr)
    cos, slo, shi = _rope_tables(seq)

    xf = x.reshape(t, d)
    mod = _modulation(c, w_mod[lyr], b_mod[lyr]).reshape(bsz, 6, d)

    qT, k, vT, hq, hf, hi, hg, ga, gb = _in_projection(
        xf, mod, norm_pre_mix[lyr][None], w_in[lyr].astype(BF16), cos, slo, shi, seq, tm=512)
    ya = _diff_attention(qT, k, vT, da_lambda_q1[lyr][None], da_lambda_k1[lyr][None],
                         da_lambda_q2[lyr][None], da_lambda_k2[lyr][None], da_subln[lyr],
                         bsz, seq, lambda_init, tq=1024, tk=512)
    yb = _hgrn2(hq, hf, hi, hg, hg_lb_logits[lyr:lyr + 2], hg_norm[lyr][None], bsz, seq, tc=512, nh=4)

    rw = jnp.zeros((d, LANES), BF16).at[:, :N_EXPERTS].set(router_w[lyr].astype(BF16))
    rb = jnp.full((1, LANES), NEG_BIG, F32).at[0, :N_EXPERTS].set(router_b[lyr])
    x1, h2, topi, gate, rank, counts = _merge_route(
        xf, mod, ya, yb, ga, gb, w_branch_a[lyr].astype(BF16), w_branch_b[lyr].astype(BF16),
        w_out[lyr].astype(BF16), norm_post_mix[lyr][None], norm_pre_ffn[lyr][None], rw, rb,
        seq, tm=MOE_TILE)

    n_tiles = t // MOE_TILE
    up = lambda a, m: (a + m - 1) // m * m
    cnt = counts[:, 0, :N_EXPERTS].astype(I32)
    run = up(cnt, RUN_ALIGN)
    srun = jnp.cumsum(run, axis=1) - run
    e_len = up(jnp.sum(run, axis=0), MOE_BLOCK)
    e_end = jnp.cumsum(e_len)
    start = (e_end - e_len)[None, :] + jnp.cumsum(run, axis=0) - run
    n_assign = t * TOP_K
    n_rows = (up(n_assign + n_tiles * N_EXPERTS * (RUN_ALIGN - 1), MOE_BLOCK)
              + N_EXPERTS * MOE_BLOCK)
    n_blocks = n_rows // MOE_BLOCK
    block_start = jnp.arange(n_blocks, dtype=I32) * MOE_BLOCK
    block_e = jnp.minimum(jnp.sum((e_end[None, :] <= block_start[:, None]).astype(I32), axis=1),
                          N_EXPERTS - 1)
    n_used = (e_end[-1:] // MOE_BLOCK).astype(I32)
    srow = jnp.broadcast_to(jnp.pad(srun.astype(F32), ((0, 0), (0, LANES - N_EXPERTS)))[:, None, :],
                            (n_tiles, SUBLANES, LANES))
    run_tables = (start.reshape(-1), srun.reshape(-1), (run // RUN_ALIGN).reshape(-1),
                  jnp.sum(run, axis=1) // RUN_ALIGN)

    xs = _dispatch(h2, topi, rank, srow, run_tables + (e_end.astype(I32), e_len.astype(I32)),
                   n_rows, tm=MOE_TILE)
    y = _experts(xs, block_e, n_used, w_gate_up[lyr], b_gate_up[lyr], w_down[lyr], b_down[lyr])
    out = _combine(y, topi, rank, srow, gate, x1, mod, norm_post_ffn[lyr][None], run_tables,
                   seq, tm=MOE_TILE)
    return out.reshape(bsz, seq, d)
```

```python
import functools
import math

import jax
import jax.numpy as jnp
from jax import lax
from jax.experimental import pallas as pl
from jax.experimental.pallas import tpu as pltpu

F32 = jnp.float32
BF16 = jnp.bfloat16
I32 = jnp.int32

D_MODEL = 1024
CHUNK = 64
EPS = 1e-6
ROPE_THETA = 10000.0
DA_HEADS = 4
DA_HEAD_DIM = 64
HG_HEADS = 4
HEAD_W = 128
BRANCH_W = 512
V_ROWS = HEAD_W + 16
ATTN_STRIP = 256
N_EXPERTS = 32
TOP_K = 4
SWIGLU_LIMIT = 7.0
SWIGLU_ALPHA = 1.702
MOE_BLOCK = 512
MOE_TILE = 512
LANES = 128
SUBLANES = 8
RUN_ALIGN = SUBLANES
RUN_BITS = (MOE_TILE // RUN_ALIGN).bit_length()
RUN_SMALL_BITS = 4
SORT_ROWS = MOE_TILE * TOP_K + N_EXPERTS * RUN_ALIGN
NEG_BIG = -1e30

VMEM_LIMIT = 56 * 1024 * 1024


def _cparams(sem):
    return pltpu.CompilerParams(dimension_semantics=sem, vmem_limit_bytes=VMEM_LIMIT)


def _rms(x, w):
    return x * lax.rsqrt(jnp.mean(x * x, axis=-1, keepdims=True) + EPS) * w


def _sigmoid(x):
    return 1.0 / (1.0 + jnp.exp(-x))


def _split3(x):
    a = x.astype(BF16)
    r = x - a.astype(F32)
    b = r.astype(BF16)
    c = (r - b.astype(F32)).astype(BF16)
    return a, b, c


def _mod_kernel(c_ref, w_ref, b_ref, o_ref):
    c = c_ref[...]
    ca = c * _sigmoid(c)
    o_ref[...] = jnp.dot(ca, w_ref[...], preferred_element_type=F32,
                         precision=lax.Precision.HIGHEST) + b_ref[...]


def _modulation(c, w_mod, b_mod):
    bsz, d = c.shape
    n = w_mod.shape[1]
    tn = 1536
    return pl.pallas_call(
        _mod_kernel,
        grid=(n // tn,),
        in_specs=[pl.BlockSpec((bsz, d), lambda j: (0, 0)),
                  pl.BlockSpec((d, tn), lambda j: (0, j)),
                  pl.BlockSpec((1, tn), lambda j: (0, j))],
        out_specs=pl.BlockSpec((bsz, tn), lambda j: (0, j)),
        out_shape=jax.ShapeDtypeStruct((bsz, n), F32),
        compiler_params=_cparams(("arbitrary",)),
        name="modulation",
    )(c, w_mod, b_mod.reshape(1, n))


def _rope(x, cos, sin_lo, sin_hi):
    return x * cos + pltpu.roll(x, 96, 1) * sin_lo + pltpu.roll(x, 32, 1) * sin_hi


def _inproj_kernel(x_ref, mod_ref, nw_ref, w_ref, cos_ref, slo_ref, shi_ref,
                   qT_ref, k_ref, vT_ref, hq_ref, hf_ref, hi_ref, hg_ref, ga_ref, gb_ref):
    x = x_ref[...]
    sh1 = mod_ref[0:1, :]
    sc1 = mod_ref[1:2, :]
    h = (_rms(x, nw_ref[...]) * (1.0 + sc1) + sh1).astype(BF16)
    cos, slo, shi = cos_ref[...], slo_ref[...], shi_ref[...]
    scale = DA_HEAD_DIM ** -0.5 * math.log2(math.e)

    def seg(i, width=BRANCH_W):
        return jnp.dot(h, w_ref[:, i:i + width], preferred_element_type=F32)

    q = seg(0)
    k = seg(BRANCH_W)
    for hd in range(DA_HEADS):
        sl = slice(hd * HEAD_W, (hd + 1) * HEAD_W)
        qT_ref[sl, :] = (_rope(q[:, sl], cos, slo, shi) * scale).T.astype(BF16)
        k_ref[:, sl] = _rope(k[:, sl], cos, slo, shi).astype(BF16)
    vT = seg(2 * BRANCH_W).T.astype(BF16)
    ones = jnp.ones((V_ROWS - HEAD_W, vT.shape[1]), BF16)
    for hd in range(DA_HEADS):
        vT_ref[hd * V_ROWS:hd * V_ROWS + HEAD_W, :] = vT[hd * HEAD_W:(hd + 1) * HEAD_W]
        vT_ref[hd * V_ROWS + HEAD_W:(hd + 1) * V_ROWS, :] = ones
    hq = seg(3 * BRANCH_W)
    hq_ref[...] = (hq * _sigmoid(hq)).astype(BF16)
    hf_ref[...] = seg(4 * BRANCH_W)
    hi_ref[...] = seg(5 * BRANCH_W).astype(BF16)
    hg = seg(6 * BRANCH_W)
    hg_ref[...] = (hg * _sigmoid(hg)).astype(BF16)
    for j in range(2):
        ga_ref[:, j * BRANCH_W:(j + 1) * BRANCH_W] = _sigmoid(seg((7 + j) * BRANCH_W)).astype(BF16)
        gb_ref[:, j * BRANCH_W:(j + 1) * BRANCH_W] = _sigmoid(seg((9 + j) * BRANCH_W)).astype(BF16)


def _in_projection(xf, mod, norm_w, w_in_bf, cos, slo, shi, seq, tm):
    t, d = xf.shape
    tiles_per_batch = seq // tm
    bspec = lambda w: pl.BlockSpec((tm, w), lambda i: (i, 0))
    tab = pl.BlockSpec((tm, LANES), lambda i: (i % tiles_per_batch, 0))
    widths = [BRANCH_W] * 7 + [D_MODEL, D_MODEL]
    dtypes = [BF16, BF16, BF16, BF16, F32, BF16, BF16, BF16, BF16]
    out_specs = [bspec(w) for w in widths]
    out_shape = [jax.ShapeDtypeStruct((t, w), dt) for w, dt in zip(widths, dtypes)]
    for idx, rows in ((0, BRANCH_W), (2, DA_HEADS * V_ROWS)):
        out_specs[idx] = pl.BlockSpec((None, rows, tm),
                                      lambda i: (i // tiles_per_batch, 0, i % tiles_per_batch))
        out_shape[idx] = jax.ShapeDtypeStruct((t // seq, rows, seq), BF16)
    return pl.pallas_call(
        _inproj_kernel,
        grid=(t // tm,),
        in_specs=[bspec(d),
                  pl.BlockSpec((None, 6, d), lambda i: (i // tiles_per_batch, 0, 0)),
                  pl.BlockSpec((1, d), lambda i: (0, 0)),
                  pl.BlockSpec(w_in_bf.shape, lambda i: (0, 0), pipeline_mode=pl.Buffered(1)),
                  tab, tab, tab],
        out_specs=out_specs,
        out_shape=out_shape,
        compiler_params=_cparams(("parallel",)),
        name="in_projection",
    )(xf, mod, norm_w, w_in_bf, cos, slo, shi)


def _attn_kernel(lq1_ref, lk1_ref, lq2_ref, lk2_ref, subln_ref, qT_ref, k_ref, vT_ref, o_ref,
                 qq_ref, s0_ref, s1_ref, p0_ref, p1_ref, a0_ref, a1_ref, m_ref, acc_ref,
                 *, tq, tk, lambda_init):
    qi = pl.program_id(2)
    ndiag = tq // tk
    s_refs, p_refs, a_refs = (s0_ref, s1_ref), (p0_ref, p1_ref), (a0_ref, a1_ref)
    qT = qT_ref[...]
    feat = lax.broadcasted_iota(I32, qT.shape, 0)
    zero = jnp.zeros_like(qT)
    qq_ref[:, :tq] = jnp.where(feat < DA_HEAD_DIM, qT, zero)
    qq_ref[:, tq:] = jnp.where(feat >= DA_HEAD_DIM, qT, zero)

    strips = [slice(c * ATTN_STRIP, (c + 1) * ATTN_STRIP) for c in range(2 * tq // ATTN_STRIP)]

    def scores(kb, dst, cs):
        start = pl.multiple_of(kb * tk, tk)
        dst[:, cs] = jnp.dot(k_ref[pl.ds(start, tk), :], qq_ref[:, cs], preferred_element_type=F32)

    def accumulate(kb, slot, cs):
        start = pl.multiple_of(kb * tk, tk)
        pv = jnp.dot(vT_ref[:, pl.ds(start, tk)], p_refs[slot][:, cs], preferred_element_type=F32)
        acc_ref[:, cs] = a_refs[slot][:, cs] * acc_ref[:, cs] + pv

    def softmax(slot, cs, diag):
        def piece(rc):
            sc = s_refs[slot][rc * CHUNK:(rc + 1) * CHUNK, cs]
            if diag is not None:
                qchunk = (lax.broadcasted_iota(I32, (1, ATTN_STRIP), 1) + cs.start) % tq // CHUNK
                sc = jnp.where(qchunk >= diag * (tk // CHUNK) + rc, sc, NEG_BIG)
            return sc

        top = piece(0)
        for rc in range(1, tk // CHUNK):
            top = jnp.maximum(top, piece(rc))
        m_old = m_ref[:, cs]
        m_new = jnp.maximum(m_old, jnp.max(top, axis=0, keepdims=True))
        m_ref[:, cs] = m_new
        a_refs[slot][:, cs] = jnp.exp2(m_old - m_new)
        for rc in range(tk // CHUNK):
            p_refs[slot][rc * CHUNK:(rc + 1) * CHUNK, cs] = jnp.exp2((piece(rc) - m_new).astype(BF16))

    def seen(diag, cs):
        return diag is None or diag < 0 or (cs.start % tq) + ATTN_STRIP > diag * tk

    def stage(kb, slot, diag=None, last=False):
        for cs in strips:
            if seen(None if diag is None else diag - 1, cs):
                accumulate(jnp.maximum(kb - 1, 0), 1 - slot, cs)
            if seen(diag, cs):
                softmax(slot, cs, diag)
            if not last and seen(None if diag is None else diag + 1, cs):
                scores(kb + 1, s_refs[1 - slot], cs)

    p1_ref[...] = jnp.zeros_like(p1_ref)
    a1_ref[...] = jnp.ones_like(a1_ref)
    m_ref[...] = jnp.full(m_ref.shape, NEG_BIG, F32)
    acc_ref[...] = jnp.zeros_like(acc_ref)
    for cs in strips:
        scores(0, s0_ref, cs)

    def pair(i, c):
        stage(2 * i, 0)
        stage(2 * i + 1, 1)
        return c

    first_diag = qi * ndiag
    lax.fori_loop(0, first_diag // 2, pair, 0)
    for dg in range(ndiag):
        stage(first_diag + dg, dg % 2, diag=dg, last=dg == ndiag - 1)
    for cs in strips:
        if seen(ndiag - 1, cs):
            accumulate(first_diag + ndiag - 1, (ndiag - 1) % 2, cs)

    acc = acc_ref[...]
    o = acc[:HEAD_W] / acc[HEAD_W:HEAD_W + 1]
    lam = (jnp.exp(jnp.sum(lq1_ref[...] * lk1_ref[...], axis=1, keepdims=True))
           - jnp.exp(jnp.sum(lq2_ref[...] * lk2_ref[...], axis=1, keepdims=True))
           + lambda_init)
    o = o[:, :tq] - lam * o[:, tq:]
    o = o * lax.rsqrt(jnp.mean(o * o, axis=0, keepdims=True) + EPS) * subln_ref[...]
    o_ref[...] = (o * (1.0 - lambda_init)).T.astype(BF16)


def _diff_attention(qT, k, vT, lq1, lk1, lq2, lk2, subln, bsz, seq, lambda_init, tq, tk):
    assert tq % (2 * tk) == 0
    k3 = k.reshape(bsz, seq, BRANCH_W)
    small = lambda n: pl.BlockSpec((1, n), lambda b, h, i: (0, 0))
    out = pl.pallas_call(
        functools.partial(_attn_kernel, tq=tq, tk=tk, lambda_init=lambda_init),
        grid=(bsz, DA_HEADS, seq // tq),
        in_specs=[small(DA_HEAD_DIM)] * 4 + [
            pl.BlockSpec((HEAD_W, 1), lambda b, h, i: (0, 0)),
            pl.BlockSpec((None, HEAD_W, tq), lambda b, h, i: (b, h, i)),
            pl.BlockSpec((None, seq, HEAD_W), lambda b, h, i: (b, 0, h)),
            pl.BlockSpec((None, V_ROWS, seq), lambda b, h, i: (b, h, 0))],
        out_specs=pl.BlockSpec((None, tq, HEAD_W), lambda b, h, i: (b, i, h)),
        out_shape=jax.ShapeDtypeStruct((bsz, seq, BRANCH_W), BF16),
        scratch_shapes=[pltpu.VMEM((HEAD_W, 2 * tq), BF16),
                        pltpu.VMEM((tk, 2 * tq), F32), pltpu.VMEM((tk, 2 * tq), F32),
                        pltpu.VMEM((tk, 2 * tq), BF16), pltpu.VMEM((tk, 2 * tq), BF16),
                        pltpu.VMEM((1, 2 * tq), F32), pltpu.VMEM((1, 2 * tq), F32),
                        pltpu.VMEM((1, 2 * tq), F32), pltpu.VMEM((V_ROWS, 2 * tq), F32)],
        compiler_params=_cparams(("parallel", "parallel", "arbitrary")),
        name="diff_attention",
    )(lq1, lk1, lq2, lk2, subln.reshape(HEAD_W, 1), qT, k3, vT)
    return out.reshape(bsz * seq, BRANCH_W)


def _hgrn_kernel(lbl_ref, nw_ref, q_ref, f_ref, i_ref, g_ref, o_ref, st_ref, *, tc, nh):
    @pl.when(pl.program_id(2) == 0)
    def _():
        st_ref[...] = jnp.zeros_like(st_ref)

    row = lax.broadcasted_iota(I32, (CHUNK, CHUNK), 0)
    col = lax.broadcasted_iota(I32, (CHUNK, CHUNK), 1)
    causal = col <= row
    tril = jnp.where(causal, 1.0, 0.0).astype(BF16)
    nt = (((1,), (1,)), ((), ()))
    tn = (((0,), (0,)), ((), ()))

    nchunk = tc // CHUNK
    units = [(hh, c) for hh in range(nh) for c in range(nchunk)]
    rows = lambda c: slice(c * CHUNK, (c + 1) * CHUNK)
    q, k, v, parts = [], [], [], []
    for hh in range(nh):
        hs = slice(hh * HEAD_W, (hh + 1) * HEAD_W)
        lg = lbl_ref[:, hs]
        ex = jnp.exp(lg - jnp.max(lg, axis=0, keepdims=True))
        lb = ex[0:1] / jnp.sum(ex, axis=0, keepdims=True)
        f = lb + (1.0 - lb) * _sigmoid(f_ref[:, hs])
        q.append(q_ref[:, hs].astype(F32))
        k.append(1.0 - f)
        v.append(i_ref[:, hs])
        parts.append(_split3(jnp.log(f)))
    gall = sum(jnp.dot(tril, jnp.concatenate([parts[hh][p][rows(c)] for hh, c in units], axis=1),
                       preferred_element_type=F32) for p in range(3))
    gcum = [gall[:, u * HEAD_W:(u + 1) * HEAD_W] for u in range(len(units))]
    gtot = [g[CHUNK - 1:CHUNK, :] for g in gcum]
    q_t = [(q[hh][rows(c)] * jnp.exp(gcum[u])).astype(BF16) for u, (hh, c) in enumerate(units)]
    k_t = [(k[hh][rows(c)] * jnp.exp(-gcum[u])).astype(BF16) for u, (hh, c) in enumerate(units)]
    k_dec = [(k[hh][rows(c)] * jnp.exp(gtot[u] - gcum[u])).astype(BF16)
             for u, (hh, c) in enumerate(units)]
    a = [lax.dot_general(q_t[u], k_t[u], nt, preferred_element_type=F32) for u in range(len(units))]
    ds = [lax.dot_general(v[hh][rows(c)], k_dec[u], tn, preferred_element_type=F32)
          for u, (hh, c) in enumerate(units)]
    a = [jnp.where(causal, x, 0.0).astype(BF16) for x in a]
    st_in = []
    for hh in range(nh):
        st = st_ref[hh]
        for c in range(nchunk):
            u = hh * nchunk + c
            st_in.append(st.astype(BF16))
            st = jnp.exp(gtot[u]) * st + ds[u]
        st_ref[hh] = st
    outs = [jnp.dot(a[u], v[hh][rows(c)], preferred_element_type=F32)
            + lax.dot_general(q_t[u], st_in[u], nt, preferred_element_type=F32)
            for u, (hh, c) in enumerate(units)]
    for hh in range(nh):
        hs = slice(hh * HEAD_W, (hh + 1) * HEAD_W)
        o = jnp.concatenate(outs[hh * nchunk:(hh + 1) * nchunk], axis=0)
        o_ref[:, hs] = (_rms(o, nw_ref[...]) * g_ref[:, hs].astype(F32)).astype(BF16)


def _hgrn2(hq, hf, hi, hg, lb_logits, norm_w, bsz, seq, tc, nh):
    r3 = lambda a: a.reshape(bsz, seq, BRANCH_W)
    blk = pl.BlockSpec((None, tc, nh * HEAD_W), lambda b, h, i: (b, i, h))
    out = pl.pallas_call(
        functools.partial(_hgrn_kernel, tc=tc, nh=nh),
        grid=(bsz, HG_HEADS // nh, seq // tc),
        in_specs=[pl.BlockSpec((2, nh * HEAD_W), lambda b, h, i: (0, h)),
                  pl.BlockSpec((1, HEAD_W), lambda b, h, i: (0, 0)),
                  blk, blk, blk, blk],
        out_specs=blk,
        out_shape=jax.ShapeDtypeStruct((bsz, seq, BRANCH_W), BF16),
        scratch_shapes=[pltpu.VMEM((nh, HEAD_W, HEAD_W), F32)],
        compiler_params=_cparams(("parallel", "parallel", "arbitrary")),
        name="hgrn2",
    )(lb_logits, norm_w, r3(hq), r3(hf), r3(hi), r3(hg))
    return out.reshape(bsz * seq, BRANCH_W)


def _merge_kernel(x_ref, mod_ref, ya_ref, yb_ref, ga_ref, gb_ref, wa_ref, wb_ref, wo_ref,
                  npost_ref, npre_ref, rw_ref, rb_ref,
                  x1_ref, h2_ref, topi_ref, gate_ref, rank_ref, cnt_ref, lg_ref, *, tm):
    @pl.when(pl.program_id(0) == 0)
    def _():
        lg_ref[...] = jnp.zeros_like(lg_ref)

    logits = lg_ref[...]

    g1 = mod_ref[2:3, :]
    sh2 = mod_ref[3:4, :]
    sc2 = mod_ref[4:5, :]
    a = jnp.dot(ya_ref[...], wa_ref[...], preferred_element_type=F32)
    b = jnp.dot(yb_ref[...], wb_ref[...], preferred_element_type=F32)
    merged = ga_ref[...] * a.astype(BF16) + gb_ref[...] * b.astype(BF16)
    y = jnp.dot(merged, wo_ref[...], preferred_element_type=F32)
    x1 = x_ref[...] + g1 * _rms(y, npost_ref[...])
    x1_ref[...] = x1
    h2 = (_rms(x1, npre_ref[...]) * (1.0 + sc2) + sh2).astype(BF16)
    h2_ref[...] = h2
    lg_ref[...] = jnp.dot(h2, rw_ref[...], preferred_element_type=F32) + rb_ref[...]

    lane = lax.broadcasted_iota(I32, logits.shape, 1)
    work = logits
    vals, idxs = [], []
    for _ in range(TOP_K):
        mx = jnp.max(work, axis=1, keepdims=True)
        idx = jnp.min(jnp.where(work == mx, lane, LANES), axis=1, keepdims=True)
        vals.append(mx)
        idxs.append(idx)
        work = jnp.where(lane == idx, -jnp.inf, work)
    es = [jnp.exp(vk - vals[0]) for vk in vals]
    den = es[0] + es[1] + es[2] + es[3]

    onehot = jnp.zeros(logits.shape, F32)
    for idx in idxs:
        onehot = jnp.where(lane == idx, 1.0, onehot)
    row = lax.broadcasted_iota(I32, (tm, tm), 0)
    col = lax.broadcasted_iota(I32, (tm, tm), 1)
    strict = jnp.where(col < row, 1.0, 0.0).astype(BF16)
    prefix = jnp.dot(strict, onehot.astype(BF16), preferred_element_type=F32)
    cnt_ref[...] = jnp.broadcast_to(jnp.sum(onehot, axis=0, keepdims=True), cnt_ref.shape)

    topi = jnp.zeros(logits.shape, I32)
    gate = jnp.zeros(logits.shape, F32)
    rank = jnp.zeros(logits.shape, I32)
    for kk in range(TOP_K):
        rk = jnp.sum(jnp.where(lane == idxs[kk], prefix, 0.0), axis=1, keepdims=True)
        topi = jnp.where(lane == kk, idxs[kk], topi)
        gate = jnp.where(lane == kk, es[kk] / den, gate)
        rank = jnp.where(lane == kk, rk.astype(I32), rank)
    topi_ref[...] = topi
    gate_ref[...] = gate
    rank_ref[...] = rank


def _merge_route(xf, mod, ya, yb, ga, gb, wa, wb, wo, npost, npre, rw, rb, seq, tm):
    t, d = xf.shape
    n = t // tm
    tiles_per_batch = seq // tm
    proj = lambda w: pl.BlockSpec((tm, w), lambda i: (jnp.minimum(i, n - 1), 0))
    route = lambda w: pl.BlockSpec((tm, w), lambda i: (jnp.maximum(i - 1, 0), 0))
    const = lambda a: pl.BlockSpec(a.shape, lambda i: (0,) * a.ndim)
    lane_out = jax.ShapeDtypeStruct((t, LANES), I32)
    return pl.pallas_call(
        functools.partial(_merge_kernel, tm=tm),
        grid=(n + 1,),
        in_specs=[proj(d),
                  pl.BlockSpec((None, 6, d),
                               lambda i: (jnp.minimum(i, n - 1) // tiles_per_batch, 0, 0)),
                  proj(BRANCH_W), proj(BRANCH_W), proj(d), proj(d),
                  const(wa), const(wb), const(wo), const(npost), const(npre),
                  const(rw), const(rb)],
        out_specs=[proj(d), proj(d), route(LANES), route(LANES), route(LANES),
                   pl.BlockSpec((None, SUBLANES, LANES), lambda i: (jnp.maximum(i - 1, 0), 0, 0))],
        out_shape=[jax.ShapeDtypeStruct((t, d), F32), jax.ShapeDtypeStruct((t, d), BF16),
                   lane_out, jax.ShapeDtypeStruct((t, LANES), F32), lane_out,
                   jax.ShapeDtypeStruct((n, SUBLANES, LANES), F32)],
        scratch_shapes=[pltpu.VMEM((tm, LANES), F32)],
        compiler_params=_cparams(("arbitrary",)),
        name="merge_route",
    )(xf, mod, ya, yb, ga, gb, wa, wb, wo, npost, npre, rw, rb)


def _positions(topi_ref, rank_ref, srow_ref):
    topi = topi_ref[...]
    rank = rank_ref[...]
    lane = lax.broadcasted_iota(I32, topi.shape, 1)
    srow = srow_ref[0:1, :]
    pos = []
    for kk in range(TOP_K):
        base = jnp.sum(jnp.where(lane == topi[:, kk:kk + 1], srow, 0.0), axis=1, keepdims=True)
        pos.append(base.astype(I32) + rank[:, kk:kk + 1])
    return pos


def _wait_rows(n8, make_copy):
    for b in range((SORT_ROWS // RUN_ALIGN).bit_length()):
        @pl.when((n8 >> b) & 1 == 1)
        def _(b=b):
            make_copy(RUN_ALIGN << b).wait()


def _run_copies(tile, l8_ref, make_copy):
    def pieces(e, n8, bits, off):
        for b in bits:
            size = RUN_ALIGN << b
            bit = (n8 >> b) & 1

            @pl.when(bit == 1)
            def _(off=off, size=size):
                make_copy(e, off, size).start()

            off = off + bit * size

    for e in range(N_EXPERTS):
        n8 = l8_ref[tile * N_EXPERTS + e]
        big = (n8 >> RUN_SMALL_BITS) << RUN_SMALL_BITS

        @pl.when(big > 0)
        def _(e=e, n8=n8):
            pieces(e, n8, range(RUN_BITS - 1, RUN_SMALL_BITS - 1, -1), jnp.int32(0))

        pieces(e, n8, range(RUN_SMALL_BITS - 1, -1, -1), big * RUN_ALIGN)


def _dispatch_kernel(start_ref, srun_ref, l8_ref, tot8_ref, eend_ref, elen_ref,
                     h_ref, topi_ref, rank_ref, srow_ref, xs_out,
                     sel0, sel1, srt0, srt1, zbuf, zsem, sem, *, min_blocks):
    i = pl.program_id(0)
    n_tiles = pl.num_programs(0) - 1
    sels, srts = (sel0, sel1), (srt0, srt1)
    tm = h_ref.shape[0]
    half = h_ref.shape[1] // 2
    n_blocks = xs_out.shape[0] // MOE_BLOCK
    n_used = eend_ref[N_EXPERTS - 1] // MOE_BLOCK

    def zero_copy(start):
        start = pl.multiple_of(start, MOE_BLOCK)
        return pltpu.make_async_copy(zbuf, xs_out.at[pl.ds(start, MOE_BLOCK)], zsem)

    def zero_fill(wait):
        for e in range(N_EXPERTS):
            @pl.when(elen_ref[e] > 0)
            def _(e=e):
                cp = zero_copy(eend_ref[e] - MOE_BLOCK)
                cp.wait() if wait else cp.start()
        for blk in range(min_blocks, n_blocks):
            @pl.when(blk >= n_used)
            def _(blk=blk):
                cp = zero_copy(blk * MOE_BLOCK)
                cp.wait() if wait else cp.start()

    @pl.when(i == 0)
    def _():
        zbuf[...] = jnp.zeros_like(zbuf)
        sel0[...] = jnp.zeros_like(sel0)
        sel1[...] = jnp.zeros_like(sel1)
        zero_fill(wait=False)
        zero_fill(wait=True)

    def start_copies(tile, slot):
        def make_copy(e, off, size):
            src = pl.multiple_of(srun_ref[tile * N_EXPERTS + e] + off, RUN_ALIGN)
            dst = pl.multiple_of(start_ref[tile * N_EXPERTS + e] + off, RUN_ALIGN)
            return pltpu.make_async_copy(srts[slot].at[pl.ds(src, size)],
                                         xs_out.at[pl.ds(dst, size)], sem.at[slot])
        _run_copies(tile, l8_ref, make_copy)

    def wait_copies(tile, slot):
        def make_copy(size):
            return pltpu.make_async_copy(srts[slot].at[pl.ds(0, size)],
                                         xs_out.at[pl.ds(0, size)], sem.at[slot])
        _wait_rows(tot8_ref[tile], make_copy)

    for s in range(2):
        @pl.when(i % 2 == s)
        def _(s=s):
            pos = _positions(topi_ref, rank_ref, srow_ref)
            slot = lax.broadcasted_iota(I32, (tm, SORT_ROWS), 1)
            sel = jnp.zeros((tm, SORT_ROWS), F32)
            for p in pos:
                sel = jnp.where(slot == p, 1.0, sel)
            sels[s][...] = sel.astype(BF16)
            srt = lax.dot_general(sels[1 - s][...], h_ref[...], (((0,), (0,)), ((), ())),
                                  preferred_element_type=F32)
            lo = pltpu.bitcast(srt[:, :half], jnp.uint32) >> 16
            hi = pltpu.bitcast(srt[:, half:], jnp.uint32) & jnp.uint32(0xFFFF0000)
            srts[1 - s][...] = hi | lo

        @pl.when((i >= 1) & (i % 2 == s))
        def _(s=s):
            start_copies(i - 1, 1 - s)

        @pl.when((i >= 2) & (i % 2 == s))
        def _(s=s):
            wait_copies(i - 2, s)

        @pl.when((i == n_tiles) & (i % 2 == s))
        def _(s=s):
            wait_copies(i - 1, 1 - s)


def _dispatch(h2, topi, rank, srow, tables, n_rows, tm):
    t, d = h2.shape
    n = t // tm
    cur = lambda w: pl.BlockSpec((tm, w), lambda i, *_: (jnp.minimum(i, n - 1), 0))
    grid_spec = pltpu.PrefetchScalarGridSpec(
        num_scalar_prefetch=len(tables),
        grid=(n + 1,),
        in_specs=[pl.BlockSpec((tm, d), lambda i, *_: (jnp.maximum(i - 1, 0), 0)),
                  cur(LANES), cur(LANES),
                  pl.BlockSpec((None, SUBLANES, LANES), lambda i, *_: (jnp.minimum(i, n - 1), 0, 0))],
        out_specs=pl.BlockSpec(memory_space=pl.ANY),
        scratch_shapes=[pltpu.VMEM((tm, SORT_ROWS), BF16), pltpu.VMEM((tm, SORT_ROWS), BF16),
                        pltpu.VMEM((SORT_ROWS, d // 2), jnp.uint32),
                        pltpu.VMEM((SORT_ROWS, d // 2), jnp.uint32),
                        pltpu.VMEM((MOE_BLOCK, d // 2), jnp.uint32),
                        pltpu.SemaphoreType.DMA, pltpu.SemaphoreType.DMA((2,))],
    )
    return pl.pallas_call(
        functools.partial(_dispatch_kernel, min_blocks=t * TOP_K // MOE_BLOCK),
        grid_spec=grid_spec,
        out_shape=jax.ShapeDtypeStruct((n_rows, d // 2), jnp.uint32),
        compiler_params=_cparams(("arbitrary",)),
        name="moe_dispatch",
    )(*tables, h2, topi, rank, srow)


def _expert_kernel(be_ref, nu_ref, x_ref, wgu_ref, bgu_ref, wd_ref, bd_ref, y_ref,
                   wgu_bf, wd_bf):
    j = pl.program_id(0)
    half = D_MODEL // 2

    @pl.when((j == 0) | (be_ref[j] != be_ref[jnp.maximum(j - 1, 0)]))
    def _():
        wgu_bf[...] = wgu_ref[...].astype(BF16)
        wd_bf[...] = wd_ref[...].astype(BF16)

    @pl.when(j < nu_ref[0])
    def _():
        xu = x_ref[...]
        x_lo = pltpu.bitcast(xu << 16, F32).astype(BF16)
        x_hi = pltpu.bitcast(xu & jnp.uint32(0xFFFF0000), F32).astype(BF16)
        gu = (jnp.dot(x_lo, wgu_bf[:half, :], preferred_element_type=F32)
              + jnp.dot(x_hi, wgu_bf[half:, :], preferred_element_type=F32) + bgu_ref[...])
        g = jnp.minimum(gu[:, :D_MODEL], SWIGLU_LIMIT)
        u = jnp.clip(gu[:, D_MODEL:], -SWIGLU_LIMIT, SWIGLU_LIMIT)
        act = (u + 1.0) * (g * _sigmoid(g * SWIGLU_ALPHA))
        y_ref[...] = jnp.dot(act.astype(BF16), wd_bf[...], preferred_element_type=F32) + bd_ref[...]

    @pl.when(j >= nu_ref[0])
    def _():
        y_ref[...] = jnp.zeros_like(y_ref)


def _experts(xs, block_e, n_used, wgu, bgu, wd, bd):
    p = xs.shape[0]
    e, d, _ = wgu.shape
    n_blocks = p // MOE_BLOCK
    grid_spec = pltpu.PrefetchScalarGridSpec(
        num_scalar_prefetch=2,
        grid=(n_blocks,),
        in_specs=[pl.BlockSpec((MOE_BLOCK, d // 2), lambda j, be, nu: (j, 0)),
                  pl.BlockSpec((None, d, 2 * d), lambda j, be, nu: (be[j], 0, 0)),
                  pl.BlockSpec((None, 1, 2 * d), lambda j, be, nu: (be[j], 0, 0)),
                  pl.BlockSpec((None, d, d), lambda j, be, nu: (be[j], 0, 0)),
                  pl.BlockSpec((None, 1, d), lambda j, be, nu: (be[j], 0, 0))],
        out_specs=pl.BlockSpec((MOE_BLOCK, d), lambda j, be, nu: (j, 0)),
        scratch_shapes=[pltpu.VMEM((d, 2 * d), BF16), pltpu.VMEM((d, d), BF16)],
    )
    return pl.pallas_call(
        _expert_kernel,
        grid_spec=grid_spec,
        out_shape=jax.ShapeDtypeStruct((p, d), F32),
        compiler_params=_cparams(("arbitrary",)),
        name="moe_experts",
    )(block_e, n_used, xs, wgu, bgu.reshape(e, 1, 2 * d), wd, bd.reshape(e, 1, d))


def _combine_kernel(start_ref, srun_ref, l8_ref, tot8_ref,
                    y_hbm, topi_ref, rank_ref, srow_ref, gate_ref, x1_ref, mod_ref, nw_ref, o_ref,
                    sel0, sel1, yb0, yb1, sem):
    i = pl.program_id(0)
    n_tiles = pl.num_programs(0) - 1
    tm = x1_ref.shape[0]
    sels, ybufs = (sel0, sel1), (yb0, yb1)

    def start_copies(tile, slot):
        def make_copy(e, off, size):
            src = pl.multiple_of(start_ref[tile * N_EXPERTS + e] + off, RUN_ALIGN)
            dst = pl.multiple_of(srun_ref[tile * N_EXPERTS + e] + off, RUN_ALIGN)
            return pltpu.make_async_copy(y_hbm.at[pl.ds(src, size)],
                                         ybufs[slot].at[pl.ds(dst, size)], sem.at[slot])
        _run_copies(tile, l8_ref, make_copy)

    def wait_copies(tile, slot):
        def make_copy(size):
            return pltpu.make_async_copy(y_hbm.at[pl.ds(0, size)],
                                         ybufs[slot].at[pl.ds(0, size)], sem.at[slot])
        _wait_rows(tot8_ref[tile], make_copy)

    @pl.when(i == 0)
    def _():
        yb0[...] = jnp.zeros_like(yb0)
        yb1[...] = jnp.zeros_like(yb1)
        sel0[...] = jnp.zeros_like(sel0)
        sel1[...] = jnp.zeros_like(sel1)

    for s in range(2):
        @pl.when((i < n_tiles) & (i % 2 == s))
        def _(s=s):
            start_copies(i, s)

        @pl.when((i >= 1) & (i % 2 == s))
        def _(s=s):
            wait_copies(i - 1, 1 - s)

        @pl.when(i % 2 == s)
        def _(s=s):
            pos = _positions(topi_ref, rank_ref, srow_ref)
            gate = gate_ref[...]
            row = lax.broadcasted_iota(I32, (tm, SORT_ROWS), 1)
            sel = jnp.zeros((tm, SORT_ROWS), F32)
            for kk, p in enumerate(pos):
                sel = jnp.where(row == p, gate[:, kk:kk + 1], sel)
            sels[s][...] = sel.astype(BF16)
            moe = jnp.dot(sels[1 - s][...], ybufs[1 - s][...].astype(BF16),
                          preferred_element_type=F32)
            g2 = mod_ref[5:6, :]
            o_ref[...] = x1_ref[...] + g2 * _rms(moe, nw_ref[...])


def _combine(y, topi, rank, srow, gate, x1, mod, nw, tables, seq, tm):
    t, d = x1.shape
    n = t // tm
    tiles_per_batch = seq // tm
    cur = lambda w: pl.BlockSpec((tm, w), lambda i, *_: (jnp.minimum(i, n - 1), 0))
    prev = lambda w: pl.BlockSpec((tm, w), lambda i, *_: (jnp.maximum(i - 1, 0), 0))
    grid_spec = pltpu.PrefetchScalarGridSpec(
        num_scalar_prefetch=len(tables),
        grid=(n + 1,),
        in_specs=[pl.BlockSpec(memory_space=pl.ANY),
                  cur(LANES), cur(LANES),
                  pl.BlockSpec((None, SUBLANES, LANES), lambda i, *_: (jnp.minimum(i, n - 1), 0, 0)),
                  cur(LANES), prev(d),
                  pl.BlockSpec((None, 6, d),
                               lambda i, *_: (jnp.maximum(i - 1, 0) // tiles_per_batch, 0, 0)),
                  pl.BlockSpec((1, d), lambda i, *_: (0, 0))],
        out_specs=prev(d),
        scratch_shapes=[pltpu.VMEM((tm, SORT_ROWS), BF16), pltpu.VMEM((tm, SORT_ROWS), BF16),
                        pltpu.VMEM((SORT_ROWS, d), F32), pltpu.VMEM((SORT_ROWS, d), F32),
                        pltpu.SemaphoreType.DMA((2,))],
    )
    return pl.pallas_call(
        _combine_kernel,
        grid_spec=grid_spec,
        out_shape=jax.ShapeDtypeStruct((t, d), F32),
        compiler_params=_cparams(("arbitrary",)),
        name="moe_combine",
    )(*tables, y, topi, rank, srow, gate, x1, mod, nw)


def _rope_tables(seq):
    pos = jnp.arange(seq, dtype=F32)
    inv = ROPE_THETA ** (-jnp.arange(0, DA_HEAD_DIM, 2, dtype=F32) / DA_HEAD_DIM)
    ang = pos[:, None] * inv[None, :]
    cos, sin = jnp.cos(ang), jnp.sin(ang)
    zero = jnp.zeros_like(sin)
    cos_t = jnp.tile(cos, (1, LANES // cos.shape[1]))
    sin_lo = jnp.tile(jnp.concatenate([-sin, zero], axis=1), (1, 2))
    sin_hi = jnp.tile(jnp.concatenate([zero, sin], axis=1), (1, 2))
    return cos_t, sin_lo, sin_hi


def kernel(x, c, w_mod, b_mod, norm_pre_mix, norm_post_mix, w_in, da_lambda_q1, da_lambda_k1,
           da_lambda_q2, da_lambda_k2, da_subln, hg_lb_logits, hg_norm, w_branch_a, w_branch_b,
           w_out, norm_pre_ffn, norm_post_ffn, router_w, router_b, w_gate_up, b_gate_up,
           w_down, b_down):
    bsz, seq, d = x.shape
    depth = w_mod.shape[0]
    assert depth == 1 and d == D_MODEL
    t = bsz * seq
    lyr = 0
    lambda_init = 0.8 - 0.6 * math.exp(-0.3 * lyr)
    cos, slo, shi = _rope_tables(seq)

    xf = x.reshape(t, d)
    mod = _modulation(c, w_mod[lyr], b_mod[lyr]).reshape(bsz, 6, d)

    qT, k, vT, hq, hf, hi, hg, ga, gb = _in_projection(
        xf, mod, norm_pre_mix[lyr][None], w_in[lyr].astype(BF16), cos, slo, shi, seq, tm=512)
    ya = _diff_attention(qT, k, vT, da_lambda_q1[lyr][None], da_lambda_k1[lyr][None],
                         da_lambda_q2[lyr][None], da_lambda_k2[lyr][None], da_subln[lyr],
                         bsz, seq, lambda_init, tq=1024, tk=512)
    yb = _hgrn2(hq, hf, hi, hg, hg_lb_logits[lyr:lyr + 2], hg_norm[lyr][None], bsz, seq, tc=512, nh=4)

    rw = jnp.zeros((d, LANES), BF16).at[:, :N_EXPERTS].set(router_w[lyr].astype(BF16))
    rb = jnp.full((1, LANES), NEG_BIG, F32).at[0, :N_EXPERTS].set(router_b[lyr])
    x1, h2, topi, gate, rank, counts = _merge_route(
        xf, mod, ya, yb, ga, gb, w_branch_a[lyr].astype(BF16), w_branch_b[lyr].astype(BF16),
        w_out[lyr].astype(BF16), norm_post_mix[lyr][None], norm_pre_ffn[lyr][None], rw, rb,
        seq, tm=MOE_TILE)

    n_tiles = t // MOE_TILE
    up = lambda a, m: (a + m - 1) // m * m
    cnt = counts[:, 0, :N_EXPERTS].astype(I32)
    run = up(cnt, RUN_ALIGN)
    srun = jnp.cumsum(run, axis=1) - run
    e_len = up(jnp.sum(run, axis=0), MOE_BLOCK)
    e_end = jnp.cumsum(e_len)
    start = (e_end - e_len)[None, :] + jnp.cumsum(run, axis=0) - run
    n_assign = t * TOP_K
    n_rows = (up(n_assign + n_tiles * N_EXPERTS * (RUN_ALIGN - 1), MOE_BLOCK)
              + N_EXPERTS * MOE_BLOCK)
    n_blocks = n_rows // MOE_BLOCK
    block_start = jnp.arange(n_blocks, dtype=I32) * MOE_BLOCK
    block_e = jnp.minimum(jnp.sum((e_end[None, :] <= block_start[:, None]).astype(I32), axis=1),
                          N_EXPERTS - 1)
    n_used = (e_end[-1:] // MOE_BLOCK).astype(I32)
    srow = jnp.broadcast_to(jnp.pad(srun.astype(F32), ((0, 0), (0, LANES - N_EXPERTS)))[:, None, :],
                            (n_tiles, SUBLANES, LANES))
    run_tables = (start.reshape(-1), srun.reshape(-1), (run // RUN_ALIGN).reshape(-1),
                  jnp.sum(run, axis=1) // RUN_ALIGN)

    xs = _dispatch(h2, topi, rank, srow, run_tables + (e_end.astype(I32), e_len.astype(I32)),
                   n_rows, tm=MOE_TILE)
    y = _experts(xs, block_e, n_used, w_gate_up[lyr], b_gate_up[lyr], w_down[lyr], b_down[lyr])
    out = _combine(y, topi, rank, srow, gate, x1, mod, norm_post_ffn[lyr][None], run_tables,
                   seq, tm=MOE_TILE)
    return out.reshape(bsz, seq, d)
```

```python
import functools
import math

import jax
import jax.numpy as jnp
from jax import lax
from jax.experimental import pallas as pl
from jax.experimental.pallas import tpu as pltpu

F32 = jnp.float32
BF16 = jnp.bfloat16
I32 = jnp.int32

D_MODEL = 1024
CHUNK = 64
EPS = 1e-6
ROPE_THETA = 10000.0
DA_HEADS = 4
DA_HEAD_DIM = 64
HG_HEADS = 4
HEAD_W = 128
BRANCH_W = 512
V_ROWS = HEAD_W + 16
ATTN_STRIP = 256
N_EXPERTS = 32
TOP_K = 4
SWIGLU_LIMIT = 7.0
SWIGLU_ALPHA = 1.702
MOE_BLOCK = 512
MOE_TILE = 512
LANES = 128
SUBLANES = 8
RUN_ALIGN = SUBLANES
RUN_BITS = (MOE_TILE // RUN_ALIGN).bit_length()
RUN_SMALL_BITS = 4
SORT_ROWS = MOE_TILE * TOP_K + N_EXPERTS * RUN_ALIGN
NEG_BIG = -1e30

VMEM_LIMIT = 56 * 1024 * 1024


def _cparams(sem):
    return pltpu.CompilerParams(dimension_semantics=sem, vmem_limit_bytes=VMEM_LIMIT)


def _rms(x, w):
    return x * lax.rsqrt(jnp.mean(x * x, axis=-1, keepdims=True) + EPS) * w


def _sigmoid(x):
    return 1.0 / (1.0 + jnp.exp(-x))


def _split3(x):
    a = x.astype(BF16)
    r = x - a.astype(F32)
    b = r.astype(BF16)
    c = (r - b.astype(F32)).astype(BF16)
    return a, b, c


def _mod_kernel(c_ref, w_ref, b_ref, o_ref):
    c = c_ref[...]
    ca = c * _sigmoid(c)
    o_ref[...] = jnp.dot(ca, w_ref[...], preferred_element_type=F32,
                         precision=lax.Precision.HIGHEST) + b_ref[...]


def _modulation(c, w_mod, b_mod):
    bsz, d = c.shape
    n = w_mod.shape[1]
    tn = 1536
    return pl.pallas_call(
        _mod_kernel,
        grid=(n // tn,),
        in_specs=[pl.BlockSpec((bsz, d), lambda j: (0, 0)),
                  pl.BlockSpec((d, tn), lambda j: (0, j)),
                  pl.BlockSpec((1, tn), lambda j: (0, j))],
        out_specs=pl.BlockSpec((bsz, tn), lambda j: (0, j)),
        out_shape=jax.ShapeDtypeStruct((bsz, n), F32),
        compiler_params=_cparams(("arbitrary",)),
        name="modulation",
    )(c, w_mod, b_mod.reshape(1, n))


def _rope(x, cos, sin_lo, sin_hi):
    return x * cos + pltpu.roll(x, 96, 1) * sin_lo + pltpu.roll(x, 32, 1) * sin_hi


def _inproj_kernel(x_ref, mod_ref, nw_ref, w_ref, cos_ref, slo_ref, shi_ref,
                   qT_ref, k_ref, vT_ref, hq_ref, hf_ref, hi_ref, hg_ref, ga_ref, gb_ref):
    x = x_ref[...]
    sh1 = mod_ref[0:1, :]
    sc1 = mod_ref[1:2, :]
    h = (_rms(x, nw_ref[...]) * (1.0 + sc1) + sh1).astype(BF16)
    cos, slo, shi = cos_ref[...], slo_ref[...], shi_ref[...]
    scale = DA_HEAD_DIM ** -0.5 * math.log2(math.e)

    def seg(i, width=BRANCH_W):
        return jnp.dot(h, w_ref[:, i:i + width], preferred_element_type=F32)

    q = seg(0)
    k = seg(BRANCH_W)
    for hd in range(DA_HEADS):
        sl = slice(hd * HEAD_W, (hd + 1) * HEAD_W)
        qT_ref[sl, :] = (_rope(q[:, sl], cos, slo, shi) * scale).T.astype(BF16)
        k_ref[:, sl] = _rope(k[:, sl], cos, slo, shi).astype(BF16)
    vT = seg(2 * BRANCH_W).T.astype(BF16)
    ones = jnp.ones((V_ROWS - HEAD_W, vT.shape[1]), BF16)
    for hd in range(DA_HEADS):
        vT_ref[hd * V_ROWS:hd * V_ROWS + HEAD_W, :] = vT[hd * HEAD_W:(hd + 1) * HEAD_W]
        vT_ref[hd * V_ROWS + HEAD_W:(hd + 1) * V_ROWS, :] = ones
    hq = seg(3 * BRANCH_W)
    hq_ref[...] = (hq * _sigmoid(hq)).astype(BF16)
    hf_ref[...] = seg(4 * BRANCH_W)
    hi_ref[...] = seg(5 * BRANCH_W).astype(BF16)
    hg = seg(6 * BRANCH_W)
    hg_ref[...] = (hg * _sigmoid(hg)).astype(BF16)
    for j in range(2):
        ga_ref[:, j * BRANCH_W:(j + 1) * BRANCH_W] = _sigmoid(seg((7 + j) * BRANCH_W)).astype(BF16)
        gb_ref[:, j * BRANCH_W:(j + 1) * BRANCH_W] = _sigmoid(seg((9 + j) * BRANCH_W)).astype(BF16)


def _in_projection(xf, mod, norm_w, w_in_bf, cos, slo, shi, seq, tm):
    t, d = xf.shape
    tiles_per_batch = seq // tm
    bspec = lambda w: pl.BlockSpec((tm, w), lambda i: (i, 0))
    tab = pl.BlockSpec((tm, LANES), lambda i: (i % tiles_per_batch, 0))
    widths = [BRANCH_W] * 7 + [D_MODEL, D_MODEL]
    dtypes = [BF16, BF16, BF16, BF16, F32, BF16, BF16, BF16, BF16]
    out_specs = [bspec(w) for w in widths]
    out_shape = [jax.ShapeDtypeStruct((t, w), dt) for w, dt in zip(widths, dtypes)]
    for idx, rows in ((0, BRANCH_W), (2, DA_HEADS * V_ROWS)):
        out_specs[idx] = pl.BlockSpec((None, rows, tm),
                                      lambda i: (i // tiles_per_batch, 0, i % tiles_per_batch))
        out_shape[idx] = jax.ShapeDtypeStruct((t // seq, rows, seq), BF16)
    return pl.pallas_call(
        _inproj_kernel,
        grid=(t // tm,),
        in_specs=[bspec(d),
                  pl.BlockSpec((None, 6, d), lambda i: (i // tiles_per_batch, 0, 0)),
                  pl.BlockSpec((1, d), lambda i: (0, 0)),
                  pl.BlockSpec(w_in_bf.shape, lambda i: (0, 0), pipeline_mode=pl.Buffered(1)),
                  tab, tab, tab],
        out_specs=out_specs,
        out_shape=out_shape,
        compiler_params=_cparams(("parallel",)),
        name="in_projection",
    )(xf, mod, norm_w, w_in_bf, cos, slo, shi)


def _attn_kernel(lq1_ref, lk1_ref, lq2_ref, lk2_ref, subln_ref, qT_ref, k_ref, vT_ref, o_ref,
                 qq_ref, s0_ref, s1_ref, p0_ref, p1_ref, a0_ref, a1_ref, m_ref, acc_ref,
                 *, tq, tk, lambda_init):
    qi = pl.program_id(2)
    ndiag = tq // tk
    s_refs, p_refs, a_refs = (s0_ref, s1_ref), (p0_ref, p1_ref), (a0_ref, a1_ref)
    qT = qT_ref[...]
    feat = lax.broadcasted_iota(I32, qT.shape, 0)
    zero = jnp.zeros_like(qT)
    qq_ref[:, :tq] = jnp.where(feat < DA_HEAD_DIM, qT, zero)
    qq_ref[:, tq:] = jnp.where(feat >= DA_HEAD_DIM, qT, zero)

    strips = [slice(c * ATTN_STRIP, (c + 1) * ATTN_STRIP) for c in range(2 * tq // ATTN_STRIP)]

    def scores(kb, dst, cs):
        start = pl.multiple_of(kb * tk, tk)
        dst[:, cs] = jnp.dot(k_ref[pl.ds(start, tk), :], qq_ref[:, cs], preferred_element_type=F32)

    def accumulate(kb, slot, cs):
        start = pl.multiple_of(kb * tk, tk)
        pv = jnp.dot(vT_ref[:, pl.ds(start, tk)], p_refs[slot][:, cs], preferred_element_type=F32)
        acc_ref[:, cs] = a_refs[slot][:, cs] * acc_ref[:, cs] + pv

    def softmax(slot, cs, diag):
        def piece(rc):
            sc = s_refs[slot][rc * CHUNK:(rc + 1) * CHUNK, cs]
            if diag is not None:
                qchunk = (lax.broadcasted_iota(I32, (1, ATTN_STRIP), 1) + cs.start) % tq // CHUNK
                sc = jnp.where(qchunk >= diag * (tk // CHUNK) + rc, sc, NEG_BIG)
            return sc

        top = piece(0)
        for rc in range(1, tk // CHUNK):
            top = jnp.maximum(top, piece(rc))
        m_old = m_ref[:, cs]
        m_new = jnp.maximum(m_old, jnp.max(top, axis=0, keepdims=True))
        m_ref[:, cs] = m_new
        a_refs[slot][:, cs] = jnp.exp2(m_old - m_new)
        for rc in range(tk // CHUNK):
            p_refs[slot][rc * CHUNK:(rc + 1) * CHUNK, cs] = jnp.exp2((piece(rc) - m_new).astype(BF16))

    def seen(diag, cs):
        return diag is None or diag < 0 or (cs.start % tq) + ATTN_STRIP > diag * tk

    def stage(kb, slot, diag=None, last=False):
        for cs in strips:
            if seen(None if diag is None else diag - 1, cs):
                accumulate(jnp.maximum(kb - 1, 0), 1 - slot, cs)
            if seen(diag, cs):
                softmax(slot, cs, diag)
            if not last and seen(None if diag is None else diag + 1, cs):
                scores(kb + 1, s_refs[1 - slot], cs)

    p1_ref[...] = jnp.zeros_like(p1_ref)
    a1_ref[...] = jnp.ones_like(a1_ref)
    m_ref[...] = jnp.full(m_ref.shape, NEG_BIG, F32)
    acc_ref[...] = jnp.zeros_like(acc_ref)
    for cs in strips:
        scores(0, s0_ref, cs)

    def pair(i, c):
        stage(2 * i, 0)
        stage(2 * i + 1, 1)
        return c

    first_diag = qi * ndiag
    lax.fori_loop(0, first_diag // 2, pair, 0)
    for dg in range(ndiag):
        stage(first_diag + dg, dg % 2, diag=dg, last=dg == ndiag - 1)
    for cs in strips:
        if seen(ndiag - 1, cs):
            accumulate(first_diag + ndiag - 1, (ndiag - 1) % 2, cs)

    acc = acc_ref[...]
    o = acc[:HEAD_W] / acc[HEAD_W:HEAD_W + 1]
    lam = (jnp.exp(jnp.sum(lq1_ref[...] * lk1_ref[...], axis=1, keepdims=True))
           - jnp.exp(jnp.sum(lq2_ref[...] * lk2_ref[...], axis=1, keepdims=True))
           + lambda_init)
    o = o[:, :tq] - lam * o[:, tq:]
    o = o * lax.rsqrt(jnp.mean(o * o, axis=0, keepdims=True) + EPS) * subln_ref[...]
    o_ref[...] = (o * (1.0 - lambda_init)).T.astype(BF16)


def _diff_attention(qT, k, vT, lq1, lk1, lq2, lk2, subln, bsz, seq, lambda_init, tq, tk):
    assert tq % (2 * tk) == 0
    k3 = k.reshape(bsz, seq, BRANCH_W)
    small = lambda n: pl.BlockSpec((1, n), lambda b, h, i: (0, 0))
    out = pl.pallas_call(
        functools.partial(_attn_kernel, tq=tq, tk=tk, lambda_init=lambda_init),
        grid=(bsz, DA_HEADS, seq // tq),
        in_specs=[small(DA_HEAD_DIM)] * 4 + [
            pl.BlockSpec((HEAD_W, 1), lambda b, h, i: (0, 0)),
            pl.BlockSpec((None, HEAD_W, tq), lambda b, h, i: (b, h, i)),
            pl.BlockSpec((None, seq, HEAD_W), lambda b, h, i: (b, 0, h)),
            pl.BlockSpec((None, V_ROWS, seq), lambda b, h, i: (b, h, 0))],
        out_specs=pl.BlockSpec((None, tq, HEAD_W), lambda b, h, i: (b, i, h)),
        out_shape=jax.ShapeDtypeStruct((bsz, seq, BRANCH_W), BF16),
        scratch_shapes=[pltpu.VMEM((HEAD_W, 2 * tq), BF16),
                        pltpu.VMEM((tk, 2 * tq), F32), pltpu.VMEM((tk, 2 * tq), F32),
                        pltpu.VMEM((tk, 2 * tq), BF16), pltpu.VMEM((tk, 2 * tq), BF16),
                        pltpu.VMEM((1, 2 * tq), F32), pltpu.VMEM((1, 2 * tq), F32),
                        pltpu.VMEM((1, 2 * tq), F32), pltpu.VMEM((V_ROWS, 2 * tq), F32)],
        compiler_params=_cparams(("parallel", "parallel", "arbitrary")),
        name="diff_attention",
    )(lq1, lk1, lq2, lk2, subln.reshape(HEAD_W, 1), qT, k3, vT)
    return out.reshape(bsz * seq, BRANCH_W)


def _hgrn_kernel(lbl_ref, nw_ref, q_ref, f_ref, i_ref, g_ref, o_ref, st_ref, *, tc, nh):
    @pl.when(pl.program_id(2) == 0)
    def _():
        st_ref[...] = jnp.zeros_like(st_ref)

    row = lax.broadcasted_iota(I32, (CHUNK, CHUNK), 0)
    col = lax.broadcasted_iota(I32, (CHUNK, CHUNK), 1)
    causal = col <= row
    tril = jnp.where(causal, 1.0, 0.0).astype(BF16)
    nt = (((1,), (1,)), ((), ()))
    tn = (((0,), (0,)), ((), ()))

    nchunk = tc // CHUNK
    units = [(hh, c) for hh in range(nh) for c in range(nchunk)]
    rows = lambda c: slice(c * CHUNK, (c + 1) * CHUNK)
    q, k, v, parts = [], [], [], []
    for hh in range(nh):
        hs = slice(hh * HEAD_W, (hh + 1) * HEAD_W)
        lg = lbl_ref[:, hs]
        ex = jnp.exp(lg - jnp.max(lg, axis=0, keepdims=True))
        lb = ex[0:1] / jnp.sum(ex, axis=0, keepdims=True)
        f = lb + (1.0 - lb) * _sigmoid(f_ref[:, hs])
        q.append(q_ref[:, hs].astype(F32))
        k.append(1.0 - f)
        v.append(i_ref[:, hs])
        parts.append(_split3(jnp.log(f)))
    gall = sum(jnp.dot(tril, jnp.concatenate([parts[hh][p][rows(c)] for hh, c in units], axis=1),
                       preferred_element_type=F32) for p in range(3))
    gcum = [gall[:, u * HEAD_W:(u + 1) * HEAD_W] for u in range(len(units))]
    gtot = [g[CHUNK - 1:CHUNK, :] for g in gcum]
    q_t = [(q[hh][rows(c)] * jnp.exp(gcum[u])).astype(BF16) for u, (hh, c) in enumerate(units)]
    k_t = [(k[hh][rows(c)] * jnp.exp(-gcum[u])).astype(BF16) for u, (hh, c) in enumerate(units)]
    k_dec = [(k[hh][rows(c)] * jnp.exp(gtot[u] - gcum[u])).astype(BF16)
             for u, (hh, c) in enumerate(units)]
    a = [lax.dot_general(q_t[u], k_t[u], nt, preferred_element_type=F32) for u in range(len(units))]
    ds = [lax.dot_general(v[hh][rows(c)], k_dec[u], tn, preferred_element_type=F32)
          for u, (hh, c) in enumerate(units)]
    a = [jnp.where(causal, x, 0.0).astype(BF16) for x in a]
    st_in = []
    for hh in range(nh):
        st = st_ref[hh]
        for c in range(nchunk):
            u = hh * nchunk + c
            st_in.append(st.astype(BF16))
            st = jnp.exp(gtot[u]) * st + ds[u]
        st_ref[hh] = st
    outs = [jnp.dot(a[u], v[hh][rows(c)], preferred_element_type=F32)
            + lax.dot_general(q_t[u], st_in[u], nt, preferred_element_type=F32)
            for u, (hh, c) in enumerate(units)]
    for hh in range(nh):
        hs = slice(hh * HEAD_W, (hh + 1) * HEAD_W)
        o = jnp.concatenate(outs[hh * nchunk:(hh + 1) * nchunk], axis=0)
        o_ref[:, hs] = (_rms(o, nw_ref[...]) * g_ref[:, hs].astype(F32)).astype(BF16)


def _hgrn2(hq, hf, hi, hg, lb_logits, norm_w, bsz, seq, tc, nh):
    r3 = lambda a: a.reshape(bsz, seq, BRANCH_W)
    blk = pl.BlockSpec((None, tc, nh * HEAD_W), lambda b, h, i: (b, i, h))
    out = pl.pallas_call(
        functools.partial(_hgrn_kernel, tc=tc, nh=nh),
        grid=(bsz, HG_HEADS // nh, seq // tc),
        in_specs=[pl.BlockSpec((2, nh * HEAD_W), lambda b, h, i: (0, h)),
                  pl.BlockSpec((1, HEAD_W), lambda b, h, i: (0, 0)),
                  blk, blk, blk, blk],
        out_specs=blk,
        out_shape=jax.ShapeDtypeStruct((bsz, seq, BRANCH_W), BF16),
        scratch_shapes=[pltpu.VMEM((nh, HEAD_W, HEAD_W), F32)],
        compiler_params=_cparams(("parallel", "parallel", "arbitrary")),
        name="hgrn2",
    )(lb_logits, norm_w, r3(hq), r3(hf), r3(hi), r3(hg))
    return out.reshape(bsz * seq, BRANCH_W)


def _merge_kernel(x_ref, mod_ref, ya_ref, yb_ref, ga_ref, gb_ref, wa_ref, wb_ref, wo_ref,
                  npost_ref, npre_ref, rw_ref, rb_ref,
                  x1_ref, h2_ref, topi_ref, gate_ref, rank_ref, cnt_ref, lg_ref, *, tm):
    @pl.when(pl.program_id(0) == 0)
    def _():
        lg_ref[...] = jnp.zeros_like(lg_ref)

    logits = lg_ref[...]

    g1 = mod_ref[2:3, :]
    sh2 = mod_ref[3:4, :]
    sc2 = mod_ref[4:5, :]
    a = jnp.dot(ya_ref[...], wa_ref[...], preferred_element_type=F32)
    b = jnp.dot(yb_ref[...], wb_ref[...], preferred_element_type=F32)
    merged = ga_ref[...] * a.astype(BF16) + gb_ref[...] * b.astype(BF16)
    y = jnp.dot(merged, wo_ref[...], preferred_element_type=F32)
    x1 = x_ref[...] + g1 * _rms(y, npost_ref[...])
    x1_ref[...] = x1
    h2 = (_rms(x1, npre_ref[...]) * (1.0 + sc2) + sh2).astype(BF16)
    h2_ref[...] = h2
    lg_ref[...] = jnp.dot(h2, rw_ref[...], preferred_element_type=F32) + rb_ref[...]

    lane = lax.broadcasted_iota(I32, logits.shape, 1)
    work = logits
    vals, idxs = [], []
    for _ in range(TOP_K):
        mx = jnp.max(work, axis=1, keepdims=True)
        idx = jnp.min(jnp.where(work == mx, lane, LANES), axis=1, keepdims=True)
        vals.append(mx)
        idxs.append(idx)
        work = jnp.where(lane == idx, -jnp.inf, work)
    es = [jnp.exp(vk - vals[0]) for vk in vals]
    den = es[0] + es[1] + es[2] + es[3]

    onehot = jnp.zeros(logits.shape, F32)
    for idx in idxs:
        onehot = jnp.where(lane == idx, 1.0, onehot)
    row = lax.broadcasted_iota(I32, (tm, tm), 0)
    col = lax.broadcasted_iota(I32, (tm, tm), 1)
    strict = jnp.where(col < row, 1.0, 0.0).astype(BF16)
    prefix = jnp.dot(strict, onehot.astype(BF16), preferred_element_type=F32)
    cnt_ref[...] = jnp.broadcast_to(jnp.sum(onehot, axis=0, keepdims=True), cnt_ref.shape)

    topi = jnp.zeros(logits.shape, I32)
    gate = jnp.zeros(logits.shape, F32)
    rank = jnp.zeros(logits.shape, I32)
    for kk in range(TOP_K):
        rk = jnp.sum(jnp.where(lane == idxs[kk], prefix, 0.0), axis=1, keepdims=True)
        topi = jnp.where(lane == kk, idxs[kk], topi)
        gate = jnp.where(lane == kk, es[kk] / den, gate)
        rank = jnp.where(lane == kk, rk.astype(I32), rank)
    topi_ref[...] = topi
    gate_ref[...] = gate
    rank_ref[...] = rank


def _merge_route(xf, mod, ya, yb, ga, gb, wa, wb, wo, npost, npre, rw, rb, seq, tm):
    t, d = xf.shape
    n = t // tm
    tiles_per_batch = seq // tm
    proj = lambda w: pl.BlockSpec((tm, w), lambda i: (jnp.minimum(i, n - 1), 0))
    route = lambda w: pl.BlockSpec((tm, w), lambda i: (jnp.maximum(i - 1, 0), 0))
    const = lambda a: pl.BlockSpec(a.shape, lambda i: (0,) * a.ndim)
    lane_out = jax.ShapeDtypeStruct((t, LANES), I32)
    return pl.pallas_call(
        functools.partial(_merge_kernel, tm=tm),
        grid=(n + 1,),
        in_specs=[proj(d),
                  pl.BlockSpec((None, 6, d),
                               lambda i: (jnp.minimum(i, n - 1) // tiles_per_batch, 0, 0)),
                  proj(BRANCH_W), proj(BRANCH_W), proj(d), proj(d),
                  const(wa), const(wb), const(wo), const(npost), const(npre),
                  const(rw), const(rb)],
        out_specs=[proj(d), proj(d), route(LANES), route(LANES), route(LANES),
                   pl.BlockSpec((None, SUBLANES, LANES), lambda i: (jnp.maximum(i - 1, 0), 0, 0))],
        out_shape=[jax.ShapeDtypeStruct((t, d), F32), jax.ShapeDtypeStruct((t, d), BF16),
                   lane_out, jax.ShapeDtypeStruct((t, LANES), F32), lane_out,
                   jax.ShapeDtypeStruct((n, SUBLANES, LANES), F32)],
        scratch_shapes=[pltpu.VMEM((tm, LANES), F32)],
        compiler_params=_cparams(("arbitrary",)),
        name="merge_route",
    )(xf, mod, ya, yb, ga, gb, wa, wb, wo, npost, npre, rw, rb)


def _positions(topi_ref, rank_ref, srow_ref):
    topi = topi_ref[...]
    rank = rank_ref[...]
    lane = lax.broadcasted_iota(I32, topi.shape, 1)
    srow = srow_ref[0:1, :]
    pos = []
    for kk in range(TOP_K):
        base = jnp.sum(jnp.where(lane == topi[:, kk:kk + 1], srow, 0.0), axis=1, keepdims=True)
        pos.append(base.astype(I32) + rank[:, kk:kk + 1])
    return pos


def _wait_rows(n8, make_copy):
    for b in range((SORT_ROWS // RUN_ALIGN).bit_length()):
        @pl.when((n8 >> b) & 1 == 1)
        def _(b=b):
            make_copy(RUN_ALIGN << b).wait()


def _run_copies(tile, l8_ref, make_copy):
    def pieces(e, n8, bits, off):
        for b in bits:
            size = RUN_ALIGN << b
            bit = (n8 >> b) & 1

            @pl.when(bit == 1)
            def _(off=off, size=size):
                make_copy(e, off, size).start()

            off = off + bit * size

    for e in range(N_EXPERTS):
        n8 = l8_ref[tile * N_EXPERTS + e]
        big = (n8 >> RUN_SMALL_BITS) << RUN_SMALL_BITS

        @pl.when(big > 0)
        def _(e=e, n8=n8):
            pieces(e, n8, range(RUN_BITS - 1, RUN_SMALL_BITS - 1, -1), jnp.int32(0))

        pieces(e, n8, range(RUN_SMALL_BITS - 1, -1, -1), big * RUN_ALIGN)


def _dispatch_kernel(start_ref, srun_ref, l8_ref, tot8_ref, eend_ref, elen_ref,
                     h_ref, topi_ref, rank_ref, srow_ref, xs_out,
                     sel0, sel1, srt0, srt1, zbuf, zsem, sem, *, min_blocks):
    i = pl.program_id(0)
    n_tiles = pl.num_programs(0) - 1
    sels, srts = (sel0, sel1), (srt0, srt1)
    tm = h_ref.shape[0]
    half = h_ref.shape[1] // 2
    n_blocks = xs_out.shape[0] // MOE_BLOCK
    n_used = eend_ref[N_EXPERTS - 1] // MOE_BLOCK

    def zero_copy(start):
        start = pl.multiple_of(start, MOE_BLOCK)
        return pltpu.make_async_copy(zbuf, xs_out.at[pl.ds(start, MOE_BLOCK)], zsem)

    def zero_fill(wait):
        for e in range(N_EXPERTS):
            @pl.when(elen_ref[e] > 0)
            def _(e=e):
                cp = zero_copy(eend_ref[e] - MOE_BLOCK)
                cp.wait() if wait else cp.start()
        for blk in range(min_blocks, n_blocks):
            @pl.when(blk >= n_used)
            def _(blk=blk):
                cp = zero_copy(blk * MOE_BLOCK)
                cp.wait() if wait else cp.start()

    @pl.when(i == 0)
    def _():
        zbuf[...] = jnp.zeros_like(zbuf)
        sel0[...] = jnp.zeros_like(sel0)
        sel1[...] = jnp.zeros_like(sel1)
        zero_fill(wait=False)
        zero_fill(wait=True)

    def start_copies(tile, slot):
        def make_copy(e, off, size):
            src = pl.multiple_of(srun_ref[tile * N_EXPERTS + e] + off, RUN_ALIGN)
            dst = pl.multiple_of(start_ref[tile * N_EXPERTS + e] + off, RUN_ALIGN)
            return pltpu.make_async_copy(srts[slot].at[pl.ds(src, size)],
                                         xs_out.at[pl.ds(dst, size)], sem.at[slot])
        _run_copies(tile, l8_ref, make_copy)

    def wait_copies(tile, slot):
        def make_copy(size):
            return pltpu.make_async_copy(srts[slot].at[pl.ds(0, size)],
                                         xs_out.at[pl.ds(0, size)], sem.at[slot])
        _wait_rows(tot8_ref[tile], make_copy)

    for s in range(2):
        @pl.when(i % 2 == s)
        def _(s=s):
            pos = _positions(topi_ref, rank_ref, srow_ref)
            slot = lax.broadcasted_iota(I32, (tm, SORT_ROWS), 1).astype(jnp.int16)
            sel = jnp.zeros((tm, SORT_ROWS), BF16)
            for p in pos:
                sel = jnp.where(slot == p.astype(jnp.int16), jnp.ones((), BF16), sel)
            sels[s][...] = sel
            srt = lax.dot_general(sels[1 - s][...], h_ref[...], (((0,), (0,)), ((), ())),
                                  preferred_element_type=F32)
            lo = pltpu.bitcast(srt[:, :half], jnp.uint32) >> 16
            hi = pltpu.bitcast(srt[:, half:], jnp.uint32) & jnp.uint32(0xFFFF0000)
            srts[1 - s][...] = hi | lo

        @pl.when((i >= 1) & (i % 2 == s))
        def _(s=s):
            start_copies(i - 1, 1 - s)

        @pl.when((i >= 2) & (i % 2 == s))
        def _(s=s):
            wait_copies(i - 2, s)

        @pl.when((i == n_tiles) & (i % 2 == s))
        def _(s=s):
            wait_copies(i - 1, 1 - s)


def _dispatch(h2, topi, rank, srow, tables, n_rows, tm):
    t, d = h2.shape
    n = t // tm
    cur = lambda w: pl.BlockSpec((tm, w), lambda i, *_: (jnp.minimum(i, n - 1), 0))
    grid_spec = pltpu.PrefetchScalarGridSpec(
        num_scalar_prefetch=len(tables),
        grid=(n + 1,),
        in_specs=[pl.BlockSpec((tm, d), lambda i, *_: (jnp.maximum(i - 1, 0), 0)),
                  cur(LANES), cur(LANES),
                  pl.BlockSpec((None, SUBLANES, LANES), lambda i, *_: (jnp.minimum(i, n - 1), 0, 0))],
        out_specs=pl.BlockSpec(memory_space=pl.ANY),
        scratch_shapes=[pltpu.VMEM((tm, SORT_ROWS), BF16), pltpu.VMEM((tm, SORT_ROWS), BF16),
                        pltpu.VMEM((SORT_ROWS, d // 2), jnp.uint32),
                        pltpu.VMEM((SORT_ROWS, d // 2), jnp.uint32),
                        pltpu.VMEM((MOE_BLOCK, d // 2), jnp.uint32),
                        pltpu.SemaphoreType.DMA, pltpu.SemaphoreType.DMA((2,))],
    )
    return pl.pallas_call(
        functools.partial(_dispatch_kernel, min_blocks=t * TOP_K // MOE_BLOCK),
        grid_spec=grid_spec,
        out_shape=jax.ShapeDtypeStruct((n_rows, d // 2), jnp.uint32),
        compiler_params=_cparams(("arbitrary",)),
        name="moe_dispatch",
    )(*tables, h2, topi, rank, srow)


def _expert_kernel(be_ref, nu_ref, x_ref, wgu_ref, bgu_ref, wd_ref, bd_ref, y_ref,
                   wgu_bf, wd_bf):
    j = pl.program_id(0)
    half = D_MODEL // 2

    @pl.when((j == 0) | (be_ref[j] != be_ref[jnp.maximum(j - 1, 0)]))
    def _():
        wgu_bf[...] = wgu_ref[...].astype(BF16)
        wd_bf[...] = wd_ref[...].astype(BF16)

    @pl.when(j < nu_ref[0])
    def _():
        xu = x_ref[...]
        x_lo = pltpu.bitcast(xu << 16, F32).astype(BF16)
        x_hi = pltpu.bitcast(xu & jnp.uint32(0xFFFF0000), F32).astype(BF16)
        gu = (jnp.dot(x_lo, wgu_bf[:half, :], preferred_element_type=F32)
              + jnp.dot(x_hi, wgu_bf[half:, :], preferred_element_type=F32) + bgu_ref[...])
        g = jnp.minimum(gu[:, :D_MODEL], SWIGLU_LIMIT)
        u = jnp.clip(gu[:, D_MODEL:], -SWIGLU_LIMIT, SWIGLU_LIMIT)
        act = (u + 1.0) * (g * _sigmoid(g * SWIGLU_ALPHA))
        y_ref[...] = jnp.dot(act.astype(BF16), wd_bf[...], preferred_element_type=F32) + bd_ref[...]

    @pl.when(j >= nu_ref[0])
    def _():
        y_ref[...] = jnp.zeros_like(y_ref)


def _experts(xs, block_e, n_used, wgu, bgu, wd, bd):
    p = xs.shape[0]
    e, d, _ = wgu.shape
    n_blocks = p // MOE_BLOCK
    grid_spec = pltpu.PrefetchScalarGridSpec(
        num_scalar_prefetch=2,
        grid=(n_blocks,),
        in_specs=[pl.BlockSpec((MOE_BLOCK, d // 2), lambda j, be, nu: (j, 0)),
                  pl.BlockSpec((None, d, 2 * d), lambda j, be, nu: (be[j], 0, 0)),
                  pl.BlockSpec((None, 1, 2 * d), lambda j, be, nu: (be[j], 0, 0)),
                  pl.BlockSpec((None, d, d), lambda j, be, nu: (be[j], 0, 0)),
                  pl.BlockSpec((None, 1, d), lambda j, be, nu: (be[j], 0, 0))],
        out_specs=pl.BlockSpec((MOE_BLOCK, d), lambda j, be, nu: (j, 0)),
        scratch_shapes=[pltpu.VMEM((d, 2 * d), BF16), pltpu.VMEM((d, d), BF16)],
    )
    return pl.pallas_call(
        _expert_kernel,
        grid_spec=grid_spec,
        out_shape=jax.ShapeDtypeStruct((p, d), F32),
        compiler_params=_cparams(("arbitrary",)),
        name="moe_experts",
    )(block_e, n_used, xs, wgu, bgu.reshape(e, 1, 2 * d), wd, bd.reshape(e, 1, d))


def _combine_kernel(start_ref, srun_ref, l8_ref, tot8_ref,
                    y_hbm, topi_ref, rank_ref, srow_ref, gate_ref, x1_ref, mod_ref, nw_ref, o_ref,
                    sel0, sel1, yb0, yb1, sem):
    i = pl.program_id(0)
    n_tiles = pl.num_programs(0) - 1
    tm = x1_ref.shape[0]
    sels, ybufs = (sel0, sel1), (yb0, yb1)

    def start_copies(tile, slot):
        def make_copy(e, off, size):
            src = pl.multiple_of(start_ref[tile * N_EXPERTS + e] + off, RUN_ALIGN)
            dst = pl.multiple_of(srun_ref[tile * N_EXPERTS + e] + off, RUN_ALIGN)
            return pltpu.make_async_copy(y_hbm.at[pl.ds(src, size)],
                                         ybufs[slot].at[pl.ds(dst, size)], sem.at[slot])
        _run_copies(tile, l8_ref, make_copy)

    def wait_copies(tile, slot):
        def make_copy(size):
            return pltpu.make_async_copy(y_hbm.at[pl.ds(0, size)],
                                         ybufs[slot].at[pl.ds(0, size)], sem.at[slot])
        _wait_rows(tot8_ref[tile], make_copy)

    @pl.when(i == 0)
    def _():
        yb0[...] = jnp.zeros_like(yb0)
        yb1[...] = jnp.zeros_like(yb1)
        sel0[...] = jnp.zeros_like(sel0)
        sel1[...] = jnp.zeros_like(sel1)

    for s in range(2):
        @pl.when((i < n_tiles) & (i % 2 == s))
        def _(s=s):
            start_copies(i, s)

        @pl.when((i >= 1) & (i % 2 == s))
        def _(s=s):
            wait_copies(i - 1, 1 - s)

        @pl.when(i % 2 == s)
        def _(s=s):
            pos = _positions(topi_ref, rank_ref, srow_ref)
            gate = gate_ref[...].astype(BF16)
            row = lax.broadcasted_iota(I32, (tm, SORT_ROWS), 1).astype(jnp.int16)
            sel = jnp.zeros((tm, SORT_ROWS), BF16)
            for kk, p in enumerate(pos):
                sel = jnp.where(row == p.astype(jnp.int16), gate[:, kk:kk + 1], sel)
            sels[s][...] = sel
            moe = jnp.dot(sels[1 - s][...], ybufs[1 - s][...].astype(BF16),
                          preferred_element_type=F32)
            g2 = mod_ref[5:6, :]
            o_ref[...] = x1_ref[...] + g2 * _rms(moe, nw_ref[...])


def _combine(y, topi, rank, srow, gate, x1, mod, nw, tables, seq, tm):
    t, d = x1.shape
    n = t // tm
    tiles_per_batch = seq // tm
    cur = lambda w: pl.BlockSpec((tm, w), lambda i, *_: (jnp.minimum(i, n - 1), 0))
    prev = lambda w: pl.BlockSpec((tm, w), lambda i, *_: (jnp.maximum(i - 1, 0), 0))
    grid_spec = pltpu.PrefetchScalarGridSpec(
        num_scalar_prefetch=len(tables),
        grid=(n + 1,),
        in_specs=[pl.BlockSpec(memory_space=pl.ANY),
                  cur(LANES), cur(LANES),
                  pl.BlockSpec((None, SUBLANES, LANES), lambda i, *_: (jnp.minimum(i, n - 1), 0, 0)),
                  cur(LANES), prev(d),
                  pl.BlockSpec((None, 6, d),
                               lambda i, *_: (jnp.maximum(i - 1, 0) // tiles_per_batch, 0, 0)),
                  pl.BlockSpec((1, d), lambda i, *_: (0, 0))],
        out_specs=prev(d),
        scratch_shapes=[pltpu.VMEM((tm, SORT_ROWS), BF16), pltpu.VMEM((tm, SORT_ROWS), BF16),
                        pltpu.VMEM((SORT_ROWS, d), F32), pltpu.VMEM((SORT_ROWS, d), F32),
                        pltpu.SemaphoreType.DMA((2,))],
    )
    return pl.pallas_call(
        _combine_kernel,
        grid_spec=grid_spec,
        out_shape=jax.ShapeDtypeStruct((t, d), F32),
        compiler_params=_cparams(("arbitrary",)),
        name="moe_combine",
    )(*tables, y, topi, rank, srow, gate, x1, mod, nw)


def _rope_tables(seq):
    pos = jnp.arange(seq, dtype=F32)
    inv = ROPE_THETA ** (-jnp.arange(0, DA_HEAD_DIM, 2, dtype=F32) / DA_HEAD_DIM)
    ang = pos[:, None] * inv[None, :]
    cos, sin = jnp.cos(ang), jnp.sin(ang)
    zero = jnp.zeros_like(sin)
    cos_t = jnp.tile(cos, (1, LANES // cos.shape[1]))
    sin_lo = jnp.tile(jnp.concatenate([-sin, zero], axis=1), (1, 2))
    sin_hi = jnp.tile(jnp.concatenate([zero, sin], axis=1), (1, 2))
    return cos_t, sin_lo, sin_hi


def kernel(x, c, w_mod, b_mod, norm_pre_mix, norm_post_mix, w_in, da_lambda_q1, da_lambda_k1,
           da_lambda_q2, da_lambda_k2, da_subln, hg_lb_logits, hg_norm, w_branch_a, w_branch_b,
           w_out, norm_pre_ffn, norm_post_ffn, router_w, router_b, w_gate_up, b_gate_up,
           w_down, b_down):
    bsz, seq, d = x.shape
    depth = w_mod.shape[0]
    assert depth == 1 and d == D_MODEL
    t = bsz * seq
    lyr = 0
    lambda_init = 0.8 - 0.6 * math.exp(-0.3 * lyr)
    cos, slo, shi = _rope_tables(seq)

    xf = x.reshape(t, d)
    mod = _modulation(c, w_mod[lyr], b_mod[lyr]).reshape(bsz, 6, d)

    qT, k, vT, hq, hf, hi, hg, ga, gb = _in_projection(
        xf, mod, norm_pre_mix[lyr][None], w_in[lyr].astype(BF16), cos, slo, shi, seq, tm=512)
    ya = _diff_attention(qT, k, vT, da_lambda_q1[lyr][None], da_lambda_k1[lyr][None],
                         da_lambda_q2[lyr][None], da_lambda_k2[lyr][None], da_subln[lyr],
                         bsz, seq, lambda_init, tq=1024, tk=512)
    yb = _hgrn2(hq, hf, hi, hg, hg_lb_logits[lyr:lyr + 2], hg_norm[lyr][None], bsz, seq, tc=512, nh=4)

    rw = jnp.zeros((d, LANES), BF16).at[:, :N_EXPERTS].set(router_w[lyr].astype(BF16))
    rb = jnp.full((1, LANES), NEG_BIG, F32).at[0, :N_EXPERTS].set(router_b[lyr])
    x1, h2, topi, gate, rank, counts = _merge_route(
        xf, mod, ya, yb, ga, gb, w_branch_a[lyr].astype(BF16), w_branch_b[lyr].astype(BF16),
        w_out[lyr].astype(BF16), norm_post_mix[lyr][None], norm_pre_ffn[lyr][None], rw, rb,
        seq, tm=MOE_TILE)

    n_tiles = t // MOE_TILE
    up = lambda a, m: (a + m - 1) // m * m
    cnt = counts[:, 0, :N_EXPERTS].astype(I32)
    run = up(cnt, RUN_ALIGN)
    srun = jnp.cumsum(run, axis=1) - run
    e_len = up(jnp.sum(run, axis=0), MOE_BLOCK)
    e_end = jnp.cumsum(e_len)
    start = (e_end - e_len)[None, :] + jnp.cumsum(run, axis=0) - run
    n_assign = t * TOP_K
    n_rows = (up(n_assign + n_tiles * N_EXPERTS * (RUN_ALIGN - 1), MOE_BLOCK)
              + N_EXPERTS * MOE_BLOCK)
    n_blocks = n_rows // MOE_BLOCK
    block_start = jnp.arange(n_blocks, dtype=I32) * MOE_BLOCK
    block_e = jnp.minimum(jnp.sum((e_end[None, :] <= block_start[:, None]).astype(I32), axis=1),
                          N_EXPERTS - 1)
    n_used = (e_end[-1:] // MOE_BLOCK).astype(I32)
    srow = jnp.broadcast_to(jnp.pad(srun.astype(F32), ((0, 0), (0, LANES - N_EXPERTS)))[:, None, :],
                            (n_tiles, SUBLANES, LANES))
    run_tables = (start.reshape(-1), srun.reshape(-1), (run // RUN_ALIGN).reshape(-1),
                  jnp.sum(run, axis=1) // RUN_ALIGN)

    xs = _dispatch(h2, topi, rank, srow, run_tables + (e_end.astype(I32), e_len.astype(I32)),
                   n_rows, tm=MOE_TILE)
    y = _experts(xs, block_e, n_used, w_gate_up[lyr], b_gate_up[lyr], w_down[lyr], b_down[lyr])
    out = _combine(y, topi, rank, srow, gate, x1, mod, norm_post_ffn[lyr][None], run_tables,
                   seq, tm=MOE_TILE)
    return out.reshape(bsz, seq, d)
```

```python
import functools
import math

import jax
import jax.numpy as jnp
from jax import lax
from jax.experimental import pallas as pl
from jax.experimental.pallas import tpu as pltpu

F32 = jnp.float32
BF16 = jnp.bfloat16
I32 = jnp.int32

D_MODEL = 1024
CHUNK = 64
EPS = 1e-6
ROPE_THETA = 10000.0
DA_HEADS = 4
DA_HEAD_DIM = 64
HG_HEADS = 4
HEAD_W = 128
BRANCH_W = 512
V_ROWS = HEAD_W + 16
ATTN_STRIP = 256
N_EXPERTS = 32
TOP_K = 4
SWIGLU_LIMIT = 7.0
SWIGLU_ALPHA = 1.702
MOE_BLOCK = 512
MOE_TILE = 512
LANES = 128
SUBLANES = 8
RUN_ALIGN = SUBLANES
RUN_BITS = (MOE_TILE // RUN_ALIGN).bit_length()
RUN_SMALL_BITS = 4
SORT_ROWS = MOE_TILE * TOP_K + N_EXPERTS * RUN_ALIGN
NEG_BIG = -1e30

VMEM_LIMIT = 56 * 1024 * 1024


def _cparams(sem):
    return pltpu.CompilerParams(dimension_semantics=sem, vmem_limit_bytes=VMEM_LIMIT)


def _rms(x, w):
    return x * lax.rsqrt(jnp.mean(x * x, axis=-1, keepdims=True) + EPS) * w


def _sigmoid(x):
    return 1.0 / (1.0 + jnp.exp(-x))


def _split3(x):
    a = x.astype(BF16)
    r = x - a.astype(F32)
    b = r.astype(BF16)
    c = (r - b.astype(F32)).astype(BF16)
    return a, b, c


def _mod_kernel(c_ref, w_ref, b_ref, o_ref):
    c = c_ref[...]
    ca = c * _sigmoid(c)
    o_ref[...] = jnp.dot(ca, w_ref[...], preferred_element_type=F32,
                         precision=lax.Precision.HIGHEST) + b_ref[...]


def _modulation(c, w_mod, b_mod):
    bsz, d = c.shape
    n = w_mod.shape[1]
    tn = 1536
    return pl.pallas_call(
        _mod_kernel,
        grid=(n // tn,),
        in_specs=[pl.BlockSpec((bsz, d), lambda j: (0, 0)),
                  pl.BlockSpec((d, tn), lambda j: (0, j)),
                  pl.BlockSpec((1, tn), lambda j: (0, j))],
        out_specs=pl.BlockSpec((bsz, tn), lambda j: (0, j)),
        out_shape=jax.ShapeDtypeStruct((bsz, n), F32),
        compiler_params=_cparams(("arbitrary",)),
        name="modulation",
    )(c, w_mod, b_mod.reshape(1, n))


def _rope(x, cos, sin_lo, sin_hi):
    return x * cos + pltpu.roll(x, 96, 1) * sin_lo + pltpu.roll(x, 32, 1) * sin_hi


def _inproj_kernel(x_ref, mod_ref, nw_ref, w_ref, cos_ref, slo_ref, shi_ref,
                   qT_ref, k_ref, vT_ref, hq_ref, hf_ref, hi_ref, hg_ref, ga_ref, gb_ref):
    x = x_ref[...]
    sh1 = mod_ref[0:1, :]
    sc1 = mod_ref[1:2, :]
    h = (_rms(x, nw_ref[...]) * (1.0 + sc1) + sh1).astype(BF16)
    cos, slo, shi = cos_ref[...], slo_ref[...], shi_ref[...]
    scale = DA_HEAD_DIM ** -0.5 * math.log2(math.e)

    def seg(i, width=BRANCH_W):
        return jnp.dot(h, w_ref[:, i:i + width], preferred_element_type=F32)

    q = seg(0)
    k = seg(BRANCH_W)
    for hd in range(DA_HEADS):
        sl = slice(hd * HEAD_W, (hd + 1) * HEAD_W)
        qT_ref[sl, :] = (_rope(q[:, sl], cos, slo, shi) * scale).T.astype(BF16)
        k_ref[:, sl] = _rope(k[:, sl], cos, slo, shi).astype(BF16)
    vT = seg(2 * BRANCH_W).T.astype(BF16)
    ones = jnp.ones((V_ROWS - HEAD_W, vT.shape[1]), BF16)
    for hd in range(DA_HEADS):
        vT_ref[hd * V_ROWS:hd * V_ROWS + HEAD_W, :] = vT[hd * HEAD_W:(hd + 1) * HEAD_W]
        vT_ref[hd * V_ROWS + HEAD_W:(hd + 1) * V_ROWS, :] = ones
    hq = seg(3 * BRANCH_W)
    hq_ref[...] = (hq * _sigmoid(hq)).astype(BF16)
    hf_ref[...] = seg(4 * BRANCH_W)
    hi_ref[...] = seg(5 * BRANCH_W).astype(BF16)
    hg = seg(6 * BRANCH_W)
    hg_ref[...] = (hg * _sigmoid(hg)).astype(BF16)
    for j in range(2):
        ga_ref[:, j * BRANCH_W:(j + 1) * BRANCH_W] = _sigmoid(seg((7 + j) * BRANCH_W)).astype(BF16)
        gb_ref[:, j * BRANCH_W:(j + 1) * BRANCH_W] = _sigmoid(seg((9 + j) * BRANCH_W)).astype(BF16)


def _in_projection(xf, mod, norm_w, w_in_bf, cos, slo, shi, seq, tm):
    t, d = xf.shape
    tiles_per_batch = seq // tm
    bspec = lambda w: pl.BlockSpec((tm, w), lambda i: (i, 0))
    tab = pl.BlockSpec((tm, LANES), lambda i: (i % tiles_per_batch, 0))
    widths = [BRANCH_W] * 7 + [D_MODEL, D_MODEL]
    dtypes = [BF16, BF16, BF16, BF16, F32, BF16, BF16, BF16, BF16]
    out_specs = [bspec(w) for w in widths]
    out_shape = [jax.ShapeDtypeStruct((t, w), dt) for w, dt in zip(widths, dtypes)]
    for idx, rows in ((0, BRANCH_W), (2, DA_HEADS * V_ROWS)):
        out_specs[idx] = pl.BlockSpec((None, rows, tm),
                                      lambda i: (i // tiles_per_batch, 0, i % tiles_per_batch))
        out_shape[idx] = jax.ShapeDtypeStruct((t // seq, rows, seq), BF16)
    return pl.pallas_call(
        _inproj_kernel,
        grid=(t // tm,),
        in_specs=[bspec(d),
                  pl.BlockSpec((None, 6, d), lambda i: (i // tiles_per_batch, 0, 0)),
                  pl.BlockSpec((1, d), lambda i: (0, 0)),
                  pl.BlockSpec(w_in_bf.shape, lambda i: (0, 0), pipeline_mode=pl.Buffered(1)),
                  tab, tab, tab],
        out_specs=out_specs,
        out_shape=out_shape,
        compiler_params=_cparams(("parallel",)),
        name="in_projection",
    )(xf, mod, norm_w, w_in_bf, cos, slo, shi)


def _attn_kernel(lq1_ref, lk1_ref, lq2_ref, lk2_ref, subln_ref, qT_ref, k_ref, vT_ref, o_ref,
                 qq_ref, s0_ref, s1_ref, p0_ref, p1_ref, a0_ref, a1_ref, m_ref, acc_ref,
                 *, tq, tk, lambda_init):
    qi = pl.program_id(2)
    ndiag = tq // tk
    s_refs, p_refs, a_refs = (s0_ref, s1_ref), (p0_ref, p1_ref), (a0_ref, a1_ref)
    qT = qT_ref[...]
    feat = lax.broadcasted_iota(I32, qT.shape, 0)
    zero = jnp.zeros_like(qT)
    qq_ref[:, :tq] = jnp.where(feat < DA_HEAD_DIM, qT, zero)
    qq_ref[:, tq:] = jnp.where(feat >= DA_HEAD_DIM, qT, zero)

    strips = [slice(c * ATTN_STRIP, (c + 1) * ATTN_STRIP) for c in range(2 * tq // ATTN_STRIP)]

    def scores(kb, dst, cs):
        start = pl.multiple_of(kb * tk, tk)
        dst[:, cs] = jnp.dot(k_ref[pl.ds(start, tk), :], qq_ref[:, cs], preferred_element_type=F32)

    def accumulate(kb, slot, cs):
        start = pl.multiple_of(kb * tk, tk)
        pv = jnp.dot(vT_ref[:, pl.ds(start, tk)], p_refs[slot][:, cs], preferred_element_type=F32)
        acc_ref[:, cs] = a_refs[slot][:, cs] * acc_ref[:, cs] + pv

    def softmax(slot, cs, diag):
        def piece(rc):
            sc = s_refs[slot][rc * CHUNK:(rc + 1) * CHUNK, cs]
            if diag is not None:
                qchunk = (lax.broadcasted_iota(I32, (1, ATTN_STRIP), 1) + cs.start) % tq // CHUNK
                sc = jnp.where(qchunk >= diag * (tk // CHUNK) + rc, sc, NEG_BIG)
            return sc

        top = piece(0)
        for rc in range(1, tk // CHUNK):
            top = jnp.maximum(top, piece(rc))
        m_old = m_ref[:, cs]
        m_new = jnp.maximum(m_old, jnp.max(top, axis=0, keepdims=True))
        m_ref[:, cs] = m_new
        a_refs[slot][:, cs] = jnp.exp2(m_old - m_new)
        for rc in range(tk // CHUNK):
            p_refs[slot][rc * CHUNK:(rc + 1) * CHUNK, cs] = jnp.exp2((piece(rc) - m_new).astype(BF16))

    def seen(diag, cs):
        return diag is None or diag < 0 or (cs.start % tq) + ATTN_STRIP > diag * tk

    def stage(kb, slot, diag=None, last=False):
        for cs in strips:
            if seen(None if diag is None else diag - 1, cs):
                accumulate(jnp.maximum(kb - 1, 0), 1 - slot, cs)
            if seen(diag, cs):
                softmax(slot, cs, diag)
            if not last and seen(None if diag is None else diag + 1, cs):
                scores(kb + 1, s_refs[1 - slot], cs)

    p1_ref[...] = jnp.zeros_like(p1_ref)
    a1_ref[...] = jnp.ones_like(a1_ref)
    m_ref[...] = jnp.full(m_ref.shape, NEG_BIG, F32)
    acc_ref[...] = jnp.zeros_like(acc_ref)
    for cs in strips:
        scores(0, s0_ref, cs)

    def pair(i, c):
        stage(2 * i, 0)
        stage(2 * i + 1, 1)
        return c

    first_diag = qi * ndiag
    lax.fori_loop(0, first_diag // 2, pair, 0)
    for dg in range(ndiag):
        stage(first_diag + dg, dg % 2, diag=dg, last=dg == ndiag - 1)
    for cs in strips:
        if seen(ndiag - 1, cs):
            accumulate(first_diag + ndiag - 1, (ndiag - 1) % 2, cs)

    acc = acc_ref[...]
    o = acc[:HEAD_W] / acc[HEAD_W:HEAD_W + 1]
    lam = (jnp.exp(jnp.sum(lq1_ref[...] * lk1_ref[...], axis=1, keepdims=True))
           - jnp.exp(jnp.sum(lq2_ref[...] * lk2_ref[...], axis=1, keepdims=True))
           + lambda_init)
    o = o[:, :tq] - lam * o[:, tq:]
    o = o * lax.rsqrt(jnp.mean(o * o, axis=0, keepdims=True) + EPS) * subln_ref[...]
    o_ref[...] = (o * (1.0 - lambda_init)).T.astype(BF16)


def _diff_attention(qT, k, vT, lq1, lk1, lq2, lk2, subln, bsz, seq, lambda_init, tq, tk):
    assert tq % (2 * tk) == 0
    k3 = k.reshape(bsz, seq, BRANCH_W)
    small = lambda n: pl.BlockSpec((1, n), lambda b, h, i: (0, 0))
    out = pl.pallas_call(
        functools.partial(_attn_kernel, tq=tq, tk=tk, lambda_init=lambda_init),
        grid=(bsz, DA_HEADS, seq // tq),
        in_specs=[small(DA_HEAD_DIM)] * 4 + [
            pl.BlockSpec((HEAD_W, 1), lambda b, h, i: (0, 0)),
            pl.BlockSpec((None, HEAD_W, tq), lambda b, h, i: (b, h, i)),
            pl.BlockSpec((None, seq, HEAD_W), lambda b, h, i: (b, 0, h)),
            pl.BlockSpec((None, V_ROWS, seq), lambda b, h, i: (b, h, 0))],
        out_specs=pl.BlockSpec((None, tq, HEAD_W), lambda b, h, i: (b, i, h)),
        out_shape=jax.ShapeDtypeStruct((bsz, seq, BRANCH_W), BF16),
        scratch_shapes=[pltpu.VMEM((HEAD_W, 2 * tq), BF16),
                        pltpu.VMEM((tk, 2 * tq), F32), pltpu.VMEM((tk, 2 * tq), F32),
                        pltpu.VMEM((tk, 2 * tq), BF16), pltpu.VMEM((tk, 2 * tq), BF16),
                        pltpu.VMEM((1, 2 * tq), F32), pltpu.VMEM((1, 2 * tq), F32),
                        pltpu.VMEM((1, 2 * tq), F32), pltpu.VMEM((V_ROWS, 2 * tq), F32)],
        compiler_params=_cparams(("parallel", "parallel", "arbitrary")),
        name="diff_attention",
    )(lq1, lk1, lq2, lk2, subln.reshape(HEAD_W, 1), qT, k3, vT)
    return out.reshape(bsz * seq, BRANCH_W)


def _hgrn_kernel(lbl_ref, nw_ref, q_ref, f_ref, i_ref, g_ref, o_ref, st_ref, *, tc, nh):
    @pl.when(pl.program_id(2) == 0)
    def _():
        st_ref[...] = jnp.zeros_like(st_ref)

    row = lax.broadcasted_iota(I32, (CHUNK, CHUNK), 0)
    col = lax.broadcasted_iota(I32, (CHUNK, CHUNK), 1)
    causal = col <= row
    tril = jnp.where(causal, 1.0, 0.0).astype(BF16)
    nt = (((1,), (1,)), ((), ()))
    tn = (((0,), (0,)), ((), ()))

    nchunk = tc // CHUNK
    units = [(hh, c) for hh in range(nh) for c in range(nchunk)]
    rows = lambda c: slice(c * CHUNK, (c + 1) * CHUNK)
    q, k, v, parts = [], [], [], []
    for hh in range(nh):
        hs = slice(hh * HEAD_W, (hh + 1) * HEAD_W)
        lg = lbl_ref[:, hs]
        ex = jnp.exp(lg - jnp.max(lg, axis=0, keepdims=True))
        lb = ex[0:1] / jnp.sum(ex, axis=0, keepdims=True)
        f = lb + (1.0 - lb) * _sigmoid(f_ref[:, hs])
        q.append(q_ref[:, hs].astype(F32))
        k.append(1.0 - f)
        v.append(i_ref[:, hs])
        parts.append(_split3(jnp.log(f)))
    gall = sum(jnp.dot(tril, jnp.concatenate([parts[hh][p][rows(c)] for hh, c in units], axis=1),
                       preferred_element_type=F32) for p in range(3))
    gcum = [gall[:, u * HEAD_W:(u + 1) * HEAD_W] for u in range(len(units))]
    gtot = [g[CHUNK - 1:CHUNK, :] for g in gcum]
    q_t = [(q[hh][rows(c)] * jnp.exp(gcum[u])).astype(BF16) for u, (hh, c) in enumerate(units)]
    k_t = [(k[hh][rows(c)] * jnp.exp(-gcum[u])).astype(BF16) for u, (hh, c) in enumerate(units)]
    k_dec = [(k[hh][rows(c)] * jnp.exp(gtot[u] - gcum[u])).astype(BF16)
             for u, (hh, c) in enumerate(units)]
    a = [lax.dot_general(q_t[u], k_t[u], nt, preferred_element_type=F32) for u in range(len(units))]
    ds = [lax.dot_general(v[hh][rows(c)], k_dec[u], tn, preferred_element_type=F32)
          for u, (hh, c) in enumerate(units)]
    a = [jnp.where(causal, x, 0.0).astype(BF16) for x in a]
    st_in = []
    for hh in range(nh):
        st = st_ref[hh]
        for c in range(nchunk):
            u = hh * nchunk + c
            st_in.append(st.astype(BF16))
            st = jnp.exp(gtot[u]) * st + ds[u]
        st_ref[hh] = st
    outs = [jnp.dot(a[u], v[hh][rows(c)], preferred_element_type=F32)
            + lax.dot_general(q_t[u], st_in[u], nt, preferred_element_type=F32)
            for u, (hh, c) in enumerate(units)]
    for hh in range(nh):
        hs = slice(hh * HEAD_W, (hh + 1) * HEAD_W)
        o = jnp.concatenate(outs[hh * nchunk:(hh + 1) * nchunk], axis=0)
        o_ref[:, hs] = (_rms(o, nw_ref[...]) * g_ref[:, hs].astype(F32)).astype(BF16)


def _hgrn2(hq, hf, hi, hg, lb_logits, norm_w, bsz, seq, tc, nh):
    r3 = lambda a: a.reshape(bsz, seq, BRANCH_W)
    blk = pl.BlockSpec((None, tc, nh * HEAD_W), lambda b, h, i: (b, i, h))
    out = pl.pallas_call(
        functools.partial(_hgrn_kernel, tc=tc, nh=nh),
        grid=(bsz, HG_HEADS // nh, seq // tc),
        in_specs=[pl.BlockSpec((2, nh * HEAD_W), lambda b, h, i: (0, h)),
                  pl.BlockSpec((1, HEAD_W), lambda b, h, i: (0, 0)),
                  blk, blk, blk, blk],
        out_specs=blk,
        out_shape=jax.ShapeDtypeStruct((bsz, seq, BRANCH_W), BF16),
        scratch_shapes=[pltpu.VMEM((nh, HEAD_W, HEAD_W), F32)],
        compiler_params=_cparams(("parallel", "parallel", "arbitrary")),
        name="hgrn2",
    )(lb_logits, norm_w, r3(hq), r3(hf), r3(hi), r3(hg))
    return out.reshape(bsz * seq, BRANCH_W)


def _merge_kernel(x_ref, mod_ref, ya_ref, yb_ref, ga_ref, gb_ref, wa_ref, wb_ref, wo_ref,
                  npost_ref, npre_ref, rw_ref, rb_ref,
                  x1_ref, h2_ref, topi_ref, gate_ref, rank_ref, cnt_ref, lg_ref, *, tm):
    @pl.when(pl.program_id(0) == 0)
    def _():
        lg_ref[...] = jnp.zeros_like(lg_ref)

    logits = lg_ref[...]

    g1 = mod_ref[2:3, :]
    sh2 = mod_ref[3:4, :]
    sc2 = mod_ref[4:5, :]
    a = jnp.dot(ya_ref[...], wa_ref[...], preferred_element_type=F32)
    b = jnp.dot(yb_ref[...], wb_ref[...], preferred_element_type=F32)
    merged = ga_ref[...] * a.astype(BF16) + gb_ref[...] * b.astype(BF16)
    y = jnp.dot(merged, wo_ref[...], preferred_element_type=F32)
    x1 = x_ref[...] + g1 * _rms(y, npost_ref[...])
    x1_ref[...] = x1
    h2 = (_rms(x1, npre_ref[...]) * (1.0 + sc2) + sh2).astype(BF16)
    h2_ref[...] = h2
    lg_ref[...] = jnp.dot(h2, rw_ref[...], preferred_element_type=F32) + rb_ref[...]

    lane = lax.broadcasted_iota(I32, logits.shape, 1)
    work = logits
    vals, idxs = [], []
    for _ in range(TOP_K):
        mx = jnp.max(work, axis=1, keepdims=True)
        idx = jnp.min(jnp.where(work == mx, lane, LANES), axis=1, keepdims=True)
        vals.append(mx)
        idxs.append(idx)
        work = jnp.where(lane == idx, -jnp.inf, work)
    es = [jnp.exp(vk - vals[0]) for vk in vals]
    den = es[0] + es[1] + es[2] + es[3]

    onehot = jnp.zeros(logits.shape, F32)
    for idx in idxs:
        onehot = jnp.where(lane == idx, 1.0, onehot)
    row = lax.broadcasted_iota(I32, (tm, tm), 0)
    col = lax.broadcasted_iota(I32, (tm, tm), 1)
    strict = jnp.where(col < row, 1.0, 0.0).astype(BF16)
    prefix = jnp.dot(strict, onehot.astype(BF16), preferred_element_type=F32)
    cnt_ref[...] = jnp.broadcast_to(jnp.sum(onehot, axis=0, keepdims=True), cnt_ref.shape)

    topi = jnp.zeros(logits.shape, I32)
    gate = jnp.zeros(logits.shape, F32)
    rank = jnp.zeros(logits.shape, I32)
    for kk in range(TOP_K):
        rk = jnp.sum(jnp.where(lane == idxs[kk], prefix, 0.0), axis=1, keepdims=True)
        topi = jnp.where(lane == kk, idxs[kk], topi)
        gate = jnp.where(lane == kk, es[kk] / den, gate)
        rank = jnp.where(lane == kk, rk.astype(I32), rank)
    topi_ref[...] = topi
    gate_ref[...] = gate
    rank_ref[...] = rank


def _merge_route(xf, mod, ya, yb, ga, gb, wa, wb, wo, npost, npre, rw, rb, seq, tm):
    t, d = xf.shape
    n = t // tm
    tiles_per_batch = seq // tm
    proj = lambda w: pl.BlockSpec((tm, w), lambda i: (jnp.minimum(i, n - 1), 0))
    route = lambda w: pl.BlockSpec((tm, w), lambda i: (jnp.maximum(i - 1, 0), 0))
    const = lambda a: pl.BlockSpec(a.shape, lambda i: (0,) * a.ndim)
    lane_out = jax.ShapeDtypeStruct((t, LANES), I32)
    return pl.pallas_call(
        functools.partial(_merge_kernel, tm=tm),
        grid=(n + 1,),
        in_specs=[proj(d),
                  pl.BlockSpec((None, 6, d),
                               lambda i: (jnp.minimum(i, n - 1) // tiles_per_batch, 0, 0)),
                  proj(BRANCH_W), proj(BRANCH_W), proj(d), proj(d),
                  const(wa), const(wb), const(wo), const(npost), const(npre),
                  const(rw), const(rb)],
        out_specs=[proj(d), proj(d), route(LANES), route(LANES), route(LANES),
                   pl.BlockSpec((None, SUBLANES, LANES), lambda i: (jnp.maximum(i - 1, 0), 0, 0))],
        out_shape=[jax.ShapeDtypeStruct((t, d), F32), jax.ShapeDtypeStruct((t, d), BF16),
                   lane_out, jax.ShapeDtypeStruct((t, LANES), F32), lane_out,
                   jax.ShapeDtypeStruct((n, SUBLANES, LANES), F32)],
        scratch_shapes=[pltpu.VMEM((tm, LANES), F32)],
        compiler_params=_cparams(("arbitrary",)),
        name="merge_route",
    )(xf, mod, ya, yb, ga, gb, wa, wb, wo, npost, npre, rw, rb)


def _positions(topi_ref, rank_ref, srow_ref):
    topi = topi_ref[...]
    rank = rank_ref[...]
    lane = lax.broadcasted_iota(I32, topi.shape, 1)
    srow = srow_ref[0:1, :]
    pos = []
    for kk in range(TOP_K):
        base = jnp.sum(jnp.where(lane == topi[:, kk:kk + 1], srow, 0.0), axis=1, keepdims=True)
        pos.append(base.astype(I32) + rank[:, kk:kk + 1])
    return pos


def _wait_rows(n8, make_copy):
    for b in range((SORT_ROWS // RUN_ALIGN).bit_length()):
        @pl.when((n8 >> b) & 1 == 1)
        def _(b=b):
            make_copy(RUN_ALIGN << b).wait()


def _run_copies(tile, l8_ref, make_copy):
    def pieces(e, n8, bits, off):
        for b in bits:
            size = RUN_ALIGN << b
            bit = (n8 >> b) & 1

            @pl.when(bit == 1)
            def _(off=off, size=size):
                make_copy(e, off, size).start()

            off = off + bit * size

    for e in range(N_EXPERTS):
        n8 = l8_ref[tile * N_EXPERTS + e]
        big = (n8 >> RUN_SMALL_BITS) << RUN_SMALL_BITS

        @pl.when(big > 0)
        def _(e=e, n8=n8):
            pieces(e, n8, range(RUN_BITS - 1, RUN_SMALL_BITS - 1, -1), jnp.int32(0))

        pieces(e, n8, range(RUN_SMALL_BITS - 1, -1, -1), big * RUN_ALIGN)


def _dispatch_kernel(start_ref, srun_ref, l8_ref, tot8_ref, eend_ref, elen_ref,
                     h_ref, topi_ref, rank_ref, srow_ref, xs_out,
                     sel0, sel1, srt0, srt1, zbuf, zsem, sem, *, min_blocks):
    i = pl.program_id(0)
    n_tiles = pl.num_programs(0) - 1
    sels, srts = (sel0, sel1), (srt0, srt1)
    tm = h_ref.shape[0]
    half = h_ref.shape[1] // 2
    n_blocks = xs_out.shape[0] // MOE_BLOCK
    n_used = eend_ref[N_EXPERTS - 1] // MOE_BLOCK

    def zero_copy(start):
        start = pl.multiple_of(start, MOE_BLOCK)
        return pltpu.make_async_copy(zbuf, xs_out.at[pl.ds(start, MOE_BLOCK)], zsem)

    def zero_fill(wait):
        for e in range(N_EXPERTS):
            @pl.when(elen_ref[e] > 0)
            def _(e=e):
                cp = zero_copy(eend_ref[e] - MOE_BLOCK)
                cp.wait() if wait else cp.start()
        for blk in range(min_blocks, n_blocks):
            @pl.when(blk >= n_used)
            def _(blk=blk):
                cp = zero_copy(blk * MOE_BLOCK)
                cp.wait() if wait else cp.start()

    @pl.when(i == 0)
    def _():
        zbuf[...] = jnp.zeros_like(zbuf)
        sel0[...] = jnp.zeros_like(sel0)
        sel1[...] = jnp.zeros_like(sel1)
        zero_fill(wait=False)
        zero_fill(wait=True)

    def start_copies(tile, slot):
        def make_copy(e, off, size):
            src = pl.multiple_of(srun_ref[tile * N_EXPERTS + e] + off, RUN_ALIGN)
            dst = pl.multiple_of(start_ref[tile * N_EXPERTS + e] + off, RUN_ALIGN)
            return pltpu.make_async_copy(srts[slot].at[pl.ds(src, size)],
                                         xs_out.at[pl.ds(dst, size)], sem.at[slot])
        _run_copies(tile, l8_ref, make_copy)

    def wait_copies(tile, slot):
        def make_copy(size):
            return pltpu.make_async_copy(srts[slot].at[pl.ds(0, size)],
                                         xs_out.at[pl.ds(0, size)], sem.at[slot])
        _wait_rows(tot8_ref[tile], make_copy)

    for s in range(2):
        @pl.when(i % 2 == s)
        def _(s=s):
            pos = _positions(topi_ref, rank_ref, srow_ref)
            slot = lax.broadcasted_iota(I32, (tm, SORT_ROWS), 1).astype(jnp.int16)
            sel = jnp.zeros((tm, SORT_ROWS), BF16)
            for p in pos:
                sel = jnp.where(slot == p.astype(jnp.int16), jnp.ones((), BF16), sel)
            sels[s][...] = sel
            srt = lax.dot_general(sels[1 - s][...], h_ref[...], (((0,), (0,)), ((), ())),
                                  preferred_element_type=F32)
            lo = pltpu.bitcast(srt[:, :half], jnp.uint32) >> 16
            hi = pltpu.bitcast(srt[:, half:], jnp.uint32) & jnp.uint32(0xFFFF0000)
            srts[1 - s][...] = hi | lo

        @pl.when((i >= 1) & (i % 2 == s))
        def _(s=s):
            start_copies(i - 1, 1 - s)

        @pl.when((i >= 2) & (i % 2 == s))
        def _(s=s):
            wait_copies(i - 2, s)

        @pl.when((i == n_tiles) & (i % 2 == s))
        def _(s=s):
            wait_copies(i - 1, 1 - s)


def _dispatch(h2, topi, rank, srow, tables, n_rows, tm):
    t, d = h2.shape
    n = t // tm
    cur = lambda w: pl.BlockSpec((tm, w), lambda i, *_: (jnp.minimum(i, n - 1), 0))
    grid_spec = pltpu.PrefetchScalarGridSpec(
        num_scalar_prefetch=len(tables),
        grid=(n + 1,),
        in_specs=[pl.BlockSpec((tm, d), lambda i, *_: (jnp.maximum(i - 1, 0), 0)),
                  cur(LANES), cur(LANES),
                  pl.BlockSpec((None, SUBLANES, LANES), lambda i, *_: (jnp.minimum(i, n - 1), 0, 0))],
        out_specs=pl.BlockSpec(memory_space=pl.ANY),
        scratch_shapes=[pltpu.VMEM((tm, SORT_ROWS), BF16), pltpu.VMEM((tm, SORT_ROWS), BF16),
                        pltpu.VMEM((SORT_ROWS, d // 2), jnp.uint32),
                        pltpu.VMEM((SORT_ROWS, d // 2), jnp.uint32),
                        pltpu.VMEM((MOE_BLOCK, d // 2), jnp.uint32),
                        pltpu.SemaphoreType.DMA, pltpu.SemaphoreType.DMA((2,))],
    )
    return pl.pallas_call(
        functools.partial(_dispatch_kernel, min_blocks=t * TOP_K // MOE_BLOCK),
        grid_spec=grid_spec,
        out_shape=jax.ShapeDtypeStruct((n_rows, d // 2), jnp.uint32),
        compiler_params=_cparams(("arbitrary",)),
        name="moe_dispatch",
    )(*tables, h2, topi, rank, srow)


def _expert_kernel(be_ref, nu_ref, x_ref, wgu_ref, bgu_ref, wd_ref, bd_ref, y_ref,
                   wgu_bf, wd_bf):
    j = pl.program_id(0)
    half = D_MODEL // 2

    @pl.when((j == 0) | (be_ref[j] != be_ref[jnp.maximum(j - 1, 0)]))
    def _():
        wgu_bf[...] = wgu_ref[...].astype(BF16)
        wd_bf[...] = wd_ref[...].astype(BF16)

    @pl.when(j < nu_ref[0])
    def _():
        xu = x_ref[...]
        x_lo = pltpu.bitcast(xu << 16, F32).astype(BF16)
        x_hi = pltpu.bitcast(xu & jnp.uint32(0xFFFF0000), F32).astype(BF16)
        gu = (jnp.dot(x_lo, wgu_bf[:half, :], preferred_element_type=F32)
              + jnp.dot(x_hi, wgu_bf[half:, :], preferred_element_type=F32) + bgu_ref[...])
        g = jnp.minimum(gu[:, :D_MODEL], SWIGLU_LIMIT)
        u = jnp.clip(gu[:, D_MODEL:], -SWIGLU_LIMIT, SWIGLU_LIMIT)
        act = (u + 1.0) * (g * _sigmoid(g * SWIGLU_ALPHA))
        y_ref[...] = jnp.dot(act.astype(BF16), wd_bf[...], preferred_element_type=F32) + bd_ref[...]

    @pl.when(j >= nu_ref[0])
    def _():
        y_ref[...] = jnp.zeros_like(y_ref)


def _experts(xs, block_e, n_used, wgu, bgu, wd, bd):
    p = xs.shape[0]
    e, d, _ = wgu.shape
    n_blocks = p // MOE_BLOCK
    grid_spec = pltpu.PrefetchScalarGridSpec(
        num_scalar_prefetch=2,
        grid=(n_blocks,),
        in_specs=[pl.BlockSpec((MOE_BLOCK, d // 2), lambda j, be, nu: (j, 0)),
                  pl.BlockSpec((None, d, 2 * d), lambda j, be, nu: (be[j], 0, 0)),
                  pl.BlockSpec((None, 1, 2 * d), lambda j, be, nu: (be[j], 0, 0)),
                  pl.BlockSpec((None, d, d), lambda j, be, nu: (be[j], 0, 0)),
                  pl.BlockSpec((None, 1, d), lambda j, be, nu: (be[j], 0, 0))],
        out_specs=pl.BlockSpec((MOE_BLOCK, d), lambda j, be, nu: (j, 0)),
        scratch_shapes=[pltpu.VMEM((d, 2 * d), BF16), pltpu.VMEM((d, d), BF16)],
    )
    return pl.pallas_call(
        _expert_kernel,
        grid_spec=grid_spec,
        out_shape=jax.ShapeDtypeStruct((p, d), F32),
        compiler_params=_cparams(("arbitrary",)),
        name="moe_experts",
    )(block_e, n_used, xs, wgu, bgu.reshape(e, 1, 2 * d), wd, bd.reshape(e, 1, d))


def _combine_kernel(start_ref, srun_ref, l8_ref, tot8_ref,
                    y_hbm, topi_ref, rank_ref, srow_ref, gate_ref, x1_ref, mod_ref, nw_ref, o_ref,
                    sel0, sel1, yb0, yb1, sem):
    i = pl.program_id(0)
    n_tiles = pl.num_programs(0) - 1
    tm = x1_ref.shape[0]
    sels, ybufs = (sel0, sel1), (yb0, yb1)

    def start_copies(tile, slot):
        def make_copy(e, off, size):
            src = pl.multiple_of(start_ref[tile * N_EXPERTS + e] + off, RUN_ALIGN)
            dst = pl.multiple_of(srun_ref[tile * N_EXPERTS + e] + off, RUN_ALIGN)
            return pltpu.make_async_copy(y_hbm.at[pl.ds(src, size)],
                                         ybufs[slot].at[pl.ds(dst, size)], sem.at[slot])
        _run_copies(tile, l8_ref, make_copy)

    def wait_copies(tile, slot):
        def make_copy(size):
            return pltpu.make_async_copy(y_hbm.at[pl.ds(0, size)],
                                         ybufs[slot].at[pl.ds(0, size)], sem.at[slot])
        _wait_rows(tot8_ref[tile], make_copy)

    @pl.when(i == 0)
    def _():
        yb0[...] = jnp.zeros_like(yb0)
        yb1[...] = jnp.zeros_like(yb1)
        sel0[...] = jnp.zeros_like(sel0)
        sel1[...] = jnp.zeros_like(sel1)

    for s in range(2):
        @pl.when((i < n_tiles) & (i % 2 == s))
        def _(s=s):
            start_copies(i, s)

        @pl.when((i >= 1) & (i % 2 == s))
        def _(s=s):
            wait_copies(i - 1, 1 - s)

        @pl.when(i % 2 == s)
        def _(s=s):
            pos = _positions(topi_ref, rank_ref, srow_ref)
            gate = gate_ref[...].astype(BF16)
            row = lax.broadcasted_iota(I32, (tm, SORT_ROWS), 1).astype(jnp.int16)
            sel = jnp.zeros((tm, SORT_ROWS), BF16)
            for kk, p in enumerate(pos):
                sel = jnp.where(row == p.astype(jnp.int16), gate[:, kk:kk + 1], sel)
            sels[s][...] = sel
            moe = jnp.dot(sels[1 - s][...], ybufs[1 - s][...].astype(BF16),
                          preferred_element_type=F32)
            g2 = mod_ref[5:6, :]
            o_ref[...] = x1_ref[...] + g2 * _rms(moe, nw_ref[...])


def _combine(y, topi, rank, srow, gate, x1, mod, nw, tables, seq, tm):
    t, d = x1.shape
    n = t // tm
    tiles_per_batch = seq // tm
    cur = lambda w: pl.BlockSpec((tm, w), lambda i, *_: (jnp.minimum(i, n - 1), 0))
    prev = lambda w: pl.BlockSpec((tm, w), lambda i, *_: (jnp.maximum(i - 1, 0), 0))
    grid_spec = pltpu.PrefetchScalarGridSpec(
        num_scalar_prefetch=len(tables),
        grid=(n + 1,),
        in_specs=[pl.BlockSpec(memory_space=pl.ANY),
                  cur(LANES), cur(LANES),
                  pl.BlockSpec((None, SUBLANES, LANES), lambda i, *_: (jnp.minimum(i, n - 1), 0, 0)),
                  cur(LANES), prev(d),
                  pl.BlockSpec((None, 6, d),
                               lambda i, *_: (jnp.maximum(i - 1, 0) // tiles_per_batch, 0, 0)),
                  pl.BlockSpec((1, d), lambda i, *_: (0, 0))],
        out_specs=prev(d),
        scratch_shapes=[pltpu.VMEM((tm, SORT_ROWS), BF16), pltpu.VMEM((tm, SORT_ROWS), BF16),
                        pltpu.VMEM((SORT_ROWS, d), F32), pltpu.VMEM((SORT_ROWS, d), F32),
                        pltpu.SemaphoreType.DMA((2,))],
    )
    return pl.pallas_call(
        _combine_kernel,
        grid_spec=grid_spec,
        out_shape=jax.ShapeDtypeStruct((t, d), F32),
        compiler_params=_cparams(("arbitrary",)),
        name="moe_combine",
    )(*tables, y, topi, rank, srow, gate, x1, mod, nw)


def _rope_tables(seq):
    pos = jnp.arange(seq, dtype=F32)
    inv = ROPE_THETA ** (-jnp.arange(0, DA_HEAD_DIM, 2, dtype=F32) / DA_HEAD_DIM)
    ang = pos[:, None] * inv[None, :]
    cos, sin = jnp.cos(ang), jnp.sin(ang)
    zero = jnp.zeros_like(sin)
    cos_t = jnp.tile(cos, (1, LANES // cos.shape[1]))
    sin_lo = jnp.tile(jnp.concatenate([-sin, zero], axis=1), (1, 2))
    sin_hi = jnp.tile(jnp.concatenate([zero, sin], axis=1), (1, 2))
    return cos_t, sin_lo, sin_hi


def kernel(x, c, w_mod, b_mod, norm_pre_mix, norm_post_mix, w_in, da_lambda_q1, da_lambda_k1,
           da_lambda_q2, da_lambda_k2, da_subln, hg_lb_logits, hg_norm, w_branch_a, w_branch_b,
           w_out, norm_pre_ffn, norm_post_ffn, router_w, router_b, w_gate_up, b_gate_up,
           w_down, b_down):
    bsz, seq, d = x.shape
    depth = w_mod.shape[0]
    assert depth == 1 and d == D_MODEL
    t = bsz * seq
    lyr = 0
    lambda_init = 0.8 - 0.6 * math.exp(-0.3 * lyr)
    cos, slo, shi = _rope_tables(seq)

    xf = x.reshape(t, d)
    mod = _modulation(c, w_mod[lyr], b_mod[lyr]).reshape(bsz, 6, d)

    qT, k, vT, hq, hf, hi, hg, ga, gb = _in_projection(
        xf, mod, norm_pre_mix[lyr][None], w_in[lyr].astype(BF16), cos, slo, shi, seq, tm=512)
    ya = _diff_attention(qT, k, vT, da_lambda_q1[lyr][None], da_lambda_k1[lyr][None],
                         da_lambda_q2[lyr][None], da_lambda_k2[lyr][None], da_subln[lyr],
                         bsz, seq, lambda_init, tq=2048, tk=512)
    yb = _hgrn2(hq, hf, hi, hg, hg_lb_logits[lyr:lyr + 2], hg_norm[lyr][None], bsz, seq, tc=512, nh=4)

    rw = jnp.zeros((d, LANES), BF16).at[:, :N_EXPERTS].set(router_w[lyr].astype(BF16))
    rb = jnp.full((1, LANES), NEG_BIG, F32).at[0, :N_EXPERTS].set(router_b[lyr])
    x1, h2, topi, gate, rank, counts = _merge_route(
        xf, mod, ya, yb, ga, gb, w_branch_a[lyr].astype(BF16), w_branch_b[lyr].astype(BF16),
        w_out[lyr].astype(BF16), norm_post_mix[lyr][None], norm_pre_ffn[lyr][None], rw, rb,
        seq, tm=MOE_TILE)

    n_tiles = t // MOE_TILE
    up = lambda a, m: (a + m - 1) // m * m
    cnt = counts[:, 0, :N_EXPERTS].astype(I32)
    run = up(cnt, RUN_ALIGN)
    srun = jnp.cumsum(run, axis=1) - run
    e_len = up(jnp.sum(run, axis=0), MOE_BLOCK)
    e_end = jnp.cumsum(e_len)
    start = (e_end - e_len)[None, :] + jnp.cumsum(run, axis=0) - run
    n_assign = t * TOP_K
    n_rows = (up(n_assign + n_tiles * N_EXPERTS * (RUN_ALIGN - 1), MOE_BLOCK)
              + N_EXPERTS * MOE_BLOCK)
    n_blocks = n_rows // MOE_BLOCK
    block_start = jnp.arange(n_blocks, dtype=I32) * MOE_BLOCK
    block_e = jnp.minimum(jnp.sum((e_end[None, :] <= block_start[:, None]).astype(I32), axis=1),
                          N_EXPERTS - 1)
    n_used = (e_end[-1:] // MOE_BLOCK).astype(I32)
    srow = jnp.broadcast_to(jnp.pad(srun.astype(F32), ((0, 0), (0, LANES - N_EXPERTS)))[:, None, :],
                            (n_tiles, SUBLANES, LANES))
    run_tables = (start.reshape(-1), srun.reshape(-1), (run // RUN_ALIGN).reshape(-1),
                  jnp.sum(run, axis=1) // RUN_ALIGN)

    xs = _dispatch(h2, topi, rank, srow, run_tables + (e_end.astype(I32), e_len.astype(I32)),
                   n_rows, tm=MOE_TILE)
    y = _experts(xs, block_e, n_used, w_gate_up[lyr], b_gate_up[lyr], w_down[lyr], b_down[lyr])
    out = _combine(y, topi, rank, srow, gate, x1, mod, norm_post_ffn[lyr][None], run_tables,
                   seq, tm=MOE_TILE)
    return out.reshape(bsz, seq, d)
```

```python
import functools
import math

import jax
import jax.numpy as jnp
from jax import lax
from jax.experimental import pallas as pl
from jax.experimental.pallas import tpu as pltpu

F32 = jnp.float32
BF16 = jnp.bfloat16
I32 = jnp.int32

D_MODEL = 1024
CHUNK = 64
EPS = 1e-6
ROPE_THETA = 10000.0
DA_HEADS = 4
DA_HEAD_DIM = 64
HG_HEADS = 4
HEAD_W = 128
BRANCH_W = 512
V_ROWS = HEAD_W + 16
ATTN_STRIP = 256
N_EXPERTS = 32
TOP_K = 4
SWIGLU_LIMIT = 7.0
SWIGLU_ALPHA = 1.702
MOE_BLOCK = 512
MOE_TILE = 512
LANES = 128
SUBLANES = 8
RUN_ALIGN = SUBLANES
RUN_BITS = (MOE_TILE // RUN_ALIGN).bit_length()
RUN_SMALL_BITS = 4
SORT_ROWS = MOE_TILE * TOP_K + N_EXPERTS * RUN_ALIGN
NEG_BIG = -1e30

VMEM_LIMIT = 56 * 1024 * 1024


def _cparams(sem):
    return pltpu.CompilerParams(dimension_semantics=sem, vmem_limit_bytes=VMEM_LIMIT)


def _rms(x, w):
    return x * lax.rsqrt(jnp.mean(x * x, axis=-1, keepdims=True) + EPS) * w


def _sigmoid(x):
    return 1.0 / (1.0 + jnp.exp(-x))


def _split3(x):
    a = x.astype(BF16)
    r = x - a.astype(F32)
    b = r.astype(BF16)
    c = (r - b.astype(F32)).astype(BF16)
    return a, b, c


def _mod_kernel(c_ref, w_ref, b_ref, o_ref):
    c = c_ref[...]
    ca = c * _sigmoid(c)
    o_ref[...] = jnp.dot(ca, w_ref[...], preferred_element_type=F32,
                         precision=lax.Precision.HIGHEST) + b_ref[...]


def _modulation(c, w_mod, b_mod):
    bsz, d = c.shape
    n = w_mod.shape[1]
    tn = 1536
    return pl.pallas_call(
        _mod_kernel,
        grid=(n // tn,),
        in_specs=[pl.BlockSpec((bsz, d), lambda j: (0, 0)),
                  pl.BlockSpec((d, tn), lambda j: (0, j)),
                  pl.BlockSpec((1, tn), lambda j: (0, j))],
        out_specs=pl.BlockSpec((bsz, tn), lambda j: (0, j)),
        out_shape=jax.ShapeDtypeStruct((bsz, n), F32),
        compiler_params=_cparams(("arbitrary",)),
        name="modulation",
    )(c, w_mod, b_mod.reshape(1, n))


def _rope(x, cos, sin_lo, sin_hi):
    return x * cos + pltpu.roll(x, 96, 1) * sin_lo + pltpu.roll(x, 32, 1) * sin_hi


def _inproj_kernel(x_ref, mod_ref, nw_ref, w_ref, cos_ref, slo_ref, shi_ref,
                   qT_ref, k_ref, vT_ref, hq_ref, hf_ref, hi_ref, hg_ref, ga_ref, gb_ref):
    x = x_ref[...]
    sh1 = mod_ref[0:1, :]
    sc1 = mod_ref[1:2, :]
    h = (_rms(x, nw_ref[...]) * (1.0 + sc1) + sh1).astype(BF16)
    cos, slo, shi = cos_ref[...], slo_ref[...], shi_ref[...]
    scale = DA_HEAD_DIM ** -0.5 * math.log2(math.e)

    def seg(i, width=BRANCH_W):
        return jnp.dot(h, w_ref[:, i:i + width], preferred_element_type=F32)

    q = seg(0)
    k = seg(BRANCH_W)
    for hd in range(DA_HEADS):
        sl = slice(hd * HEAD_W, (hd + 1) * HEAD_W)
        qT_ref[sl, :] = (_rope(q[:, sl], cos, slo, shi) * scale).T.astype(BF16)
        k_ref[:, sl] = _rope(k[:, sl], cos, slo, shi).astype(BF16)
    vT = seg(2 * BRANCH_W).T.astype(BF16)
    ones = jnp.ones((V_ROWS - HEAD_W, vT.shape[1]), BF16)
    for hd in range(DA_HEADS):
        vT_ref[hd * V_ROWS:hd * V_ROWS + HEAD_W, :] = vT[hd * HEAD_W:(hd + 1) * HEAD_W]
        vT_ref[hd * V_ROWS + HEAD_W:(hd + 1) * V_ROWS, :] = ones
    hq = seg(3 * BRANCH_W)
    hq_ref[...] = (hq * _sigmoid(hq)).astype(BF16)
    hf_ref[...] = seg(4 * BRANCH_W)
    hi_ref[...] = seg(5 * BRANCH_W).astype(BF16)
    hg = seg(6 * BRANCH_W)
    hg_ref[...] = (hg * _sigmoid(hg)).astype(BF16)
    for j in range(2):
        ga_ref[:, j * BRANCH_W:(j + 1) * BRANCH_W] = _sigmoid(seg((7 + j) * BRANCH_W)).astype(BF16)
        gb_ref[:, j * BRANCH_W:(j + 1) * BRANCH_W] = _sigmoid(seg((9 + j) * BRANCH_W)).astype(BF16)


def _in_projection(xf, mod, norm_w, w_in_bf, cos, slo, shi, seq, tm):
    t, d = xf.shape
    tiles_per_batch = seq // tm
    bspec = lambda w: pl.BlockSpec((tm, w), lambda i: (i, 0))
    tab = pl.BlockSpec((tm, LANES), lambda i: (i % tiles_per_batch, 0))
    widths = [BRANCH_W] * 7 + [D_MODEL, D_MODEL]
    dtypes = [BF16, BF16, BF16, BF16, F32, BF16, BF16, BF16, BF16]
    out_specs = [bspec(w) for w in widths]
    out_shape = [jax.ShapeDtypeStruct((t, w), dt) for w, dt in zip(widths, dtypes)]
    for idx, rows in ((0, BRANCH_W), (2, DA_HEADS * V_ROWS)):
        out_specs[idx] = pl.BlockSpec((None, rows, tm),
                                      lambda i: (i // tiles_per_batch, 0, i % tiles_per_batch))
        out_shape[idx] = jax.ShapeDtypeStruct((t // seq, rows, seq), BF16)
    return pl.pallas_call(
        _inproj_kernel,
        grid=(t // tm,),
        in_specs=[bspec(d),
                  pl.BlockSpec((None, 6, d), lambda i: (i // tiles_per_batch, 0, 0)),
                  pl.BlockSpec((1, d), lambda i: (0, 0)),
                  pl.BlockSpec(w_in_bf.shape, lambda i: (0, 0), pipeline_mode=pl.Buffered(1)),
                  tab, tab, tab],
        out_specs=out_specs,
        out_shape=out_shape,
        compiler_params=_cparams(("parallel",)),
        name="in_projection",
    )(xf, mod, norm_w, w_in_bf, cos, slo, shi)


def _attn_kernel(lq1_ref, lk1_ref, lq2_ref, lk2_ref, subln_ref, qT_ref, k_ref, vT_ref, o_ref,
                 qq_ref, s0_ref, s1_ref, p0_ref, p1_ref, a0_ref, a1_ref, m_ref, acc_ref,
                 *, tq, tk, lambda_init):
    qi = pl.program_id(2)
    ndiag = tq // tk
    s_refs, p_refs, a_refs = (s0_ref, s1_ref), (p0_ref, p1_ref), (a0_ref, a1_ref)
    qT = qT_ref[...]
    feat = lax.broadcasted_iota(I32, qT.shape, 0)
    zero = jnp.zeros_like(qT)
    qq_ref[:, :tq] = jnp.where(feat < DA_HEAD_DIM, qT, zero)
    qq_ref[:, tq:] = jnp.where(feat >= DA_HEAD_DIM, qT, zero)

    strips = [slice(c * ATTN_STRIP, (c + 1) * ATTN_STRIP) for c in range(2 * tq // ATTN_STRIP)]

    def scores(kb, dst, cs):
        start = pl.multiple_of(kb * tk, tk)
        dst[:, cs] = jnp.dot(k_ref[pl.ds(start, tk), :], qq_ref[:, cs], preferred_element_type=F32)

    def accumulate(kb, slot, cs):
        start = pl.multiple_of(kb * tk, tk)
        pv = jnp.dot(vT_ref[:, pl.ds(start, tk)], p_refs[slot][:, cs], preferred_element_type=F32)
        acc_ref[:, cs] = a_refs[slot][:, cs] * acc_ref[:, cs] + pv

    def softmax(slot, cs, diag):
        def piece(rc):
            sc = s_refs[slot][rc * CHUNK:(rc + 1) * CHUNK, cs]
            if diag is not None:
                qchunk = (lax.broadcasted_iota(I32, (1, ATTN_STRIP), 1) + cs.start) % tq // CHUNK
                sc = jnp.where(qchunk >= diag * (tk // CHUNK) + rc, sc, NEG_BIG)
            return sc

        top = piece(0)
        for rc in range(1, tk // CHUNK):
            top = jnp.maximum(top, piece(rc))
        m_old = m_ref[:, cs]
        m_new = jnp.maximum(m_old, jnp.max(top, axis=0, keepdims=True))
        m_ref[:, cs] = m_new
        a_refs[slot][:, cs] = jnp.exp2(m_old - m_new)
        for rc in range(tk // CHUNK):
            p_refs[slot][rc * CHUNK:(rc + 1) * CHUNK, cs] = jnp.exp2((piece(rc) - m_new).astype(BF16))

    def seen(diag, cs):
        return diag is None or diag < 0 or (cs.start % tq) + ATTN_STRIP > diag * tk

    def stage(kb, slot, diag=None, last=False):
        for cs in strips:
            if seen(None if diag is None else diag - 1, cs):
                accumulate(jnp.maximum(kb - 1, 0), 1 - slot, cs)
            if seen(diag, cs):
                softmax(slot, cs, diag)
            if not last and seen(None if diag is None else diag + 1, cs):
                scores(kb + 1, s_refs[1 - slot], cs)

    p1_ref[...] = jnp.zeros_like(p1_ref)
    a1_ref[...] = jnp.ones_like(a1_ref)
    m_ref[...] = jnp.full(m_ref.shape, NEG_BIG, F32)
    acc_ref[...] = jnp.zeros_like(acc_ref)
    for cs in strips:
        scores(0, s0_ref, cs)

    def pair(i, c):
        stage(2 * i, 0)
        stage(2 * i + 1, 1)
        return c

    first_diag = qi * ndiag
    lax.fori_loop(0, first_diag // 2, pair, 0)
    for dg in range(ndiag):
        stage(first_diag + dg, dg % 2, diag=dg, last=dg == ndiag - 1)
    for cs in strips:
        if seen(ndiag - 1, cs):
            accumulate(first_diag + ndiag - 1, (ndiag - 1) % 2, cs)

    acc = acc_ref[...]
    o = acc[:HEAD_W] / acc[HEAD_W:HEAD_W + 1]
    lam = (jnp.exp(jnp.sum(lq1_ref[...] * lk1_ref[...], axis=1, keepdims=True))
           - jnp.exp(jnp.sum(lq2_ref[...] * lk2_ref[...], axis=1, keepdims=True))
           + lambda_init)
    o = o[:, :tq] - lam * o[:, tq:]
    o = o * lax.rsqrt(jnp.mean(o * o, axis=0, keepdims=True) + EPS) * subln_ref[...]
    o_ref[...] = (o * (1.0 - lambda_init)).T.astype(BF16)


def _diff_attention(qT, k, vT, lq1, lk1, lq2, lk2, subln, bsz, seq, lambda_init, tq, tk):
    assert tq % (2 * tk) == 0
    k3 = k.reshape(bsz, seq, BRANCH_W)
    small = lambda n: pl.BlockSpec((1, n), lambda b, h, i: (0, 0))
    out = pl.pallas_call(
        functools.partial(_attn_kernel, tq=tq, tk=tk, lambda_init=lambda_init),
        grid=(bsz, DA_HEADS, seq // tq),
        in_specs=[small(DA_HEAD_DIM)] * 4 + [
            pl.BlockSpec((HEAD_W, 1), lambda b, h, i: (0, 0)),
            pl.BlockSpec((None, HEAD_W, tq), lambda b, h, i: (b, h, i)),
            pl.BlockSpec((None, seq, HEAD_W), lambda b, h, i: (b, 0, h)),
            pl.BlockSpec((None, V_ROWS, seq), lambda b, h, i: (b, h, 0))],
        out_specs=pl.BlockSpec((None, tq, HEAD_W), lambda b, h, i: (b, i, h)),
        out_shape=jax.ShapeDtypeStruct((bsz, seq, BRANCH_W), BF16),
        scratch_shapes=[pltpu.VMEM((HEAD_W, 2 * tq), BF16),
                        pltpu.VMEM((tk, 2 * tq), F32), pltpu.VMEM((tk, 2 * tq), F32),
                        pltpu.VMEM((tk, 2 * tq), BF16), pltpu.VMEM((tk, 2 * tq), BF16),
                        pltpu.VMEM((1, 2 * tq), F32), pltpu.VMEM((1, 2 * tq), F32),
                        pltpu.VMEM((1, 2 * tq), F32), pltpu.VMEM((V_ROWS, 2 * tq), F32)],
        compiler_params=_cparams(("parallel", "parallel", "arbitrary")),
        name="diff_attention",
    )(lq1, lk1, lq2, lk2, subln.reshape(HEAD_W, 1), qT, k3, vT)
    return out.reshape(bsz * seq, BRANCH_W)


def _hgrn_kernel(lbl_ref, nw_ref, q_ref, f_ref, i_ref, g_ref, o_ref, st_ref, *, tc, nh):
    @pl.when(pl.program_id(2) == 0)
    def _():
        st_ref[...] = jnp.zeros_like(st_ref)

    row = lax.broadcasted_iota(I32, (CHUNK, CHUNK), 0)
    col = lax.broadcasted_iota(I32, (CHUNK, CHUNK), 1)
    causal = col <= row
    tril = jnp.where(causal, 1.0, 0.0).astype(BF16)
    nt = (((1,), (1,)), ((), ()))
    tn = (((0,), (0,)), ((), ()))

    nchunk = tc // CHUNK
    units = [(hh, c) for hh in range(nh) for c in range(nchunk)]
    rows = lambda c: slice(c * CHUNK, (c + 1) * CHUNK)
    q, k, v, parts = [], [], [], []
    for hh in range(nh):
        hs = slice(hh * HEAD_W, (hh + 1) * HEAD_W)
        lg = lbl_ref[:, hs]
        ex = jnp.exp(lg - jnp.max(lg, axis=0, keepdims=True))
        lb = ex[0:1] / jnp.sum(ex, axis=0, keepdims=True)
        f = lb + (1.0 - lb) * _sigmoid(f_ref[:, hs])
        q.append(q_ref[:, hs].astype(F32))
        k.append(1.0 - f)
        v.append(i_ref[:, hs])
        parts.append(_split3(jnp.log(f)))
    gall = sum(jnp.dot(tril, jnp.concatenate([parts[hh][p][rows(c)] for hh, c in units], axis=1),
                       preferred_element_type=F32) for p in range(3))
    gcum = [gall[:, u * HEAD_W:(u + 1) * HEAD_W] for u in range(len(units))]
    gtot = [g[CHUNK - 1:CHUNK, :] for g in gcum]
    q_t = [(q[hh][rows(c)] * jnp.exp(gcum[u])).astype(BF16) for u, (hh, c) in enumerate(units)]
    k_t = [(k[hh][rows(c)] * jnp.exp(-gcum[u])).astype(BF16) for u, (hh, c) in enumerate(units)]
    k_dec = [(k[hh][rows(c)] * jnp.exp(gtot[u] - gcum[u])).astype(BF16)
             for u, (hh, c) in enumerate(units)]
    a = [lax.dot_general(q_t[u], k_t[u], nt, preferred_element_type=F32) for u in range(len(units))]
    ds = [lax.dot_general(v[hh][rows(c)], k_dec[u], tn, preferred_element_type=F32)
          for u, (hh, c) in enumerate(units)]
    a = [jnp.where(causal, x, 0.0).astype(BF16) for x in a]
    st_in = []
    for hh in range(nh):
        st = st_ref[hh]
        for c in range(nchunk):
            u = hh * nchunk + c
            st_in.append(st.astype(BF16))
            st = jnp.exp(gtot[u]) * st + ds[u]
        st_ref[hh] = st
    outs = [jnp.dot(a[u], v[hh][rows(c)], preferred_element_type=F32)
            + lax.dot_general(q_t[u], st_in[u], nt, preferred_element_type=F32)
            for u, (hh, c) in enumerate(units)]
    for hh in range(nh):
        hs = slice(hh * HEAD_W, (hh + 1) * HEAD_W)
        o = jnp.concatenate(outs[hh * nchunk:(hh + 1) * nchunk], axis=0)
        o_ref[:, hs] = (_rms(o, nw_ref[...]) * g_ref[:, hs].astype(F32)).astype(BF16)


def _hgrn2(hq, hf, hi, hg, lb_logits, norm_w, bsz, seq, tc, nh):
    r3 = lambda a: a.reshape(bsz, seq, BRANCH_W)
    blk = pl.BlockSpec((None, tc, nh * HEAD_W), lambda b, h, i: (b, i, h))
    out = pl.pallas_call(
        functools.partial(_hgrn_kernel, tc=tc, nh=nh),
        grid=(bsz, HG_HEADS // nh, seq // tc),
        in_specs=[pl.BlockSpec((2, nh * HEAD_W), lambda b, h, i: (0, h)),
                  pl.BlockSpec((1, HEAD_W), lambda b, h, i: (0, 0)),
                  blk, blk, blk, blk],
        out_specs=blk,
        out_shape=jax.ShapeDtypeStruct((bsz, seq, BRANCH_W), BF16),
        scratch_shapes=[pltpu.VMEM((nh, HEAD_W, HEAD_W), F32)],
        compiler_params=_cparams(("parallel", "parallel", "arbitrary")),
        name="hgrn2",
    )(lb_logits, norm_w, r3(hq), r3(hf), r3(hi), r3(hg))
    return out.reshape(bsz * seq, BRANCH_W)


def _merge_kernel(x_ref, mod_ref, ya_ref, yb_ref, ga_ref, gb_ref, wa_ref, wb_ref, wo_ref,
                  npost_ref, npre_ref, rw_ref, rb_ref,
                  x1_ref, h2_ref, topi_ref, gate_ref, rank_ref, cnt_ref, lg_ref, *, tm):
    @pl.when(pl.program_id(0) == 0)
    def _():
        lg_ref[...] = jnp.zeros_like(lg_ref)

    logits = lg_ref[...]

    g1 = mod_ref[2:3, :]
    sh2 = mod_ref[3:4, :]
    sc2 = mod_ref[4:5, :]
    a = jnp.dot(ya_ref[...], wa_ref[...], preferred_element_type=F32)
    b = jnp.dot(yb_ref[...], wb_ref[...], preferred_element_type=F32)
    merged = ga_ref[...] * a.astype(BF16) + gb_ref[...] * b.astype(BF16)
    y = jnp.dot(merged, wo_ref[...], preferred_element_type=F32)
    x1 = x_ref[...] + g1 * _rms(y, npost_ref[...])
    x1_ref[...] = x1
    h2 = (_rms(x1, npre_ref[...]) * (1.0 + sc2) + sh2).astype(BF16)
    h2_ref[...] = h2
    lg_ref[...] = jnp.dot(h2, rw_ref[...], preferred_element_type=F32) + rb_ref[...]

    lane = lax.broadcasted_iota(I32, logits.shape, 1)
    work = logits
    vals, idxs = [], []
    for _ in range(TOP_K):
        mx = jnp.max(work, axis=1, keepdims=True)
        idx = jnp.min(jnp.where(work == mx, lane, LANES), axis=1, keepdims=True)
        vals.append(mx)
        idxs.append(idx)
        work = jnp.where(lane == idx, -jnp.inf, work)
    es = [jnp.exp(vk - vals[0]) for vk in vals]
    den = es[0] + es[1] + es[2] + es[3]

    onehot = jnp.zeros(logits.shape, F32)
    for idx in idxs:
        onehot = jnp.where(lane == idx, 1.0, onehot)
    row = lax.broadcasted_iota(I32, (tm, tm), 0)
    col = lax.broadcasted_iota(I32, (tm, tm), 1)
    strict = jnp.where(col < row, 1.0, 0.0).astype(BF16)
    prefix = jnp.dot(strict, onehot.astype(BF16), preferred_element_type=F32)
    cnt_ref[...] = jnp.broadcast_to(jnp.sum(onehot, axis=0, keepdims=True), cnt_ref.shape)

    topi = jnp.zeros(logits.shape, I32)
    gate = jnp.zeros(logits.shape, F32)
    rank = jnp.zeros(logits.shape, I32)
    for kk in range(TOP_K):
        rk = jnp.sum(jnp.where(lane == idxs[kk], prefix, 0.0), axis=1, keepdims=True)
        topi = jnp.where(lane == kk, idxs[kk], topi)
        gate = jnp.where(lane == kk, es[kk] / den, gate)
        rank = jnp.where(lane == kk, rk.astype(I32), rank)
    topi_ref[...] = topi
    gate_ref[...] = gate
    rank_ref[...] = rank


def _merge_route(xf, mod, ya, yb, ga, gb, wa, wb, wo, npost, npre, rw, rb, seq, tm):
    t, d = xf.shape
    n = t // tm
    tiles_per_batch = seq // tm
    proj = lambda w: pl.BlockSpec((tm, w), lambda i: (jnp.minimum(i, n - 1), 0))
    route = lambda w: pl.BlockSpec((tm, w), lambda i: (jnp.maximum(i - 1, 0), 0))
    const = lambda a: pl.BlockSpec(a.shape, lambda i: (0,) * a.ndim)
    lane_out = jax.ShapeDtypeStruct((t, LANES), I32)
    return pl.pallas_call(
        functools.partial(_merge_kernel, tm=tm),
        grid=(n + 1,),
        in_specs=[proj(d),
                  pl.BlockSpec((None, 6, d),
                               lambda i: (jnp.minimum(i, n - 1) // tiles_per_batch, 0, 0)),
                  proj(BRANCH_W), proj(BRANCH_W), proj(d), proj(d),
                  const(wa), const(wb), const(wo), const(npost), const(npre),
                  const(rw), const(rb)],
        out_specs=[proj(d), proj(d), route(LANES), route(LANES), route(LANES),
                   pl.BlockSpec((None, SUBLANES, LANES), lambda i: (jnp.maximum(i - 1, 0), 0, 0))],
        out_shape=[jax.ShapeDtypeStruct((t, d), F32), jax.ShapeDtypeStruct((t, d), BF16),
                   lane_out, jax.ShapeDtypeStruct((t, LANES), F32), lane_out,
                   jax.ShapeDtypeStruct((n, SUBLANES, LANES), F32)],
        scratch_shapes=[pltpu.VMEM((tm, LANES), F32)],
        compiler_params=_cparams(("arbitrary",)),
        name="merge_route",
    )(xf, mod, ya, yb, ga, gb, wa, wb, wo, npost, npre, rw, rb)


def _positions(topi_ref, rank_ref, srow_ref):
    topi = topi_ref[...]
    rank = rank_ref[...]
    lane = lax.broadcasted_iota(I32, topi.shape, 1)
    srow = srow_ref[0:1, :]
    pos = []
    for kk in range(TOP_K):
        base = jnp.sum(jnp.where(lane == topi[:, kk:kk + 1], srow, 0.0), axis=1, keepdims=True)
        pos.append(base.astype(I32) + rank[:, kk:kk + 1])
    return pos


def _wait_rows(n8, make_copy):
    for b in range((SORT_ROWS // RUN_ALIGN).bit_length()):
        @pl.when((n8 >> b) & 1 == 1)
        def _(b=b):
            make_copy(RUN_ALIGN << b).wait()


def _run_copies(tile, l8_ref, make_copy):
    def pieces(e, n8, bits, off):
        for b in bits:
            size = RUN_ALIGN << b
            bit = (n8 >> b) & 1

            @pl.when(bit == 1)
            def _(off=off, size=size):
                make_copy(e, off, size).start()

            off = off + bit * size

    for e in range(N_EXPERTS):
        n8 = l8_ref[tile * N_EXPERTS + e]
        big = (n8 >> RUN_SMALL_BITS) << RUN_SMALL_BITS

        @pl.when(big > 0)
        def _(e=e, n8=n8):
            pieces(e, n8, range(RUN_BITS - 1, RUN_SMALL_BITS - 1, -1), jnp.int32(0))

        pieces(e, n8, range(RUN_SMALL_BITS - 1, -1, -1), big * RUN_ALIGN)


def _dispatch_kernel(start_ref, srun_ref, l8_ref, tot8_ref, eend_ref, elen_ref,
                     h_ref, topi_ref, rank_ref, srow_ref, xs_out,
                     sel0, sel1, srt0, srt1, zbuf, zsem, sem, *, min_blocks):
    i = pl.program_id(0)
    n_tiles = pl.num_programs(0) - 1
    sels, srts = (sel0, sel1), (srt0, srt1)
    tm = h_ref.shape[0]
    half = h_ref.shape[1] // 2
    n_blocks = xs_out.shape[0] // MOE_BLOCK
    n_used = eend_ref[N_EXPERTS - 1] // MOE_BLOCK

    def zero_copy(start):
        start = pl.multiple_of(start, MOE_BLOCK)
        return pltpu.make_async_copy(zbuf, xs_out.at[pl.ds(start, MOE_BLOCK)], zsem)

    def zero_fill(wait):
        for e in range(N_EXPERTS):
            @pl.when(elen_ref[e] > 0)
            def _(e=e):
                cp = zero_copy(eend_ref[e] - MOE_BLOCK)
                cp.wait() if wait else cp.start()
        for blk in range(min_blocks, n_blocks):
            @pl.when(blk >= n_used)
            def _(blk=blk):
                cp = zero_copy(blk * MOE_BLOCK)
                cp.wait() if wait else cp.start()

    @pl.when(i == 0)
    def _():
        zbuf[...] = jnp.zeros_like(zbuf)
        sel0[...] = jnp.zeros_like(sel0)
        sel1[...] = jnp.zeros_like(sel1)
        zero_fill(wait=False)
        zero_fill(wait=True)

    def start_copies(tile, slot):
        def make_copy(e, off, size):
            src = pl.multiple_of(srun_ref[tile * N_EXPERTS + e] + off, RUN_ALIGN)
            dst = pl.multiple_of(start_ref[tile * N_EXPERTS + e] + off, RUN_ALIGN)
            return pltpu.make_async_copy(srts[slot].at[pl.ds(src, size)],
                                         xs_out.at[pl.ds(dst, size)], sem.at[slot])
        _run_copies(tile, l8_ref, make_copy)

    def wait_copies(tile, slot):
        def make_copy(size):
            return pltpu.make_async_copy(srts[slot].at[pl.ds(0, size)],
                                         xs_out.at[pl.ds(0, size)], sem.at[slot])
        _wait_rows(tot8_ref[tile], make_copy)

    for s in range(2):
        @pl.when(i % 2 == s)
        def _(s=s):
            pos = _positions(topi_ref, rank_ref, srow_ref)
            slot = lax.broadcasted_iota(I32, (tm, SORT_ROWS), 1).astype(jnp.int16)
            sel = jnp.zeros((tm, SORT_ROWS), BF16)
            for p in pos:
                sel = jnp.where(slot == p.astype(jnp.int16), jnp.ones((), BF16), sel)
            sels[s][...] = sel
            srt = lax.dot_general(sels[1 - s][...], h_ref[...], (((0,), (0,)), ((), ())),
                                  preferred_element_type=F32)
            lo = pltpu.bitcast(srt[:, :half], jnp.uint32) >> 16
            hi = pltpu.bitcast(srt[:, half:], jnp.uint32) & jnp.uint32(0xFFFF0000)
            srts[1 - s][...] = hi | lo

        @pl.when((i >= 1) & (i % 2 == s))
        def _(s=s):
            start_copies(i - 1, 1 - s)

        @pl.when((i >= 2) & (i % 2 == s))
        def _(s=s):
            wait_copies(i - 2, s)

        @pl.when((i == n_tiles) & (i % 2 == s))
        def _(s=s):
            wait_copies(i - 1, 1 - s)


def _dispatch(h2, topi, rank, srow, tables, n_rows, tm):
    t, d = h2.shape
    n = t // tm
    cur = lambda w: pl.BlockSpec((tm, w), lambda i, *_: (jnp.minimum(i, n - 1), 0))
    grid_spec = pltpu.PrefetchScalarGridSpec(
        num_scalar_prefetch=len(tables),
        grid=(n + 1,),
        in_specs=[pl.BlockSpec((tm, d), lambda i, *_: (jnp.maximum(i - 1, 0), 0)),
                  cur(LANES), cur(LANES),
                  pl.BlockSpec((None, SUBLANES, LANES), lambda i, *_: (jnp.minimum(i, n - 1), 0, 0))],
        out_specs=pl.BlockSpec(memory_space=pl.ANY),
        scratch_shapes=[pltpu.VMEM((tm, SORT_ROWS), BF16), pltpu.VMEM((tm, SORT_ROWS), BF16),
                        pltpu.VMEM((SORT_ROWS, d // 2), jnp.uint32),
                        pltpu.VMEM((SORT_ROWS, d // 2), jnp.uint32),
                        pltpu.VMEM((MOE_BLOCK, d // 2), jnp.uint32),
                        pltpu.SemaphoreType.DMA, pltpu.SemaphoreType.DMA((2,))],
    )
    return pl.pallas_call(
        functools.partial(_dispatch_kernel, min_blocks=t * TOP_K // MOE_BLOCK),
        grid_spec=grid_spec,
        out_shape=jax.ShapeDtypeStruct((n_rows, d // 2), jnp.uint32),
        compiler_params=_cparams(("arbitrary",)),
        name="moe_dispatch",
    )(*tables, h2, topi, rank, srow)


def _expert_kernel(be_ref, nu_ref, x_ref, wgu_ref, bgu_ref, wd_ref, bd_ref, y_ref,
                   wgu_bf, wd_bf):
    j = pl.program_id(0)

    @pl.when((j == 0) | (be_ref[j] != be_ref[jnp.maximum(j - 1, 0)]))
    def _():
        wgu_bf[...] = wgu_ref[...].astype(BF16)
        wd_bf[...] = wd_ref[...].astype(BF16)

    @pl.when(j < nu_ref[0])
    def _():
        xu = x_ref[...]
        x_lo = pltpu.bitcast(xu << 16, F32).astype(BF16)
        x_hi = pltpu.bitcast(xu & jnp.uint32(0xFFFF0000), F32).astype(BF16)
        x = jnp.concatenate([x_lo, x_hi], axis=1)
        gu = jnp.dot(x, wgu_bf[...], preferred_element_type=F32) + bgu_ref[...]
        g = jnp.minimum(gu[:, :D_MODEL], SWIGLU_LIMIT)
        u = jnp.clip(gu[:, D_MODEL:], -SWIGLU_LIMIT, SWIGLU_LIMIT)
        act = (u + 1.0) * (g * _sigmoid(g * SWIGLU_ALPHA))
        y_ref[...] = jnp.dot(act.astype(BF16), wd_bf[...], preferred_element_type=F32) + bd_ref[...]

    @pl.when(j >= nu_ref[0])
    def _():
        y_ref[...] = jnp.zeros_like(y_ref)


def _experts(xs, block_e, n_used, wgu, bgu, wd, bd):
    p = xs.shape[0]
    e, d, _ = wgu.shape
    n_blocks = p // MOE_BLOCK
    grid_spec = pltpu.PrefetchScalarGridSpec(
        num_scalar_prefetch=2,
        grid=(n_blocks,),
        in_specs=[pl.BlockSpec((MOE_BLOCK, d // 2), lambda j, be, nu: (j, 0)),
                  pl.BlockSpec((None, d, 2 * d), lambda j, be, nu: (be[j], 0, 0)),
                  pl.BlockSpec((None, 1, 2 * d), lambda j, be, nu: (be[j], 0, 0)),
                  pl.BlockSpec((None, d, d), lambda j, be, nu: (be[j], 0, 0)),
                  pl.BlockSpec((None, 1, d), lambda j, be, nu: (be[j], 0, 0))],
        out_specs=pl.BlockSpec((MOE_BLOCK, d), lambda j, be, nu: (j, 0)),
        scratch_shapes=[pltpu.VMEM((d, 2 * d), BF16), pltpu.VMEM((d, d), BF16)],
    )
    return pl.pallas_call(
        _expert_kernel,
        grid_spec=grid_spec,
        out_shape=jax.ShapeDtypeStruct((p, d), F32),
        compiler_params=_cparams(("arbitrary",)),
        name="moe_experts",
    )(block_e, n_used, xs, wgu, bgu.reshape(e, 1, 2 * d), wd, bd.reshape(e, 1, d))


def _combine_kernel(start_ref, srun_ref, l8_ref, tot8_ref,
                    y_hbm, topi_ref, rank_ref, srow_ref, gate_ref, x1_ref, mod_ref, nw_ref, o_ref,
                    sel0, sel1, yb0, yb1, sem):
    i = pl.program_id(0)
    n_tiles = pl.num_programs(0) - 1
    tm = x1_ref.shape[0]
    sels, ybufs = (sel0, sel1), (yb0, yb1)

    def start_copies(tile, slot):
        def make_copy(e, off, size):
            src = pl.multiple_of(start_ref[tile * N_EXPERTS + e] + off, RUN_ALIGN)
            dst = pl.multiple_of(srun_ref[tile * N_EXPERTS + e] + off, RUN_ALIGN)
            return pltpu.make_async_copy(y_hbm.at[pl.ds(src, size)],
                                         ybufs[slot].at[pl.ds(dst, size)], sem.at[slot])
        _run_copies(tile, l8_ref, make_copy)

    def wait_copies(tile, slot):
        def make_copy(size):
            return pltpu.make_async_copy(y_hbm.at[pl.ds(0, size)],
                                         ybufs[slot].at[pl.ds(0, size)], sem.at[slot])
        _wait_rows(tot8_ref[tile], make_copy)

    @pl.when(i == 0)
    def _():
        yb0[...] = jnp.zeros_like(yb0)
        yb1[...] = jnp.zeros_like(yb1)
        sel0[...] = jnp.zeros_like(sel0)
        sel1[...] = jnp.zeros_like(sel1)

    for s in range(2):
        @pl.when((i < n_tiles) & (i % 2 == s))
        def _(s=s):
            start_copies(i, s)

        @pl.when((i >= 1) & (i % 2 == s))
        def _(s=s):
            wait_copies(i - 1, 1 - s)

        @pl.when(i % 2 == s)
        def _(s=s):
            pos = _positions(topi_ref, rank_ref, srow_ref)
            gate = gate_ref[...].astype(BF16)
            row = lax.broadcasted_iota(I32, (tm, SORT_ROWS), 1).astype(jnp.int16)
            sel = jnp.zeros((tm, SORT_ROWS), BF16)
            for kk, p in enumerate(pos):
                sel = jnp.where(row == p.astype(jnp.int16), gate[:, kk:kk + 1], sel)
            sels[s][...] = sel
            moe = jnp.dot(sels[1 - s][...], ybufs[1 - s][...].astype(BF16),
                          preferred_element_type=F32)
            g2 = mod_ref[5:6, :]
            o_ref[...] = x1_ref[...] + g2 * _rms(moe, nw_ref[...])


def _combine(y, topi, rank, srow, gate, x1, mod, nw, tables, seq, tm):
    t, d = x1.shape
    n = t // tm
    tiles_per_batch = seq // tm
    cur = lambda w: pl.BlockSpec((tm, w), lambda i, *_: (jnp.minimum(i, n - 1), 0))
    prev = lambda w: pl.BlockSpec((tm, w), lambda i, *_: (jnp.maximum(i - 1, 0), 0))
    grid_spec = pltpu.PrefetchScalarGridSpec(
        num_scalar_prefetch=len(tables),
        grid=(n + 1,),
        in_specs=[pl.BlockSpec(memory_space=pl.ANY),
                  cur(LANES), cur(LANES),
                  pl.BlockSpec((None, SUBLANES, LANES), lambda i, *_: (jnp.minimum(i, n - 1), 0, 0)),
                  cur(LANES), prev(d),
                  pl.BlockSpec((None, 6, d),
                               lambda i, *_: (jnp.maximum(i - 1, 0) // tiles_per_batch, 0, 0)),
                  pl.BlockSpec((1, d), lambda i, *_: (0, 0))],
        out_specs=prev(d),
        scratch_shapes=[pltpu.VMEM((tm, SORT_ROWS), BF16), pltpu.VMEM((tm, SORT_ROWS), BF16),
                        pltpu.VMEM((SORT_ROWS, d), F32), pltpu.VMEM((SORT_ROWS, d), F32),
                        pltpu.SemaphoreType.DMA((2,))],
    )
    return pl.pallas_call(
        _combine_kernel,
        grid_spec=grid_spec,
        out_shape=jax.ShapeDtypeStruct((t, d), F32),
        compiler_params=_cparams(("arbitrary",)),
        name="moe_combine",
    )(*tables, y, topi, rank, srow, gate, x1, mod, nw)


def _rope_tables(seq):
    pos = jnp.arange(seq, dtype=F32)
    inv = ROPE_THETA ** (-jnp.arange(0, DA_HEAD_DIM, 2, dtype=F32) / DA_HEAD_DIM)
    ang = pos[:, None] * inv[None, :]
    cos, sin = jnp.cos(ang), jnp.sin(ang)
    zero = jnp.zeros_like(sin)
    cos_t = jnp.tile(cos, (1, LANES // cos.shape[1]))
    sin_lo = jnp.tile(jnp.concatenate([-sin, zero], axis=1), (1, 2))
    sin_hi = jnp.tile(jnp.concatenate([zero, sin], axis=1), (1, 2))
    return cos_t, sin_lo, sin_hi


def kernel(x, c, w_mod, b_mod, norm_pre_mix, norm_post_mix, w_in, da_lambda_q1, da_lambda_k1,
           da_lambda_q2, da_lambda_k2, da_subln, hg_lb_logits, hg_norm, w_branch_a, w_branch_b,
           w_out, norm_pre_ffn, norm_post_ffn, router_w, router_b, w_gate_up, b_gate_up,
           w_down, b_down):
    bsz, seq, d = x.shape
    depth = w_mod.shape[0]
    assert depth == 1 and d == D_MODEL
    t = bsz * seq
    lyr = 0
    lambda_init = 0.8 - 0.6 * math.exp(-0.3 * lyr)
    cos, slo, shi = _rope_tables(seq)

    xf = x.reshape(t, d)
    mod = _modulation(c, w_mod[lyr], b_mod[lyr]).reshape(bsz, 6, d)

    qT, k, vT, hq, hf, hi, hg, ga, gb = _in_projection(
        xf, mod, norm_pre_mix[lyr][None], w_in[lyr].astype(BF16), cos, slo, shi, seq, tm=512)
    ya = _diff_attention(qT, k, vT, da_lambda_q1[lyr][None], da_lambda_k1[lyr][None],
                         da_lambda_q2[lyr][None], da_lambda_k2[lyr][None], da_subln[lyr],
                         bsz, seq, lambda_init, tq=2048, tk=512)
    yb = _hgrn2(hq, hf, hi, hg, hg_lb_logits[lyr:lyr + 2], hg_norm[lyr][None], bsz, seq, tc=512, nh=4)

    rw = jnp.zeros((d, LANES), BF16).at[:, :N_EXPERTS].set(router_w[lyr].astype(BF16))
    rb = jnp.full((1, LANES), NEG_BIG, F32).at[0, :N_EXPERTS].set(router_b[lyr])
    x1, h2, topi, gate, rank, counts = _merge_route(
        xf, mod, ya, yb, ga, gb, w_branch_a[lyr].astype(BF16), w_branch_b[lyr].astype(BF16),
        w_out[lyr].astype(BF16), norm_post_mix[lyr][None], norm_pre_ffn[lyr][None], rw, rb,
        seq, tm=MOE_TILE)

    n_tiles = t // MOE_TILE
    up = lambda a, m: (a + m - 1) // m * m
    cnt = counts[:, 0, :N_EXPERTS].astype(I32)
    run = up(cnt, RUN_ALIGN)
    srun = jnp.cumsum(run, axis=1) - run
    e_len = up(jnp.sum(run, axis=0), MOE_BLOCK)
    e_end = jnp.cumsum(e_len)
    start = (e_end - e_len)[None, :] + jnp.cumsum(run, axis=0) - run
    n_assign = t * TOP_K
    n_rows = (up(n_assign + n_tiles * N_EXPERTS * (RUN_ALIGN - 1), MOE_BLOCK)
              + N_EXPERTS * MOE_BLOCK)
    n_blocks = n_rows // MOE_BLOCK
    block_start = jnp.arange(n_blocks, dtype=I32) * MOE_BLOCK
    block_e = jnp.minimum(jnp.sum((e_end[None, :] <= block_start[:, None]).astype(I32), axis=1),
                          N_EXPERTS - 1)
    n_used = (e_end[-1:] // MOE_BLOCK).astype(I32)
    srow = jnp.broadcast_to(jnp.pad(srun.astype(F32), ((0, 0), (0, LANES - N_EXPERTS)))[:, None, :],
                            (n_tiles, SUBLANES, LANES))
    run_tables = (start.reshape(-1), srun.reshape(-1), (run // RUN_ALIGN).reshape(-1),
                  jnp.sum(run, axis=1) // RUN_ALIGN)

    xs = _dispatch(h2, topi, rank, srow, run_tables + (e_end.astype(I32), e_len.astype(I32)),
                   n_rows, tm=MOE_TILE)
    y = _experts(xs, block_e, n_used, w_gate_up[lyr], b_gate_up[lyr], w_down[lyr], b_down[lyr])
    out = _combine(y, topi, rank, srow, gate, x1, mod, norm_post_ffn[lyr][None], run_tables,
                   seq, tm=MOE_TILE)
    return out.reshape(bsz, seq, d)
```

```python
import functools
import math

import jax
import jax.numpy as jnp
from jax import lax
from jax.experimental import pallas as pl
from jax.experimental.pallas import tpu as pltpu

F32 = jnp.float32
BF16 = jnp.bfloat16
I32 = jnp.int32

D_MODEL = 1024
CHUNK = 64
EPS = 1e-6
ROPE_THETA = 10000.0
DA_HEADS = 4
DA_HEAD_DIM = 64
HG_HEADS = 4
HEAD_W = 128
BRANCH_W = 512
V_ROWS = HEAD_W + 16
ATTN_STRIP = 256
N_EXPERTS = 32
TOP_K = 4
SWIGLU_LIMIT = 7.0
SWIGLU_ALPHA = 1.702
MOE_BLOCK = 512
MOE_TILE = 512
LANES = 128
SUBLANES = 8
RUN_ALIGN = SUBLANES
RUN_BITS = (MOE_TILE // RUN_ALIGN).bit_length()
RUN_SMALL_BITS = 4
SORT_ROWS = MOE_TILE * TOP_K + N_EXPERTS * RUN_ALIGN
NEG_BIG = -1e30

VMEM_LIMIT = 56 * 1024 * 1024


def _cparams(sem):
    return pltpu.CompilerParams(dimension_semantics=sem, vmem_limit_bytes=VMEM_LIMIT)


def _rms(x, w):
    return x * lax.rsqrt(jnp.mean(x * x, axis=-1, keepdims=True) + EPS) * w


def _sigmoid(x):
    return 1.0 / (1.0 + jnp.exp(-x))


def _split3(x):
    a = x.astype(BF16)
    r = x - a.astype(F32)
    b = r.astype(BF16)
    c = (r - b.astype(F32)).astype(BF16)
    return a, b, c


def _mod_kernel(c_ref, w_ref, b_ref, o_ref):
    c = c_ref[...]
    ca = c * _sigmoid(c)
    o_ref[...] = jnp.dot(ca, w_ref[...], preferred_element_type=F32,
                         precision=lax.Precision.HIGHEST) + b_ref[...]


def _modulation(c, w_mod, b_mod):
    bsz, d = c.shape
    n = w_mod.shape[1]
    tn = 1536
    return pl.pallas_call(
        _mod_kernel,
        grid=(n // tn,),
        in_specs=[pl.BlockSpec((bsz, d), lambda j: (0, 0)),
                  pl.BlockSpec((d, tn), lambda j: (0, j)),
                  pl.BlockSpec((1, tn), lambda j: (0, j))],
        out_specs=pl.BlockSpec((bsz, tn), lambda j: (0, j)),
        out_shape=jax.ShapeDtypeStruct((bsz, n), F32),
        compiler_params=_cparams(("arbitrary",)),
        name="modulation",
    )(c, w_mod, b_mod.reshape(1, n))


def _rope(x, cos, sin_lo, sin_hi):
    return x * cos + pltpu.roll(x, 96, 1) * sin_lo + pltpu.roll(x, 32, 1) * sin_hi


def _inproj_kernel(x_ref, mod_ref, nw_ref, w_ref, cos_ref, slo_ref, shi_ref,
                   qT_ref, k_ref, vT_ref, hq_ref, hf_ref, hi_ref, hg_ref, ga_ref, gb_ref):
    x = x_ref[...]
    sh1 = mod_ref[0:1, :]
    sc1 = mod_ref[1:2, :]
    h = (_rms(x, nw_ref[...]) * (1.0 + sc1) + sh1).astype(BF16)
    cos, slo, shi = cos_ref[...], slo_ref[...], shi_ref[...]
    scale = DA_HEAD_DIM ** -0.5 * math.log2(math.e)

    def seg(i, width=BRANCH_W):
        return jnp.dot(h, w_ref[:, i:i + width], preferred_element_type=F32)

    q = seg(0)
    k = seg(BRANCH_W)
    for hd in range(DA_HEADS):
        sl = slice(hd * HEAD_W, (hd + 1) * HEAD_W)
        qT_ref[sl, :] = (_rope(q[:, sl], cos, slo, shi) * scale).T.astype(BF16)
        k_ref[:, sl] = _rope(k[:, sl], cos, slo, shi).astype(BF16)
    vT = seg(2 * BRANCH_W).T.astype(BF16)
    ones = jnp.ones((V_ROWS - HEAD_W, vT.shape[1]), BF16)
    for hd in range(DA_HEADS):
        vT_ref[hd * V_ROWS:hd * V_ROWS + HEAD_W, :] = vT[hd * HEAD_W:(hd + 1) * HEAD_W]
        vT_ref[hd * V_ROWS + HEAD_W:(hd + 1) * V_ROWS, :] = ones
    hq = seg(3 * BRANCH_W)
    hq_ref[...] = (hq * _sigmoid(hq)).astype(BF16)
    hf_ref[...] = seg(4 * BRANCH_W)
    hi_ref[...] = seg(5 * BRANCH_W).astype(BF16)
    hg = seg(6 * BRANCH_W)
    hg_ref[...] = (hg * _sigmoid(hg)).astype(BF16)
    for j in range(2):
        ga_ref[:, j * BRANCH_W:(j + 1) * BRANCH_W] = _sigmoid(seg((7 + j) * BRANCH_W)).astype(BF16)
        gb_ref[:, j * BRANCH_W:(j + 1) * BRANCH_W] = _sigmoid(seg((9 + j) * BRANCH_W)).astype(BF16)


def _in_projection(xf, mod, norm_w, w_in_bf, cos, slo, shi, seq, tm):
    t, d = xf.shape
    tiles_per_batch = seq // tm
    bspec = lambda w: pl.BlockSpec((tm, w), lambda i: (i, 0))
    tab = pl.BlockSpec((tm, LANES), lambda i: (i % tiles_per_batch, 0))
    widths = [BRANCH_W] * 7 + [D_MODEL, D_MODEL]
    dtypes = [BF16, BF16, BF16, BF16, F32, BF16, BF16, BF16, BF16]
    out_specs = [bspec(w) for w in widths]
    out_shape = [jax.ShapeDtypeStruct((t, w), dt) for w, dt in zip(widths, dtypes)]
    for idx, rows in ((0, BRANCH_W), (2, DA_HEADS * V_ROWS)):
        out_specs[idx] = pl.BlockSpec((None, rows, tm),
                                      lambda i: (i // tiles_per_batch, 0, i % tiles_per_batch))
        out_shape[idx] = jax.ShapeDtypeStruct((t // seq, rows, seq), BF16)
    return pl.pallas_call(
        _inproj_kernel,
        grid=(t // tm,),
        in_specs=[bspec(d),
                  pl.BlockSpec((None, 6, d), lambda i: (i // tiles_per_batch, 0, 0)),
                  pl.BlockSpec((1, d), lambda i: (0, 0)),
                  pl.BlockSpec(w_in_bf.shape, lambda i: (0, 0), pipeline_mode=pl.Buffered(1)),
                  tab, tab, tab],
        out_specs=out_specs,
        out_shape=out_shape,
        compiler_params=_cparams(("parallel",)),
        name="in_projection",
    )(xf, mod, norm_w, w_in_bf, cos, slo, shi)


def _attn_kernel(lq1_ref, lk1_ref, lq2_ref, lk2_ref, subln_ref, qT_ref, k_ref, vT_ref, o_ref,
                 qq_ref, s0_ref, s1_ref, p0_ref, p1_ref, a0_ref, a1_ref, m_ref, acc_ref,
                 *, tq, tk, lambda_init):
    qi = pl.program_id(2)
    ndiag = tq // tk
    s_refs, p_refs, a_refs = (s0_ref, s1_ref), (p0_ref, p1_ref), (a0_ref, a1_ref)
    qT = qT_ref[...]
    feat = lax.broadcasted_iota(I32, qT.shape, 0)
    zero = jnp.zeros_like(qT)
    qq_ref[:, :tq] = jnp.where(feat < DA_HEAD_DIM, qT, zero)
    qq_ref[:, tq:] = jnp.where(feat >= DA_HEAD_DIM, qT, zero)

    strips = [slice(c * ATTN_STRIP, (c + 1) * ATTN_STRIP) for c in range(2 * tq // ATTN_STRIP)]

    def scores(kb, dst, cs):
        start = pl.multiple_of(kb * tk, tk)
        dst[:, cs] = jnp.dot(k_ref[pl.ds(start, tk), :], qq_ref[:, cs], preferred_element_type=F32)

    def accumulate(kb, slot, cs):
        start = pl.multiple_of(kb * tk, tk)
        pv = jnp.dot(vT_ref[:, pl.ds(start, tk)], p_refs[slot][:, cs], preferred_element_type=F32)
        acc_ref[:, cs] = a_refs[slot][:, cs] * acc_ref[:, cs] + pv

    def softmax(slot, cs, diag):
        def piece(rc):
            sc = s_refs[slot][rc * CHUNK:(rc + 1) * CHUNK, cs]
            if diag is not None:
                qchunk = (lax.broadcasted_iota(I32, (1, ATTN_STRIP), 1) + cs.start) % tq // CHUNK
                sc = jnp.where(qchunk >= diag * (tk // CHUNK) + rc, sc, NEG_BIG)
            return sc

        top = piece(0)
        for rc in range(1, tk // CHUNK):
            top = jnp.maximum(top, piece(rc))
        m_old = m_ref[:, cs]
        m_new = jnp.maximum(m_old, jnp.max(top, axis=0, keepdims=True))
        m_ref[:, cs] = m_new
        a_refs[slot][:, cs] = jnp.exp2(m_old - m_new)
        for rc in range(tk // CHUNK):
            p_refs[slot][rc * CHUNK:(rc + 1) * CHUNK, cs] = jnp.exp2((piece(rc) - m_new).astype(BF16))

    def seen(diag, cs):
        return diag is None or diag < 0 or (cs.start % tq) + ATTN_STRIP > diag * tk

    def stage(kb, slot, diag=None, last=False):
        for cs in strips:
            if seen(None if diag is None else diag - 1, cs):
                accumulate(jnp.maximum(kb - 1, 0), 1 - slot, cs)
            if seen(diag, cs):
                softmax(slot, cs, diag)
            if not last and seen(None if diag is None else diag + 1, cs):
                scores(kb + 1, s_refs[1 - slot], cs)

    p1_ref[...] = jnp.zeros_like(p1_ref)
    a1_ref[...] = jnp.ones_like(a1_ref)
    m_ref[...] = jnp.full(m_ref.shape, NEG_BIG, F32)
    acc_ref[...] = jnp.zeros_like(acc_ref)
    for cs in strips:
        scores(0, s0_ref, cs)

    def pair(i, c):
        stage(2 * i, 0)
        stage(2 * i + 1, 1)
        return c

    first_diag = qi * ndiag
    lax.fori_loop(0, first_diag // 2, pair, 0)
    for dg in range(ndiag):
        stage(first_diag + dg, dg % 2, diag=dg, last=dg == ndiag - 1)
    for cs in strips:
        if seen(ndiag - 1, cs):
            accumulate(first_diag + ndiag - 1, (ndiag - 1) % 2, cs)

    acc = acc_ref[...]
    o = acc[:HEAD_W] / acc[HEAD_W:HEAD_W + 1]
    lam = (jnp.exp(jnp.sum(lq1_ref[...] * lk1_ref[...], axis=1, keepdims=True))
           - jnp.exp(jnp.sum(lq2_ref[...] * lk2_ref[...], axis=1, keepdims=True))
           + lambda_init)
    o = o[:, :tq] - lam * o[:, tq:]
    o = o * lax.rsqrt(jnp.mean(o * o, axis=0, keepdims=True) + EPS) * subln_ref[...]
    o_ref[...] = (o * (1.0 - lambda_init)).T.astype(BF16)


def _diff_attention(qT, k, vT, lq1, lk1, lq2, lk2, subln, bsz, seq, lambda_init, tq, tk):
    assert tq % (2 * tk) == 0
    k3 = k.reshape(bsz, seq, BRANCH_W)
    small = lambda n: pl.BlockSpec((1, n), lambda b, h, i: (0, 0))
    out = pl.pallas_call(
        functools.partial(_attn_kernel, tq=tq, tk=tk, lambda_init=lambda_init),
        grid=(bsz, DA_HEADS, seq // tq),
        in_specs=[small(DA_HEAD_DIM)] * 4 + [
            pl.BlockSpec((HEAD_W, 1), lambda b, h, i: (0, 0)),
            pl.BlockSpec((None, HEAD_W, tq), lambda b, h, i: (b, h, i)),
            pl.BlockSpec((None, seq, HEAD_W), lambda b, h, i: (b, 0, h)),
            pl.BlockSpec((None, V_ROWS, seq), lambda b, h, i: (b, h, 0))],
        out_specs=pl.BlockSpec((None, tq, HEAD_W), lambda b, h, i: (b, i, h)),
        out_shape=jax.ShapeDtypeStruct((bsz, seq, BRANCH_W), BF16),
        scratch_shapes=[pltpu.VMEM((HEAD_W, 2 * tq), BF16),
                        pltpu.VMEM((tk, 2 * tq), F32), pltpu.VMEM((tk, 2 * tq), F32),
                        pltpu.VMEM((tk, 2 * tq), BF16), pltpu.VMEM((tk, 2 * tq), BF16),
                        pltpu.VMEM((1, 2 * tq), F32), pltpu.VMEM((1, 2 * tq), F32),
                        pltpu.VMEM((1, 2 * tq), F32), pltpu.VMEM((V_ROWS, 2 * tq), F32)],
        compiler_params=_cparams(("parallel", "parallel", "arbitrary")),
        name="diff_attention",
    )(lq1, lk1, lq2, lk2, subln.reshape(HEAD_W, 1), qT, k3, vT)
    return out.reshape(bsz * seq, BRANCH_W)


def _hgrn_kernel(lbl_ref, nw_ref, q_ref, f_ref, i_ref, g_ref, o_ref, st_ref, *, tc, nh):
    @pl.when(pl.program_id(2) == 0)
    def _():
        st_ref[...] = jnp.zeros_like(st_ref)

    row = lax.broadcasted_iota(I32, (CHUNK, CHUNK), 0)
    col = lax.broadcasted_iota(I32, (CHUNK, CHUNK), 1)
    causal = col <= row
    tril = jnp.where(causal, 1.0, 0.0).astype(BF16)
    nt = (((1,), (1,)), ((), ()))
    tn = (((0,), (0,)), ((), ()))

    nchunk = tc // CHUNK
    units = [(hh, c) for hh in range(nh) for c in range(nchunk)]
    rows = lambda c: slice(c * CHUNK, (c + 1) * CHUNK)
    q, k, v, parts = [], [], [], []
    for hh in range(nh):
        hs = slice(hh * HEAD_W, (hh + 1) * HEAD_W)
        lg = lbl_ref[:, hs]
        ex = jnp.exp(lg - jnp.max(lg, axis=0, keepdims=True))
        lb = ex[0:1] / jnp.sum(ex, axis=0, keepdims=True)
        f = lb + (1.0 - lb) * _sigmoid(f_ref[:, hs])
        q.append(q_ref[:, hs].astype(F32))
        k.append(1.0 - f)
        v.append(i_ref[:, hs])
        parts.append(_split3(jnp.log(f)))
    pieces = jnp.concatenate(
        [jnp.concatenate([parts[hh][p][rows(c)] for hh, c in units], axis=1) for p in range(3)],
        axis=0)
    gall = jnp.dot(jnp.concatenate([tril] * 3, axis=1), pieces, preferred_element_type=F32)
    gcum = [gall[:, u * HEAD_W:(u + 1) * HEAD_W] for u in range(len(units))]
    gtot = [g[CHUNK - 1:CHUNK, :] for g in gcum]
    q_t = [(q[hh][rows(c)] * jnp.exp(gcum[u])).astype(BF16) for u, (hh, c) in enumerate(units)]
    k_t = [(k[hh][rows(c)] * jnp.exp(-gcum[u])).astype(BF16) for u, (hh, c) in enumerate(units)]
    k_dec = [(k[hh][rows(c)] * jnp.exp(gtot[u] - gcum[u])).astype(BF16)
             for u, (hh, c) in enumerate(units)]
    a = [lax.dot_general(q_t[u], k_t[u], nt, preferred_element_type=F32) for u in range(len(units))]
    ds = [lax.dot_general(v[hh][rows(c)], k_dec[u], tn, preferred_element_type=F32)
          for u, (hh, c) in enumerate(units)]
    a = [jnp.where(causal, x, 0.0).astype(BF16) for x in a]
    st_in = []
    for hh in range(nh):
        st = st_ref[hh]
        for c in range(nchunk):
            u = hh * nchunk + c
            st_in.append(st.astype(BF16))
            st = jnp.exp(gtot[u]) * st + ds[u]
        st_ref[hh] = st
    outs = [jnp.dot(a[u], v[hh][rows(c)], preferred_element_type=F32)
            + lax.dot_general(q_t[u], st_in[u], nt, preferred_element_type=F32)
            for u, (hh, c) in enumerate(units)]
    for hh in range(nh):
        hs = slice(hh * HEAD_W, (hh + 1) * HEAD_W)
        o = jnp.concatenate(outs[hh * nchunk:(hh + 1) * nchunk], axis=0)
        o_ref[:, hs] = (_rms(o, nw_ref[...]) * g_ref[:, hs].astype(F32)).astype(BF16)


def _hgrn2(hq, hf, hi, hg, lb_logits, norm_w, bsz, seq, tc, nh):
    r3 = lambda a: a.reshape(bsz, seq, BRANCH_W)
    blk = pl.BlockSpec((None, tc, nh * HEAD_W), lambda b, h, i: (b, i, h))
    out = pl.pallas_call(
        functools.partial(_hgrn_kernel, tc=tc, nh=nh),
        grid=(bsz, HG_HEADS // nh, seq // tc),
        in_specs=[pl.BlockSpec((2, nh * HEAD_W), lambda b, h, i: (0, h)),
                  pl.BlockSpec((1, HEAD_W), lambda b, h, i: (0, 0)),
                  blk, blk, blk, blk],
        out_specs=blk,
        out_shape=jax.ShapeDtypeStruct((bsz, seq, BRANCH_W), BF16),
        scratch_shapes=[pltpu.VMEM((nh, HEAD_W, HEAD_W), F32)],
        compiler_params=_cparams(("parallel", "parallel", "arbitrary")),
        name="hgrn2",
    )(lb_logits, norm_w, r3(hq), r3(hf), r3(hi), r3(hg))
    return out.reshape(bsz * seq, BRANCH_W)


def _merge_kernel(x_ref, mod_ref, ya_ref, yb_ref, ga_ref, gb_ref, wa_ref, wb_ref, wo_ref,
                  npost_ref, npre_ref, rw_ref, rb_ref,
                  x1_ref, h2_ref, topi_ref, gate_ref, rank_ref, cnt_ref, lg_ref, *, tm):
    @pl.when(pl.program_id(0) == 0)
    def _():
        lg_ref[...] = jnp.zeros_like(lg_ref)

    logits = lg_ref[...]

    g1 = mod_ref[2:3, :]
    sh2 = mod_ref[3:4, :]
    sc2 = mod_ref[4:5, :]
    a = jnp.dot(ya_ref[...], wa_ref[...], preferred_element_type=F32)
    b = jnp.dot(yb_ref[...], wb_ref[...], preferred_element_type=F32)
    merged = ga_ref[...] * a.astype(BF16) + gb_ref[...] * b.astype(BF16)
    y = jnp.dot(merged, wo_ref[...], preferred_element_type=F32)
    x1 = x_ref[...] + g1 * _rms(y, npost_ref[...])
    x1_ref[...] = x1
    h2 = (_rms(x1, npre_ref[...]) * (1.0 + sc2) + sh2).astype(BF16)
    h2_ref[...] = h2
    lg_ref[...] = jnp.dot(h2, rw_ref[...], preferred_element_type=F32) + rb_ref[...]

    lane = lax.broadcasted_iota(I32, logits.shape, 1)
    work = logits
    vals, idxs = [], []
    for _ in range(TOP_K):
        mx = jnp.max(work, axis=1, keepdims=True)
        idx = jnp.min(jnp.where(work == mx, lane, LANES), axis=1, keepdims=True)
        vals.append(mx)
        idxs.append(idx)
        work = jnp.where(lane == idx, -jnp.inf, work)
    es = [jnp.exp(vk - vals[0]) for vk in vals]
    den = es[0] + es[1] + es[2] + es[3]

    onehot = jnp.zeros(logits.shape, F32)
    for idx in idxs:
        onehot = jnp.where(lane == idx, 1.0, onehot)
    row = lax.broadcasted_iota(I32, (tm, tm), 0)
    col = lax.broadcasted_iota(I32, (tm, tm), 1)
    strict = jnp.where(col < row, 1.0, 0.0).astype(BF16)
    prefix = jnp.dot(strict, onehot.astype(BF16), preferred_element_type=F32)
    cnt_ref[...] = jnp.broadcast_to(jnp.sum(onehot, axis=0, keepdims=True), cnt_ref.shape)

    topi = jnp.zeros(logits.shape, I32)
    gate = jnp.zeros(logits.shape, F32)
    rank = jnp.zeros(logits.shape, I32)
    for kk in range(TOP_K):
        rk = jnp.sum(jnp.where(lane == idxs[kk], prefix, 0.0), axis=1, keepdims=True)
        topi = jnp.where(lane == kk, idxs[kk], topi)
        gate = jnp.where(lane == kk, es[kk] / den, gate)
        rank = jnp.where(lane == kk, rk.astype(I32), rank)
    topi_ref[...] = topi
    gate_ref[...] = gate
    rank_ref[...] = rank


def _merge_route(xf, mod, ya, yb, ga, gb, wa, wb, wo, npost, npre, rw, rb, seq, tm):
    t, d = xf.shape
    n = t // tm
    tiles_per_batch = seq // tm
    proj = lambda w: pl.BlockSpec((tm, w), lambda i: (jnp.minimum(i, n - 1), 0))
    route = lambda w: pl.BlockSpec((tm, w), lambda i: (jnp.maximum(i - 1, 0), 0))
    const = lambda a: pl.BlockSpec(a.shape, lambda i: (0,) * a.ndim)
    lane_out = jax.ShapeDtypeStruct((t, LANES), I32)
    return pl.pallas_call(
        functools.partial(_merge_kernel, tm=tm),
        grid=(n + 1,),
        in_specs=[proj(d),
                  pl.BlockSpec((None, 6, d),
                               lambda i: (jnp.minimum(i, n - 1) // tiles_per_batch, 0, 0)),
                  proj(BRANCH_W), proj(BRANCH_W), proj(d), proj(d),
                  const(wa), const(wb), const(wo), const(npost), const(npre),
                  const(rw), const(rb)],
        out_specs=[proj(d), proj(d), route(LANES), route(LANES), route(LANES),
                   pl.BlockSpec((None, SUBLANES, LANES), lambda i: (jnp.maximum(i - 1, 0), 0, 0))],
        out_shape=[jax.ShapeDtypeStruct((t, d), F32), jax.ShapeDtypeStruct((t, d), BF16),
                   lane_out, jax.ShapeDtypeStruct((t, LANES), F32), lane_out,
                   jax.ShapeDtypeStruct((n, SUBLANES, LANES), F32)],
        scratch_shapes=[pltpu.VMEM((tm, LANES), F32)],
        compiler_params=_cparams(("arbitrary",)),
        name="merge_route",
    )(xf, mod, ya, yb, ga, gb, wa, wb, wo, npost, npre, rw, rb)


def _positions(topi_ref, rank_ref, srow_ref):
    topi = topi_ref[...]
    rank = rank_ref[...]
    lane = lax.broadcasted_iota(I32, topi.shape, 1)
    srow = srow_ref[0:1, :]
    pos = []
    for kk in range(TOP_K):
        base = jnp.sum(jnp.where(lane == topi[:, kk:kk + 1], srow, 0.0), axis=1, keepdims=True)
        pos.append(base.astype(I32) + rank[:, kk:kk + 1])
    return pos


def _wait_rows(n8, make_copy):
    for b in range((SORT_ROWS // RUN_ALIGN).bit_length()):
        @pl.when((n8 >> b) & 1 == 1)
        def _(b=b):
            make_copy(RUN_ALIGN << b).wait()


def _run_copies(tile, l8_ref, make_copy):
    def pieces(e, n8, bits, off):
        for b in bits:
            size = RUN_ALIGN << b
            bit = (n8 >> b) & 1

            @pl.when(bit == 1)
            def _(off=off, size=size):
                make_copy(e, off, size).start()

            off = off + bit * size

    for e in range(N_EXPERTS):
        n8 = l8_ref[tile * N_EXPERTS + e]
        big = (n8 >> RUN_SMALL_BITS) << RUN_SMALL_BITS

        @pl.when(big > 0)
        def _(e=e, n8=n8):
            pieces(e, n8, range(RUN_BITS - 1, RUN_SMALL_BITS - 1, -1), jnp.int32(0))

        pieces(e, n8, range(RUN_SMALL_BITS - 1, -1, -1), big * RUN_ALIGN)


def _dispatch_kernel(start_ref, srun_ref, l8_ref, tot8_ref, eend_ref, elen_ref,
                     h_ref, topi_ref, rank_ref, srow_ref, xs_out,
                     sel0, sel1, srt0, srt1, zbuf, zsem, sem, *, min_blocks):
    i = pl.program_id(0)
    n_tiles = pl.num_programs(0) - 1
    sels, srts = (sel0, sel1), (srt0, srt1)
    tm = h_ref.shape[0]
    half = h_ref.shape[1] // 2
    n_blocks = xs_out.shape[0] // MOE_BLOCK
    n_used = eend_ref[N_EXPERTS - 1] // MOE_BLOCK

    def zero_copy(start):
        start = pl.multiple_of(start, MOE_BLOCK)
        return pltpu.make_async_copy(zbuf, xs_out.at[pl.ds(start, MOE_BLOCK)], zsem)

    def zero_fill(wait):
        for e in range(N_EXPERTS):
            @pl.when(elen_ref[e] > 0)
            def _(e=e):
                cp = zero_copy(eend_ref[e] - MOE_BLOCK)
                cp.wait() if wait else cp.start()
        for blk in range(min_blocks, n_blocks):
            @pl.when(blk >= n_used)
            def _(blk=blk):
                cp = zero_copy(blk * MOE_BLOCK)
                cp.wait() if wait else cp.start()

    @pl.when(i == 0)
    def _():
        zbuf[...] = jnp.zeros_like(zbuf)
        sel0[...] = jnp.zeros_like(sel0)
        sel1[...] = jnp.zeros_like(sel1)
        zero_fill(wait=False)
        zero_fill(wait=True)

    def start_copies(tile, slot):
        def make_copy(e, off, size):
            src = pl.multiple_of(srun_ref[tile * N_EXPERTS + e] + off, RUN_ALIGN)
            dst = pl.multiple_of(start_ref[tile * N_EXPERTS + e] + off, RUN_ALIGN)
            return pltpu.make_async_copy(srts[slot].at[pl.ds(src, size)],
                                         xs_out.at[pl.ds(dst, size)], sem.at[slot])
        _run_copies(tile, l8_ref, make_copy)

    def wait_copies(tile, slot):
        def make_copy(size):
            return pltpu.make_async_copy(srts[slot].at[pl.ds(0, size)],
                                         xs_out.at[pl.ds(0, size)], sem.at[slot])
        _wait_rows(tot8_ref[tile], make_copy)

    for s in range(2):
        @pl.when(i % 2 == s)
        def _(s=s):
            pos = _positions(topi_ref, rank_ref, srow_ref)
            slot = lax.broadcasted_iota(I32, (tm, SORT_ROWS), 1).astype(jnp.int16)
            sel = jnp.zeros((tm, SORT_ROWS), BF16)
            for p in pos:
                sel = jnp.where(slot == p.astype(jnp.int16), jnp.ones((), BF16), sel)
            sels[s][...] = sel
            srt = lax.dot_general(sels[1 - s][...], h_ref[...], (((0,), (0,)), ((), ())),
                                  preferred_element_type=F32)
            lo = pltpu.bitcast(srt[:, :half], jnp.uint32) >> 16
            hi = pltpu.bitcast(srt[:, half:], jnp.uint32) & jnp.uint32(0xFFFF0000)
            srts[1 - s][...] = hi | lo

        @pl.when((i >= 1) & (i % 2 == s))
        def _(s=s):
            start_copies(i - 1, 1 - s)

        @pl.when((i >= 2) & (i % 2 == s))
        def _(s=s):
            wait_copies(i - 2, s)

        @pl.when((i == n_tiles) & (i % 2 == s))
        def _(s=s):
            wait_copies(i - 1, 1 - s)


def _dispatch(h2, topi, rank, srow, tables, n_rows, tm):
    t, d = h2.shape
    n = t // tm
    cur = lambda w: pl.BlockSpec((tm, w), lambda i, *_: (jnp.minimum(i, n - 1), 0))
    grid_spec = pltpu.PrefetchScalarGridSpec(
        num_scalar_prefetch=len(tables),
        grid=(n + 1,),
        in_specs=[pl.BlockSpec((tm, d), lambda i, *_: (jnp.maximum(i - 1, 0), 0)),
                  cur(LANES), cur(LANES),
                  pl.BlockSpec((None, SUBLANES, LANES), lambda i, *_: (jnp.minimum(i, n - 1), 0, 0))],
        out_specs=pl.BlockSpec(memory_space=pl.ANY),
        scratch_shapes=[pltpu.VMEM((tm, SORT_ROWS), BF16), pltpu.VMEM((tm, SORT_ROWS), BF16),
                        pltpu.VMEM((SORT_ROWS, d // 2), jnp.uint32),
                        pltpu.VMEM((SORT_ROWS, d // 2), jnp.uint32),
                        pltpu.VMEM((MOE_BLOCK, d // 2), jnp.uint32),
                        pltpu.SemaphoreType.DMA, pltpu.SemaphoreType.DMA((2,))],
    )
    return pl.pallas_call(
        functools.partial(_dispatch_kernel, min_blocks=t * TOP_K // MOE_BLOCK),
        grid_spec=grid_spec,
        out_shape=jax.ShapeDtypeStruct((n_rows, d // 2), jnp.uint32),
        compiler_params=_cparams(("arbitrary",)),
        name="moe_dispatch",
    )(*tables, h2, topi, rank, srow)


def _expert_kernel(be_ref, nv_ref, x_ref, wgu_ref, bgu_ref, wd_ref, bd_ref, y_ref,
                   wgu_bf, wd_bf):
    j = pl.program_id(0)
    valid = nv_ref[j]
    half = MOE_BLOCK // 2

    @pl.when((j == 0) | (be_ref[j] != be_ref[jnp.maximum(j - 1, 0)]))
    def _():
        wgu_bf[...] = wgu_ref[...].astype(BF16)
        wd_bf[...] = wd_ref[...].astype(BF16)

    def mlp(rows):
        xu = x_ref[:rows, :]
        x_lo = pltpu.bitcast(xu << 16, F32).astype(BF16)
        x_hi = pltpu.bitcast(xu & jnp.uint32(0xFFFF0000), F32).astype(BF16)
        x = jnp.concatenate([x_lo, x_hi], axis=1)
        gu = jnp.dot(x, wgu_bf[...], preferred_element_type=F32) + bgu_ref[...]
        g = jnp.minimum(gu[:, :D_MODEL], SWIGLU_LIMIT)
        u = jnp.clip(gu[:, D_MODEL:], -SWIGLU_LIMIT, SWIGLU_LIMIT)
        act = (u + 1.0) * (g * _sigmoid(g * SWIGLU_ALPHA))
        y_ref[:rows, :] = (jnp.dot(act.astype(BF16), wd_bf[...], preferred_element_type=F32)
                           + bd_ref[...])

    @pl.when(valid > half)
    def _():
        mlp(MOE_BLOCK)

    @pl.when((valid > 0) & (valid <= half))
    def _():
        mlp(half)
        y_ref[half:, :] = jnp.zeros((MOE_BLOCK - half, y_ref.shape[1]), y_ref.dtype)

    @pl.when(valid == 0)
    def _():
        y_ref[...] = jnp.zeros_like(y_ref)


def _experts(xs, block_e, block_rows, wgu, bgu, wd, bd):
    p = xs.shape[0]
    e, d, _ = wgu.shape
    n_blocks = p // MOE_BLOCK
    grid_spec = pltpu.PrefetchScalarGridSpec(
        num_scalar_prefetch=2,
        grid=(n_blocks,),
        in_specs=[pl.BlockSpec((MOE_BLOCK, d // 2), lambda j, be, nu: (j, 0)),
                  pl.BlockSpec((None, d, 2 * d), lambda j, be, nu: (be[j], 0, 0)),
                  pl.BlockSpec((None, 1, 2 * d), lambda j, be, nu: (be[j], 0, 0)),
                  pl.BlockSpec((None, d, d), lambda j, be, nu: (be[j], 0, 0)),
                  pl.BlockSpec((None, 1, d), lambda j, be, nu: (be[j], 0, 0))],
        out_specs=pl.BlockSpec((MOE_BLOCK, d), lambda j, be, nu: (j, 0)),
        scratch_shapes=[pltpu.VMEM((d, 2 * d), BF16), pltpu.VMEM((d, d), BF16)],
    )
    return pl.pallas_call(
        _expert_kernel,
        grid_spec=grid_spec,
        out_shape=jax.ShapeDtypeStruct((p, d), F32),
        compiler_params=_cparams(("arbitrary",)),
        name="moe_experts",
    )(block_e, block_rows, xs, wgu, bgu.reshape(e, 1, 2 * d), wd, bd.reshape(e, 1, d))


def _combine_kernel(start_ref, srun_ref, l8_ref, tot8_ref,
                    y_hbm, topi_ref, rank_ref, srow_ref, gate_ref, x1_ref, mod_ref, nw_ref, o_ref,
                    sel0, sel1, yb0, yb1, sem):
    i = pl.program_id(0)
    n_tiles = pl.num_programs(0) - 1
    tm = x1_ref.shape[0]
    sels, ybufs = (sel0, sel1), (yb0, yb1)

    def start_copies(tile, slot):
        def make_copy(e, off, size):
            src = pl.multiple_of(start_ref[tile * N_EXPERTS + e] + off, RUN_ALIGN)
            dst = pl.multiple_of(srun_ref[tile * N_EXPERTS + e] + off, RUN_ALIGN)
            return pltpu.make_async_copy(y_hbm.at[pl.ds(src, size)],
                                         ybufs[slot].at[pl.ds(dst, size)], sem.at[slot])
        _run_copies(tile, l8_ref, make_copy)

    def wait_copies(tile, slot):
        def make_copy(size):
            return pltpu.make_async_copy(y_hbm.at[pl.ds(0, size)],
                                         ybufs[slot].at[pl.ds(0, size)], sem.at[slot])
        _wait_rows(tot8_ref[tile], make_copy)

    @pl.when(i == 0)
    def _():
        yb0[...] = jnp.zeros_like(yb0)
        yb1[...] = jnp.zeros_like(yb1)
        sel0[...] = jnp.zeros_like(sel0)
        sel1[...] = jnp.zeros_like(sel1)

    for s in range(2):
        @pl.when((i < n_tiles) & (i % 2 == s))
        def _(s=s):
            start_copies(i, s)

        @pl.when((i >= 1) & (i % 2 == s))
        def _(s=s):
            wait_copies(i - 1, 1 - s)

        @pl.when(i % 2 == s)
        def _(s=s):
            pos = _positions(topi_ref, rank_ref, srow_ref)
            gate = gate_ref[...].astype(BF16)
            row = lax.broadcasted_iota(I32, (tm, SORT_ROWS), 1).astype(jnp.int16)
            sel = jnp.zeros((tm, SORT_ROWS), BF16)
            for kk, p in enumerate(pos):
                sel = jnp.where(row == p.astype(jnp.int16), gate[:, kk:kk + 1], sel)
            sels[s][...] = sel
            moe = jnp.dot(sels[1 - s][...], ybufs[1 - s][...].astype(BF16),
                          preferred_element_type=F32)
            g2 = mod_ref[5:6, :]
            o_ref[...] = x1_ref[...] + g2 * _rms(moe, nw_ref[...])


def _combine(y, topi, rank, srow, gate, x1, mod, nw, tables, seq, tm):
    t, d = x1.shape
    n = t // tm
    tiles_per_batch = seq // tm
    cur = lambda w: pl.BlockSpec((tm, w), lambda i, *_: (jnp.minimum(i, n - 1), 0))
    prev = lambda w: pl.BlockSpec((tm, w), lambda i, *_: (jnp.maximum(i - 1, 0), 0))
    grid_spec = pltpu.PrefetchScalarGridSpec(
        num_scalar_prefetch=len(tables),
        grid=(n + 1,),
        in_specs=[pl.BlockSpec(memory_space=pl.ANY),
                  cur(LANES), cur(LANES),
                  pl.BlockSpec((None, SUBLANES, LANES), lambda i, *_: (jnp.minimum(i, n - 1), 0, 0)),
                  cur(LANES), prev(d),
                  pl.BlockSpec((None, 6, d),
                               lambda i, *_: (jnp.maximum(i - 1, 0) // tiles_per_batch, 0, 0)),
                  pl.BlockSpec((1, d), lambda i, *_: (0, 0))],
        out_specs=prev(d),
        scratch_shapes=[pltpu.VMEM((tm, SORT_ROWS), BF16), pltpu.VMEM((tm, SORT_ROWS), BF16),
                        pltpu.VMEM((SORT_ROWS, d), F32), pltpu.VMEM((SORT_ROWS, d), F32),
                        pltpu.SemaphoreType.DMA((2,))],
    )
    return pl.pallas_call(
        _combine_kernel,
        grid_spec=grid_spec,
        out_shape=jax.ShapeDtypeStruct((t, d), F32),
        compiler_params=_cparams(("arbitrary",)),
        name="moe_combine",
    )(*tables, y, topi, rank, srow, gate, x1, mod, nw)


def _rope_tables(seq):
    pos = jnp.arange(seq, dtype=F32)
    inv = ROPE_THETA ** (-jnp.arange(0, DA_HEAD_DIM, 2, dtype=F32) / DA_HEAD_DIM)
    ang = pos[:, None] * inv[None, :]
    cos, sin = jnp.cos(ang), jnp.sin(ang)
    zero = jnp.zeros_like(sin)
    cos_t = jnp.tile(cos, (1, LANES // cos.shape[1]))
    sin_lo = jnp.tile(jnp.concatenate([-sin, zero], axis=1), (1, 2))
    sin_hi = jnp.tile(jnp.concatenate([zero, sin], axis=1), (1, 2))
    return cos_t, sin_lo, sin_hi


def kernel(x, c, w_mod, b_mod, norm_pre_mix, norm_post_mix, w_in, da_lambda_q1, da_lambda_k1,
           da_lambda_q2, da_lambda_k2, da_subln, hg_lb_logits, hg_norm, w_branch_a, w_branch_b,
           w_out, norm_pre_ffn, norm_post_ffn, router_w, router_b, w_gate_up, b_gate_up,
           w_down, b_down):
    bsz, seq, d = x.shape
    depth = w_mod.shape[0]
    assert depth == 1 and d == D_MODEL
    t = bsz * seq
    lyr = 0
    lambda_init = 0.8 - 0.6 * math.exp(-0.3 * lyr)
    cos, slo, shi = _rope_tables(seq)

    xf = x.reshape(t, d)
    mod = _modulation(c, w_mod[lyr], b_mod[lyr]).reshape(bsz, 6, d)

    qT, k, vT, hq, hf, hi, hg, ga, gb = _in_projection(
        xf, mod, norm_pre_mix[lyr][None], w_in[lyr].astype(BF16), cos, slo, shi, seq, tm=512)
    ya = _diff_attention(qT, k, vT, da_lambda_q1[lyr][None], da_lambda_k1[lyr][None],
                         da_lambda_q2[lyr][None], da_lambda_k2[lyr][None], da_subln[lyr],
                         bsz, seq, lambda_init, tq=2048, tk=512)
    yb = _hgrn2(hq, hf, hi, hg, hg_lb_logits[lyr:lyr + 2], hg_norm[lyr][None], bsz, seq, tc=512, nh=4)

    rw = jnp.zeros((d, LANES), BF16).at[:, :N_EXPERTS].set(router_w[lyr].astype(BF16))
    rb = jnp.full((1, LANES), NEG_BIG, F32).at[0, :N_EXPERTS].set(router_b[lyr])
    x1, h2, topi, gate, rank, counts = _merge_route(
        xf, mod, ya, yb, ga, gb, w_branch_a[lyr].astype(BF16), w_branch_b[lyr].astype(BF16),
        w_out[lyr].astype(BF16), norm_post_mix[lyr][None], norm_pre_ffn[lyr][None], rw, rb,
        seq, tm=MOE_TILE)

    n_tiles = t // MOE_TILE
    up = lambda a, m: (a + m - 1) // m * m
    cnt = counts[:, 0, :N_EXPERTS].astype(I32)
    run = up(cnt, RUN_ALIGN)
    srun = jnp.cumsum(run, axis=1) - run
    e_rows = jnp.sum(run, axis=0)
    e_len = up(e_rows, MOE_BLOCK)
    e_end = jnp.cumsum(e_len)
    start = (e_end - e_len)[None, :] + jnp.cumsum(run, axis=0) - run
    n_assign = t * TOP_K
    n_rows = (up(n_assign + n_tiles * N_EXPERTS * (RUN_ALIGN - 1), MOE_BLOCK)
              + N_EXPERTS * MOE_BLOCK)
    n_blocks = n_rows // MOE_BLOCK
    block_start = jnp.arange(n_blocks, dtype=I32) * MOE_BLOCK
    block_e = jnp.minimum(jnp.sum((e_end[None, :] <= block_start[:, None]).astype(I32), axis=1),
                          N_EXPERTS - 1)
    block_rows = jnp.clip((e_end - e_len + e_rows)[block_e] - block_start, 0, MOE_BLOCK).astype(I32)
    srow = jnp.broadcast_to(jnp.pad(srun.astype(F32), ((0, 0), (0, LANES - N_EXPERTS)))[:, None, :],
                            (n_tiles, SUBLANES, LANES))
    run_tables = (start.reshape(-1), srun.reshape(-1), (run // RUN_ALIGN).reshape(-1),
                  jnp.sum(run, axis=1) // RUN_ALIGN)

    xs = _dispatch(h2, topi, rank, srow, run_tables + (e_end.astype(I32), e_len.astype(I32)),
                   n_rows, tm=MOE_TILE)
    y = _experts(xs, block_e, block_rows, w_gate_up[lyr], b_gate_up[lyr], w_down[lyr], b_down[lyr])
    out = _combine(y, topi, rank, srow, gate, x1, mod, norm_post_ffn[lyr][None], run_tables,
                   seq, tm=MOE_TILE)
    return out.reshape(bsz, seq, d)
```

```python
import functools
import math

import jax
import jax.numpy as jnp
from jax import lax
from jax.experimental import pallas as pl
from jax.experimental.pallas import tpu as pltpu

F32 = jnp.float32
BF16 = jnp.bfloat16
I32 = jnp.int32

D_MODEL = 1024
CHUNK = 64
EPS = 1e-6
ROPE_THETA = 10000.0
DA_HEADS = 4
DA_HEAD_DIM = 64
HG_HEADS = 4
HEAD_W = 128
BRANCH_W = 512
V_ROWS = HEAD_W + 16
ATTN_STRIP = 256
N_EXPERTS = 32
TOP_K = 4
SWIGLU_LIMIT = 7.0
SWIGLU_ALPHA = 1.702
MOE_BLOCK = 512
MOE_TILE = 512
LANES = 128
SUBLANES = 8
RUN_ALIGN = SUBLANES
RUN_BITS = (MOE_TILE // RUN_ALIGN).bit_length()
RUN_SMALL_BITS = 4
SORT_ROWS = MOE_TILE * TOP_K + N_EXPERTS * RUN_ALIGN
NEG_BIG = -1e30

VMEM_LIMIT = 56 * 1024 * 1024


def _cparams(sem):
    return pltpu.CompilerParams(dimension_semantics=sem, vmem_limit_bytes=VMEM_LIMIT)


def _rms(x, w):
    return x * lax.rsqrt(jnp.mean(x * x, axis=-1, keepdims=True) + EPS) * w


def _sigmoid(x):
    return 1.0 / (1.0 + jnp.exp(-x))


def _split3(x):
    a = x.astype(BF16)
    r = x - a.astype(F32)
    b = r.astype(BF16)
    c = (r - b.astype(F32)).astype(BF16)
    return a, b, c


def _mod_kernel(c_ref, w_ref, b_ref, o_ref):
    c = c_ref[...]
    ca = c * _sigmoid(c)
    o_ref[...] = jnp.dot(ca, w_ref[...], preferred_element_type=F32,
                         precision=lax.Precision.HIGHEST) + b_ref[...]


def _modulation(c, w_mod, b_mod):
    bsz, d = c.shape
    n = w_mod.shape[1]
    tn = 1536
    return pl.pallas_call(
        _mod_kernel,
        grid=(n // tn,),
        in_specs=[pl.BlockSpec((bsz, d), lambda j: (0, 0)),
                  pl.BlockSpec((d, tn), lambda j: (0, j)),
                  pl.BlockSpec((1, tn), lambda j: (0, j))],
        out_specs=pl.BlockSpec((bsz, tn), lambda j: (0, j)),
        out_shape=jax.ShapeDtypeStruct((bsz, n), F32),
        compiler_params=_cparams(("arbitrary",)),
        name="modulation",
    )(c, w_mod, b_mod.reshape(1, n))


def _rope(x, cos, sin_lo, sin_hi):
    return x * cos + pltpu.roll(x, 96, 1) * sin_lo + pltpu.roll(x, 32, 1) * sin_hi


def _inproj_kernel(x_ref, mod_ref, nw_ref, w_ref, cos_ref, slo_ref, shi_ref,
                   qT_ref, k_ref, vT_ref, hq_ref, hf_ref, hi_ref, hg_ref, ga_ref, gb_ref):
    x = x_ref[...]
    sh1 = mod_ref[0:1, :]
    sc1 = mod_ref[1:2, :]
    h = (_rms(x, nw_ref[...]) * (1.0 + sc1) + sh1).astype(BF16)
    cos, slo, shi = cos_ref[...], slo_ref[...], shi_ref[...]
    scale = DA_HEAD_DIM ** -0.5 * math.log2(math.e)

    def seg(i, width=BRANCH_W):
        return jnp.dot(h, w_ref[:, i:i + width], preferred_element_type=F32)

    q = seg(0)
    k = seg(BRANCH_W)
    for hd in range(DA_HEADS):
        sl = slice(hd * HEAD_W, (hd + 1) * HEAD_W)
        qT_ref[sl, :] = (_rope(q[:, sl], cos, slo, shi) * scale).T.astype(BF16)
        k_ref[:, sl] = _rope(k[:, sl], cos, slo, shi).astype(BF16)
    vT = seg(2 * BRANCH_W).T.astype(BF16)
    ones = jnp.ones((V_ROWS - HEAD_W, vT.shape[1]), BF16)
    for hd in range(DA_HEADS):
        vT_ref[hd * V_ROWS:hd * V_ROWS + HEAD_W, :] = vT[hd * HEAD_W:(hd + 1) * HEAD_W]
        vT_ref[hd * V_ROWS + HEAD_W:(hd + 1) * V_ROWS, :] = ones
    hq = seg(3 * BRANCH_W)
    hq_ref[...] = (hq * _sigmoid(hq)).astype(BF16)
    hf_ref[...] = seg(4 * BRANCH_W)
    hi_ref[...] = seg(5 * BRANCH_W).astype(BF16)
    hg = seg(6 * BRANCH_W)
    hg_ref[...] = (hg * _sigmoid(hg)).astype(BF16)
    for j in range(2):
        ga_ref[:, j * BRANCH_W:(j + 1) * BRANCH_W] = _sigmoid(seg((7 + j) * BRANCH_W)).astype(BF16)
        gb_ref[:, j * BRANCH_W:(j + 1) * BRANCH_W] = _sigmoid(seg((9 + j) * BRANCH_W)).astype(BF16)


def _in_projection(xf, mod, norm_w, w_in_bf, cos, slo, shi, seq, tm):
    t, d = xf.shape
    tiles_per_batch = seq // tm
    bspec = lambda w: pl.BlockSpec((tm, w), lambda i: (i, 0))
    tab = pl.BlockSpec((tm, LANES), lambda i: (i % tiles_per_batch, 0))
    widths = [BRANCH_W] * 7 + [D_MODEL, D_MODEL]
    dtypes = [BF16, BF16, BF16, BF16, F32, BF16, BF16, BF16, BF16]
    out_specs = [bspec(w) for w in widths]
    out_shape = [jax.ShapeDtypeStruct((t, w), dt) for w, dt in zip(widths, dtypes)]
    for idx, rows in ((0, BRANCH_W), (2, DA_HEADS * V_ROWS)):
        out_specs[idx] = pl.BlockSpec((None, rows, tm),
                                      lambda i: (i // tiles_per_batch, 0, i % tiles_per_batch))
        out_shape[idx] = jax.ShapeDtypeStruct((t // seq, rows, seq), BF16)
    return pl.pallas_call(
        _inproj_kernel,
        grid=(t // tm,),
        in_specs=[bspec(d),
                  pl.BlockSpec((None, 6, d), lambda i: (i // tiles_per_batch, 0, 0)),
                  pl.BlockSpec((1, d), lambda i: (0, 0)),
                  pl.BlockSpec(w_in_bf.shape, lambda i: (0, 0), pipeline_mode=pl.Buffered(1)),
                  tab, tab, tab],
        out_specs=out_specs,
        out_shape=out_shape,
        compiler_params=_cparams(("parallel",)),
        name="in_projection",
    )(xf, mod, norm_w, w_in_bf, cos, slo, shi)


def _attn_kernel(lq1_ref, lk1_ref, lq2_ref, lk2_ref, subln_ref, qT_ref, k_ref, vT_ref, o_ref,
                 qq_ref, s0_ref, s1_ref, p0_ref, p1_ref, a0_ref, a1_ref, m_ref, acc_ref,
                 *, tq, tk, lambda_init):
    qi = pl.program_id(2)
    ndiag = tq // tk
    s_refs, p_refs, a_refs = (s0_ref, s1_ref), (p0_ref, p1_ref), (a0_ref, a1_ref)
    qT = qT_ref[...]
    feat = lax.broadcasted_iota(I32, qT.shape, 0)
    zero = jnp.zeros_like(qT)
    qq_ref[:, :tq] = jnp.where(feat < DA_HEAD_DIM, qT, zero)
    qq_ref[:, tq:] = jnp.where(feat >= DA_HEAD_DIM, qT, zero)

    strips = [slice(c * ATTN_STRIP, (c + 1) * ATTN_STRIP) for c in range(2 * tq // ATTN_STRIP)]

    def scores(kb, dst, cs):
        start = pl.multiple_of(kb * tk, tk)
        dst[:, cs] = jnp.dot(k_ref[pl.ds(start, tk), :], qq_ref[:, cs], preferred_element_type=F32)

    def accumulate(kb, slot, cs):
        start = pl.multiple_of(kb * tk, tk)
        pv = jnp.dot(vT_ref[:, pl.ds(start, tk)], p_refs[slot][:, cs], preferred_element_type=F32)
        acc_ref[:, cs] = a_refs[slot][:, cs] * acc_ref[:, cs] + pv

    def softmax(slot, cs, diag):
        def piece(rc):
            sc = s_refs[slot][rc * CHUNK:(rc + 1) * CHUNK, cs]
            if diag is not None:
                qchunk = (lax.broadcasted_iota(I32, (1, ATTN_STRIP), 1) + cs.start) % tq // CHUNK
                sc = jnp.where(qchunk >= diag * (tk // CHUNK) + rc, sc, NEG_BIG)
            return sc

        top = piece(0)
        for rc in range(1, tk // CHUNK):
            top = jnp.maximum(top, piece(rc))
        m_old = m_ref[:, cs]
        m_new = jnp.maximum(m_old, jnp.max(top, axis=0, keepdims=True))
        m_ref[:, cs] = m_new
        a_refs[slot][:, cs] = jnp.exp2(m_old - m_new)
        for rc in range(tk // CHUNK):
            p_refs[slot][rc * CHUNK:(rc + 1) * CHUNK, cs] = jnp.exp2((piece(rc) - m_new).astype(BF16))

    def seen(diag, cs):
        return diag is None or diag < 0 or (cs.start % tq) + ATTN_STRIP > diag * tk

    def stage(kb, slot, diag=None, last=False):
        for cs in strips:
            if seen(None if diag is None else diag - 1, cs):
                accumulate(jnp.maximum(kb - 1, 0), 1 - slot, cs)
            if seen(diag, cs):
                softmax(slot, cs, diag)
            if not last and seen(None if diag is None else diag + 1, cs):
                scores(kb + 1, s_refs[1 - slot], cs)

    p1_ref[...] = jnp.zeros_like(p1_ref)
    a1_ref[...] = jnp.ones_like(a1_ref)
    m_ref[...] = jnp.full(m_ref.shape, NEG_BIG, F32)
    acc_ref[...] = jnp.zeros_like(acc_ref)
    for cs in strips:
        scores(0, s0_ref, cs)

    def pair(i, c):
        stage(2 * i, 0)
        stage(2 * i + 1, 1)
        return c

    first_diag = qi * ndiag
    lax.fori_loop(0, first_diag // 2, pair, 0)
    for dg in range(ndiag):
        stage(first_diag + dg, dg % 2, diag=dg, last=dg == ndiag - 1)
    for cs in strips:
        if seen(ndiag - 1, cs):
            accumulate(first_diag + ndiag - 1, (ndiag - 1) % 2, cs)

    acc = acc_ref[...]
    o = acc[:HEAD_W] / acc[HEAD_W:HEAD_W + 1]
    lam = (jnp.exp(jnp.sum(lq1_ref[...] * lk1_ref[...], axis=1, keepdims=True))
           - jnp.exp(jnp.sum(lq2_ref[...] * lk2_ref[...], axis=1, keepdims=True))
           + lambda_init)
    o = o[:, :tq] - lam * o[:, tq:]
    o = o * lax.rsqrt(jnp.mean(o * o, axis=0, keepdims=True) + EPS) * subln_ref[...]
    o_ref[...] = (o * (1.0 - lambda_init)).T.astype(BF16)


def _diff_attention(qT, k, vT, lq1, lk1, lq2, lk2, subln, bsz, seq, lambda_init, tq, tk):
    assert tq % (2 * tk) == 0
    k3 = k.reshape(bsz, seq, BRANCH_W)
    small = lambda n: pl.BlockSpec((1, n), lambda b, h, i: (0, 0))
    out = pl.pallas_call(
        functools.partial(_attn_kernel, tq=tq, tk=tk, lambda_init=lambda_init),
        grid=(bsz, DA_HEADS, seq // tq),
        in_specs=[small(DA_HEAD_DIM)] * 4 + [
            pl.BlockSpec((HEAD_W, 1), lambda b, h, i: (0, 0)),
            pl.BlockSpec((None, HEAD_W, tq), lambda b, h, i: (b, h, i)),
            pl.BlockSpec((None, seq, HEAD_W), lambda b, h, i: (b, 0, h)),
            pl.BlockSpec((None, V_ROWS, seq), lambda b, h, i: (b, h, 0))],
        out_specs=pl.BlockSpec((None, tq, HEAD_W), lambda b, h, i: (b, i, h)),
        out_shape=jax.ShapeDtypeStruct((bsz, seq, BRANCH_W), BF16),
        scratch_shapes=[pltpu.VMEM((HEAD_W, 2 * tq), BF16),
                        pltpu.VMEM((tk, 2 * tq), F32), pltpu.VMEM((tk, 2 * tq), F32),
                        pltpu.VMEM((tk, 2 * tq), BF16), pltpu.VMEM((tk, 2 * tq), BF16),
                        pltpu.VMEM((1, 2 * tq), F32), pltpu.VMEM((1, 2 * tq), F32),
                        pltpu.VMEM((1, 2 * tq), F32), pltpu.VMEM((V_ROWS, 2 * tq), F32)],
        compiler_params=_cparams(("parallel", "parallel", "arbitrary")),
        name="diff_attention",
    )(lq1, lk1, lq2, lk2, subln.reshape(HEAD_W, 1), qT, k3, vT)
    return out.reshape(bsz * seq, BRANCH_W)


def _hgrn_kernel(lbl_ref, nw_ref, q_ref, f_ref, i_ref, g_ref, o_ref, st_ref, *, tc, nh):
    @pl.when(pl.program_id(2) == 0)
    def _():
        st_ref[...] = jnp.zeros_like(st_ref)

    row = lax.broadcasted_iota(I32, (CHUNK, CHUNK), 0)
    col = lax.broadcasted_iota(I32, (CHUNK, CHUNK), 1)
    causal = col <= row
    tril = jnp.where(causal, 1.0, 0.0).astype(BF16)
    nt = (((1,), (1,)), ((), ()))
    tn = (((0,), (0,)), ((), ()))

    nchunk = tc // CHUNK
    units = [(hh, c) for hh in range(nh) for c in range(nchunk)]
    rows = lambda c: slice(c * CHUNK, (c + 1) * CHUNK)
    q, k, v, parts = [], [], [], []
    for hh in range(nh):
        hs = slice(hh * HEAD_W, (hh + 1) * HEAD_W)
        lg = lbl_ref[:, hs]
        ex = jnp.exp(lg - jnp.max(lg, axis=0, keepdims=True))
        lb = ex[0:1] / jnp.sum(ex, axis=0, keepdims=True)
        f = lb + (1.0 - lb) * _sigmoid(f_ref[:, hs])
        q.append(q_ref[:, hs].astype(F32))
        k.append(1.0 - f)
        v.append(i_ref[:, hs])
        parts.append(_split3(jnp.log(f)))
    pieces = jnp.concatenate(
        [jnp.concatenate([parts[hh][p][rows(c)] for hh, c in units], axis=1) for p in range(3)],
        axis=0)
    gall = jnp.dot(jnp.concatenate([tril] * 3, axis=1), pieces, preferred_element_type=F32)
    gcum = [gall[:, u * HEAD_W:(u + 1) * HEAD_W] for u in range(len(units))]
    gtot = [g[CHUNK - 1:CHUNK, :] for g in gcum]
    q_t = [(q[hh][rows(c)] * jnp.exp(gcum[u])).astype(BF16) for u, (hh, c) in enumerate(units)]
    k_t = [(k[hh][rows(c)] * jnp.exp(-gcum[u])).astype(BF16) for u, (hh, c) in enumerate(units)]
    k_dec = [(k[hh][rows(c)] * jnp.exp(gtot[u] - gcum[u])).astype(BF16)
             for u, (hh, c) in enumerate(units)]
    a = [lax.dot_general(q_t[u], k_t[u], nt, preferred_element_type=F32) for u in range(len(units))]
    ds = [lax.dot_general(v[hh][rows(c)], k_dec[u], tn, preferred_element_type=F32)
          for u, (hh, c) in enumerate(units)]
    a = [jnp.where(causal, x, 0.0).astype(BF16) for x in a]
    st_in = []
    for hh in range(nh):
        st = st_ref[hh]
        for c in range(nchunk):
            u = hh * nchunk + c
            st_in.append(st.astype(BF16))
            st = jnp.exp(gtot[u]) * st + ds[u]
        st_ref[hh] = st
    outs = [jnp.dot(a[u], v[hh][rows(c)], preferred_element_type=F32)
            + lax.dot_general(q_t[u], st_in[u], nt, preferred_element_type=F32)
            for u, (hh, c) in enumerate(units)]
    for hh in range(nh):
        hs = slice(hh * HEAD_W, (hh + 1) * HEAD_W)
        o = jnp.concatenate(outs[hh * nchunk:(hh + 1) * nchunk], axis=0)
        o_ref[:, hs] = (_rms(o, nw_ref[...]) * g_ref[:, hs].astype(F32)).astype(BF16)


def _hgrn2(hq, hf, hi, hg, lb_logits, norm_w, bsz, seq, tc, nh):
    r3 = lambda a: a.reshape(bsz, seq, BRANCH_W)
    blk = pl.BlockSpec((None, tc, nh * HEAD_W), lambda b, h, i: (b, i, h))
    out = pl.pallas_call(
        functools.partial(_hgrn_kernel, tc=tc, nh=nh),
        grid=(bsz, HG_HEADS // nh, seq // tc),
        in_specs=[pl.BlockSpec((2, nh * HEAD_W), lambda b, h, i: (0, h)),
                  pl.BlockSpec((1, HEAD_W), lambda b, h, i: (0, 0)),
                  blk, blk, blk, blk],
        out_specs=blk,
        out_shape=jax.ShapeDtypeStruct((bsz, seq, BRANCH_W), BF16),
        scratch_shapes=[pltpu.VMEM((nh, HEAD_W, HEAD_W), F32)],
        compiler_params=_cparams(("parallel", "parallel", "arbitrary")),
        name="hgrn2",
    )(lb_logits, norm_w, r3(hq), r3(hf), r3(hi), r3(hg))
    return out.reshape(bsz * seq, BRANCH_W)


def _merge_kernel(x_ref, mod_ref, ya_ref, yb_ref, ga_ref, gb_ref, wa_ref, wb_ref, wo_ref,
                  npost_ref, npre_ref, rw_ref, rb_ref,
                  x1_ref, h2_ref, topi_ref, gate_ref, rank_ref, cnt_ref, lg_ref, *, tm):
    @pl.when(pl.program_id(0) == 0)
    def _():
        lg_ref[...] = jnp.zeros_like(lg_ref)

    logits = lg_ref[...]

    g1 = mod_ref[2:3, :]
    sh2 = mod_ref[3:4, :]
    sc2 = mod_ref[4:5, :]
    a = jnp.dot(ya_ref[...], wa_ref[...], preferred_element_type=F32)
    b = jnp.dot(yb_ref[...], wb_ref[...], preferred_element_type=F32)
    merged = ga_ref[...] * a.astype(BF16) + gb_ref[...] * b.astype(BF16)
    y = jnp.dot(merged, wo_ref[...], preferred_element_type=F32)
    x1 = x_ref[...] + g1 * _rms(y, npost_ref[...])
    x1_ref[...] = x1
    h2 = (_rms(x1, npre_ref[...]) * (1.0 + sc2) + sh2).astype(BF16)
    h2_ref[...] = h2
    lg_ref[...] = jnp.dot(h2, rw_ref[...], preferred_element_type=F32) + rb_ref[...]

    lane = lax.broadcasted_iota(I32, logits.shape, 1)
    work = logits
    vals, idxs = [], []
    for _ in range(TOP_K):
        mx = jnp.max(work, axis=1, keepdims=True)
        idx = jnp.min(jnp.where(work == mx, lane, LANES), axis=1, keepdims=True)
        vals.append(mx)
        idxs.append(idx)
        work = jnp.where(lane == idx, -jnp.inf, work)
    es = [jnp.exp(vk - vals[0]) for vk in vals]
    den = es[0] + es[1] + es[2] + es[3]

    onehot = jnp.zeros(logits.shape, F32)
    for idx in idxs:
        onehot = jnp.where(lane == idx, 1.0, onehot)
    row = lax.broadcasted_iota(I32, (tm, tm), 0)
    col = lax.broadcasted_iota(I32, (tm, tm), 1)
    strict = jnp.where(col < row, 1.0, 0.0).astype(BF16)
    prefix = jnp.dot(strict, onehot.astype(BF16), preferred_element_type=F32)
    cnt_ref[...] = jnp.broadcast_to(jnp.sum(onehot, axis=0, keepdims=True), cnt_ref.shape)

    topi = jnp.zeros(logits.shape, I32)
    gate = jnp.zeros(logits.shape, F32)
    rank = jnp.zeros(logits.shape, I32)
    for kk in range(TOP_K):
        rk = jnp.sum(jnp.where(lane == idxs[kk], prefix, 0.0), axis=1, keepdims=True)
        topi = jnp.where(lane == kk, idxs[kk], topi)
        gate = jnp.where(lane == kk, es[kk] / den, gate)
        rank = jnp.where(lane == kk, rk.astype(I32), rank)
    topi_ref[...] = topi
    gate_ref[...] = gate
    rank_ref[...] = rank


def _merge_route(xf, mod, ya, yb, ga, gb, wa, wb, wo, npost, npre, rw, rb, seq, tm):
    t, d = xf.shape
    n = t // tm
    tiles_per_batch = seq // tm
    proj = lambda w: pl.BlockSpec((tm, w), lambda i: (jnp.minimum(i, n - 1), 0))
    route = lambda w: pl.BlockSpec((tm, w), lambda i: (jnp.maximum(i - 1, 0), 0))
    const = lambda a: pl.BlockSpec(a.shape, lambda i: (0,) * a.ndim)
    lane_out = jax.ShapeDtypeStruct((t, LANES), I32)
    return pl.pallas_call(
        functools.partial(_merge_kernel, tm=tm),
        grid=(n + 1,),
        in_specs=[proj(d),
                  pl.BlockSpec((None, 6, d),
                               lambda i: (jnp.minimum(i, n - 1) // tiles_per_batch, 0, 0)),
                  proj(BRANCH_W), proj(BRANCH_W), proj(d), proj(d),
                  const(wa), const(wb), const(wo), const(npost), const(npre),
                  const(rw), const(rb)],
        out_specs=[proj(d), proj(d), route(LANES), route(LANES), route(LANES),
                   pl.BlockSpec((None, SUBLANES, LANES), lambda i: (jnp.maximum(i - 1, 0), 0, 0))],
        out_shape=[jax.ShapeDtypeStruct((t, d), F32), jax.ShapeDtypeStruct((t, d), BF16),
                   lane_out, jax.ShapeDtypeStruct((t, LANES), F32), lane_out,
                   jax.ShapeDtypeStruct((n, SUBLANES, LANES), F32)],
        scratch_shapes=[pltpu.VMEM((tm, LANES), F32)],
        compiler_params=_cparams(("arbitrary",)),
        name="merge_route",
    )(xf, mod, ya, yb, ga, gb, wa, wb, wo, npost, npre, rw, rb)


def _positions(topi_ref, rank_ref, srow_ref):
    topi = topi_ref[...]
    rank = rank_ref[...]
    lane = lax.broadcasted_iota(I32, topi.shape, 1)
    srow = srow_ref[0:1, :]
    pos = []
    for kk in range(TOP_K):
        base = jnp.sum(jnp.where(lane == topi[:, kk:kk + 1], srow, 0.0), axis=1, keepdims=True)
        pos.append(base.astype(I32) + rank[:, kk:kk + 1])
    return pos


def _wait_rows(n8, make_copy):
    for b in range((SORT_ROWS // RUN_ALIGN).bit_length()):
        @pl.when((n8 >> b) & 1 == 1)
        def _(b=b):
            make_copy(RUN_ALIGN << b).wait()


def _run_copies(tile, l8_ref, make_copy):
    def pieces(e, n8, bits, off):
        for b in bits:
            size = RUN_ALIGN << b
            bit = (n8 >> b) & 1

            @pl.when(bit == 1)
            def _(off=off, size=size):
                make_copy(e, off, size).start()

            off = off + bit * size

    for e in range(N_EXPERTS):
        n8 = l8_ref[tile * N_EXPERTS + e]
        big = (n8 >> RUN_SMALL_BITS) << RUN_SMALL_BITS

        @pl.when(big > 0)
        def _(e=e, n8=n8):
            pieces(e, n8, range(RUN_BITS - 1, RUN_SMALL_BITS - 1, -1), jnp.int32(0))

        pieces(e, n8, range(RUN_SMALL_BITS - 1, -1, -1), big * RUN_ALIGN)


def _dispatch_kernel(start_ref, srun_ref, l8_ref, tot8_ref, eend_ref, elen_ref,
                     h_ref, topi_ref, rank_ref, srow_ref, xs_out,
                     sel0, sel1, srt0, srt1, zbuf, zsem, sem, *, min_blocks):
    i = pl.program_id(0)
    n_tiles = pl.num_programs(0) - 1
    sels, srts = (sel0, sel1), (srt0, srt1)
    tm = h_ref.shape[0]
    half = h_ref.shape[1] // 2
    n_blocks = xs_out.shape[0] // MOE_BLOCK
    n_used = eend_ref[N_EXPERTS - 1] // MOE_BLOCK

    def zero_copy(start):
        start = pl.multiple_of(start, MOE_BLOCK)
        return pltpu.make_async_copy(zbuf, xs_out.at[pl.ds(start, MOE_BLOCK)], zsem)

    def zero_fill(wait):
        for e in range(N_EXPERTS):
            @pl.when(elen_ref[e] > 0)
            def _(e=e):
                cp = zero_copy(eend_ref[e] - MOE_BLOCK)
                cp.wait() if wait else cp.start()
        for blk in range(min_blocks, n_blocks):
            @pl.when(blk >= n_used)
            def _(blk=blk):
                cp = zero_copy(blk * MOE_BLOCK)
                cp.wait() if wait else cp.start()

    @pl.when(i == 0)
    def _():
        zbuf[...] = jnp.zeros_like(zbuf)
        sel0[...] = jnp.zeros_like(sel0)
        sel1[...] = jnp.zeros_like(sel1)
        zero_fill(wait=False)
        zero_fill(wait=True)

    def start_copies(tile, slot):
        def make_copy(e, off, size):
            src = pl.multiple_of(srun_ref[tile * N_EXPERTS + e] + off, RUN_ALIGN)
            dst = pl.multiple_of(start_ref[tile * N_EXPERTS + e] + off, RUN_ALIGN)
            return pltpu.make_async_copy(srts[slot].at[pl.ds(src, size)],
                                         xs_out.at[pl.ds(dst, size)], sem.at[slot])
        _run_copies(tile, l8_ref, make_copy)

    def wait_copies(tile, slot):
        def make_copy(size):
            return pltpu.make_async_copy(srts[slot].at[pl.ds(0, size)],
                                         xs_out.at[pl.ds(0, size)], sem.at[slot])
        _wait_rows(tot8_ref[tile], make_copy)

    for s in range(2):
        @pl.when(i % 2 == s)
        def _(s=s):
            pos = _positions(topi_ref, rank_ref, srow_ref)
            slot = lax.broadcasted_iota(I32, (tm, SORT_ROWS), 1).astype(jnp.int16)
            sel = jnp.zeros((tm, SORT_ROWS), BF16)
            for p in pos:
                sel = jnp.where(slot == p.astype(jnp.int16), jnp.ones((), BF16), sel)
            sels[s][...] = sel
            srt = lax.dot_general(sels[1 - s][...], h_ref[...], (((0,), (0,)), ((), ())),
                                  preferred_element_type=F32)
            lo = pltpu.bitcast(srt[:, :half], jnp.uint32) >> 16
            hi = pltpu.bitcast(srt[:, half:], jnp.uint32) & jnp.uint32(0xFFFF0000)
            srts[1 - s][...] = hi | lo

        @pl.when((i >= 1) & (i % 2 == s))
        def _(s=s):
            start_copies(i - 1, 1 - s)

        @pl.when((i >= 2) & (i % 2 == s))
        def _(s=s):
            wait_copies(i - 2, s)

        @pl.when((i == n_tiles) & (i % 2 == s))
        def _(s=s):
            wait_copies(i - 1, 1 - s)


def _dispatch(h2, topi, rank, srow, tables, n_rows, tm):
    t, d = h2.shape
    n = t // tm
    cur = lambda w: pl.BlockSpec((tm, w), lambda i, *_: (jnp.minimum(i, n - 1), 0))
    grid_spec = pltpu.PrefetchScalarGridSpec(
        num_scalar_prefetch=len(tables),
        grid=(n + 1,),
        in_specs=[pl.BlockSpec((tm, d), lambda i, *_: (jnp.maximum(i - 1, 0), 0)),
                  cur(LANES), cur(LANES),
                  pl.BlockSpec((None, SUBLANES, LANES), lambda i, *_: (jnp.minimum(i, n - 1), 0, 0))],
        out_specs=pl.BlockSpec(memory_space=pl.ANY),
        scratch_shapes=[pltpu.VMEM((tm, SORT_ROWS), BF16), pltpu.VMEM((tm, SORT_ROWS), BF16),
                        pltpu.VMEM((SORT_ROWS, d // 2), jnp.uint32),
                        pltpu.VMEM((SORT_ROWS, d // 2), jnp.uint32),
                        pltpu.VMEM((MOE_BLOCK, d // 2), jnp.uint32),
                        pltpu.SemaphoreType.DMA, pltpu.SemaphoreType.DMA((2,))],
    )
    return pl.pallas_call(
        functools.partial(_dispatch_kernel, min_blocks=t * TOP_K // MOE_BLOCK),
        grid_spec=grid_spec,
        out_shape=jax.ShapeDtypeStruct((n_rows, d // 2), jnp.uint32),
        compiler_params=_cparams(("arbitrary",)),
        name="moe_dispatch",
    )(*tables, h2, topi, rank, srow)


def _expert_kernel(be_ref, nu_ref, x_ref, wgu_ref, bgu_ref, wd_ref, bd_ref, y_ref,
                   wgu_bf, wd_bf):
    j = pl.program_id(0)

    @pl.when((j == 0) | (be_ref[j] != be_ref[jnp.maximum(j - 1, 0)]))
    def _():
        wgu_bf[...] = wgu_ref[...].astype(BF16)
        wd_bf[...] = wd_ref[...].astype(BF16)

    @pl.when(j < nu_ref[0])
    def _():
        xu = x_ref[...]
        x_lo = pltpu.bitcast(xu << 16, F32).astype(BF16)
        x_hi = pltpu.bitcast(xu & jnp.uint32(0xFFFF0000), F32).astype(BF16)
        x = jnp.concatenate([x_lo, x_hi], axis=1)
        gu = jnp.dot(x, wgu_bf[...], preferred_element_type=F32) + bgu_ref[...]
        g = jnp.minimum(gu[:, :D_MODEL], SWIGLU_LIMIT)
        u = jnp.clip(gu[:, D_MODEL:], -SWIGLU_LIMIT, SWIGLU_LIMIT)
        act = (u + 1.0) * (g * _sigmoid(g * SWIGLU_ALPHA))
        y_ref[...] = jnp.dot(act.astype(BF16), wd_bf[...], preferred_element_type=F32) + bd_ref[...]

    @pl.when(j >= nu_ref[0])
    def _():
        y_ref[...] = jnp.zeros_like(y_ref)


def _experts(xs, block_e, n_used, wgu, bgu, wd, bd):
    p = xs.shape[0]
    e, d, _ = wgu.shape
    n_blocks = p // MOE_BLOCK
    grid_spec = pltpu.PrefetchScalarGridSpec(
        num_scalar_prefetch=2,
        grid=(n_blocks,),
        in_specs=[pl.BlockSpec((MOE_BLOCK, d // 2), lambda j, be, nu: (j, 0)),
                  pl.BlockSpec((None, d, 2 * d), lambda j, be, nu: (be[j], 0, 0)),
                  pl.BlockSpec((None, 1, 2 * d), lambda j, be, nu: (be[j], 0, 0)),
                  pl.BlockSpec((None, d, d), lambda j, be, nu: (be[j], 0, 0)),
                  pl.BlockSpec((None, 1, d), lambda j, be, nu: (be[j], 0, 0))],
        out_specs=pl.BlockSpec((MOE_BLOCK, d), lambda j, be, nu: (j, 0)),
        scratch_shapes=[pltpu.VMEM((d, 2 * d), BF16), pltpu.VMEM((d, d), BF16)],
    )
    return pl.pallas_call(
        _expert_kernel,
        grid_spec=grid_spec,
        out_shape=jax.ShapeDtypeStruct((p, d), F32),
        compiler_params=_cparams(("arbitrary",)),
        name="moe_experts",
    )(block_e, n_used, xs, wgu, bgu.reshape(e, 1, 2 * d), wd, bd.reshape(e, 1, d))


def _combine_kernel(start_ref, srun_ref, l8_ref, tot8_ref,
                    y_hbm, topi_ref, rank_ref, srow_ref, gate_ref, x1_ref, mod_ref, nw_ref, o_ref,
                    sel0, sel1, yb0, yb1, sem):
    i = pl.program_id(0)
    n_tiles = pl.num_programs(0) - 1
    tm = x1_ref.shape[0]
    sels, ybufs = (sel0, sel1), (yb0, yb1)

    def start_copies(tile, slot):
        def make_copy(e, off, size):
            src = pl.multiple_of(start_ref[tile * N_EXPERTS + e] + off, RUN_ALIGN)
            dst = pl.multiple_of(srun_ref[tile * N_EXPERTS + e] + off, RUN_ALIGN)
            return pltpu.make_async_copy(y_hbm.at[pl.ds(src, size)],
                                         ybufs[slot].at[pl.ds(dst, size)], sem.at[slot])
        _run_copies(tile, l8_ref, make_copy)

    def wait_copies(tile, slot):
        def make_copy(size):
            return pltpu.make_async_copy(y_hbm.at[pl.ds(0, size)],
                                         ybufs[slot].at[pl.ds(0, size)], sem.at[slot])
        _wait_rows(tot8_ref[tile], make_copy)

    @pl.when(i == 0)
    def _():
        yb0[...] = jnp.zeros_like(yb0)
        yb1[...] = jnp.zeros_like(yb1)
        sel0[...] = jnp.zeros_like(sel0)
        sel1[...] = jnp.zeros_like(sel1)

    for s in range(2):
        @pl.when((i < n_tiles) & (i % 2 == s))
        def _(s=s):
            start_copies(i, s)

        @pl.when((i >= 1) & (i % 2 == s))
        def _(s=s):
            wait_copies(i - 1, 1 - s)

        @pl.when(i % 2 == s)
        def _(s=s):
            pos = _positions(topi_ref, rank_ref, srow_ref)
            gate = gate_ref[...].astype(BF16)
            row = lax.broadcasted_iota(I32, (tm, SORT_ROWS), 1).astype(jnp.int16)
            sel = jnp.zeros((tm, SORT_ROWS), BF16)
            for kk, p in enumerate(pos):
                sel = jnp.where(row == p.astype(jnp.int16), gate[:, kk:kk + 1], sel)
            sels[s][...] = sel
            moe = jnp.dot(sels[1 - s][...], ybufs[1 - s][...].astype(BF16),
                          preferred_element_type=F32)
            g2 = mod_ref[5:6, :]
            o_ref[...] = x1_ref[...] + g2 * _rms(moe, nw_ref[...])


def _combine(y, topi, rank, srow, gate, x1, mod, nw, tables, seq, tm):
    t, d = x1.shape
    n = t // tm
    tiles_per_batch = seq // tm
    cur = lambda w: pl.BlockSpec((tm, w), lambda i, *_: (jnp.minimum(i, n - 1), 0))
    prev = lambda w: pl.BlockSpec((tm, w), lambda i, *_: (jnp.maximum(i - 1, 0), 0))
    grid_spec = pltpu.PrefetchScalarGridSpec(
        num_scalar_prefetch=len(tables),
        grid=(n + 1,),
        in_specs=[pl.BlockSpec(memory_space=pl.ANY),
                  cur(LANES), cur(LANES),
                  pl.BlockSpec((None, SUBLANES, LANES), lambda i, *_: (jnp.minimum(i, n - 1), 0, 0)),
                  cur(LANES), prev(d),
                  pl.BlockSpec((None, 6, d),
                               lambda i, *_: (jnp.maximum(i - 1, 0) // tiles_per_batch, 0, 0)),
                  pl.BlockSpec((1, d), lambda i, *_: (0, 0))],
        out_specs=prev(d),
        scratch_shapes=[pltpu.VMEM((tm, SORT_ROWS), BF16), pltpu.VMEM((tm, SORT_ROWS), BF16),
                        pltpu.VMEM((SORT_ROWS, d), F32), pltpu.VMEM((SORT_ROWS, d), F32),
                        pltpu.SemaphoreType.DMA((2,))],
    )
    return pl.pallas_call(
        _combine_kernel,
        grid_spec=grid_spec,
        out_shape=jax.ShapeDtypeStruct((t, d), F32),
        compiler_params=_cparams(("arbitrary",)),
        name="moe_combine",
    )(*tables, y, topi, rank, srow, gate, x1, mod, nw)


def _rope_tables(seq):
    pos = jnp.arange(seq, dtype=F32)
    inv = ROPE_THETA ** (-jnp.arange(0, DA_HEAD_DIM, 2, dtype=F32) / DA_HEAD_DIM)
    ang = pos[:, None] * inv[None, :]
    cos, sin = jnp.cos(ang), jnp.sin(ang)
    zero = jnp.zeros_like(sin)
    cos_t = jnp.tile(cos, (1, LANES // cos.shape[1]))
    sin_lo = jnp.tile(jnp.concatenate([-sin, zero], axis=1), (1, 2))
    sin_hi = jnp.tile(jnp.concatenate([zero, sin], axis=1), (1, 2))
    return cos_t, sin_lo, sin_hi


def kernel(x, c, w_mod, b_mod, norm_pre_mix, norm_post_mix, w_in, da_lambda_q1, da_lambda_k1,
           da_lambda_q2, da_lambda_k2, da_subln, hg_lb_logits, hg_norm, w_branch_a, w_branch_b,
           w_out, norm_pre_ffn, norm_post_ffn, router_w, router_b, w_gate_up, b_gate_up,
           w_down, b_down):
    bsz, seq, d = x.shape
    depth = w_mod.shape[0]
    assert depth == 1 and d == D_MODEL
    t = bsz * seq
    lyr = 0
    lambda_init = 0.8 - 0.6 * math.exp(-0.3 * lyr)
    cos, slo, shi = _rope_tables(seq)

    xf = x.reshape(t, d)
    mod = _modulation(c, w_mod[lyr], b_mod[lyr]).reshape(bsz, 6, d)

    qT, k, vT, hq, hf, hi, hg, ga, gb = _in_projection(
        xf, mod, norm_pre_mix[lyr][None], w_in[lyr].astype(BF16), cos, slo, shi, seq, tm=512)
    ya = _diff_attention(qT, k, vT, da_lambda_q1[lyr][None], da_lambda_k1[lyr][None],
                         da_lambda_q2[lyr][None], da_lambda_k2[lyr][None], da_subln[lyr],
                         bsz, seq, lambda_init, tq=2048, tk=512)
    yb = _hgrn2(hq, hf, hi, hg, hg_lb_logits[lyr:lyr + 2], hg_norm[lyr][None], bsz, seq, tc=512, nh=4)

    rw = jnp.zeros((d, LANES), BF16).at[:, :N_EXPERTS].set(router_w[lyr].astype(BF16))
    rb = jnp.full((1, LANES), NEG_BIG, F32).at[0, :N_EXPERTS].set(router_b[lyr])
    x1, h2, topi, gate, rank, counts = _merge_route(
        xf, mod, ya, yb, ga, gb, w_branch_a[lyr].astype(BF16), w_branch_b[lyr].astype(BF16),
        w_out[lyr].astype(BF16), norm_post_mix[lyr][None], norm_pre_ffn[lyr][None], rw, rb,
        seq, tm=MOE_TILE)

    n_tiles = t // MOE_TILE
    up = lambda a, m: (a + m - 1) // m * m
    cnt = counts[:, 0, :N_EXPERTS].astype(I32)
    run = up(cnt, RUN_ALIGN)
    srun = jnp.cumsum(run, axis=1) - run
    e_len = up(jnp.sum(run, axis=0), MOE_BLOCK)
    e_end = jnp.cumsum(e_len)
    start = (e_end - e_len)[None, :] + jnp.cumsum(run, axis=0) - run
    n_assign = t * TOP_K
    n_rows = (up(n_assign + n_tiles * N_EXPERTS * (RUN_ALIGN - 1), MOE_BLOCK)
              + N_EXPERTS * MOE_BLOCK)
    n_blocks = n_rows // MOE_BLOCK
    block_start = jnp.arange(n_blocks, dtype=I32) * MOE_BLOCK
    block_e = jnp.minimum(jnp.sum((e_end[None, :] <= block_start[:, None]).astype(I32), axis=1),
                          N_EXPERTS - 1)
    n_used = (e_end[-1:] // MOE_BLOCK).astype(I32)
    srow = jnp.broadcast_to(jnp.pad(srun.astype(F32), ((0, 0), (0, LANES - N_EXPERTS)))[:, None, :],
                            (n_tiles, SUBLANES, LANES))
    run_tables = (start.reshape(-1), srun.reshape(-1), (run // RUN_ALIGN).reshape(-1),
                  jnp.sum(run, axis=1) // RUN_ALIGN)

    xs = _dispatch(h2, topi, rank, srow, run_tables + (e_end.astype(I32), e_len.astype(I32)),
                   n_rows, tm=MOE_TILE)
    y = _experts(xs, block_e, n_used, w_gate_up[lyr], b_gate_up[lyr], w_down[lyr], b_down[lyr])
    out = _combine(y, topi, rank, srow, gate, x1, mod, norm_post_ffn[lyr][None], run_tables,
                   seq, tm=MOE_TILE)
    return out.reshape(bsz, seq, d)
```

```python
import functools
import math

import jax
import jax.numpy as jnp
from jax import lax
from jax.experimental import pallas as pl
from jax.experimental.pallas import tpu as pltpu

F32 = jnp.float32
BF16 = jnp.bfloat16
I32 = jnp.int32

D_MODEL = 1024
CHUNK = 64
EPS = 1e-6
ROPE_THETA = 10000.0
DA_HEADS = 4
DA_HEAD_DIM = 64
HG_HEADS = 4
HEAD_W = 128
BRANCH_W = 512
V_ROWS = HEAD_W + 16
ATTN_STRIP = 256
N_EXPERTS = 32
TOP_K = 4
SWIGLU_LIMIT = 7.0
SWIGLU_ALPHA = 1.702
MOE_BLOCK = 512
MOE_TILE = 512
LANES = 128
SUBLANES = 8
RUN_ALIGN = SUBLANES
RUN_BITS = (MOE_TILE // RUN_ALIGN).bit_length()
RUN_SMALL_BITS = 4
SORT_ROWS = MOE_TILE * TOP_K + N_EXPERTS * RUN_ALIGN
NEG_BIG = -1e30

VMEM_LIMIT = 56 * 1024 * 1024


def _cparams(sem):
    return pltpu.CompilerParams(dimension_semantics=sem, vmem_limit_bytes=VMEM_LIMIT)


def _rms(x, w):
    return x * lax.rsqrt(jnp.mean(x * x, axis=-1, keepdims=True) + EPS) * w


def _sigmoid(x):
    return 1.0 / (1.0 + jnp.exp(-x))


def _split3(x):
    a = x.astype(BF16)
    r = x - a.astype(F32)
    b = r.astype(BF16)
    c = (r - b.astype(F32)).astype(BF16)
    return a, b, c


def _mod_kernel(c_ref, w_ref, b_ref, o_ref):
    c = c_ref[...]
    ca = c * _sigmoid(c)
    o_ref[...] = jnp.dot(ca, w_ref[...], preferred_element_type=F32,
                         precision=lax.Precision.HIGHEST) + b_ref[...]


def _modulation(c, w_mod, b_mod):
    bsz, d = c.shape
    n = w_mod.shape[1]
    tn = 1536
    return pl.pallas_call(
        _mod_kernel,
        grid=(n // tn,),
        in_specs=[pl.BlockSpec((bsz, d), lambda j: (0, 0)),
                  pl.BlockSpec((d, tn), lambda j: (0, j)),
                  pl.BlockSpec((1, tn), lambda j: (0, j))],
        out_specs=pl.BlockSpec((bsz, tn), lambda j: (0, j)),
        out_shape=jax.ShapeDtypeStruct((bsz, n), F32),
        compiler_params=_cparams(("arbitrary",)),
        name="modulation",
    )(c, w_mod, b_mod.reshape(1, n))


def _rope(x, cos, sin_lo, sin_hi):
    return x * cos + pltpu.roll(x, 96, 1) * sin_lo + pltpu.roll(x, 32, 1) * sin_hi


def _inproj_kernel(x_ref, mod_ref, nw_ref, w_ref, cos_ref, slo_ref, shi_ref,
                   qT_ref, k_ref, vT_ref, hq_ref, hf_ref, hi_ref, hg_ref, ga_ref, gb_ref):
    x = x_ref[...]
    sh1 = mod_ref[0:1, :]
    sc1 = mod_ref[1:2, :]
    h = (_rms(x, nw_ref[...]) * (1.0 + sc1) + sh1).astype(BF16)
    cos, slo, shi = cos_ref[...], slo_ref[...], shi_ref[...]
    scale = DA_HEAD_DIM ** -0.5 * math.log2(math.e)

    def seg(i, width=BRANCH_W):
        return jnp.dot(h, w_ref[:, i:i + width], preferred_element_type=F32)

    q = seg(0)
    k = seg(BRANCH_W)
    for hd in range(DA_HEADS):
        sl = slice(hd * HEAD_W, (hd + 1) * HEAD_W)
        qT_ref[sl, :] = (_rope(q[:, sl], cos, slo, shi) * scale).T.astype(BF16)
        k_ref[:, sl] = _rope(k[:, sl], cos, slo, shi).astype(BF16)
    vT = seg(2 * BRANCH_W).T.astype(BF16)
    ones = jnp.ones((V_ROWS - HEAD_W, vT.shape[1]), BF16)
    for hd in range(DA_HEADS):
        vT_ref[hd * V_ROWS:hd * V_ROWS + HEAD_W, :] = vT[hd * HEAD_W:(hd + 1) * HEAD_W]
        vT_ref[hd * V_ROWS + HEAD_W:(hd + 1) * V_ROWS, :] = ones
    hq = seg(3 * BRANCH_W)
    hq_ref[...] = (hq * _sigmoid(hq)).astype(BF16)
    hf_ref[...] = seg(4 * BRANCH_W)
    hi_ref[...] = seg(5 * BRANCH_W).astype(BF16)
    hg = seg(6 * BRANCH_W)
    hg_ref[...] = (hg * _sigmoid(hg)).astype(BF16)
    for j in range(2):
        ga_ref[:, j * BRANCH_W:(j + 1) * BRANCH_W] = _sigmoid(seg((7 + j) * BRANCH_W)).astype(BF16)
        gb_ref[:, j * BRANCH_W:(j + 1) * BRANCH_W] = _sigmoid(seg((9 + j) * BRANCH_W)).astype(BF16)


def _in_projection(xf, mod, norm_w, w_in_bf, cos, slo, shi, seq, tm):
    t, d = xf.shape
    tiles_per_batch = seq // tm
    bspec = lambda w: pl.BlockSpec((tm, w), lambda i: (i, 0))
    tab = pl.BlockSpec((tm, LANES), lambda i: (i % tiles_per_batch, 0))
    widths = [BRANCH_W] * 7 + [D_MODEL, D_MODEL]
    dtypes = [BF16, BF16, BF16, BF16, F32, BF16, BF16, BF16, BF16]
    out_specs = [bspec(w) for w in widths]
    out_shape = [jax.ShapeDtypeStruct((t, w), dt) for w, dt in zip(widths, dtypes)]
    for idx, rows in ((0, BRANCH_W), (2, DA_HEADS * V_ROWS)):
        out_specs[idx] = pl.BlockSpec((None, rows, tm),
                                      lambda i: (i // tiles_per_batch, 0, i % tiles_per_batch))
        out_shape[idx] = jax.ShapeDtypeStruct((t // seq, rows, seq), BF16)
    return pl.pallas_call(
        _inproj_kernel,
        grid=(t // tm,),
        in_specs=[bspec(d),
                  pl.BlockSpec((None, 6, d), lambda i: (i // tiles_per_batch, 0, 0)),
                  pl.BlockSpec((1, d), lambda i: (0, 0)),
                  pl.BlockSpec(w_in_bf.shape, lambda i: (0, 0), pipeline_mode=pl.Buffered(1)),
                  tab, tab, tab],
        out_specs=out_specs,
        out_shape=out_shape,
        compiler_params=_cparams(("parallel",)),
        name="in_projection",
    )(xf, mod, norm_w, w_in_bf, cos, slo, shi)


def _attn_kernel(lq1_ref, lk1_ref, lq2_ref, lk2_ref, subln_ref, qT_ref, k_ref, vT_ref, o_ref,
                 qq_ref, s0_ref, s1_ref, p0_ref, p1_ref, a0_ref, a1_ref, m_ref, acc_ref,
                 *, tq, tk, lambda_init):
    qi = pl.program_id(2)
    ndiag = tq // tk
    s_refs, p_refs, a_refs = (s0_ref, s1_ref), (p0_ref, p1_ref), (a0_ref, a1_ref)
    qT = qT_ref[...]
    feat = lax.broadcasted_iota(I32, qT.shape, 0)
    zero = jnp.zeros_like(qT)
    qq_ref[:, :tq] = jnp.where(feat < DA_HEAD_DIM, qT, zero)
    qq_ref[:, tq:] = jnp.where(feat >= DA_HEAD_DIM, qT, zero)

    strips = [slice(c * ATTN_STRIP, (c + 1) * ATTN_STRIP) for c in range(2 * tq // ATTN_STRIP)]

    def scores(kb, dst, cs):
        start = pl.multiple_of(kb * tk, tk)
        dst[:, cs] = jnp.dot(k_ref[pl.ds(start, tk), :], qq_ref[:, cs], preferred_element_type=F32)

    def accumulate(kb, slot, cs):
        start = pl.multiple_of(kb * tk, tk)
        pv = jnp.dot(vT_ref[:, pl.ds(start, tk)], p_refs[slot][:, cs], preferred_element_type=F32)
        acc_ref[:, cs] = a_refs[slot][:, cs] * acc_ref[:, cs] + pv

    def softmax(slot, cs, diag):
        def piece(rc):
            sc = s_refs[slot][rc * CHUNK:(rc + 1) * CHUNK, cs]
            if diag is not None:
                qchunk = (lax.broadcasted_iota(I32, (1, ATTN_STRIP), 1) + cs.start) % tq // CHUNK
                sc = jnp.where(qchunk >= diag * (tk // CHUNK) + rc, sc, NEG_BIG)
            return sc

        top = piece(0)
        for rc in range(1, tk // CHUNK):
            top = jnp.maximum(top, piece(rc))
        m_old = m_ref[:, cs]
        m_new = jnp.maximum(m_old, jnp.max(top, axis=0, keepdims=True))
        m_ref[:, cs] = m_new
        a_refs[slot][:, cs] = jnp.exp2(m_old - m_new)
        for rc in range(tk // CHUNK):
            p_refs[slot][rc * CHUNK:(rc + 1) * CHUNK, cs] = jnp.exp2((piece(rc) - m_new).astype(BF16))

    def seen(diag, cs):
        return diag is None or diag < 0 or (cs.start % tq) + ATTN_STRIP > diag * tk

    def stage(kb, slot, diag=None, last=False):
        for cs in strips:
            if seen(None if diag is None else diag - 1, cs):
                accumulate(jnp.maximum(kb - 1, 0), 1 - slot, cs)
            if seen(diag, cs):
                softmax(slot, cs, diag)
            if not last and seen(None if diag is None else diag + 1, cs):
                scores(kb + 1, s_refs[1 - slot], cs)

    p1_ref[...] = jnp.zeros_like(p1_ref)
    a1_ref[...] = jnp.ones_like(a1_ref)
    m_ref[...] = jnp.full(m_ref.shape, NEG_BIG, F32)
    acc_ref[...] = jnp.zeros_like(acc_ref)
    for cs in strips:
        scores(0, s0_ref, cs)

    def pair(i, c):
        stage(2 * i, 0)
        stage(2 * i + 1, 1)
        return c

    first_diag = qi * ndiag
    lax.fori_loop(0, first_diag // 2, pair, 0)
    for dg in range(ndiag):
        stage(first_diag + dg, dg % 2, diag=dg, last=dg == ndiag - 1)
    for cs in strips:
        if seen(ndiag - 1, cs):
            accumulate(first_diag + ndiag - 1, (ndiag - 1) % 2, cs)

    acc = acc_ref[...]
    o = acc[:HEAD_W] / acc[HEAD_W:HEAD_W + 1]
    lam = (jnp.exp(jnp.sum(lq1_ref[...] * lk1_ref[...], axis=1, keepdims=True))
           - jnp.exp(jnp.sum(lq2_ref[...] * lk2_ref[...], axis=1, keepdims=True))
           + lambda_init)
    o = o[:, :tq] - lam * o[:, tq:]
    o = o * lax.rsqrt(jnp.mean(o * o, axis=0, keepdims=True) + EPS) * subln_ref[...]
    o_ref[...] = (o * (1.0 - lambda_init)).T.astype(BF16)


def _diff_attention(qT, k, vT, lq1, lk1, lq2, lk2, subln, bsz, seq, lambda_init, tq, tk):
    assert tq % (2 * tk) == 0
    k3 = k.reshape(bsz, seq, BRANCH_W)
    small = lambda n: pl.BlockSpec((1, n), lambda b, h, i: (0, 0))
    out = pl.pallas_call(
        functools.partial(_attn_kernel, tq=tq, tk=tk, lambda_init=lambda_init),
        grid=(bsz, DA_HEADS, seq // tq),
        in_specs=[small(DA_HEAD_DIM)] * 4 + [
            pl.BlockSpec((HEAD_W, 1), lambda b, h, i: (0, 0)),
            pl.BlockSpec((None, HEAD_W, tq), lambda b, h, i: (b, h, i)),
            pl.BlockSpec((None, seq, HEAD_W), lambda b, h, i: (b, 0, h)),
            pl.BlockSpec((None, V_ROWS, seq), lambda b, h, i: (b, h, 0))],
        out_specs=pl.BlockSpec((None, tq, HEAD_W), lambda b, h, i: (b, i, h)),
        out_shape=jax.ShapeDtypeStruct((bsz, seq, BRANCH_W), BF16),
        scratch_shapes=[pltpu.VMEM((HEAD_W, 2 * tq), BF16),
                        pltpu.VMEM((tk, 2 * tq), F32), pltpu.VMEM((tk, 2 * tq), F32),
                        pltpu.VMEM((tk, 2 * tq), BF16), pltpu.VMEM((tk, 2 * tq), BF16),
                        pltpu.VMEM((1, 2 * tq), F32), pltpu.VMEM((1, 2 * tq), F32),
                        pltpu.VMEM((1, 2 * tq), F32), pltpu.VMEM((V_ROWS, 2 * tq), F32)],
        compiler_params=_cparams(("parallel", "parallel", "arbitrary")),
        name="diff_attention",
    )(lq1, lk1, lq2, lk2, subln.reshape(HEAD_W, 1), qT, k3, vT)
    return out.reshape(bsz * seq, BRANCH_W)


def _hgrn_kernel(lbl_ref, nw_ref, q_ref, f_ref, i_ref, g_ref, o_ref, st_ref, *, tc, nh):
    @pl.when(pl.program_id(2) == 0)
    def _():
        st_ref[...] = jnp.zeros_like(st_ref)

    row = lax.broadcasted_iota(I32, (CHUNK, CHUNK), 0)
    col = lax.broadcasted_iota(I32, (CHUNK, CHUNK), 1)
    causal = col <= row
    tril = jnp.where(causal, 1.0, 0.0).astype(BF16)
    nt = (((1,), (1,)), ((), ()))
    tn = (((0,), (0,)), ((), ()))

    nchunk = tc // CHUNK
    units = [(hh, c) for hh in range(nh) for c in range(nchunk)]
    rows = lambda c: slice(c * CHUNK, (c + 1) * CHUNK)
    q, k, v, parts = [], [], [], []
    for hh in range(nh):
        hs = slice(hh * HEAD_W, (hh + 1) * HEAD_W)
        lg = lbl_ref[:, hs]
        ex = jnp.exp(lg - jnp.max(lg, axis=0, keepdims=True))
        lb = ex[0:1] / jnp.sum(ex, axis=0, keepdims=True)
        f = lb + (1.0 - lb) * _sigmoid(f_ref[:, hs])
        q.append(q_ref[:, hs].astype(F32))
        k.append(1.0 - f)
        v.append(i_ref[:, hs])
        parts.append(_split3(jnp.log(f)))
    pieces = jnp.concatenate(
        [jnp.concatenate([parts[hh][p][rows(c)] for hh, c in units], axis=1) for p in range(3)],
        axis=0)
    gall = jnp.dot(jnp.concatenate([tril] * 3, axis=1), pieces, preferred_element_type=F32)
    gcum = [gall[:, u * HEAD_W:(u + 1) * HEAD_W] for u in range(len(units))]
    gtot = [g[CHUNK - 1:CHUNK, :] for g in gcum]
    q_t = [(q[hh][rows(c)] * jnp.exp(gcum[u])).astype(BF16) for u, (hh, c) in enumerate(units)]
    k_t = [(k[hh][rows(c)] * jnp.exp(-gcum[u])).astype(BF16) for u, (hh, c) in enumerate(units)]
    k_dec = [(k[hh][rows(c)] * jnp.exp(gtot[u] - gcum[u])).astype(BF16)
             for u, (hh, c) in enumerate(units)]
    a = [lax.dot_general(q_t[u], k_t[u], nt, preferred_element_type=F32) for u in range(len(units))]
    ds = [lax.dot_general(v[hh][rows(c)], k_dec[u], tn, preferred_element_type=F32)
          for u, (hh, c) in enumerate(units)]
    a = [jnp.where(causal, x, 0.0).astype(BF16) for x in a]
    st_in = []
    for hh in range(nh):
        st = st_ref[hh]
        for c in range(nchunk):
            u = hh * nchunk + c
            st_in.append(st.astype(BF16))
            st = jnp.exp(gtot[u]) * st + ds[u]
        st_ref[hh] = st
    outs = [jnp.dot(a[u], v[hh][rows(c)], preferred_element_type=F32)
            + lax.dot_general(q_t[u], st_in[u], nt, preferred_element_type=F32)
            for u, (hh, c) in enumerate(units)]
    for hh in range(nh):
        hs = slice(hh * HEAD_W, (hh + 1) * HEAD_W)
        o = jnp.concatenate(outs[hh * nchunk:(hh + 1) * nchunk], axis=0)
        o_ref[:, hs] = (_rms(o, nw_ref[...]) * g_ref[:, hs].astype(F32)).astype(BF16)


def _hgrn2(hq, hf, hi, hg, lb_logits, norm_w, bsz, seq, tc, nh):
    r3 = lambda a: a.reshape(bsz, seq, BRANCH_W)
    blk = pl.BlockSpec((None, tc, nh * HEAD_W), lambda b, h, i: (b, i, h))
    out = pl.pallas_call(
        functools.partial(_hgrn_kernel, tc=tc, nh=nh),
        grid=(bsz, HG_HEADS // nh, seq // tc),
        in_specs=[pl.BlockSpec((2, nh * HEAD_W), lambda b, h, i: (0, h)),
                  pl.BlockSpec((1, HEAD_W), lambda b, h, i: (0, 0)),
                  blk, blk, blk, blk],
        out_specs=blk,
        out_shape=jax.ShapeDtypeStruct((bsz, seq, BRANCH_W), BF16),
        scratch_shapes=[pltpu.VMEM((nh, HEAD_W, HEAD_W), F32)],
        compiler_params=_cparams(("parallel", "parallel", "arbitrary")),
        name="hgrn2",
    )(lb_logits, norm_w, r3(hq), r3(hf), r3(hi), r3(hg))
    return out.reshape(bsz * seq, BRANCH_W)


def _merge_kernel(x_ref, mod_ref, ya_ref, yb_ref, ga_ref, gb_ref, wa_ref, wb_ref, wo_ref,
                  npost_ref, npre_ref, rw_ref, rb_ref,
                  x1_ref, h2_ref, topi_ref, gate_ref, rank_ref, cnt_ref, lg_ref, *, tm):
    @pl.when(pl.program_id(0) == 0)
    def _():
        lg_ref[...] = jnp.zeros_like(lg_ref)

    logits = lg_ref[...]

    g1 = mod_ref[2:3, :]
    sh2 = mod_ref[3:4, :]
    sc2 = mod_ref[4:5, :]
    a = jnp.dot(ya_ref[...], wa_ref[...], preferred_element_type=F32)
    b = jnp.dot(yb_ref[...], wb_ref[...], preferred_element_type=F32)
    merged = ga_ref[...] * a.astype(BF16) + gb_ref[...] * b.astype(BF16)
    y = jnp.dot(merged, wo_ref[...], preferred_element_type=F32)
    x1 = x_ref[...] + g1 * _rms(y, npost_ref[...])
    x1_ref[...] = x1
    h2 = (_rms(x1, npre_ref[...]) * (1.0 + sc2) + sh2).astype(BF16)
    h2_ref[...] = h2
    lg_ref[...] = jnp.dot(h2, rw_ref[...], preferred_element_type=F32) + rb_ref[...]

    lane = lax.broadcasted_iota(I32, logits.shape, 1)
    work = logits
    vals, idxs = [], []
    for _ in range(TOP_K):
        mx = jnp.max(work, axis=1, keepdims=True)
        idx = jnp.min(jnp.where(work == mx, lane, LANES), axis=1, keepdims=True)
        vals.append(mx)
        idxs.append(idx)
        work = jnp.where(lane == idx, -jnp.inf, work)
    es = [jnp.exp(vk - vals[0]) for vk in vals]
    den = es[0] + es[1] + es[2] + es[3]

    onehot = jnp.zeros(logits.shape, F32)
    for idx in idxs:
        onehot = jnp.where(lane == idx, 1.0, onehot)
    row = lax.broadcasted_iota(I32, (tm, tm), 0)
    col = lax.broadcasted_iota(I32, (tm, tm), 1)
    strict = jnp.where(col < row, 1.0, 0.0).astype(BF16)
    prefix = jnp.dot(strict, onehot.astype(BF16), preferred_element_type=F32)
    cnt_ref[...] = jnp.broadcast_to(jnp.sum(onehot, axis=0, keepdims=True), cnt_ref.shape)

    topi = jnp.zeros(logits.shape, I32)
    gate = jnp.zeros(logits.shape, F32)
    rank = jnp.zeros(logits.shape, I32)
    for kk in range(TOP_K):
        rk = jnp.sum(jnp.where(lane == idxs[kk], prefix, 0.0), axis=1, keepdims=True)
        topi = jnp.where(lane == kk, idxs[kk], topi)
        gate = jnp.where(lane == kk, es[kk] / den, gate)
        rank = jnp.where(lane == kk, rk.astype(I32), rank)
    topi_ref[...] = topi
    gate_ref[...] = gate
    rank_ref[...] = rank


def _merge_route(xf, mod, ya, yb, ga, gb, wa, wb, wo, npost, npre, rw, rb, seq, tm):
    t, d = xf.shape
    n = t // tm
    tiles_per_batch = seq // tm
    proj = lambda w: pl.BlockSpec((tm, w), lambda i: (jnp.minimum(i, n - 1), 0))
    route = lambda w: pl.BlockSpec((tm, w), lambda i: (jnp.maximum(i - 1, 0), 0))
    const = lambda a: pl.BlockSpec(a.shape, lambda i: (0,) * a.ndim)
    lane_out = jax.ShapeDtypeStruct((t, LANES), I32)
    return pl.pallas_call(
        functools.partial(_merge_kernel, tm=tm),
        grid=(n + 1,),
        in_specs=[proj(d),
                  pl.BlockSpec((None, 6, d),
                               lambda i: (jnp.minimum(i, n - 1) // tiles_per_batch, 0, 0)),
                  proj(BRANCH_W), proj(BRANCH_W), proj(d), proj(d),
                  const(wa), const(wb), const(wo), const(npost), const(npre),
                  const(rw), const(rb)],
        out_specs=[proj(d), proj(d), route(LANES), route(LANES), route(LANES),
                   pl.BlockSpec((None, SUBLANES, LANES), lambda i: (jnp.maximum(i - 1, 0), 0, 0))],
        out_shape=[jax.ShapeDtypeStruct((t, d), F32), jax.ShapeDtypeStruct((t, d), BF16),
                   lane_out, jax.ShapeDtypeStruct((t, LANES), F32), lane_out,
                   jax.ShapeDtypeStruct((n, SUBLANES, LANES), F32)],
        scratch_shapes=[pltpu.VMEM((tm, LANES), F32)],
        compiler_params=_cparams(("arbitrary",)),
        name="merge_route",
    )(xf, mod, ya, yb, ga, gb, wa, wb, wo, npost, npre, rw, rb)


def _positions(topi_ref, rank_ref, srow_ref):
    topi = topi_ref[...]
    rank = rank_ref[...]
    lane = lax.broadcasted_iota(I32, topi.shape, 1)
    srow = srow_ref[0:1, :]
    pos = []
    for kk in range(TOP_K):
        base = jnp.sum(jnp.where(lane == topi[:, kk:kk + 1], srow, 0.0), axis=1, keepdims=True)
        pos.append(base.astype(I32) + rank[:, kk:kk + 1])
    return pos


def _wait_rows(n8, make_copy):
    for b in range((SORT_ROWS // RUN_ALIGN).bit_length()):
        @pl.when((n8 >> b) & 1 == 1)
        def _(b=b):
            make_copy(RUN_ALIGN << b).wait()


def _run_copies(tile, l8_ref, make_copy):
    def pieces(e, n8, bits, off):
        for b in bits:
            size = RUN_ALIGN << b
            bit = (n8 >> b) & 1

            @pl.when(bit == 1)
            def _(off=off, size=size):
                make_copy(e, off, size).start()

            off = off + bit * size

    for e in range(N_EXPERTS):
        n8 = l8_ref[tile * N_EXPERTS + e]
        big = (n8 >> RUN_SMALL_BITS) << RUN_SMALL_BITS

        @pl.when(big > 0)
        def _(e=e, n8=n8):
            pieces(e, n8, range(RUN_BITS - 1, RUN_SMALL_BITS - 1, -1), jnp.int32(0))

        pieces(e, n8, range(RUN_SMALL_BITS - 1, -1, -1), big * RUN_ALIGN)


def _dispatch_kernel(start_ref, srun_ref, l8_ref, tot8_ref, eend_ref, elen_ref,
                     h_ref, topi_ref, rank_ref, srow_ref, xs_out,
                     sel0, sel1, srt0, srt1, zbuf, zsem, sem, *, min_blocks):
    i = pl.program_id(0)
    n_tiles = pl.num_programs(0) - 1
    sels, srts = (sel0, sel1), (srt0, srt1)
    tm = h_ref.shape[0]
    half = h_ref.shape[1] // 2
    n_blocks = xs_out.shape[0] // MOE_BLOCK
    n_used = eend_ref[N_EXPERTS - 1] // MOE_BLOCK

    def zero_copy(start):
        start = pl.multiple_of(start, MOE_BLOCK)
        return pltpu.make_async_copy(zbuf, xs_out.at[pl.ds(start, MOE_BLOCK)], zsem)

    def zero_fill(wait):
        for e in range(N_EXPERTS):
            @pl.when(elen_ref[e] > 0)
            def _(e=e):
                cp = zero_copy(eend_ref[e] - MOE_BLOCK)
                cp.wait() if wait else cp.start()
        for blk in range(min_blocks, n_blocks):
            @pl.when(blk >= n_used)
            def _(blk=blk):
                cp = zero_copy(blk * MOE_BLOCK)
                cp.wait() if wait else cp.start()

    @pl.when(i == 0)
    def _():
        zbuf[...] = jnp.zeros_like(zbuf)
        sel0[...] = jnp.zeros_like(sel0)
        sel1[...] = jnp.zeros_like(sel1)
        zero_fill(wait=False)
        zero_fill(wait=True)

    def start_copies(tile, slot):
        def make_copy(e, off, size):
            src = pl.multiple_of(srun_ref[tile * N_EXPERTS + e] + off, RUN_ALIGN)
            dst = pl.multiple_of(start_ref[tile * N_EXPERTS + e] + off, RUN_ALIGN)
            return pltpu.make_async_copy(srts[slot].at[pl.ds(src, size)],
                                         xs_out.at[pl.ds(dst, size)], sem.at[slot])
        _run_copies(tile, l8_ref, make_copy)

    def wait_copies(tile, slot):
        def make_copy(size):
            return pltpu.make_async_copy(srts[slot].at[pl.ds(0, size)],
                                         xs_out.at[pl.ds(0, size)], sem.at[slot])
        _wait_rows(tot8_ref[tile], make_copy)

    for s in range(2):
        @pl.when(i % 2 == s)
        def _(s=s):
            pos = _positions(topi_ref, rank_ref, srow_ref)
            slot = lax.broadcasted_iota(I32, (tm, SORT_ROWS), 1).astype(jnp.int16)
            sel = jnp.zeros((tm, SORT_ROWS), BF16)
            for p in pos:
                sel = jnp.where(slot == p.astype(jnp.int16), jnp.ones((), BF16), sel)
            sels[s][...] = sel
            srt = lax.dot_general(sels[1 - s][...], h_ref[...], (((0,), (0,)), ((), ())),
                                  preferred_element_type=F32)
            lo = pltpu.bitcast(srt[:, :half], jnp.uint32) >> 16
            hi = pltpu.bitcast(srt[:, half:], jnp.uint32) & jnp.uint32(0xFFFF0000)
            srts[1 - s][...] = hi | lo

        @pl.when((i >= 1) & (i % 2 == s))
        def _(s=s):
            start_copies(i - 1, 1 - s)

        @pl.when((i >= 2) & (i % 2 == s))
        def _(s=s):
            wait_copies(i - 2, s)

        @pl.when((i == n_tiles) & (i % 2 == s))
        def _(s=s):
            wait_copies(i - 1, 1 - s)


def _dispatch(h2, topi, rank, srow, tables, n_rows, tm):
    t, d = h2.shape
    n = t // tm
    cur = lambda w: pl.BlockSpec((tm, w), lambda i, *_: (jnp.minimum(i, n - 1), 0))
    grid_spec = pltpu.PrefetchScalarGridSpec(
        num_scalar_prefetch=len(tables),
        grid=(n + 1,),
        in_specs=[pl.BlockSpec((tm, d), lambda i, *_: (jnp.maximum(i - 1, 0), 0)),
                  cur(LANES), cur(LANES),
                  pl.BlockSpec((None, SUBLANES, LANES), lambda i, *_: (jnp.minimum(i, n - 1), 0, 0))],
        out_specs=pl.BlockSpec(memory_space=pl.ANY),
        scratch_shapes=[pltpu.VMEM((tm, SORT_ROWS), BF16), pltpu.VMEM((tm, SORT_ROWS), BF16),
                        pltpu.VMEM((SORT_ROWS, d // 2), jnp.uint32),
                        pltpu.VMEM((SORT_ROWS, d // 2), jnp.uint32),
                        pltpu.VMEM((MOE_BLOCK, d // 2), jnp.uint32),
                        pltpu.SemaphoreType.DMA, pltpu.SemaphoreType.DMA((2,))],
    )
    return pl.pallas_call(
        functools.partial(_dispatch_kernel, min_blocks=t * TOP_K // MOE_BLOCK),
        grid_spec=grid_spec,
        out_shape=jax.ShapeDtypeStruct((n_rows, d // 2), jnp.uint32),
        compiler_params=_cparams(("arbitrary",)),
        name="moe_dispatch",
    )(*tables, h2, topi, rank, srow)


def _expert_kernel(be_ref, nu_ref, x_ref, wgu_ref, bgu_ref, wd_ref, bd_ref, y_ref,
                   wgu_bf, wd_bf):
    j = pl.program_id(0)
    n_blocks = pl.num_programs(0) - 1
    blk = j - 1

    @pl.when((blk >= 0) & (blk < nu_ref[0]))
    def _():
        xu = x_ref[...]
        x_lo = pltpu.bitcast(xu << 16, F32).astype(BF16)
        x_hi = pltpu.bitcast(xu & jnp.uint32(0xFFFF0000), F32).astype(BF16)
        x = jnp.concatenate([x_lo, x_hi], axis=1)
        gu = jnp.dot(x, wgu_bf[...], preferred_element_type=F32) + bgu_ref[...]
        g = jnp.minimum(gu[:, :D_MODEL], SWIGLU_LIMIT)
        u = jnp.clip(gu[:, D_MODEL:], -SWIGLU_LIMIT, SWIGLU_LIMIT)
        act = (u + 1.0) * (g * _sigmoid(g * SWIGLU_ALPHA))
        y_ref[...] = jnp.dot(act.astype(BF16), wd_bf[...], preferred_element_type=F32) + bd_ref[...]

    @pl.when(blk >= nu_ref[0])
    def _():
        y_ref[...] = jnp.zeros_like(y_ref)

    @pl.when((j < n_blocks) & ((j == 0) | (be_ref[jnp.minimum(j, n_blocks - 1)]
                                             != be_ref[jnp.maximum(j - 1, 0)])))
    def _():
        wgu_bf[...] = wgu_ref[...].astype(BF16)
        wd_bf[...] = wd_ref[...].astype(BF16)


def _experts(xs, block_e, n_used, wgu, bgu, wd, bd):
    p = xs.shape[0]
    e, d, _ = wgu.shape
    n_blocks = p // MOE_BLOCK
    run = lambda j: jnp.maximum(j - 1, 0)
    nxt = lambda j: jnp.minimum(j, n_blocks - 1)
    grid_spec = pltpu.PrefetchScalarGridSpec(
        num_scalar_prefetch=2,
        grid=(n_blocks + 1,),
        in_specs=[pl.BlockSpec((MOE_BLOCK, d // 2), lambda j, be, nu: (run(j), 0)),
                  pl.BlockSpec((None, d, 2 * d), lambda j, be, nu: (be[nxt(j)], 0, 0)),
                  pl.BlockSpec((None, 1, 2 * d), lambda j, be, nu: (be[run(j)], 0, 0)),
                  pl.BlockSpec((None, d, d), lambda j, be, nu: (be[nxt(j)], 0, 0)),
                  pl.BlockSpec((None, 1, d), lambda j, be, nu: (be[run(j)], 0, 0))],
        out_specs=pl.BlockSpec((MOE_BLOCK, d), lambda j, be, nu: (run(j), 0)),
        scratch_shapes=[pltpu.VMEM((d, 2 * d), BF16), pltpu.VMEM((d, d), BF16)],
    )
    return pl.pallas_call(
        _expert_kernel,
        grid_spec=grid_spec,
        out_shape=jax.ShapeDtypeStruct((p, d), F32),
        compiler_params=_cparams(("arbitrary",)),
        name="moe_experts",
    )(block_e, n_used, xs, wgu, bgu.reshape(e, 1, 2 * d), wd, bd.reshape(e, 1, d))


def _combine_kernel(start_ref, srun_ref, l8_ref, tot8_ref,
                    y_hbm, topi_ref, rank_ref, srow_ref, gate_ref, x1_ref, mod_ref, nw_ref, o_ref,
                    sel0, sel1, yb0, yb1, sem):
    i = pl.program_id(0)
    n_tiles = pl.num_programs(0) - 1
    tm = x1_ref.shape[0]
    sels, ybufs = (sel0, sel1), (yb0, yb1)

    def start_copies(tile, slot):
        def make_copy(e, off, size):
            src = pl.multiple_of(start_ref[tile * N_EXPERTS + e] + off, RUN_ALIGN)
            dst = pl.multiple_of(srun_ref[tile * N_EXPERTS + e] + off, RUN_ALIGN)
            return pltpu.make_async_copy(y_hbm.at[pl.ds(src, size)],
                                         ybufs[slot].at[pl.ds(dst, size)], sem.at[slot])
        _run_copies(tile, l8_ref, make_copy)

    def wait_copies(tile, slot):
        def make_copy(size):
            return pltpu.make_async_copy(y_hbm.at[pl.ds(0, size)],
                                         ybufs[slot].at[pl.ds(0, size)], sem.at[slot])
        _wait_rows(tot8_ref[tile], make_copy)

    @pl.when(i == 0)
    def _():
        yb0[...] = jnp.zeros_like(yb0)
        yb1[...] = jnp.zeros_like(yb1)
        sel0[...] = jnp.zeros_like(sel0)
        sel1[...] = jnp.zeros_like(sel1)

    for s in range(2):
        @pl.when((i < n_tiles) & (i % 2 == s))
        def _(s=s):
            start_copies(i, s)

        @pl.when((i >= 1) & (i % 2 == s))
        def _(s=s):
            wait_copies(i - 1, 1 - s)

        @pl.when(i % 2 == s)
        def _(s=s):
            pos = _positions(topi_ref, rank_ref, srow_ref)
            gate = gate_ref[...].astype(BF16)
            row = lax.broadcasted_iota(I32, (tm, SORT_ROWS), 1).astype(jnp.int16)
            sel = jnp.zeros((tm, SORT_ROWS), BF16)
            for kk, p in enumerate(pos):
                sel = jnp.where(row == p.astype(jnp.int16), gate[:, kk:kk + 1], sel)
            sels[s][...] = sel
            moe = jnp.dot(sels[1 - s][...], ybufs[1 - s][...].astype(BF16),
                          preferred_element_type=F32)
            g2 = mod_ref[5:6, :]
            o_ref[...] = x1_ref[...] + g2 * _rms(moe, nw_ref[...])


def _combine(y, topi, rank, srow, gate, x1, mod, nw, tables, seq, tm):
    t, d = x1.shape
    n = t // tm
    tiles_per_batch = seq // tm
    cur = lambda w: pl.BlockSpec((tm, w), lambda i, *_: (jnp.minimum(i, n - 1), 0))
    prev = lambda w: pl.BlockSpec((tm, w), lambda i, *_: (jnp.maximum(i - 1, 0), 0))
    grid_spec = pltpu.PrefetchScalarGridSpec(
        num_scalar_prefetch=len(tables),
        grid=(n + 1,),
        in_specs=[pl.BlockSpec(memory_space=pl.ANY),
                  cur(LANES), cur(LANES),
                  pl.BlockSpec((None, SUBLANES, LANES), lambda i, *_: (jnp.minimum(i, n - 1), 0, 0)),
                  cur(LANES), prev(d),
                  pl.BlockSpec((None, 6, d),
                               lambda i, *_: (jnp.maximum(i - 1, 0) // tiles_per_batch, 0, 0)),
                  pl.BlockSpec((1, d), lambda i, *_: (0, 0))],
        out_specs=prev(d),
        scratch_shapes=[pltpu.VMEM((tm, SORT_ROWS), BF16), pltpu.VMEM((tm, SORT_ROWS), BF16),
                        pltpu.VMEM((SORT_ROWS, d), F32), pltpu.VMEM((SORT_ROWS, d), F32),
                        pltpu.SemaphoreType.DMA((2,))],
    )
    return pl.pallas_call(
        _combine_kernel,
        grid_spec=grid_spec,
        out_shape=jax.ShapeDtypeStruct((t, d), F32),
        compiler_params=_cparams(("arbitrary",)),
        name="moe_combine",
    )(*tables, y, topi, rank, srow, gate, x1, mod, nw)


def _rope_tables(seq):
    pos = jnp.arange(seq, dtype=F32)
    inv = ROPE_THETA ** (-jnp.arange(0, DA_HEAD_DIM, 2, dtype=F32) / DA_HEAD_DIM)
    ang = pos[:, None] * inv[None, :]
    cos, sin = jnp.cos(ang), jnp.sin(ang)
    zero = jnp.zeros_like(sin)
    cos_t = jnp.tile(cos, (1, LANES // cos.shape[1]))
    sin_lo = jnp.tile(jnp.concatenate([-sin, zero], axis=1), (1, 2))
    sin_hi = jnp.tile(jnp.concatenate([zero, sin], axis=1), (1, 2))
    return cos_t, sin_lo, sin_hi


def kernel(x, c, w_mod, b_mod, norm_pre_mix, norm_post_mix, w_in, da_lambda_q1, da_lambda_k1,
           da_lambda_q2, da_lambda_k2, da_subln, hg_lb_logits, hg_norm, w_branch_a, w_branch_b,
           w_out, norm_pre_ffn, norm_post_ffn, router_w, router_b, w_gate_up, b_gate_up,
           w_down, b_down):
    bsz, seq, d = x.shape
    depth = w_mod.shape[0]
    assert depth == 1 and d == D_MODEL
    t = bsz * seq
    lyr = 0
    lambda_init = 0.8 - 0.6 * math.exp(-0.3 * lyr)
    cos, slo, shi = _rope_tables(seq)

    xf = x.reshape(t, d)
    mod = _modulation(c, w_mod[lyr], b_mod[lyr]).reshape(bsz, 6, d)

    qT, k, vT, hq, hf, hi, hg, ga, gb = _in_projection(
        xf, mod, norm_pre_mix[lyr][None], w_in[lyr].astype(BF16), cos, slo, shi, seq, tm=512)
    ya = _diff_attention(qT, k, vT, da_lambda_q1[lyr][None], da_lambda_k1[lyr][None],
                         da_lambda_q2[lyr][None], da_lambda_k2[lyr][None], da_subln[lyr],
                         bsz, seq, lambda_init, tq=2048, tk=512)
    yb = _hgrn2(hq, hf, hi, hg, hg_lb_logits[lyr:lyr + 2], hg_norm[lyr][None], bsz, seq, tc=512, nh=4)

    rw = jnp.zeros((d, LANES), BF16).at[:, :N_EXPERTS].set(router_w[lyr].astype(BF16))
    rb = jnp.full((1, LANES), NEG_BIG, F32).at[0, :N_EXPERTS].set(router_b[lyr])
    x1, h2, topi, gate, rank, counts = _merge_route(
        xf, mod, ya, yb, ga, gb, w_branch_a[lyr].astype(BF16), w_branch_b[lyr].astype(BF16),
        w_out[lyr].astype(BF16), norm_post_mix[lyr][None], norm_pre_ffn[lyr][None], rw, rb,
        seq, tm=MOE_TILE)

    n_tiles = t // MOE_TILE
    up = lambda a, m: (a + m - 1) // m * m
    cnt = counts[:, 0, :N_EXPERTS].astype(I32)
    run = up(cnt, RUN_ALIGN)
    srun = jnp.cumsum(run, axis=1) - run
    e_len = up(jnp.sum(run, axis=0), MOE_BLOCK)
    e_end = jnp.cumsum(e_len)
    start = (e_end - e_len)[None, :] + jnp.cumsum(run, axis=0) - run
    n_assign = t * TOP_K
    n_rows = (up(n_assign + n_tiles * N_EXPERTS * (RUN_ALIGN - 1), MOE_BLOCK)
              + N_EXPERTS * MOE_BLOCK)
    n_blocks = n_rows // MOE_BLOCK
    block_start = jnp.arange(n_blocks, dtype=I32) * MOE_BLOCK
    block_e = jnp.minimum(jnp.sum((e_end[None, :] <= block_start[:, None]).astype(I32), axis=1),
                          N_EXPERTS - 1)
    n_used = (e_end[-1:] // MOE_BLOCK).astype(I32)
    srow = jnp.broadcast_to(jnp.pad(srun.astype(F32), ((0, 0), (0, LANES - N_EXPERTS)))[:, None, :],
                            (n_tiles, SUBLANES, LANES))
    run_tables = (start.reshape(-1), srun.reshape(-1), (run // RUN_ALIGN).reshape(-1),
                  jnp.sum(run, axis=1) // RUN_ALIGN)

    xs = _dispatch(h2, topi, rank, srow, run_tables + (e_end.astype(I32), e_len.astype(I32)),
                   n_rows, tm=MOE_TILE)
    y = _experts(xs, block_e, n_used, w_gate_up[lyr], b_gate_up[lyr], w_down[lyr], b_down[lyr])
    out = _combine(y, topi, rank, srow, gate, x1, mod, norm_post_ffn[lyr][None], run_tables,
                   seq, tm=MOE_TILE)
    return out.reshape(bsz, seq, d)
```

```python
import functools
import math

import jax
import jax.numpy as jnp
from jax import lax
from jax.experimental import pallas as pl
from jax.experimental.pallas import tpu as pltpu

F32 = jnp.float32
BF16 = jnp.bfloat16
I32 = jnp.int32

D_MODEL = 1024
CHUNK = 64
EPS = 1e-6
ROPE_THETA = 10000.0
DA_HEADS = 4
DA_HEAD_DIM = 64
HG_HEADS = 4
HEAD_W = 128
BRANCH_W = 512
V_ROWS = HEAD_W + 16
ATTN_STRIP = 256
N_EXPERTS = 32
TOP_K = 4
SWIGLU_LIMIT = 7.0
SWIGLU_ALPHA = 1.702
MOE_BLOCK = 512
MOE_TILE = 512
LANES = 128
SUBLANES = 8
RUN_ALIGN = SUBLANES
RUN_BITS = (MOE_TILE // RUN_ALIGN).bit_length()
RUN_PIECES = N_EXPERTS * RUN_BITS
SORT_ROWS = MOE_TILE * TOP_K + N_EXPERTS * RUN_ALIGN
NEG_BIG = -1e30

VMEM_LIMIT = 56 * 1024 * 1024


def _cparams(sem):
    return pltpu.CompilerParams(dimension_semantics=sem, vmem_limit_bytes=VMEM_LIMIT)


def _rms(x, w):
    return x * lax.rsqrt(jnp.mean(x * x, axis=-1, keepdims=True) + EPS) * w


def _sigmoid(x):
    return 1.0 / (1.0 + jnp.exp(-x))


def _split3(x):
    a = x.astype(BF16)
    r = x - a.astype(F32)
    b = r.astype(BF16)
    c = (r - b.astype(F32)).astype(BF16)
    return a, b, c


def _mod_kernel(c_ref, w_ref, b_ref, o_ref):
    c = c_ref[...]
    ca = c * _sigmoid(c)
    o_ref[...] = jnp.dot(ca, w_ref[...], preferred_element_type=F32,
                         precision=lax.Precision.HIGHEST) + b_ref[...]


def _modulation(c, w_mod, b_mod):
    bsz, d = c.shape
    n = w_mod.shape[1]
    tn = 1536
    return pl.pallas_call(
        _mod_kernel,
        grid=(n // tn,),
        in_specs=[pl.BlockSpec((bsz, d), lambda j: (0, 0)),
                  pl.BlockSpec((d, tn), lambda j: (0, j)),
                  pl.BlockSpec((1, tn), lambda j: (0, j))],
        out_specs=pl.BlockSpec((bsz, tn), lambda j: (0, j)),
        out_shape=jax.ShapeDtypeStruct((bsz, n), F32),
        compiler_params=_cparams(("arbitrary",)),
        name="modulation",
    )(c, w_mod, b_mod.reshape(1, n))


def _rope(x, cos, sin_lo, sin_hi):
    return x * cos + pltpu.roll(x, 96, 1) * sin_lo + pltpu.roll(x, 32, 1) * sin_hi


def _inproj_kernel(x_ref, mod_ref, nw_ref, w_ref, cos_ref, slo_ref, shi_ref,
                   qT_ref, k_ref, vT_ref, hq_ref, hf_ref, hi_ref, hg_ref, ga_ref, gb_ref):
    x = x_ref[...]
    sh1 = mod_ref[0:1, :]
    sc1 = mod_ref[1:2, :]
    h = (_rms(x, nw_ref[...]) * (1.0 + sc1) + sh1).astype(BF16)
    cos, slo, shi = cos_ref[...], slo_ref[...], shi_ref[...]
    scale = DA_HEAD_DIM ** -0.5 * math.log2(math.e)

    def seg(i, width=BRANCH_W):
        return jnp.dot(h, w_ref[:, i:i + width], preferred_element_type=F32)

    q = seg(0)
    k = seg(BRANCH_W)
    for hd in range(DA_HEADS):
        sl = slice(hd * HEAD_W, (hd + 1) * HEAD_W)
        qT_ref[sl, :] = (_rope(q[:, sl], cos, slo, shi) * scale).T.astype(BF16)
        k_ref[:, sl] = _rope(k[:, sl], cos, slo, shi).astype(BF16)
    vT = seg(2 * BRANCH_W).T.astype(BF16)
    ones = jnp.ones((V_ROWS - HEAD_W, vT.shape[1]), BF16)
    for hd in range(DA_HEADS):
        vT_ref[hd * V_ROWS:hd * V_ROWS + HEAD_W, :] = vT[hd * HEAD_W:(hd + 1) * HEAD_W]
        vT_ref[hd * V_ROWS + HEAD_W:(hd + 1) * V_ROWS, :] = ones
    hq = seg(3 * BRANCH_W)
    hq_ref[...] = (hq * _sigmoid(hq)).astype(BF16)
    hf_ref[...] = seg(4 * BRANCH_W)
    hi_ref[...] = seg(5 * BRANCH_W).astype(BF16)
    hg = seg(6 * BRANCH_W)
    hg_ref[...] = (hg * _sigmoid(hg)).astype(BF16)
    for j in range(2):
        ga_ref[:, j * BRANCH_W:(j + 1) * BRANCH_W] = _sigmoid(seg((7 + j) * BRANCH_W)).astype(BF16)
        gb_ref[:, j * BRANCH_W:(j + 1) * BRANCH_W] = _sigmoid(seg((9 + j) * BRANCH_W)).astype(BF16)


def _in_projection(xf, mod, norm_w, w_in_bf, cos, slo, shi, seq, tm):
    t, d = xf.shape
    tiles_per_batch = seq // tm
    bspec = lambda w: pl.BlockSpec((tm, w), lambda i: (i, 0))
    tab = pl.BlockSpec((tm, LANES), lambda i: (i % tiles_per_batch, 0))
    widths = [BRANCH_W] * 7 + [D_MODEL, D_MODEL]
    dtypes = [BF16, BF16, BF16, BF16, F32, BF16, BF16, BF16, BF16]
    out_specs = [bspec(w) for w in widths]
    out_shape = [jax.ShapeDtypeStruct((t, w), dt) for w, dt in zip(widths, dtypes)]
    for idx, rows in ((0, BRANCH_W), (2, DA_HEADS * V_ROWS)):
        out_specs[idx] = pl.BlockSpec((None, rows, tm),
                                      lambda i: (i // tiles_per_batch, 0, i % tiles_per_batch))
        out_shape[idx] = jax.ShapeDtypeStruct((t // seq, rows, seq), BF16)
    return pl.pallas_call(
        _inproj_kernel,
        grid=(t // tm,),
        in_specs=[bspec(d),
                  pl.BlockSpec((None, 6, d), lambda i: (i // tiles_per_batch, 0, 0)),
                  pl.BlockSpec((1, d), lambda i: (0, 0)),
                  pl.BlockSpec(w_in_bf.shape, lambda i: (0, 0), pipeline_mode=pl.Buffered(1)),
                  tab, tab, tab],
        out_specs=out_specs,
        out_shape=out_shape,
        compiler_params=_cparams(("parallel",)),
        name="in_projection",
    )(xf, mod, norm_w, w_in_bf, cos, slo, shi)


def _attn_kernel(lq1_ref, lk1_ref, lq2_ref, lk2_ref, subln_ref, qT_ref, k_ref, vT_ref, o_ref,
                 qq_ref, s0_ref, s1_ref, p0_ref, p1_ref, a0_ref, a1_ref, m_ref, acc_ref,
                 *, tq, tk, lambda_init):
    qi = pl.program_id(2)
    ndiag = tq // tk
    s_refs, p_refs, a_refs = (s0_ref, s1_ref), (p0_ref, p1_ref), (a0_ref, a1_ref)
    qT = qT_ref[...]
    feat = lax.broadcasted_iota(I32, qT.shape, 0)
    zero = jnp.zeros_like(qT)
    qq_ref[:, :tq] = jnp.where(feat < DA_HEAD_DIM, qT, zero)
    qq_ref[:, tq:] = jnp.where(feat >= DA_HEAD_DIM, qT, zero)

    strips = [slice(c * ATTN_STRIP, (c + 1) * ATTN_STRIP) for c in range(2 * tq // ATTN_STRIP)]

    def scores(kb, dst, cs):
        start = pl.multiple_of(kb * tk, tk)
        dst[:, cs] = jnp.dot(k_ref[pl.ds(start, tk), :], qq_ref[:, cs], preferred_element_type=F32)

    def accumulate(kb, slot, cs):
        start = pl.multiple_of(kb * tk, tk)
        pv = jnp.dot(vT_ref[:, pl.ds(start, tk)], p_refs[slot][:, cs], preferred_element_type=F32)
        acc_ref[:, cs] = a_refs[slot][:, cs] * acc_ref[:, cs] + pv

    def softmax(slot, cs, diag):
        def piece(rc):
            sc = s_refs[slot][rc * CHUNK:(rc + 1) * CHUNK, cs]
            if diag is not None:
                qchunk = (lax.broadcasted_iota(I32, (1, ATTN_STRIP), 1) + cs.start) % tq // CHUNK
                sc = jnp.where(qchunk >= diag * (tk // CHUNK) + rc, sc, NEG_BIG)
            return sc

        top = piece(0)
        for rc in range(1, tk // CHUNK):
            top = jnp.maximum(top, piece(rc))
        m_old = m_ref[:, cs]
        m_new = jnp.maximum(m_old, jnp.max(top, axis=0, keepdims=True))
        m_ref[:, cs] = m_new
        a_refs[slot][:, cs] = jnp.exp2(m_old - m_new)
        for rc in range(tk // CHUNK):
            p_refs[slot][rc * CHUNK:(rc + 1) * CHUNK, cs] = jnp.exp2((piece(rc) - m_new).astype(BF16))

    def seen(diag, cs):
        return diag is None or diag < 0 or (cs.start % tq) + ATTN_STRIP > diag * tk

    def stage(kb, slot, diag=None, last=False):
        for cs in strips:
            if seen(None if diag is None else diag - 1, cs):
                accumulate(jnp.maximum(kb - 1, 0), 1 - slot, cs)
            if seen(diag, cs):
                softmax(slot, cs, diag)
            if not last and seen(None if diag is None else diag + 1, cs):
                scores(kb + 1, s_refs[1 - slot], cs)

    p1_ref[...] = jnp.zeros_like(p1_ref)
    a1_ref[...] = jnp.ones_like(a1_ref)
    m_ref[...] = jnp.full(m_ref.shape, NEG_BIG, F32)
    acc_ref[...] = jnp.zeros_like(acc_ref)
    for cs in strips:
        scores(0, s0_ref, cs)

    def pair(i, c):
        stage(2 * i, 0)
        stage(2 * i + 1, 1)
        return c

    first_diag = qi * ndiag
    lax.fori_loop(0, first_diag // 2, pair, 0)
    for dg in range(ndiag):
        stage(first_diag + dg, dg % 2, diag=dg, last=dg == ndiag - 1)
    for cs in strips:
        if seen(ndiag - 1, cs):
            accumulate(first_diag + ndiag - 1, (ndiag - 1) % 2, cs)

    acc = acc_ref[...]
    o = acc[:HEAD_W] / acc[HEAD_W:HEAD_W + 1]
    lam = (jnp.exp(jnp.sum(lq1_ref[...] * lk1_ref[...], axis=1, keepdims=True))
           - jnp.exp(jnp.sum(lq2_ref[...] * lk2_ref[...], axis=1, keepdims=True))
           + lambda_init)
    o = o[:, :tq] - lam * o[:, tq:]
    o = o * lax.rsqrt(jnp.mean(o * o, axis=0, keepdims=True) + EPS) * subln_ref[...]
    o_ref[...] = (o * (1.0 - lambda_init)).T.astype(BF16)


def _diff_attention(qT, k, vT, lq1, lk1, lq2, lk2, subln, bsz, seq, lambda_init, tq, tk):
    assert tq % (2 * tk) == 0
    k3 = k.reshape(bsz, seq, BRANCH_W)
    small = lambda n: pl.BlockSpec((1, n), lambda b, h, i: (0, 0))
    out = pl.pallas_call(
        functools.partial(_attn_kernel, tq=tq, tk=tk, lambda_init=lambda_init),
        grid=(bsz, DA_HEADS, seq // tq),
        in_specs=[small(DA_HEAD_DIM)] * 4 + [
            pl.BlockSpec((HEAD_W, 1), lambda b, h, i: (0, 0)),
            pl.BlockSpec((None, HEAD_W, tq), lambda b, h, i: (b, h, i)),
            pl.BlockSpec((None, seq, HEAD_W), lambda b, h, i: (b, 0, h)),
            pl.BlockSpec((None, V_ROWS, seq), lambda b, h, i: (b, h, 0))],
        out_specs=pl.BlockSpec((None, tq, HEAD_W), lambda b, h, i: (b, i, h)),
        out_shape=jax.ShapeDtypeStruct((bsz, seq, BRANCH_W), BF16),
        scratch_shapes=[pltpu.VMEM((HEAD_W, 2 * tq), BF16),
                        pltpu.VMEM((tk, 2 * tq), F32), pltpu.VMEM((tk, 2 * tq), F32),
                        pltpu.VMEM((tk, 2 * tq), BF16), pltpu.VMEM((tk, 2 * tq), BF16),
                        pltpu.VMEM((1, 2 * tq), F32), pltpu.VMEM((1, 2 * tq), F32),
                        pltpu.VMEM((1, 2 * tq), F32), pltpu.VMEM((V_ROWS, 2 * tq), F32)],
        compiler_params=_cparams(("parallel", "parallel", "arbitrary")),
        name="diff_attention",
    )(lq1, lk1, lq2, lk2, subln.reshape(HEAD_W, 1), qT, k3, vT)
    return out.reshape(bsz * seq, BRANCH_W)


def _hgrn_kernel(lbl_ref, nw_ref, q_ref, f_ref, i_ref, g_ref, o_ref, st_ref, *, tc, nh):
    @pl.when(pl.program_id(2) == 0)
    def _():
        st_ref[...] = jnp.zeros_like(st_ref)

    row = lax.broadcasted_iota(I32, (CHUNK, CHUNK), 0)
    col = lax.broadcasted_iota(I32, (CHUNK, CHUNK), 1)
    causal = col <= row
    tril = jnp.where(causal, 1.0, 0.0).astype(BF16)
    nt = (((1,), (1,)), ((), ()))
    tn = (((0,), (0,)), ((), ()))

    nchunk = tc // CHUNK
    units = [(hh, c) for hh in range(nh) for c in range(nchunk)]
    rows = lambda c: slice(c * CHUNK, (c + 1) * CHUNK)
    q, k, v, parts = [], [], [], []
    for hh in range(nh):
        hs = slice(hh * HEAD_W, (hh + 1) * HEAD_W)
        lg = lbl_ref[:, hs]
        ex = jnp.exp(lg - jnp.max(lg, axis=0, keepdims=True))
        lb = ex[0:1] / jnp.sum(ex, axis=0, keepdims=True)
        f = lb + (1.0 - lb) * _sigmoid(f_ref[:, hs])
        q.append(q_ref[:, hs].astype(F32))
        k.append(1.0 - f)
        v.append(i_ref[:, hs])
        parts.append(_split3(jnp.log(f)))
    pieces = jnp.concatenate(
        [jnp.concatenate([parts[hh][p][rows(c)] for hh, c in units], axis=1) for p in range(3)],
        axis=0)
    gall = jnp.dot(jnp.concatenate([tril] * 3, axis=1), pieces, preferred_element_type=F32)
    gcum = [gall[:, u * HEAD_W:(u + 1) * HEAD_W] for u in range(len(units))]
    gtot = [g[CHUNK - 1:CHUNK, :] for g in gcum]
    q_t = [(q[hh][rows(c)] * jnp.exp(gcum[u])).astype(BF16) for u, (hh, c) in enumerate(units)]
    k_t = [(k[hh][rows(c)] * jnp.exp(-gcum[u])).astype(BF16) for u, (hh, c) in enumerate(units)]
    k_dec = [(k[hh][rows(c)] * jnp.exp(gtot[u] - gcum[u])).astype(BF16)
             for u, (hh, c) in enumerate(units)]
    a = [lax.dot_general(q_t[u], k_t[u], nt, preferred_element_type=F32) for u in range(len(units))]
    ds = [lax.dot_general(v[hh][rows(c)], k_dec[u], tn, preferred_element_type=F32)
          for u, (hh, c) in enumerate(units)]
    a = [jnp.where(causal, x, 0.0).astype(BF16) for x in a]
    st_in = []
    for hh in range(nh):
        st = st_ref[hh]
        for c in range(nchunk):
            u = hh * nchunk + c
            st_in.append(st.astype(BF16))
            st = jnp.exp(gtot[u]) * st + ds[u]
        st_ref[hh] = st
    outs = [jnp.dot(a[u], v[hh][rows(c)], preferred_element_type=F32)
            + lax.dot_general(q_t[u], st_in[u], nt, preferred_element_type=F32)
            for u, (hh, c) in enumerate(units)]
    for hh in range(nh):
        hs = slice(hh * HEAD_W, (hh + 1) * HEAD_W)
        o = jnp.concatenate(outs[hh * nchunk:(hh + 1) * nchunk], axis=0)
        o_ref[:, hs] = (_rms(o, nw_ref[...]) * g_ref[:, hs].astype(F32)).astype(BF16)


def _hgrn2(hq, hf, hi, hg, lb_logits, norm_w, bsz, seq, tc, nh):
    r3 = lambda a: a.reshape(bsz, seq, BRANCH_W)
    blk = pl.BlockSpec((None, tc, nh * HEAD_W), lambda b, h, i: (b, i, h))
    out = pl.pallas_call(
        functools.partial(_hgrn_kernel, tc=tc, nh=nh),
        grid=(bsz, HG_HEADS // nh, seq // tc),
        in_specs=[pl.BlockSpec((2, nh * HEAD_W), lambda b, h, i: (0, h)),
                  pl.BlockSpec((1, HEAD_W), lambda b, h, i: (0, 0)),
                  blk, blk, blk, blk],
        out_specs=blk,
        out_shape=jax.ShapeDtypeStruct((bsz, seq, BRANCH_W), BF16),
        scratch_shapes=[pltpu.VMEM((nh, HEAD_W, HEAD_W), F32)],
        compiler_params=_cparams(("parallel", "parallel", "arbitrary")),
        name="hgrn2",
    )(lb_logits, norm_w, r3(hq), r3(hf), r3(hi), r3(hg))
    return out.reshape(bsz * seq, BRANCH_W)


def _merge_kernel(x_ref, mod_ref, ya_ref, yb_ref, ga_ref, gb_ref, wa_ref, wb_ref, wo_ref,
                  npost_ref, npre_ref, rw_ref, rb_ref,
                  x1_ref, h2_ref, topi_ref, gate_ref, rank_ref, cnt_ref, lg_ref, *, tm):
    @pl.when(pl.program_id(0) == 0)
    def _():
        lg_ref[...] = jnp.zeros_like(lg_ref)

    logits = lg_ref[...]

    g1 = mod_ref[2:3, :]
    sh2 = mod_ref[3:4, :]
    sc2 = mod_ref[4:5, :]
    a = jnp.dot(ya_ref[...], wa_ref[...], preferred_element_type=F32)
    b = jnp.dot(yb_ref[...], wb_ref[...], preferred_element_type=F32)
    merged = ga_ref[...] * a.astype(BF16) + gb_ref[...] * b.astype(BF16)
    y = jnp.dot(merged, wo_ref[...], preferred_element_type=F32)
    x1 = x_ref[...] + g1 * _rms(y, npost_ref[...])
    x1_ref[...] = x1
    h2 = (_rms(x1, npre_ref[...]) * (1.0 + sc2) + sh2).astype(BF16)
    h2_ref[...] = h2
    lg_ref[...] = jnp.dot(h2, rw_ref[...], preferred_element_type=F32) + rb_ref[...]

    lane = lax.broadcasted_iota(I32, logits.shape, 1)
    work = logits
    vals, idxs = [], []
    for _ in range(TOP_K):
        mx = jnp.max(work, axis=1, keepdims=True)
        idx = jnp.min(jnp.where(work == mx, lane, LANES), axis=1, keepdims=True)
        vals.append(mx)
        idxs.append(idx)
        work = jnp.where(lane == idx, -jnp.inf, work)
    es = [jnp.exp(vk - vals[0]) for vk in vals]
    den = es[0] + es[1] + es[2] + es[3]

    onehot = jnp.zeros(logits.shape, F32)
    for idx in idxs:
        onehot = jnp.where(lane == idx, 1.0, onehot)
    row = lax.broadcasted_iota(I32, (tm, tm), 0)
    col = lax.broadcasted_iota(I32, (tm, tm), 1)
    strict = jnp.where(col < row, 1.0, 0.0).astype(BF16)
    prefix = jnp.dot(strict, onehot.astype(BF16), preferred_element_type=F32)
    cnt_ref[...] = jnp.broadcast_to(jnp.sum(onehot, axis=0, keepdims=True), cnt_ref.shape)

    topi = jnp.zeros(logits.shape, I32)
    gate = jnp.zeros(logits.shape, F32)
    rank = jnp.zeros(logits.shape, I32)
    for kk in range(TOP_K):
        rk = jnp.sum(jnp.where(lane == idxs[kk], prefix, 0.0), axis=1, keepdims=True)
        topi = jnp.where(lane == kk, idxs[kk], topi)
        gate = jnp.where(lane == kk, es[kk] / den, gate)
        rank = jnp.where(lane == kk, rk.astype(I32), rank)
    topi_ref[...] = topi
    gate_ref[...] = gate
    rank_ref[...] = rank


def _merge_route(xf, mod, ya, yb, ga, gb, wa, wb, wo, npost, npre, rw, rb, seq, tm):
    t, d = xf.shape
    n = t // tm
    tiles_per_batch = seq // tm
    proj = lambda w: pl.BlockSpec((tm, w), lambda i: (jnp.minimum(i, n - 1), 0))
    route = lambda w: pl.BlockSpec((tm, w), lambda i: (jnp.maximum(i - 1, 0), 0))
    const = lambda a: pl.BlockSpec(a.shape, lambda i: (0,) * a.ndim)
    lane_out = jax.ShapeDtypeStruct((t, LANES), I32)
    return pl.pallas_call(
        functools.partial(_merge_kernel, tm=tm),
        grid=(n + 1,),
        in_specs=[proj(d),
                  pl.BlockSpec((None, 6, d),
                               lambda i: (jnp.minimum(i, n - 1) // tiles_per_batch, 0, 0)),
                  proj(BRANCH_W), proj(BRANCH_W), proj(d), proj(d),
                  const(wa), const(wb), const(wo), const(npost), const(npre),
                  const(rw), const(rb)],
        out_specs=[proj(d), proj(d), route(LANES), route(LANES), route(LANES),
                   pl.BlockSpec((None, SUBLANES, LANES), lambda i: (jnp.maximum(i - 1, 0), 0, 0))],
        out_shape=[jax.ShapeDtypeStruct((t, d), F32), jax.ShapeDtypeStruct((t, d), BF16),
                   lane_out, jax.ShapeDtypeStruct((t, LANES), F32), lane_out,
                   jax.ShapeDtypeStruct((n, SUBLANES, LANES), F32)],
        scratch_shapes=[pltpu.VMEM((tm, LANES), F32)],
        compiler_params=_cparams(("arbitrary",)),
        name="merge_route",
    )(xf, mod, ya, yb, ga, gb, wa, wb, wo, npost, npre, rw, rb)


def _positions(topi_ref, rank_ref, srow_ref):
    topi = topi_ref[...]
    rank = rank_ref[...]
    lane = lax.broadcasted_iota(I32, topi.shape, 1)
    srow = srow_ref[0:1, :]
    pos = []
    for kk in range(TOP_K):
        base = jnp.sum(jnp.where(lane == topi[:, kk:kk + 1], srow, 0.0), axis=1, keepdims=True)
        pos.append(base.astype(I32) + rank[:, kk:kk + 1])
    return pos


def _wait_rows(n8, make_copy):
    for b in range((SORT_ROWS // RUN_ALIGN).bit_length()):
        @pl.when((n8 >> b) & 1 == 1)
        def _(b=b):
            make_copy(RUN_ALIGN << b).wait()


def _run_copies(tile, cls_ref, make_copy):
    for b in range(RUN_BITS):
        def body(k, c, b=b):
            make_copy(tile * RUN_PIECES + k, RUN_ALIGN << b).start()
            return c

        lax.fori_loop(cls_ref[tile * (RUN_BITS + 1) + b], cls_ref[tile * (RUN_BITS + 1) + b + 1],
                      body, 0)


def _dispatch_kernel(loc_ref, hbm_ref, cls_ref, tot8_ref, eend_ref, elen_ref,
                     h_ref, topi_ref, rank_ref, srow_ref, xs_out,
                     sel0, sel1, srt0, srt1, zbuf, zsem, sem, *, min_blocks):
    i = pl.program_id(0)
    n_tiles = pl.num_programs(0) - 1
    sels, srts = (sel0, sel1), (srt0, srt1)
    tm = h_ref.shape[0]
    half = h_ref.shape[1] // 2
    n_blocks = xs_out.shape[0] // MOE_BLOCK
    n_used = eend_ref[N_EXPERTS - 1] // MOE_BLOCK

    def zero_copy(start):
        start = pl.multiple_of(start, MOE_BLOCK)
        return pltpu.make_async_copy(zbuf, xs_out.at[pl.ds(start, MOE_BLOCK)], zsem)

    def zero_fill(wait):
        for e in range(N_EXPERTS):
            @pl.when(elen_ref[e] > 0)
            def _(e=e):
                cp = zero_copy(eend_ref[e] - MOE_BLOCK)
                cp.wait() if wait else cp.start()
        for blk in range(min_blocks, n_blocks):
            @pl.when(blk >= n_used)
            def _(blk=blk):
                cp = zero_copy(blk * MOE_BLOCK)
                cp.wait() if wait else cp.start()

    @pl.when(i == 0)
    def _():
        zbuf[...] = jnp.zeros_like(zbuf)
        sel0[...] = jnp.zeros_like(sel0)
        sel1[...] = jnp.zeros_like(sel1)
        zero_fill(wait=False)
        zero_fill(wait=True)

    def start_copies(tile, slot):
        def make_copy(piece, size):
            src = pl.multiple_of(loc_ref[piece], RUN_ALIGN)
            dst = pl.multiple_of(hbm_ref[piece], RUN_ALIGN)
            return pltpu.make_async_copy(srts[slot].at[pl.ds(src, size)],
                                         xs_out.at[pl.ds(dst, size)], sem.at[slot])
        _run_copies(tile, cls_ref, make_copy)

    def wait_copies(tile, slot):
        def make_copy(size):
            return pltpu.make_async_copy(srts[slot].at[pl.ds(0, size)],
                                         xs_out.at[pl.ds(0, size)], sem.at[slot])
        _wait_rows(tot8_ref[tile], make_copy)

    for s in range(2):
        @pl.when(i % 2 == s)
        def _(s=s):
            pos = _positions(topi_ref, rank_ref, srow_ref)
            slot = lax.broadcasted_iota(I32, (tm, SORT_ROWS), 1).astype(jnp.int16)
            sel = jnp.zeros((tm, SORT_ROWS), BF16)
            for p in pos:
                sel = jnp.where(slot == p.astype(jnp.int16), jnp.ones((), BF16), sel)
            sels[s][...] = sel
            srt = lax.dot_general(sels[1 - s][...], h_ref[...], (((0,), (0,)), ((), ())),
                                  preferred_element_type=F32)
            lo = pltpu.bitcast(srt[:, :half], jnp.uint32) >> 16
            hi = pltpu.bitcast(srt[:, half:], jnp.uint32) & jnp.uint32(0xFFFF0000)
            srts[1 - s][...] = hi | lo

        @pl.when((i >= 1) & (i % 2 == s))
        def _(s=s):
            start_copies(i - 1, 1 - s)

        @pl.when((i >= 2) & (i % 2 == s))
        def _(s=s):
            wait_copies(i - 2, s)

        @pl.when((i == n_tiles) & (i % 2 == s))
        def _(s=s):
            wait_copies(i - 1, 1 - s)


def _dispatch(h2, topi, rank, srow, tables, n_rows, tm):
    t, d = h2.shape
    n = t // tm
    cur = lambda w: pl.BlockSpec((tm, w), lambda i, *_: (jnp.minimum(i, n - 1), 0))
    grid_spec = pltpu.PrefetchScalarGridSpec(
        num_scalar_prefetch=len(tables),
        grid=(n + 1,),
        in_specs=[pl.BlockSpec((tm, d), lambda i, *_: (jnp.maximum(i - 1, 0), 0)),
                  cur(LANES), cur(LANES),
                  pl.BlockSpec((None, SUBLANES, LANES), lambda i, *_: (jnp.minimum(i, n - 1), 0, 0))],
        out_specs=pl.BlockSpec(memory_space=pl.ANY),
        scratch_shapes=[pltpu.VMEM((tm, SORT_ROWS), BF16), pltpu.VMEM((tm, SORT_ROWS), BF16),
                        pltpu.VMEM((SORT_ROWS, d // 2), jnp.uint32),
                        pltpu.VMEM((SORT_ROWS, d // 2), jnp.uint32),
                        pltpu.VMEM((MOE_BLOCK, d // 2), jnp.uint32),
                        pltpu.SemaphoreType.DMA, pltpu.SemaphoreType.DMA((2,))],
    )
    return pl.pallas_call(
        functools.partial(_dispatch_kernel, min_blocks=t * TOP_K // MOE_BLOCK),
        grid_spec=grid_spec,
        out_shape=jax.ShapeDtypeStruct((n_rows, d // 2), jnp.uint32),
        compiler_params=_cparams(("arbitrary",)),
        name="moe_dispatch",
    )(*tables, h2, topi, rank, srow)


def _expert_kernel(be_ref, nu_ref, x_ref, wgu_ref, bgu_ref, wd_ref, bd_ref, y_ref,
                   wgu_bf, wd_bf):
    j = pl.program_id(0)

    @pl.when((j == 0) | (be_ref[j] != be_ref[jnp.maximum(j - 1, 0)]))
    def _():
        wgu_bf[...] = wgu_ref[...].astype(BF16)
        wd_bf[...] = wd_ref[...].astype(BF16)

    @pl.when(j < nu_ref[0])
    def _():
        xu = x_ref[...]
        x_lo = pltpu.bitcast(xu << 16, F32).astype(BF16)
        x_hi = pltpu.bitcast(xu & jnp.uint32(0xFFFF0000), F32).astype(BF16)
        x = jnp.concatenate([x_lo, x_hi], axis=1)
        gu = jnp.dot(x, wgu_bf[...], preferred_element_type=F32) + bgu_ref[...]
        g = jnp.minimum(gu[:, :D_MODEL], SWIGLU_LIMIT)
        u = jnp.clip(gu[:, D_MODEL:], -SWIGLU_LIMIT, SWIGLU_LIMIT)
        act = (u + 1.0) * (g * _sigmoid(g * SWIGLU_ALPHA))
        y_ref[...] = jnp.dot(act.astype(BF16), wd_bf[...], preferred_element_type=F32) + bd_ref[...]

    @pl.when(j >= nu_ref[0])
    def _():
        y_ref[...] = jnp.zeros_like(y_ref)


def _experts(xs, block_e, n_used, wgu, bgu, wd, bd):
    p = xs.shape[0]
    e, d, _ = wgu.shape
    n_blocks = p // MOE_BLOCK
    grid_spec = pltpu.PrefetchScalarGridSpec(
        num_scalar_prefetch=2,
        grid=(n_blocks,),
        in_specs=[pl.BlockSpec((MOE_BLOCK, d // 2), lambda j, be, nu: (j, 0)),
                  pl.BlockSpec((None, d, 2 * d), lambda j, be, nu: (be[j], 0, 0)),
                  pl.BlockSpec((None, 1, 2 * d), lambda j, be, nu: (be[j], 0, 0)),
                  pl.BlockSpec((None, d, d), lambda j, be, nu: (be[j], 0, 0)),
                  pl.BlockSpec((None, 1, d), lambda j, be, nu: (be[j], 0, 0))],
        out_specs=pl.BlockSpec((MOE_BLOCK, d), lambda j, be, nu: (j, 0)),
        scratch_shapes=[pltpu.VMEM((d, 2 * d), BF16), pltpu.VMEM((d, d), BF16)],
    )
    return pl.pallas_call(
        _expert_kernel,
        grid_spec=grid_spec,
        out_shape=jax.ShapeDtypeStruct((p, d), F32),
        compiler_params=_cparams(("arbitrary",)),
        name="moe_experts",
    )(block_e, n_used, xs, wgu, bgu.reshape(e, 1, 2 * d), wd, bd.reshape(e, 1, d))


def _combine_kernel(loc_ref, hbm_ref, cls_ref, tot8_ref,
                    y_hbm, topi_ref, rank_ref, srow_ref, gate_ref, x1_ref, mod_ref, nw_ref, o_ref,
                    sel0, sel1, yb0, yb1, sem):
    i = pl.program_id(0)
    n_tiles = pl.num_programs(0) - 1
    tm = x1_ref.shape[0]
    sels, ybufs = (sel0, sel1), (yb0, yb1)

    def start_copies(tile, slot):
        def make_copy(piece, size):
            src = pl.multiple_of(hbm_ref[piece], RUN_ALIGN)
            dst = pl.multiple_of(loc_ref[piece], RUN_ALIGN)
            return pltpu.make_async_copy(y_hbm.at[pl.ds(src, size)],
                                         ybufs[slot].at[pl.ds(dst, size)], sem.at[slot])
        _run_copies(tile, cls_ref, make_copy)

    def wait_copies(tile, slot):
        def make_copy(size):
            return pltpu.make_async_copy(y_hbm.at[pl.ds(0, size)],
                                         ybufs[slot].at[pl.ds(0, size)], sem.at[slot])
        _wait_rows(tot8_ref[tile], make_copy)

    @pl.when(i == 0)
    def _():
        yb0[...] = jnp.zeros_like(yb0)
        yb1[...] = jnp.zeros_like(yb1)
        sel0[...] = jnp.zeros_like(sel0)
        sel1[...] = jnp.zeros_like(sel1)

    for s in range(2):
        @pl.when((i < n_tiles) & (i % 2 == s))
        def _(s=s):
            start_copies(i, s)

        @pl.when((i >= 1) & (i % 2 == s))
        def _(s=s):
            wait_copies(i - 1, 1 - s)

        @pl.when(i % 2 == s)
        def _(s=s):
            pos = _positions(topi_ref, rank_ref, srow_ref)
            gate = gate_ref[...].astype(BF16)
            row = lax.broadcasted_iota(I32, (tm, SORT_ROWS), 1).astype(jnp.int16)
            sel = jnp.zeros((tm, SORT_ROWS), BF16)
            for kk, p in enumerate(pos):
                sel = jnp.where(row == p.astype(jnp.int16), gate[:, kk:kk + 1], sel)
            sels[s][...] = sel
            moe = jnp.dot(sels[1 - s][...], ybufs[1 - s][...].astype(BF16),
                          preferred_element_type=F32)
            g2 = mod_ref[5:6, :]
            o_ref[...] = x1_ref[...] + g2 * _rms(moe, nw_ref[...])


def _combine(y, topi, rank, srow, gate, x1, mod, nw, tables, seq, tm):
    t, d = x1.shape
    n = t // tm
    tiles_per_batch = seq // tm
    cur = lambda w: pl.BlockSpec((tm, w), lambda i, *_: (jnp.minimum(i, n - 1), 0))
    prev = lambda w: pl.BlockSpec((tm, w), lambda i, *_: (jnp.maximum(i - 1, 0), 0))
    grid_spec = pltpu.PrefetchScalarGridSpec(
        num_scalar_prefetch=len(tables),
        grid=(n + 1,),
        in_specs=[pl.BlockSpec(memory_space=pl.ANY),
                  cur(LANES), cur(LANES),
                  pl.BlockSpec((None, SUBLANES, LANES), lambda i, *_: (jnp.minimum(i, n - 1), 0, 0)),
                  cur(LANES), prev(d),
                  pl.BlockSpec((None, 6, d),
                               lambda i, *_: (jnp.maximum(i - 1, 0) // tiles_per_batch, 0, 0)),
                  pl.BlockSpec((1, d), lambda i, *_: (0, 0))],
        out_specs=prev(d),
        scratch_shapes=[pltpu.VMEM((tm, SORT_ROWS), BF16), pltpu.VMEM((tm, SORT_ROWS), BF16),
                        pltpu.VMEM((SORT_ROWS, d), F32), pltpu.VMEM((SORT_ROWS, d), F32),
                        pltpu.SemaphoreType.DMA((2,))],
    )
    return pl.pallas_call(
        _combine_kernel,
        grid_spec=grid_spec,
        out_shape=jax.ShapeDtypeStruct((t, d), F32),
        compiler_params=_cparams(("arbitrary",)),
        name="moe_combine",
    )(*tables, y, topi, rank, srow, gate, x1, mod, nw)


def _rope_tables(seq):
    pos = jnp.arange(seq, dtype=F32)
    inv = ROPE_THETA ** (-jnp.arange(0, DA_HEAD_DIM, 2, dtype=F32) / DA_HEAD_DIM)
    ang = pos[:, None] * inv[None, :]
    cos, sin = jnp.cos(ang), jnp.sin(ang)
    zero = jnp.zeros_like(sin)
    cos_t = jnp.tile(cos, (1, LANES // cos.shape[1]))
    sin_lo = jnp.tile(jnp.concatenate([-sin, zero], axis=1), (1, 2))
    sin_hi = jnp.tile(jnp.concatenate([zero, sin], axis=1), (1, 2))
    return cos_t, sin_lo, sin_hi


def kernel(x, c, w_mod, b_mod, norm_pre_mix, norm_post_mix, w_in, da_lambda_q1, da_lambda_k1,
           da_lambda_q2, da_lambda_k2, da_subln, hg_lb_logits, hg_norm, w_branch_a, w_branch_b,
           w_out, norm_pre_ffn, norm_post_ffn, router_w, router_b, w_gate_up, b_gate_up,
           w_down, b_down):
    bsz, seq, d = x.shape
    depth = w_mod.shape[0]
    assert depth == 1 and d == D_MODEL
    t = bsz * seq
    lyr = 0
    lambda_init = 0.8 - 0.6 * math.exp(-0.3 * lyr)
    cos, slo, shi = _rope_tables(seq)

    xf = x.reshape(t, d)
    mod = _modulation(c, w_mod[lyr], b_mod[lyr]).reshape(bsz, 6, d)

    qT, k, vT, hq, hf, hi, hg, ga, gb = _in_projection(
        xf, mod, norm_pre_mix[lyr][None], w_in[lyr].astype(BF16), cos, slo, shi, seq, tm=512)
    ya = _diff_attention(qT, k, vT, da_lambda_q1[lyr][None], da_lambda_k1[lyr][None],
                         da_lambda_q2[lyr][None], da_lambda_k2[lyr][None], da_subln[lyr],
                         bsz, seq, lambda_init, tq=2048, tk=512)
    yb = _hgrn2(hq, hf, hi, hg, hg_lb_logits[lyr:lyr + 2], hg_norm[lyr][None], bsz, seq, tc=512, nh=4)

    rw = jnp.zeros((d, LANES), BF16).at[:, :N_EXPERTS].set(router_w[lyr].astype(BF16))
    rb = jnp.full((1, LANES), NEG_BIG, F32).at[0, :N_EXPERTS].set(router_b[lyr])
    x1, h2, topi, gate, rank, counts = _merge_route(
        xf, mod, ya, yb, ga, gb, w_branch_a[lyr].astype(BF16), w_branch_b[lyr].astype(BF16),
        w_out[lyr].astype(BF16), norm_post_mix[lyr][None], norm_pre_ffn[lyr][None], rw, rb,
        seq, tm=MOE_TILE)

    n_tiles = t // MOE_TILE
    up = lambda a, m: (a + m - 1) // m * m
    cnt = counts[:, 0, :N_EXPERTS].astype(I32)
    run = up(cnt, RUN_ALIGN)
    srun = jnp.cumsum(run, axis=1) - run
    e_len = up(jnp.sum(run, axis=0), MOE_BLOCK)
    e_end = jnp.cumsum(e_len)
    start = (e_end - e_len)[None, :] + jnp.cumsum(run, axis=0) - run
    n_assign = t * TOP_K
    n_rows = (up(n_assign + n_tiles * N_EXPERTS * (RUN_ALIGN - 1), MOE_BLOCK)
              + N_EXPERTS * MOE_BLOCK)
    n_blocks = n_rows // MOE_BLOCK
    block_start = jnp.arange(n_blocks, dtype=I32) * MOE_BLOCK
    block_e = jnp.minimum(jnp.sum((e_end[None, :] <= block_start[:, None]).astype(I32), axis=1),
                          N_EXPERTS - 1)
    n_used = (e_end[-1:] // MOE_BLOCK).astype(I32)
    srow = jnp.broadcast_to(jnp.pad(srun.astype(F32), ((0, 0), (0, LANES - N_EXPERTS)))[:, None, :],
                            (n_tiles, SUBLANES, LANES))
    bits = jnp.arange(RUN_BITS, dtype=I32)
    n8 = (run // RUN_ALIGN)[:, :, None]
    taken = (n8 >> bits) & 1
    off = ((n8 >> (bits + 1)) << (bits + 1)) * RUN_ALIGN
    order = jnp.argsort(jnp.where(taken == 1, bits, RUN_BITS).reshape(n_tiles, RUN_PIECES),
                        axis=1, stable=True)
    listed = lambda a: jnp.take_along_axis(a.reshape(n_tiles, RUN_PIECES), order, axis=1).reshape(-1)
    cls = jnp.cumsum(jnp.pad(jnp.sum(taken, axis=1), ((0, 0), (1, 0))), axis=1)
    run_tables = (listed(srun[:, :, None] + off), listed(start[:, :, None] + off),
                  cls.reshape(-1).astype(I32), jnp.sum(run, axis=1) // RUN_ALIGN)

    xs = _dispatch(h2, topi, rank, srow, run_tables + (e_end.astype(I32), e_len.astype(I32)),
                   n_rows, tm=MOE_TILE)
    y = _experts(xs, block_e, n_used, w_gate_up[lyr], b_gate_up[lyr], w_down[lyr], b_down[lyr])
    out = _combine(y, topi, rank, srow, gate, x1, mod, norm_post_ffn[lyr][None], run_tables,
                   seq, tm=MOE_TILE)
    return out.reshape(bsz, seq, d)
```

```python
import functools
import math

import jax
import jax.numpy as jnp
from jax import lax
from jax.experimental import pallas as pl
from jax.experimental.pallas import tpu as pltpu

F32 = jnp.float32
BF16 = jnp.bfloat16
I32 = jnp.int32

D_MODEL = 1024
CHUNK = 64
EPS = 1e-6
ROPE_THETA = 10000.0
DA_HEADS = 4
DA_HEAD_DIM = 64
HG_HEADS = 4
HEAD_W = 128
BRANCH_W = 512
V_ROWS = HEAD_W + 16
ATTN_STRIP = 256
N_EXPERTS = 32
TOP_K = 4
SWIGLU_LIMIT = 7.0
SWIGLU_ALPHA = 1.702
MOE_BLOCK = 512
MOE_TILE = 512
LANES = 128
SUBLANES = 8
RUN_ALIGN = SUBLANES
RUN_BITS = (MOE_TILE // RUN_ALIGN).bit_length()
RUN_PIECES = N_EXPERTS * RUN_BITS
SORT_ROWS = MOE_TILE * TOP_K + N_EXPERTS * RUN_ALIGN
NEG_BIG = -1e30

VMEM_LIMIT = 56 * 1024 * 1024


def _cparams(sem):
    return pltpu.CompilerParams(dimension_semantics=sem, vmem_limit_bytes=VMEM_LIMIT)


def _rms(x, w):
    return x * lax.rsqrt(jnp.mean(x * x, axis=-1, keepdims=True) + EPS) * w


def _sigmoid(x):
    return 1.0 / (1.0 + jnp.exp(-x))


def _split3(x):
    a = x.astype(BF16)
    r = x - a.astype(F32)
    b = r.astype(BF16)
    c = (r - b.astype(F32)).astype(BF16)
    return a, b, c


def _mod_kernel(c_ref, w_ref, b_ref, o_ref):
    c = c_ref[...]
    ca = c * _sigmoid(c)
    o_ref[...] = jnp.dot(ca, w_ref[...], preferred_element_type=F32,
                         precision=lax.Precision.HIGHEST) + b_ref[...]


def _modulation(c, w_mod, b_mod):
    bsz, d = c.shape
    n = w_mod.shape[1]
    tn = 1536
    return pl.pallas_call(
        _mod_kernel,
        grid=(n // tn,),
        in_specs=[pl.BlockSpec((bsz, d), lambda j: (0, 0)),
                  pl.BlockSpec((d, tn), lambda j: (0, j)),
                  pl.BlockSpec((1, tn), lambda j: (0, j))],
        out_specs=pl.BlockSpec((bsz, tn), lambda j: (0, j)),
        out_shape=jax.ShapeDtypeStruct((bsz, n), F32),
        compiler_params=_cparams(("arbitrary",)),
        name="modulation",
    )(c, w_mod, b_mod.reshape(1, n))


def _rope(x, cos, sin_lo, sin_hi):
    return x * cos + pltpu.roll(x, 96, 1) * sin_lo + pltpu.roll(x, 32, 1) * sin_hi


def _inproj_kernel(x_ref, mod_ref, nw_ref, w_ref, cos_ref, slo_ref, shi_ref,
                   qT_ref, k_ref, vT_ref, hq_ref, hf_ref, hi_ref, hg_ref, ga_ref, gb_ref):
    x = x_ref[...]
    sh1 = mod_ref[0:1, :]
    sc1 = mod_ref[1:2, :]
    h = (_rms(x, nw_ref[...]) * (1.0 + sc1) + sh1).astype(BF16)
    cos, slo, shi = cos_ref[...], slo_ref[...], shi_ref[...]
    scale = DA_HEAD_DIM ** -0.5 * math.log2(math.e)

    def seg(i, width=BRANCH_W):
        return jnp.dot(h, w_ref[:, i:i + width], preferred_element_type=F32)

    q = seg(0)
    k = seg(BRANCH_W)
    for hd in range(DA_HEADS):
        sl = slice(hd * HEAD_W, (hd + 1) * HEAD_W)
        qT_ref[sl, :] = (_rope(q[:, sl], cos, slo, shi) * scale).T.astype(BF16)
        k_ref[:, sl] = _rope(k[:, sl], cos, slo, shi).astype(BF16)
    vT = seg(2 * BRANCH_W).T.astype(BF16)
    ones = jnp.ones((V_ROWS - HEAD_W, vT.shape[1]), BF16)
    for hd in range(DA_HEADS):
        vT_ref[hd * V_ROWS:hd * V_ROWS + HEAD_W, :] = vT[hd * HEAD_W:(hd + 1) * HEAD_W]
        vT_ref[hd * V_ROWS + HEAD_W:(hd + 1) * V_ROWS, :] = ones
    hq = seg(3 * BRANCH_W)
    hq_ref[...] = (hq * _sigmoid(hq)).astype(BF16)
    hf_ref[...] = seg(4 * BRANCH_W)
    hi_ref[...] = seg(5 * BRANCH_W).astype(BF16)
    hg = seg(6 * BRANCH_W)
    hg_ref[...] = (hg * _sigmoid(hg)).astype(BF16)
    for j in range(2):
        ga_ref[:, j * BRANCH_W:(j + 1) * BRANCH_W] = _sigmoid(seg((7 + j) * BRANCH_W)).astype(BF16)
        gb_ref[:, j * BRANCH_W:(j + 1) * BRANCH_W] = _sigmoid(seg((9 + j) * BRANCH_W)).astype(BF16)


def _in_projection(xf, mod, norm_w, w_in_bf, cos, slo, shi, seq, tm):
    t, d = xf.shape
    tiles_per_batch = seq // tm
    bspec = lambda w: pl.BlockSpec((tm, w), lambda i: (i, 0))
    tab = pl.BlockSpec((tm, LANES), lambda i: (i % tiles_per_batch, 0))
    widths = [BRANCH_W] * 7 + [D_MODEL, D_MODEL]
    dtypes = [BF16, BF16, BF16, BF16, F32, BF16, BF16, BF16, BF16]
    out_specs = [bspec(w) for w in widths]
    out_shape = [jax.ShapeDtypeStruct((t, w), dt) for w, dt in zip(widths, dtypes)]
    for idx, rows in ((0, BRANCH_W), (2, DA_HEADS * V_ROWS)):
        out_specs[idx] = pl.BlockSpec((None, rows, tm),
                                      lambda i: (i // tiles_per_batch, 0, i % tiles_per_batch))
        out_shape[idx] = jax.ShapeDtypeStruct((t // seq, rows, seq), BF16)
    return pl.pallas_call(
        _inproj_kernel,
        grid=(t // tm,),
        in_specs=[bspec(d),
                  pl.BlockSpec((None, 6, d), lambda i: (i // tiles_per_batch, 0, 0)),
                  pl.BlockSpec((1, d), lambda i: (0, 0)),
                  pl.BlockSpec(w_in_bf.shape, lambda i: (0, 0), pipeline_mode=pl.Buffered(1)),
                  tab, tab, tab],
        out_specs=out_specs,
        out_shape=out_shape,
        compiler_params=_cparams(("parallel",)),
        name="in_projection",
    )(xf, mod, norm_w, w_in_bf, cos, slo, shi)


def _attn_kernel(lq1_ref, lk1_ref, lq2_ref, lk2_ref, subln_ref, qT_ref, k_ref, vT_ref, o_ref,
                 qq_ref, s0_ref, s1_ref, p0_ref, p1_ref, a0_ref, a1_ref, m_ref, acc_ref,
                 *, tq, tk, lambda_init):
    qi = pl.program_id(2)
    ndiag = tq // tk
    s_refs, p_refs, a_refs = (s0_ref, s1_ref), (p0_ref, p1_ref), (a0_ref, a1_ref)
    qT = qT_ref[...]
    feat = lax.broadcasted_iota(I32, qT.shape, 0)
    zero = jnp.zeros_like(qT)
    qq_ref[:, :tq] = jnp.where(feat < DA_HEAD_DIM, qT, zero)
    qq_ref[:, tq:] = jnp.where(feat >= DA_HEAD_DIM, qT, zero)

    strips = [slice(c * ATTN_STRIP, (c + 1) * ATTN_STRIP) for c in range(2 * tq // ATTN_STRIP)]

    def scores(kb, dst, cs):
        start = pl.multiple_of(kb * tk, tk)
        dst[:, cs] = jnp.dot(k_ref[pl.ds(start, tk), :], qq_ref[:, cs], preferred_element_type=F32)

    def accumulate(kb, slot, cs):
        start = pl.multiple_of(kb * tk, tk)
        pv = jnp.dot(vT_ref[:, pl.ds(start, tk)], p_refs[slot][:, cs], preferred_element_type=F32)
        acc_ref[:, cs] = a_refs[slot][:, cs] * acc_ref[:, cs] + pv

    def softmax(slot, cs, diag):
        def piece(rc):
            sc = s_refs[slot][rc * CHUNK:(rc + 1) * CHUNK, cs]
            if diag is not None:
                qchunk = (lax.broadcasted_iota(I32, (1, ATTN_STRIP), 1) + cs.start) % tq // CHUNK
                sc = jnp.where(qchunk >= diag * (tk // CHUNK) + rc, sc, NEG_BIG)
            return sc

        top = piece(0)
        for rc in range(1, tk // CHUNK):
            top = jnp.maximum(top, piece(rc))
        m_old = m_ref[:, cs]
        m_new = jnp.maximum(m_old, jnp.max(top, axis=0, keepdims=True))
        m_ref[:, cs] = m_new
        a_refs[slot][:, cs] = jnp.exp2(m_old - m_new)
        for rc in range(tk // CHUNK):
            p_refs[slot][rc * CHUNK:(rc + 1) * CHUNK, cs] = jnp.exp2((piece(rc) - m_new).astype(BF16))

    def seen(diag, cs):
        return diag is None or diag < 0 or (cs.start % tq) + ATTN_STRIP > diag * tk

    def stage(kb, slot, diag=None, last=False):
        for cs in strips:
            if seen(None if diag is None else diag - 1, cs):
                accumulate(jnp.maximum(kb - 1, 0), 1 - slot, cs)
            if seen(diag, cs):
                softmax(slot, cs, diag)
            if not last and seen(None if diag is None else diag + 1, cs):
                scores(kb + 1, s_refs[1 - slot], cs)

    p1_ref[...] = jnp.zeros_like(p1_ref)
    a1_ref[...] = jnp.ones_like(a1_ref)
    m_ref[...] = jnp.full(m_ref.shape, NEG_BIG, F32)
    acc_ref[...] = jnp.zeros_like(acc_ref)
    for cs in strips:
        scores(0, s0_ref, cs)

    def pair(i, c):
        stage(2 * i, 0)
        stage(2 * i + 1, 1)
        return c

    first_diag = qi * ndiag
    lax.fori_loop(0, first_diag // 2, pair, 0)
    for dg in range(ndiag):
        stage(first_diag + dg, dg % 2, diag=dg, last=dg == ndiag - 1)
    for cs in strips:
        if seen(ndiag - 1, cs):
            accumulate(first_diag + ndiag - 1, (ndiag - 1) % 2, cs)

    acc = acc_ref[...]
    o = acc[:HEAD_W] / acc[HEAD_W:HEAD_W + 1]
    lam = (jnp.exp(jnp.sum(lq1_ref[...] * lk1_ref[...], axis=1, keepdims=True))
           - jnp.exp(jnp.sum(lq2_ref[...] * lk2_ref[...], axis=1, keepdims=True))
           + lambda_init)
    o = o[:, :tq] - lam * o[:, tq:]
    o = o * lax.rsqrt(jnp.mean(o * o, axis=0, keepdims=True) + EPS) * subln_ref[...]
    o_ref[...] = (o * (1.0 - lambda_init)).T.astype(BF16)


def _diff_attention(qT, k, vT, lq1, lk1, lq2, lk2, subln, bsz, seq, lambda_init, tq, tk):
    assert tq % (2 * tk) == 0
    k3 = k.reshape(bsz, seq, BRANCH_W)
    small = lambda n: pl.BlockSpec((1, n), lambda b, h, i: (0, 0))
    out = pl.pallas_call(
        functools.partial(_attn_kernel, tq=tq, tk=tk, lambda_init=lambda_init),
        grid=(bsz, DA_HEADS, seq // tq),
        in_specs=[small(DA_HEAD_DIM)] * 4 + [
            pl.BlockSpec((HEAD_W, 1), lambda b, h, i: (0, 0)),
            pl.BlockSpec((None, HEAD_W, tq), lambda b, h, i: (b, h, i)),
            pl.BlockSpec((None, seq, HEAD_W), lambda b, h, i: (b, 0, h)),
            pl.BlockSpec((None, V_ROWS, seq), lambda b, h, i: (b, h, 0))],
        out_specs=pl.BlockSpec((None, tq, HEAD_W), lambda b, h, i: (b, i, h)),
        out_shape=jax.ShapeDtypeStruct((bsz, seq, BRANCH_W), BF16),
        scratch_shapes=[pltpu.VMEM((HEAD_W, 2 * tq), BF16),
                        pltpu.VMEM((tk, 2 * tq), F32), pltpu.VMEM((tk, 2 * tq), F32),
                        pltpu.VMEM((tk, 2 * tq), BF16), pltpu.VMEM((tk, 2 * tq), BF16),
                        pltpu.VMEM((1, 2 * tq), F32), pltpu.VMEM((1, 2 * tq), F32),
                        pltpu.VMEM((1, 2 * tq), F32), pltpu.VMEM((V_ROWS, 2 * tq), F32)],
        compiler_params=_cparams(("parallel", "parallel", "arbitrary")),
        name="diff_attention",
    )(lq1, lk1, lq2, lk2, subln.reshape(HEAD_W, 1), qT, k3, vT)
    return out.reshape(bsz * seq, BRANCH_W)


def _hgrn_kernel(lbl_ref, nw_ref, q_ref, f_ref, i_ref, g_ref, o_ref, st_ref, *, tc, nh):
    @pl.when(pl.program_id(2) == 0)
    def _():
        st_ref[...] = jnp.zeros_like(st_ref)

    row = lax.broadcasted_iota(I32, (CHUNK, CHUNK), 0)
    col = lax.broadcasted_iota(I32, (CHUNK, CHUNK), 1)
    causal = col <= row
    tril = jnp.where(causal, 1.0, 0.0).astype(BF16)
    nt = (((1,), (1,)), ((), ()))
    tn = (((0,), (0,)), ((), ()))

    nchunk = tc // CHUNK
    units = [(hh, c) for hh in range(nh) for c in range(nchunk)]
    rows = lambda c: slice(c * CHUNK, (c + 1) * CHUNK)
    q, k, v, parts = [], [], [], []
    for hh in range(nh):
        hs = slice(hh * HEAD_W, (hh + 1) * HEAD_W)
        lg = lbl_ref[:, hs]
        ex = jnp.exp(lg - jnp.max(lg, axis=0, keepdims=True))
        lb = ex[0:1] / jnp.sum(ex, axis=0, keepdims=True)
        f = lb + (1.0 - lb) * _sigmoid(f_ref[:, hs])
        q.append(q_ref[:, hs].astype(F32))
        k.append(1.0 - f)
        v.append(i_ref[:, hs])
        parts.append(_split3(jnp.log(f)))
    pieces = jnp.concatenate(
        [jnp.concatenate([parts[hh][p][rows(c)] for hh, c in units], axis=1) for p in range(3)],
        axis=0)
    gall = jnp.dot(jnp.concatenate([tril] * 3, axis=1), pieces, preferred_element_type=F32)
    gcum = [gall[:, u * HEAD_W:(u + 1) * HEAD_W] for u in range(len(units))]
    gtot = [g[CHUNK - 1:CHUNK, :] for g in gcum]
    q_t = [(q[hh][rows(c)] * jnp.exp(gcum[u])).astype(BF16) for u, (hh, c) in enumerate(units)]
    k_t = [(k[hh][rows(c)] * jnp.exp(-gcum[u])).astype(BF16) for u, (hh, c) in enumerate(units)]
    k_dec = [(k[hh][rows(c)] * jnp.exp(gtot[u] - gcum[u])).astype(BF16)
             for u, (hh, c) in enumerate(units)]
    a = [lax.dot_general(q_t[u], k_t[u], nt, preferred_element_type=F32) for u in range(len(units))]
    ds = [lax.dot_general(v[hh][rows(c)], k_dec[u], tn, preferred_element_type=F32)
          for u, (hh, c) in enumerate(units)]
    a = [jnp.where(causal, x, 0.0).astype(BF16) for x in a]
    st_in = []
    for hh in range(nh):
        st = st_ref[hh]
        for c in range(nchunk):
            u = hh * nchunk + c
            st_in.append(st.astype(BF16))
            st = jnp.exp(gtot[u]) * st + ds[u]
        st_ref[hh] = st
    outs = [jnp.dot(a[u], v[hh][rows(c)], preferred_element_type=F32)
            + lax.dot_general(q_t[u], st_in[u], nt, preferred_element_type=F32)
            for u, (hh, c) in enumerate(units)]
    for hh in range(nh):
        hs = slice(hh * HEAD_W, (hh + 1) * HEAD_W)
        o = jnp.concatenate(outs[hh * nchunk:(hh + 1) * nchunk], axis=0)
        o_ref[:, hs] = (_rms(o, nw_ref[...]) * g_ref[:, hs].astype(F32)).astype(BF16)


def _hgrn2(hq, hf, hi, hg, lb_logits, norm_w, bsz, seq, tc, nh):
    r3 = lambda a: a.reshape(bsz, seq, BRANCH_W)
    blk = pl.BlockSpec((None, tc, nh * HEAD_W), lambda b, h, i: (b, i, h))
    out = pl.pallas_call(
        functools.partial(_hgrn_kernel, tc=tc, nh=nh),
        grid=(bsz, HG_HEADS // nh, seq // tc),
        in_specs=[pl.BlockSpec((2, nh * HEAD_W), lambda b, h, i: (0, h)),
                  pl.BlockSpec((1, HEAD_W), lambda b, h, i: (0, 0)),
                  blk, blk, blk, blk],
        out_specs=blk,
        out_shape=jax.ShapeDtypeStruct((bsz, seq, BRANCH_W), BF16),
        scratch_shapes=[pltpu.VMEM((nh, HEAD_W, HEAD_W), F32)],
        compiler_params=_cparams(("parallel", "parallel", "arbitrary")),
        name="hgrn2",
    )(lb_logits, norm_w, r3(hq), r3(hf), r3(hi), r3(hg))
    return out.reshape(bsz * seq, BRANCH_W)


def _merge_kernel(x_ref, mod_ref, ya_ref, yb_ref, ga_ref, gb_ref, wa_ref, wb_ref, wo_ref,
                  npost_ref, npre_ref, rw_ref, rb_ref,
                  x1_ref, h2_ref, topi_ref, gate_ref, rank_ref, cnt_ref, lg_ref, *, tm):
    @pl.when(pl.program_id(0) == 0)
    def _():
        lg_ref[...] = jnp.zeros_like(lg_ref)

    logits = lg_ref[...]

    g1 = mod_ref[2:3, :]
    sh2 = mod_ref[3:4, :]
    sc2 = mod_ref[4:5, :]
    a = jnp.dot(ya_ref[...], wa_ref[...], preferred_element_type=F32)
    b = jnp.dot(yb_ref[...], wb_ref[...], preferred_element_type=F32)
    merged = ga_ref[...] * a.astype(BF16) + gb_ref[...] * b.astype(BF16)
    y = jnp.dot(merged, wo_ref[...], preferred_element_type=F32)
    x1 = x_ref[...] + g1 * _rms(y, npost_ref[...])
    x1_ref[...] = x1
    h2 = (_rms(x1, npre_ref[...]) * (1.0 + sc2) + sh2).astype(BF16)
    h2_ref[...] = h2
    lg_ref[...] = jnp.dot(h2, rw_ref[...], preferred_element_type=F32) + rb_ref[...]

    lane = lax.broadcasted_iota(I32, logits.shape, 1)
    work = logits
    vals, idxs = [], []
    for _ in range(TOP_K):
        mx = jnp.max(work, axis=1, keepdims=True)
        idx = jnp.min(jnp.where(work == mx, lane, LANES), axis=1, keepdims=True)
        vals.append(mx)
        idxs.append(idx)
        work = jnp.where(lane == idx, -jnp.inf, work)
    es = [jnp.exp(vk - vals[0]) for vk in vals]
    den = es[0] + es[1] + es[2] + es[3]

    onehot = jnp.zeros(logits.shape, F32)
    for idx in idxs:
        onehot = jnp.where(lane == idx, 1.0, onehot)
    row = lax.broadcasted_iota(I32, (tm, tm), 0)
    col = lax.broadcasted_iota(I32, (tm, tm), 1)
    strict = jnp.where(col < row, 1.0, 0.0).astype(BF16)
    prefix = jnp.dot(strict, onehot.astype(BF16), preferred_element_type=F32)
    cnt_ref[...] = jnp.broadcast_to(jnp.sum(onehot, axis=0, keepdims=True), cnt_ref.shape)

    topi = jnp.zeros(logits.shape, I32)
    gate = jnp.zeros(logits.shape, F32)
    rank = jnp.zeros(logits.shape, I32)
    for kk in range(TOP_K):
        rk = jnp.sum(jnp.where(lane == idxs[kk], prefix, 0.0), axis=1, keepdims=True)
        topi = jnp.where(lane == kk, idxs[kk], topi)
        gate = jnp.where(lane == kk, es[kk] / den, gate)
        rank = jnp.where(lane == kk, rk.astype(I32), rank)
    topi_ref[...] = topi
    gate_ref[...] = gate
    rank_ref[...] = rank


def _merge_route(xf, mod, ya, yb, ga, gb, wa, wb, wo, npost, npre, rw, rb, seq, tm):
    t, d = xf.shape
    n = t // tm
    tiles_per_batch = seq // tm
    proj = lambda w: pl.BlockSpec((tm, w), lambda i: (jnp.minimum(i, n - 1), 0))
    route = lambda w: pl.BlockSpec((tm, w), lambda i: (jnp.maximum(i - 1, 0), 0))
    const = lambda a: pl.BlockSpec(a.shape, lambda i: (0,) * a.ndim)
    lane_out = jax.ShapeDtypeStruct((t, LANES), I32)
    return pl.pallas_call(
        functools.partial(_merge_kernel, tm=tm),
        grid=(n + 1,),
        in_specs=[proj(d),
                  pl.BlockSpec((None, 6, d),
                               lambda i: (jnp.minimum(i, n - 1) // tiles_per_batch, 0, 0)),
                  proj(BRANCH_W), proj(BRANCH_W), proj(d), proj(d),
                  const(wa), const(wb), const(wo), const(npost), const(npre),
                  const(rw), const(rb)],
        out_specs=[proj(d), proj(d), route(LANES), route(LANES), route(LANES),
                   pl.BlockSpec((None, SUBLANES, LANES), lambda i: (jnp.maximum(i - 1, 0), 0, 0))],
        out_shape=[jax.ShapeDtypeStruct((t, d), F32), jax.ShapeDtypeStruct((t, d), BF16),
                   lane_out, jax.ShapeDtypeStruct((t, LANES), F32), lane_out,
                   jax.ShapeDtypeStruct((n, SUBLANES, LANES), F32)],
        scratch_shapes=[pltpu.VMEM((tm, LANES), F32)],
        compiler_params=_cparams(("arbitrary",)),
        name="merge_route",
    )(xf, mod, ya, yb, ga, gb, wa, wb, wo, npost, npre, rw, rb)


def _positions(topi_ref, rank_ref, srow_ref):
    topi = topi_ref[...]
    rank = rank_ref[...]
    lane = lax.broadcasted_iota(I32, topi.shape, 1)
    srow = srow_ref[0:1, :]
    pos = []
    for kk in range(TOP_K):
        base = jnp.sum(jnp.where(lane == topi[:, kk:kk + 1], srow, 0.0), axis=1, keepdims=True)
        pos.append(base.astype(I32) + rank[:, kk:kk + 1])
    return pos


def _wait_rows(n8, make_copy):
    for b in range((SORT_ROWS // RUN_ALIGN).bit_length()):
        @pl.when((n8 >> b) & 1 == 1)
        def _(b=b):
            make_copy(RUN_ALIGN << b).wait()


def _run_copies(tile, cls_ref, make_copy):
    for b in range(RUN_BITS):
        def body(k, c, b=b):
            make_copy(tile * RUN_PIECES + k, RUN_ALIGN << b).start(priority=b % 2)
            return c

        lax.fori_loop(cls_ref[tile * (RUN_BITS + 1) + b], cls_ref[tile * (RUN_BITS + 1) + b + 1],
                      body, 0)


def _dispatch_kernel(loc_ref, hbm_ref, cls_ref, tot8_ref, eend_ref, elen_ref,
                     h_ref, topi_ref, rank_ref, srow_ref, xs_out,
                     sel0, sel1, srt0, srt1, zbuf, zsem, sem, *, min_blocks):
    i = pl.program_id(0)
    n_tiles = pl.num_programs(0) - 1
    sels, srts = (sel0, sel1), (srt0, srt1)
    tm = h_ref.shape[0]
    half = h_ref.shape[1] // 2
    n_blocks = xs_out.shape[0] // MOE_BLOCK
    n_used = eend_ref[N_EXPERTS - 1] // MOE_BLOCK

    def zero_copy(start):
        start = pl.multiple_of(start, MOE_BLOCK)
        return pltpu.make_async_copy(zbuf, xs_out.at[pl.ds(start, MOE_BLOCK)], zsem)

    def zero_fill(wait):
        for e in range(N_EXPERTS):
            @pl.when(elen_ref[e] > 0)
            def _(e=e):
                cp = zero_copy(eend_ref[e] - MOE_BLOCK)
                cp.wait() if wait else cp.start()
        for blk in range(min_blocks, n_blocks):
            @pl.when(blk >= n_used)
            def _(blk=blk):
                cp = zero_copy(blk * MOE_BLOCK)
                cp.wait() if wait else cp.start()

    @pl.when(i == 0)
    def _():
        zbuf[...] = jnp.zeros_like(zbuf)
        sel0[...] = jnp.zeros_like(sel0)
        sel1[...] = jnp.zeros_like(sel1)
        zero_fill(wait=False)
        zero_fill(wait=True)

    def start_copies(tile, slot):
        def make_copy(piece, size):
            src = pl.multiple_of(loc_ref[piece], RUN_ALIGN)
            dst = pl.multiple_of(hbm_ref[piece], RUN_ALIGN)
            return pltpu.make_async_copy(srts[slot].at[pl.ds(src, size)],
                                         xs_out.at[pl.ds(dst, size)], sem.at[slot])
        _run_copies(tile, cls_ref, make_copy)

    def wait_copies(tile, slot):
        def make_copy(size):
            return pltpu.make_async_copy(srts[slot].at[pl.ds(0, size)],
                                         xs_out.at[pl.ds(0, size)], sem.at[slot])
        _wait_rows(tot8_ref[tile], make_copy)

    for s in range(2):
        @pl.when(i % 2 == s)
        def _(s=s):
            pos = _positions(topi_ref, rank_ref, srow_ref)
            slot = lax.broadcasted_iota(I32, (tm, SORT_ROWS), 1).astype(jnp.int16)
            sel = jnp.zeros((tm, SORT_ROWS), BF16)
            for p in pos:
                sel = jnp.where(slot == p.astype(jnp.int16), jnp.ones((), BF16), sel)
            sels[s][...] = sel
            srt = lax.dot_general(sels[1 - s][...], h_ref[...], (((0,), (0,)), ((), ())),
                                  preferred_element_type=F32)
            lo = pltpu.bitcast(srt[:, :half], jnp.uint32) >> 16
            hi = pltpu.bitcast(srt[:, half:], jnp.uint32) & jnp.uint32(0xFFFF0000)
            srts[1 - s][...] = hi | lo

        @pl.when((i >= 1) & (i % 2 == s))
        def _(s=s):
            start_copies(i - 1, 1 - s)

        @pl.when((i >= 2) & (i % 2 == s))
        def _(s=s):
            wait_copies(i - 2, s)

        @pl.when((i == n_tiles) & (i % 2 == s))
        def _(s=s):
            wait_copies(i - 1, 1 - s)


def _dispatch(h2, topi, rank, srow, tables, n_rows, tm):
    t, d = h2.shape
    n = t // tm
    cur = lambda w: pl.BlockSpec((tm, w), lambda i, *_: (jnp.minimum(i, n - 1), 0))
    grid_spec = pltpu.PrefetchScalarGridSpec(
        num_scalar_prefetch=len(tables),
        grid=(n + 1,),
        in_specs=[pl.BlockSpec((tm, d), lambda i, *_: (jnp.maximum(i - 1, 0), 0)),
                  cur(LANES), cur(LANES),
                  pl.BlockSpec((None, SUBLANES, LANES), lambda i, *_: (jnp.minimum(i, n - 1), 0, 0))],
        out_specs=pl.BlockSpec(memory_space=pl.ANY),
        scratch_shapes=[pltpu.VMEM((tm, SORT_ROWS), BF16), pltpu.VMEM((tm, SORT_ROWS), BF16),
                        pltpu.VMEM((SORT_ROWS, d // 2), jnp.uint32),
                        pltpu.VMEM((SORT_ROWS, d // 2), jnp.uint32),
                        pltpu.VMEM((MOE_BLOCK, d // 2), jnp.uint32),
                        pltpu.SemaphoreType.DMA, pltpu.SemaphoreType.DMA((2,))],
    )
    return pl.pallas_call(
        functools.partial(_dispatch_kernel, min_blocks=t * TOP_K // MOE_BLOCK),
        grid_spec=grid_spec,
        out_shape=jax.ShapeDtypeStruct((n_rows, d // 2), jnp.uint32),
        compiler_params=_cparams(("arbitrary",)),
        name="moe_dispatch",
    )(*tables, h2, topi, rank, srow)


def _expert_kernel(be_ref, nu_ref, x_ref, wgu_ref, bgu_ref, wd_ref, bd_ref, y_ref,
                   wgu_bf, wd_bf):
    j = pl.program_id(0)

    @pl.when((j == 0) | (be_ref[j] != be_ref[jnp.maximum(j - 1, 0)]))
    def _():
        wgu_bf[...] = wgu_ref[...].astype(BF16)
        wd_bf[...] = wd_ref[...].astype(BF16)

    @pl.when(j < nu_ref[0])
    def _():
        xu = x_ref[...]
        x_lo = pltpu.bitcast(xu << 16, F32).astype(BF16)
        x_hi = pltpu.bitcast(xu & jnp.uint32(0xFFFF0000), F32).astype(BF16)
        x = jnp.concatenate([x_lo, x_hi], axis=1)
        gu = jnp.dot(x, wgu_bf[...], preferred_element_type=F32) + bgu_ref[...]
        g = jnp.minimum(gu[:, :D_MODEL], SWIGLU_LIMIT)
        u = jnp.clip(gu[:, D_MODEL:], -SWIGLU_LIMIT, SWIGLU_LIMIT)
        act = (u + 1.0) * (g * _sigmoid(g * SWIGLU_ALPHA))
        y_ref[...] = jnp.dot(act.astype(BF16), wd_bf[...], preferred_element_type=F32) + bd_ref[...]

    @pl.when(j >= nu_ref[0])
    def _():
        y_ref[...] = jnp.zeros_like(y_ref)


def _experts(xs, block_e, n_used, wgu, bgu, wd, bd):
    p = xs.shape[0]
    e, d, _ = wgu.shape
    n_blocks = p // MOE_BLOCK
    grid_spec = pltpu.PrefetchScalarGridSpec(
        num_scalar_prefetch=2,
        grid=(n_blocks,),
        in_specs=[pl.BlockSpec((MOE_BLOCK, d // 2), lambda j, be, nu: (j, 0)),
                  pl.BlockSpec((None, d, 2 * d), lambda j, be, nu: (be[j], 0, 0)),
                  pl.BlockSpec((None, 1, 2 * d), lambda j, be, nu: (be[j], 0, 0)),
                  pl.BlockSpec((None, d, d), lambda j, be, nu: (be[j], 0, 0)),
                  pl.BlockSpec((None, 1, d), lambda j, be, nu: (be[j], 0, 0))],
        out_specs=pl.BlockSpec((MOE_BLOCK, d), lambda j, be, nu: (j, 0)),
        scratch_shapes=[pltpu.VMEM((d, 2 * d), BF16), pltpu.VMEM((d, d), BF16)],
    )
    return pl.pallas_call(
        _expert_kernel,
        grid_spec=grid_spec,
        out_shape=jax.ShapeDtypeStruct((p, d), F32),
        compiler_params=_cparams(("arbitrary",)),
        name="moe_experts",
    )(block_e, n_used, xs, wgu, bgu.reshape(e, 1, 2 * d), wd, bd.reshape(e, 1, d))


def _combine_kernel(loc_ref, hbm_ref, cls_ref, tot8_ref,
                    y_hbm, topi_ref, rank_ref, srow_ref, gate_ref, x1_ref, mod_ref, nw_ref, o_ref,
                    sel0, sel1, yb0, yb1, sem):
    i = pl.program_id(0)
    n_tiles = pl.num_programs(0) - 1
    tm = x1_ref.shape[0]
    sels, ybufs = (sel0, sel1), (yb0, yb1)

    def start_copies(tile, slot):
        def make_copy(piece, size):
            src = pl.multiple_of(hbm_ref[piece], RUN_ALIGN)
            dst = pl.multiple_of(loc_ref[piece], RUN_ALIGN)
            return pltpu.make_async_copy(y_hbm.at[pl.ds(src, size)],
                                         ybufs[slot].at[pl.ds(dst, size)], sem.at[slot])
        _run_copies(tile, cls_ref, make_copy)

    def wait_copies(tile, slot):
        def make_copy(size):
            return pltpu.make_async_copy(y_hbm.at[pl.ds(0, size)],
                                         ybufs[slot].at[pl.ds(0, size)], sem.at[slot])
        _wait_rows(tot8_ref[tile], make_copy)

    @pl.when(i == 0)
    def _():
        yb0[...] = jnp.zeros_like(yb0)
        yb1[...] = jnp.zeros_like(yb1)
        sel0[...] = jnp.zeros_like(sel0)
        sel1[...] = jnp.zeros_like(sel1)

    for s in range(2):
        @pl.when((i < n_tiles) & (i % 2 == s))
        def _(s=s):
            start_copies(i, s)

        @pl.when((i >= 1) & (i % 2 == s))
        def _(s=s):
            wait_copies(i - 1, 1 - s)

        @pl.when(i % 2 == s)
        def _(s=s):
            pos = _positions(topi_ref, rank_ref, srow_ref)
            gate = gate_ref[...].astype(BF16)
            row = lax.broadcasted_iota(I32, (tm, SORT_ROWS), 1).astype(jnp.int16)
            sel = jnp.zeros((tm, SORT_ROWS), BF16)
            for kk, p in enumerate(pos):
                sel = jnp.where(row == p.astype(jnp.int16), gate[:, kk:kk + 1], sel)
            sels[s][...] = sel
            moe = jnp.dot(sels[1 - s][...], ybufs[1 - s][...].astype(BF16),
                          preferred_element_type=F32)
            g2 = mod_ref[5:6, :]
            o_ref[...] = x1_ref[...] + g2 * _rms(moe, nw_ref[...])


def _combine(y, topi, rank, srow, gate, x1, mod, nw, tables, seq, tm):
    t, d = x1.shape
    n = t // tm
    tiles_per_batch = seq // tm
    cur = lambda w: pl.BlockSpec((tm, w), lambda i, *_: (jnp.minimum(i, n - 1), 0))
    prev = lambda w: pl.BlockSpec((tm, w), lambda i, *_: (jnp.maximum(i - 1, 0), 0))
    grid_spec = pltpu.PrefetchScalarGridSpec(
        num_scalar_prefetch=len(tables),
        grid=(n + 1,),
        in_specs=[pl.BlockSpec(memory_space=pl.ANY),
                  cur(LANES), cur(LANES),
                  pl.BlockSpec((None, SUBLANES, LANES), lambda i, *_: (jnp.minimum(i, n - 1), 0, 0)),
                  cur(LANES), prev(d),
                  pl.BlockSpec((None, 6, d),
                               lambda i, *_: (jnp.maximum(i - 1, 0) // tiles_per_batch, 0, 0)),
                  pl.BlockSpec((1, d), lambda i, *_: (0, 0))],
        out_specs=prev(d),
        scratch_shapes=[pltpu.VMEM((tm, SORT_ROWS), BF16), pltpu.VMEM((tm, SORT_ROWS), BF16),
                        pltpu.VMEM((SORT_ROWS, d), F32), pltpu.VMEM((SORT_ROWS, d), F32),
                        pltpu.SemaphoreType.DMA((2,))],
    )
    return pl.pallas_call(
        _combine_kernel,
        grid_spec=grid_spec,
        out_shape=jax.ShapeDtypeStruct((t, d), F32),
        compiler_params=_cparams(("arbitrary",)),
        name="moe_combine",
    )(*tables, y, topi, rank, srow, gate, x1, mod, nw)


def _rope_tables(seq):
    pos = jnp.arange(seq, dtype=F32)
    inv = ROPE_THETA ** (-jnp.arange(0, DA_HEAD_DIM, 2, dtype=F32) / DA_HEAD_DIM)
    ang = pos[:, None] * inv[None, :]
    cos, sin = jnp.cos(ang), jnp.sin(ang)
    zero = jnp.zeros_like(sin)
    cos_t = jnp.tile(cos, (1, LANES // cos.shape[1]))
    sin_lo = jnp.tile(jnp.concatenate([-sin, zero], axis=1), (1, 2))
    sin_hi = jnp.tile(jnp.concatenate([zero, sin], axis=1), (1, 2))
    return cos_t, sin_lo, sin_hi


def kernel(x, c, w_mod, b_mod, norm_pre_mix, norm_post_mix, w_in, da_lambda_q1, da_lambda_k1,
           da_lambda_q2, da_lambda_k2, da_subln, hg_lb_logits, hg_norm, w_branch_a, w_branch_b,
           w_out, norm_pre_ffn, norm_post_ffn, router_w, router_b, w_gate_up, b_gate_up,
           w_down, b_down):
    bsz, seq, d = x.shape
    depth = w_mod.shape[0]
    assert depth == 1 and d == D_MODEL
    t = bsz * seq
    lyr = 0
    lambda_init = 0.8 - 0.6 * math.exp(-0.3 * lyr)
    cos, slo, shi = _rope_tables(seq)

    xf = x.reshape(t, d)
    mod = _modulation(c, w_mod[lyr], b_mod[lyr]).reshape(bsz, 6, d)

    qT, k, vT, hq, hf, hi, hg, ga, gb = _in_projection(
        xf, mod, norm_pre_mix[lyr][None], w_in[lyr].astype(BF16), cos, slo, shi, seq, tm=512)
    ya = _diff_attention(qT, k, vT, da_lambda_q1[lyr][None], da_lambda_k1[lyr][None],
                         da_lambda_q2[lyr][None], da_lambda_k2[lyr][None], da_subln[lyr],
                         bsz, seq, lambda_init, tq=2048, tk=512)
    yb = _hgrn2(hq, hf, hi, hg, hg_lb_logits[lyr:lyr + 2], hg_norm[lyr][None], bsz, seq, tc=512, nh=4)

    rw = jnp.zeros((d, LANES), BF16).at[:, :N_EXPERTS].set(router_w[lyr].astype(BF16))
    rb = jnp.full((1, LANES), NEG_BIG, F32).at[0, :N_EXPERTS].set(router_b[lyr])
    x1, h2, topi, gate, rank, counts = _merge_route(
        xf, mod, ya, yb, ga, gb, w_branch_a[lyr].astype(BF16), w_branch_b[lyr].astype(BF16),
        w_out[lyr].astype(BF16), norm_post_mix[lyr][None], norm_pre_ffn[lyr][None], rw, rb,
        seq, tm=MOE_TILE)

    n_tiles = t // MOE_TILE
    up = lambda a, m: (a + m - 1) // m * m
    cnt = counts[:, 0, :N_EXPERTS].astype(I32)
    run = up(cnt, RUN_ALIGN)
    srun = jnp.cumsum(run, axis=1) - run
    e_len = up(jnp.sum(run, axis=0), MOE_BLOCK)
    e_end = jnp.cumsum(e_len)
    start = (e_end - e_len)[None, :] + jnp.cumsum(run, axis=0) - run
    n_assign = t * TOP_K
    n_rows = (up(n_assign + n_tiles * N_EXPERTS * (RUN_ALIGN - 1), MOE_BLOCK)
              + N_EXPERTS * MOE_BLOCK)
    n_blocks = n_rows // MOE_BLOCK
    block_start = jnp.arange(n_blocks, dtype=I32) * MOE_BLOCK
    block_e = jnp.minimum(jnp.sum((e_end[None, :] <= block_start[:, None]).astype(I32), axis=1),
                          N_EXPERTS - 1)
    n_used = (e_end[-1:] // MOE_BLOCK).astype(I32)
    srow = jnp.broadcast_to(jnp.pad(srun.astype(F32), ((0, 0), (0, LANES - N_EXPERTS)))[:, None, :],
                            (n_tiles, SUBLANES, LANES))
    bits = jnp.arange(RUN_BITS, dtype=I32)
    n8 = (run // RUN_ALIGN)[:, :, None]
    taken = (n8 >> bits) & 1
    off = ((n8 >> (bits + 1)) << (bits + 1)) * RUN_ALIGN
    order = jnp.argsort(jnp.where(taken == 1, bits, RUN_BITS).reshape(n_tiles, RUN_PIECES),
                        axis=1, stable=True)
    listed = lambda a: jnp.take_along_axis(a.reshape(n_tiles, RUN_PIECES), order, axis=1).reshape(-1)
    cls = jnp.cumsum(jnp.pad(jnp.sum(taken, axis=1), ((0, 0), (1, 0))), axis=1)
    run_tables = (listed(srun[:, :, None] + off), listed(start[:, :, None] + off),
                  cls.reshape(-1).astype(I32), jnp.sum(run, axis=1) // RUN_ALIGN)

    xs = _dispatch(h2, topi, rank, srow, run_tables + (e_end.astype(I32), e_len.astype(I32)),
                   n_rows, tm=MOE_TILE)
    y = _experts(xs, block_e, n_used, w_gate_up[lyr], b_gate_up[lyr], w_down[lyr], b_down[lyr])
    out = _combine(y, topi, rank, srow, gate, x1, mod, norm_post_ffn[lyr][None], run_tables,
                   seq, tm=MOE_TILE)
    return out.reshape(bsz, seq, d)
```
